```python
import jax
import jax.numpy as jnp
from jax import lax
import numpy as np

D_MODEL = 1024
BATCH = 32
SEQ = 2048
DEPTH = 2
DEC_BATCH = 1
DEC_SEQ = 16384
PAST_LEN = 128

GRID_W = 64
HEAD_DIM = 64
ROPE_THETA = 10000.0
ROPE_PAIRS_PER_AXIS = HEAD_DIM // 4

RET_HEADS = 4
RET_WIDTH = RET_HEADS * HEAD_DIM
RET_CHUNK = 128
RET_EPS = 1e-5

ATT_Q_HEADS = 8
ATT_KV_HEADS = 2
ATT_GROUP = ATT_Q_HEADS // ATT_KV_HEADS
ATT_Q_WIDTH = ATT_Q_HEADS * HEAD_DIM
ATT_KV_WIDTH = ATT_KV_HEADS * HEAD_DIM
ATT_BLOCK = 128

RWKV_HEADS = 4
RWKV_WIDTH = RWKV_HEADS * HEAD_DIM
DECAY_LORA = 64
AAA_LORA = 64
GATE_LORA = 128
RWKV_GN_EPS = 64e-5

N_EXPERTS = 64
TOP_K = 8
N_GROUPS = 8
TOPK_GROUPS = 4
EXPERT_FF = 256
ROUTED_SCALE = 2.5

NORM_EPS = 1e-6

RET_COLS = 4 * RET_WIDTH
ATT_COLS = ATT_Q_WIDTH + 2 * ATT_KV_WIDTH
RW_COLS = 3 * RWKV_WIDTH + 2 * DECAY_LORA + 2 * AAA_LORA + GATE_LORA
GATE_COLS = 3 * D_MODEL
N_IN = RET_COLS + ATT_COLS + RW_COLS + GATE_COLS
MAIN_SPLITS = (RET_COLS, RET_COLS + ATT_COLS, RET_COLS + ATT_COLS + RW_COLS)
RW_SPLITS = (RWKV_WIDTH, 2 * RWKV_WIDTH, 3 * RWKV_WIDTH,
             3 * RWKV_WIDTH + DECAY_LORA, 3 * RWKV_WIDTH + 2 * DECAY_LORA,
             3 * RWKV_WIDTH + 2 * DECAY_LORA + AAA_LORA, 3 * RWKV_WIDTH + 2 * DECAY_LORA + 2 * AAA_LORA)

kernel_name = 'hybrid_bidir_encoder_ret_gqa_rwkv7_moe'


def rmsnorm(x, g):
    xf = x.astype(jnp.float32)
    y = xf * lax.rsqrt(jnp.mean(xf * xf, axis=-1, keepdims=True) + NORM_EPS)
    return (y * g.astype(jnp.float32)).astype(x.dtype)


def head_norm(y, g, eps):
    B, T, H, d = y.shape
    yc = y - jnp.mean(y, axis=-1, keepdims=True)
    out = yc * lax.rsqrt(jnp.mean(yc * yc, axis=-1, keepdims=True) + eps)
    return out.reshape(B, T, H * d) * g.astype(jnp.float32)


def axial_rope_tables(T):
    rows = T // GRID_W
    row = jnp.repeat(jnp.arange(rows, dtype=jnp.float32), GRID_W)
    col = jnp.tile(jnp.arange(GRID_W, dtype=jnp.float32), rows)
    freqs = ROPE_THETA ** (-jnp.arange(ROPE_PAIRS_PER_AXIS, dtype=jnp.float32) / ROPE_PAIRS_PER_AXIS)
    ang = jnp.concatenate([row[:, None] * freqs, col[:, None] * freqs], axis=-1)
    return jnp.cos(ang), jnp.sin(ang)


def apply_rope(x, cos, sin):
    half = HEAD_DIM // 2
    xf = x.astype(jnp.float32)
    x1, x2 = xf[..., :half], xf[..., half:]
    c = cos[None, :, None, :]
    s = sin[None, :, None, :]
    return jnp.concatenate([x1 * c - x2 * s, x2 * c + x1 * s], axis=-1).astype(x.dtype)


def centred_shift(x, mu):
    prev = jnp.pad(x[:, :-1], ((0, 0), (1, 0), (0, 0)))
    nxt = jnp.pad(x[:, 1:], ((0, 0), (0, 1), (0, 0)))
    return x + mu[0] * (prev - x) + mu[1] * (nxt - x)


def swiglu(x, wg, wu, wd):
    return (jax.nn.silu(x @ wg) * (x @ wu)) @ wd


def retention_log_decays():
    return jnp.log1p(-jnp.exp2(-5.0 - jnp.arange(RET_HEADS, dtype=jnp.float32)))


def retention_direction(q, k, v, log_gamma, strict):
    B, T, H, d = q.shape
    n_chunks = T // RET_CHUNK
    qc = q.reshape(B, n_chunks, RET_CHUNK, H, d)
    kc = k.reshape(B, n_chunks, RET_CHUNK, H, d)
    vc = v.reshape(B, n_chunks, RET_CHUNK, H, d)
    pos = jnp.arange(RET_CHUNK, dtype=jnp.float32)
    diff = pos[:, None] - pos[None, :]
    visible = (diff > 0) if strict else (diff >= 0)
    decay_in = jnp.where(visible[None], jnp.exp(log_gamma[:, None, None] * jnp.maximum(diff, 0.0)[None]), 0.0)
    scores = jnp.einsum('bcnhd,bcmhd->bchnm', qc, kc) * decay_in[None, None]
    inner = jnp.einsum('bchnm,bcmhe->bcnhe', scores, vc)
    zeta = jnp.exp(log_gamma[:, None] * (RET_CHUNK - 1.0 - pos)[None, :])
    kv = jnp.einsum('bcmhd,hm,bcmhe->cbhde', kc, zeta, vc)
    chunk_decay = jnp.exp(log_gamma * RET_CHUNK)[None, :, None, None]

    def step(R, kv_c):
        return R * chunk_decay + kv_c, R

    _, r_prev = lax.scan(step, jnp.zeros((B, H, d, d), jnp.float32), kv)
    xi = jnp.exp(log_gamma[:, None] * (pos + 1.0)[None, :])
    cross = jnp.einsum('bcnhd,hn,cbhde->bcnhe', qc, xi, r_prev)
    return (inner + cross).reshape(B, T, H, d)


def retention_mixer(cols, cos, sin, gn):
    B, T, _ = cols.shape
    q, k, v, g = jnp.split(cols, 4, axis=-1)
    q = apply_rope(q.reshape(B, T, RET_HEADS, HEAD_DIM), cos, sin).astype(jnp.float32)
    k = apply_rope(k.reshape(B, T, RET_HEADS, HEAD_DIM), cos, sin).astype(jnp.float32) * (HEAD_DIM ** -0.5)
    v = v.reshape(B, T, RET_HEADS, HEAD_DIM).astype(jnp.float32)
    log_gamma = retention_log_decays()
    y_fwd = retention_direction(q, k, v, log_gamma, strict=False)
    y_bwd = jnp.flip(retention_direction(jnp.flip(q, 1), jnp.flip(k, 1), jnp.flip(v, 1), log_gamma[::-1], strict=True), 1)
    y = head_norm(y_fwd + y_bwd, gn, RET_EPS)
    return y.astype(cols.dtype) * jax.nn.silu(g)


def block_attention(q, k, v):
    B, T, _, d = q.shape
    n_blocks = T // ATT_BLOCK
    qb = q.reshape(B, n_blocks, ATT_BLOCK, ATT_KV_HEADS, ATT_GROUP, d).transpose(1, 0, 2, 3, 4, 5) * (d ** -0.5)

    def one_block(q_blk):
        s = jnp.einsum('bqhgd,bkhd->bhgqk', q_blk, k).astype(jnp.float32)
        p = jax.nn.softmax(s, axis=-1).astype(v.dtype)
        return jnp.einsum('bhgqk,bkhd->bqhgd', p, v)

    o = lax.map(one_block, qb)
    return o.transpose(1, 0, 2, 3, 4, 5).reshape(B, T, ATT_Q_HEADS * d)


def attention_mixer(cols, cos, sin, q_norm_g, k_norm_g):
    B, T, _ = cols.shape
    q, k, v = jnp.split(cols, [ATT_Q_WIDTH, ATT_Q_WIDTH + ATT_KV_WIDTH], axis=-1)
    q = rmsnorm(q.reshape(B, T, ATT_Q_HEADS, HEAD_DIM), q_norm_g)
    k = rmsnorm(k.reshape(B, T, ATT_KV_HEADS, HEAD_DIM), k_norm_g)
    v = v.reshape(B, T, ATT_KV_HEADS, HEAD_DIM)
    q = apply_rope(q, cos, sin)
    k = apply_rope(k, cos, sin)
    return block_attention(q, k, v)


def rwkv_decay(xw, w0, w2):
    wl = w0 + jnp.tanh(xw) @ w2
    return jnp.exp(-jnp.exp(-jax.nn.softplus(-wl) - 0.5))


def rwkv_scan(r, w, k, v, a, b, reverse):
    B, T, H, N = r.shape
    xs = tuple(jnp.moveaxis(t, 1, 0) for t in (r, w, k, v, a, b))

    def step(S, inp):
        r_t, w_t, k_t, v_t, a_t, b_t = inp
        S_new = (S * w_t[:, :, None, :]
                 + jnp.einsum('bhij,bhj->bhi', S, a_t)[..., None] * b_t[:, :, None, :]
                 + v_t[..., None] * k_t[:, :, None, :])
        S_read = S if reverse else S_new
        return S_new, jnp.einsum('bhij,bhj->bhi', S_read, r_t)

    _, y = lax.scan(step, jnp.zeros((B, H, N, N), jnp.float32), xs, reverse=reverse)
    return jnp.moveaxis(y, 0, 1)


def rwkv_mixer(cols, p, l):
    B, T, _ = cols.shape
    x = centred_shift(cols, p['shift_mu'][l]).astype(jnp.float32)
    r, k, v, xwf, xwb, xaf, xab, xg = jnp.split(x, RW_SPLITS, axis=-1)
    heads = lambda t: t.reshape(B, T, RWKV_HEADS, HEAD_DIM)
    w_f = rwkv_decay(xwf, p['rw_w0_f'][l], p['rw_w2_f'][l])
    w_b = rwkv_decay(xwb, p['rw_w0_b'][l], p['rw_w2_b'][l])
    a_f = jax.nn.sigmoid(p['rw_a0_f'][l] + xaf @ p['rw_a2_f'][l])
    a_b = jax.nn.sigmoid(p['rw_a0_b'][l] + xab @ p['rw_a2_b'][l])
    g = jax.nn.sigmoid(xg) @ p['rw_g2'][l]
    k_a = p['rw_k_a'][l]
    kk = heads(k * p['rw_k_k'][l])
    kk = kk * lax.rsqrt(jnp.sum(kk * kk, axis=-1, keepdims=True) + 1e-12)
    k_f = heads(k * (1.0 + (a_f - 1.0) * k_a))
    k_b = heads(k * (1.0 + (a_b - 1.0) * k_a))
    r_h, v_h = heads(r), heads(v)
    y_f = rwkv_scan(r_h, heads(w_f), k_f, v_h, -kk, kk * heads(a_f), reverse=False)
    y_b = rwkv_scan(r_h, heads(w_b), k_b, v_h, -kk, kk * heads(a_b), reverse=True)
    y = head_norm(y_f + y_b, p['rw_gn'][l], RWKV_GN_EPS)
    bonus = (jnp.sum(r_h * k_f * p['rw_r_k'][l], axis=-1, keepdims=True) * v_h).reshape(B, T, RWKV_WIDTH)
    return ((y + bonus) * g).astype(cols.dtype)


def mixer_sublayer(h, cos, sin, p, l):
    proj = h @ p['w_in'][l]
    ret_cols, att_cols, rw_cols, gate_cols = jnp.split(proj, MAIN_SPLITS, axis=-1)
    y_a = retention_mixer(ret_cols, cos, sin, p['ret_gn'][l])
    y_b = attention_mixer(att_cols, cos, sin, p['q_norm_g'][l], p['k_norm_g'][l])
    y_c = rwkv_mixer(rw_cols, p, l)
    g_a, g_b, g_c = jnp.split(jax.nn.sigmoid(gate_cols), 3, axis=-1)
    merged = (g_a * (y_a @ p['w_branch_a'][l])
              + g_b * (y_b @ p['w_branch_b'][l])
              + g_c * (y_c @ p['w_branch_c'][l]))
    return merged @ p['w_out'][l]


def moe_sublayer(h, p, l):
    B, T, D = h.shape
    x = h.reshape(B * T, D)
    scores = jax.nn.sigmoid(x.astype(jnp.float32) @ p['router_w'][l].astype(jnp.float32))
    choice = scores + p['router_bias'][l].astype(jnp.float32)
    per_group = N_EXPERTS // N_GROUPS
    group_score = lax.top_k(choice.reshape(-1, N_GROUPS, per_group), 2)[0].sum(-1)
    _, group_idx = lax.top_k(group_score, TOPK_GROUPS)
    group_mask = jax.nn.one_hot(group_idx, N_GROUPS, dtype=jnp.float32).sum(1) > 0
    expert_mask = jnp.repeat(group_mask, per_group, axis=-1)
    _, idx = lax.top_k(jnp.where(expert_mask, choice, -jnp.inf), TOP_K)
    wts = jnp.take_along_axis(scores, idx, axis=-1)
    wts = wts / jnp.sum(wts, axis=-1, keepdims=True) * ROUTED_SCALE
    combine = jnp.einsum('nke,nk->en', jax.nn.one_hot(idx, N_EXPERTS, dtype=jnp.float32), wts).astype(x.dtype)

    def add_expert(acc, ew):
        wg, wu, wd, cw = ew
        return acc + cw[:, None] * swiglu(x, wg, wu, wd), None

    routed, _ = lax.scan(add_expert, jnp.zeros_like(x),
                         (p['exp_w_gate'][l], p['exp_w_up'][l], p['exp_w_down'][l], combine))
    shared = swiglu(x, p['sh_w_gate'][l], p['sh_w_up'][l], p['sh_w_down'][l])
    return (routed + shared).reshape(B, T, D)


def trunk(x, c, p):
    T = x.shape[1]
    cos, sin = axial_rope_tables(T)
    c_act = jax.nn.silu(c)
    for l in range(DEPTH):
        mod = (c_act @ p['ada_w'][l] + p['ada_b'][l])[:, None, :]
        sh1, sc1, g1, sh2, sc2, g2 = jnp.split(mod, 6, axis=-1)
        h = rmsnorm(x, p['norm1_g'][l]) * (1.0 + sc1) + sh1
        x = x + g1 * mixer_sublayer(h, cos, sin, p, l)
        h = rmsnorm(x, p['norm2_g'][l]) * (1.0 + sc2) + sh2
        x = x + g2 * moe_sublayer(h, p, l)
    return rmsnorm(x, p['final_g'])


def setup_inputs(seed: int = 0) -> dict:
    key = jax.random.key(seed)
    keys = iter(jax.random.split(key, 48))
    L, D, E, F = DEPTH, D_MODEL, N_EXPERTS, EXPERT_FF

    def nrm(shape, scale):
        return scale * jax.random.normal(next(keys), shape, jnp.float32)

    def unif(shape, lo, hi):
        return jax.random.uniform(next(keys), shape, jnp.float32, lo, hi)

    return {
        'x_prompt': nrm((BATCH, SEQ, D), 1.0),
        'x_sample': nrm((DEC_BATCH, DEC_SEQ, D), 1.0),
        'c_prompt': nrm((BATCH, D), 1.0),
        'c_sample': nrm((DEC_BATCH, D), 1.0),
        'norm1_g': 1.0 + nrm((L, D), 0.05),
        'norm2_g': 1.0 + nrm((L, D), 0.05),
        'final_g': 1.0 + nrm((D,), 0.05),
        'ada_w': nrm((L, D, 6 * D), 0.5 * D ** -0.5),
        'ada_b': nrm((L, 6 * D), 0.02),
        'w_in': nrm((L, D, N_IN), D ** -0.5),
        'ret_gn': 1.0 + nrm((L, RET_WIDTH), 0.05),
        'q_norm_g': 1.0 + nrm((L, HEAD_DIM), 0.05),
        'k_norm_g': 1.0 + nrm((L, HEAD_DIM), 0.05),
        'shift_mu': unif((L, 2, RW_COLS), 0.0, 0.5),
        'rw_w0_f': unif((L, RWKV_WIDTH), -6.0, -1.0),
        'rw_w2_f': nrm((L, DECAY_LORA, RWKV_WIDTH), 0.5 * DECAY_LORA ** -0.5),
        'rw_w0_b': unif((L, RWKV_WIDTH), -6.0, -1.0),
        'rw_w2_b': nrm((L, DECAY_LORA, RWKV_WIDTH), 0.5 * DECAY_LORA ** -0.5),
        'rw_a0_f': nrm((L, RWKV_WIDTH), 0.5),
        'rw_a2_f': nrm((L, AAA_LORA, RWKV_WIDTH), AAA_LORA ** -0.5),
        'rw_a0_b': nrm((L, RWKV_WIDTH), 0.5),
        'rw_a2_b': nrm((L, AAA_LORA, RWKV_WIDTH), AAA_LORA ** -0.5),
        'rw_g2': nrm((L, GATE_LORA, RWKV_WIDTH), GATE_LORA ** -0.5),
        'rw_k_k': 0.85 + nrm((L, RWKV_WIDTH), 0.05),
        'rw_k_a': 1.0 + nrm((L, RWKV_WIDTH), 0.05),
        'rw_r_k': nrm((L, RWKV_HEADS, HEAD_DIM), 0.1),
        'rw_gn': 1.0 + nrm((L, RWKV_WIDTH), 0.05),
        'w_branch_a': nrm((L, RET_WIDTH, D), RET_WIDTH ** -0.5),
        'w_branch_b': nrm((L, ATT_Q_WIDTH, D), ATT_Q_WIDTH ** -0.5),
        'w_branch_c': nrm((L, RWKV_WIDTH, D), RWKV_WIDTH ** -0.5),
        'w_out': nrm((L, D, D), D ** -0.5),
        'router_w': nrm((L, D, E), D ** -0.5),
        'router_bias': nrm((L, E), 0.01),
        'exp_w_gate': nrm((L, E, D, F), D ** -0.5),
        'exp_w_up': nrm((L, E, D, F), D ** -0.5),
        'exp_w_down': nrm((L, E, F, D), F ** -0.5),
        'sh_w_gate': nrm((L, D, F), D ** -0.5),
        'sh_w_up': nrm((L, D, F), D ** -0.5),
        'sh_w_down': nrm((L, F, D), F ** -0.5),
    }


def reference(x_prompt, x_sample, c_prompt, c_sample, norm1_g, norm2_g, final_g, ada_w, ada_b,
              w_in, ret_gn, q_norm_g, k_norm_g, shift_mu, rw_w0_f, rw_w2_f, rw_w0_b, rw_w2_b,
              rw_a0_f, rw_a2_f, rw_a0_b, rw_a2_b, rw_g2, rw_k_k, rw_k_a, rw_r_k, rw_gn,
              w_branch_a, w_branch_b, w_branch_c, w_out, router_w, router_bias,
              exp_w_gate, exp_w_up, exp_w_down, sh_w_gate, sh_w_up, sh_w_down):
    p = {
        'norm1_g': norm1_g, 'norm2_g': norm2_g, 'final_g': final_g, 'ada_w': ada_w, 'ada_b': ada_b,
        'w_in': w_in, 'ret_gn': ret_gn, 'q_norm_g': q_norm_g, 'k_norm_g': k_norm_g, 'shift_mu': shift_mu,
        'rw_w0_f': rw_w0_f, 'rw_w2_f': rw_w2_f, 'rw_w0_b': rw_w0_b, 'rw_w2_b': rw_w2_b,
        'rw_a0_f': rw_a0_f, 'rw_a2_f': rw_a2_f, 'rw_a0_b': rw_a0_b, 'rw_a2_b': rw_a2_b,
        'rw_g2': rw_g2, 'rw_k_k': rw_k_k, 'rw_k_a': rw_k_a, 'rw_r_k': rw_r_k, 'rw_gn': rw_gn,
        'w_branch_a': w_branch_a, 'w_branch_b': w_branch_b, 'w_branch_c': w_branch_c, 'w_out': w_out,
        'router_w': router_w, 'router_bias': router_bias,
        'exp_w_gate': exp_w_gate, 'exp_w_up': exp_w_up, 'exp_w_down': exp_w_down,
        'sh_w_gate': sh_w_gate, 'sh_w_up': sh_w_up, 'sh_w_down': sh_w_down,
    }
    y_prompt = trunk(x_prompt, c_prompt, p)
    y_sample = trunk(x_sample, c_sample, p)
    return (y_prompt, y_sample)
```

```python
import functools

import numpy as np
import jax
import jax.numpy as jnp
from jax import lax
from jax.experimental import pallas as pl
from jax.experimental.pallas import tpu as pltpu

F32 = jnp.float32
BF16 = jnp.bfloat16

D_MODEL = 1024
DEPTH = 2
GRID_W = 64
HEAD_DIM = 64
ROPE_THETA = 10000.0
ROPE_PAIRS_PER_AXIS = HEAD_DIM // 4

RET_HEADS = 4
RET_WIDTH = RET_HEADS * HEAD_DIM
RET_CHUNK = 128
RET_EPS = 1e-5

ATT_Q_HEADS = 8
ATT_KV_HEADS = 2
ATT_GROUP = ATT_Q_HEADS // ATT_KV_HEADS
ATT_Q_WIDTH = ATT_Q_HEADS * HEAD_DIM
ATT_KV_WIDTH = ATT_KV_HEADS * HEAD_DIM

RWKV_HEADS = 4
RWKV_WIDTH = RWKV_HEADS * HEAD_DIM
DECAY_LORA = 64
AAA_LORA = 64
GATE_LORA = 128
RWKV_GN_EPS = 64e-5
RW_CHUNK = 64
RW_FIELDS = 11

N_EXPERTS = 64
TOP_K = 8
N_GROUPS = 8
TOPK_GROUPS = 4
EXPERT_FF = 256
ROUTED_SCALE = 2.5
NORM_EPS = 1e-6

RET_COLS = 4 * RET_WIDTH
ATT_COLS = ATT_Q_WIDTH + 2 * ATT_KV_WIDTH
RW_COLS = 3 * RWKV_WIDTH + 2 * DECAY_LORA + 2 * AAA_LORA + GATE_LORA
GATE_COLS = 3 * D_MODEL

V7X_LANES = 128
VMEM_LIMIT_BYTES = 48 * 1024 * 1024


def _params(*dims):
    return pltpu.CompilerParams(dimension_semantics=dims, vmem_limit_bytes=VMEM_LIMIT_BYTES)


def _sigmoid(x):
    return 1.0 / (1.0 + jnp.exp(-x))


def _silu(x):
    return x * _sigmoid(x)


def _dot(a, b):
    return jnp.dot(a, b, preferred_element_type=F32)


def _dot_nt(a, b):
    return lax.dot_general(a, b, (((1,), (1,)), ((), ())), preferred_element_type=F32)


def _split(a):
    hi = a.astype(BF16)
    lo = (a - hi.astype(F32)).astype(BF16)
    return hi, lo


def _dot_split_lhs(a, b_bf16):
    hi, lo = _split(a)
    return _dot(hi, b_bf16) + _dot(lo, b_bf16)


def _dot3(a, b):
    ah, al = _split(a)
    bh, bl = _split(b)
    return _dot(ah, bh) + _dot(ah, bl) + _dot(al, bh)


def _norm_mod(x, gain, shift, scale):
    ms = jnp.mean(x * x, axis=-1, keepdims=True)
    return x * lax.rsqrt(ms + NORM_EPS) * gain * (1.0 + scale) + shift


def _head_layer_norm(y, ones_bd, eps):
    mean = _dot_split_lhs(y, ones_bd) * (1.0 / HEAD_DIM)
    yc = y - mean
    var = _dot_split_lhs(yc * yc, ones_bd) * (1.0 / HEAD_DIM)
    return yc * lax.rsqrt(var + eps)


def _ada_kernel(c_ref, w_ref, b_ref, o_ref):
    c = c_ref[...]
    o_ref[...] = _dot3(_silu(c), w_ref[...]) + b_ref[...]


def _ada_mod(c, ada_w, ada_b):
    B = c.shape[0]
    Bp = max(8, B)
    cp = jnp.zeros((Bp, D_MODEL), F32).at[:B].set(c)
    L = ada_w.shape[0]
    tn = 1536
    out = pl.pallas_call(
        _ada_kernel,
        grid=(L, 6 * D_MODEL // tn),
        in_specs=[pl.BlockSpec((Bp, D_MODEL), lambda l, j: (0, 0)),
                  pl.BlockSpec((None, D_MODEL, tn), lambda l, j: (l, 0, j)),
                  pl.BlockSpec((None, 1, tn), lambda l, j: (l, 0, j))],
        out_specs=pl.BlockSpec((None, Bp, tn), lambda l, j: (l, 0, j)),
        out_shape=jax.ShapeDtypeStruct((L, Bp, 6 * D_MODEL), F32),
        compiler_params=_params("parallel", "parallel"),
        name="ada_mod",
    )(cp, ada_w, ada_b.reshape(L, 1, 6 * D_MODEL))
    return out[:, :B].reshape(L, B, 6, D_MODEL)


def _proj_ret_kernel(x_ref, mod_ref, g_ref, w_ref, cos_ref, sin_ref, o_ref):
    h = _norm_mod(x_ref[...], g_ref[...], mod_ref[0:1, :], mod_ref[1:2, :]).astype(BF16)
    p = _dot(h, w_ref[...])
    W = RET_WIDTH
    cos = cos_ref[:, 0:W]
    sin = sin_ref[:, 0:W]
    o_ref[:, 0:W] = (p[:, 0:W] * cos + p[:, 4 * W:5 * W] * sin).astype(BF16)
    o_ref[:, W:2 * W] = ((p[:, W:2 * W] * cos + p[:, 5 * W:6 * W] * sin) * (HEAD_DIM ** -0.5)).astype(BF16)
    o_ref[:, 2 * W:3 * W] = p[:, 2 * W:3 * W].astype(BF16)
    o_ref[:, 3 * W:4 * W] = _silu(p[:, 3 * W:4 * W]).astype(BF16)


def _proj_att_kernel(x_ref, mod_ref, g_ref, w_ref, cos_ref, sin_ref, qkg_ref, ones_ref, q_ref, k_ref, v_ref):
    h = _norm_mod(x_ref[...], g_ref[...], mod_ref[0:1, :], mod_ref[1:2, :]).astype(BF16)
    p = _dot(h, w_ref[...])
    QW, KW = ATT_Q_WIDTH, ATT_KV_WIDTH
    q = p[:, 0:QW]
    k = p[:, QW:QW + KW]
    v = p[:, QW + KW:QW + 2 * KW]
    qs = p[:, QW + 2 * KW:2 * QW + 2 * KW]
    ks = p[:, 2 * QW + 2 * KW:2 * QW + 3 * KW]
    ones = ones_ref[...]
    rq = lax.rsqrt(_dot_split_lhs(q * q, ones) * (1.0 / HEAD_DIM) + NORM_EPS)
    rk = lax.rsqrt(_dot_split_lhs(k * k, ones[0:KW, 0:KW]) * (1.0 / HEAD_DIM) + NORM_EPS)
    cos = cos_ref[...]
    sin = sin_ref[...]
    qo = (q * qkg_ref[0:1, :] * cos + qs * qkg_ref[1:2, :] * sin) * (rq * (HEAD_DIM ** -0.5))
    ko = (k * qkg_ref[2:3, 0:KW] * cos[:, 0:KW] + ks * qkg_ref[3:4, 0:KW] * sin[:, 0:KW]) * rk
    q_ref[...] = qo.astype(BF16)
    kb = ko.astype(BF16)
    vb = v.astype(BF16)
    for hk in range(ATT_KV_HEADS):
        k_ref[hk] = kb[:, hk * HEAD_DIM:(hk + 1) * HEAD_DIM]
        v_ref[hk] = vb[:, hk * HEAD_DIM:(hk + 1) * HEAD_DIM]


def _proj_gate_kernel(x_ref, mod_ref, g_ref, w_ref, o_ref):
    h = _norm_mod(x_ref[...], g_ref[...], mod_ref[0:1, :], mod_ref[1:2, :]).astype(BF16)
    o_ref[...] = _sigmoid(_dot(h, w_ref[...])).astype(BF16)


def _proj_rw_kernel(x_ref, xp_ref, xn_ref, mod_ref, g_ref, w_ref, mu_ref, vec_ref, w2_ref, a2_ref, g2_ref,
                    ones_ref, o_ref, p_scr, *, tm, tiles_per_seq):
    i = pl.program_id(0)
    gain, shift, scale = g_ref[...], mod_ref[0:1, :], mod_ref[1:2, :]
    w = w_ref[...]
    h = _norm_mod(x_ref[...], gain, shift, scale).astype(BF16)
    p = _dot(h, w)
    hp = _norm_mod(xp_ref[...], gain, shift, scale).astype(BF16)
    hn = _norm_mod(xn_ref[...], gain, shift, scale).astype(BF16)
    first = (i % tiles_per_seq) == 0
    last = (i % tiles_per_seq) == tiles_per_seq - 1
    prev_row = jnp.where(first, 0.0, _dot(hp, w)[7:8, :])
    next_row = jnp.where(last, 0.0, _dot(hn, w)[0:1, :])
    p_scr[8:8 + tm, :] = p
    p_scr[7:8, :] = prev_row
    p_scr[8 + tm:9 + tm, :] = next_row
    prev = p_scr[7:7 + tm, :]
    nxt = p_scr[9:9 + tm, :]
    x = p + mu_ref[0:1, :] * (prev - p) + mu_ref[1:2, :] * (nxt - p)

    W = RWKV_WIDTH
    r, k, v = x[:, 0:W], x[:, W:2 * W], x[:, 2 * W:3 * W]
    xw = x[:, 3 * W:3 * W + 2 * DECAY_LORA]
    xa = x[:, 3 * W + 2 * DECAY_LORA:3 * W + 2 * DECAY_LORA + 2 * AAA_LORA]
    xg = x[:, 3 * W + 2 * DECAY_LORA + 2 * AAA_LORA:]
    w0_f, w0_b = vec_ref[0:1, :], vec_ref[1:2, :]
    a0_f, a0_b = vec_ref[2:3, :], vec_ref[3:4, :]
    k_k, k_a, r_k = vec_ref[4:5, :], vec_ref[5:6, :], vec_ref[6:7, :]
    wl = _dot(jnp.tanh(xw).astype(BF16), w2_ref[...])
    al = _dot(xa.astype(BF16), a2_ref[...])
    dec_c = float(np.exp(-0.5))
    w_f = jnp.exp(-dec_c * _sigmoid(w0_f + wl[:, 0:W]))
    w_b = jnp.exp(-dec_c * _sigmoid(w0_b + wl[:, W:2 * W]))
    a_f = _sigmoid(a0_f + al[:, 0:W])
    a_b = _sigmoid(a0_b + al[:, W:2 * W])
    gate = _dot(_sigmoid(xg).astype(BF16), g2_ref[...])
    ones = ones_ref[...]
    kk = k * k_k
    kk = kk * lax.rsqrt(_dot_split_lhs(kk * kk, ones) + 1e-12)
    k_f = k * (1.0 + (a_f - 1.0) * k_a)
    k_b = k * (1.0 + (a_b - 1.0) * k_a)
    bonus = _dot_split_lhs(r * k_f * r_k, ones) * v
    for j, val in enumerate((r, kk, v, w_f, w_b, k_f, k_b, kk * a_f, kk * a_b, gate, bonus)):
        o_ref[:, j * W:(j + 1) * W] = val


def _token_specs(tm, tiles_per_seq):
    x_spec = pl.BlockSpec((tm, D_MODEL), lambda i: (i, 0))
    mod_spec = pl.BlockSpec((None, 6, D_MODEL), lambda i: (i // tiles_per_seq, 0, 0))
    g_spec = pl.BlockSpec((1, D_MODEL), lambda i: (0, 0))
    return x_spec, mod_spec, g_spec


def _full(shape):
    nd = len(shape)
    return pl.BlockSpec(shape, lambda *_: (0,) * nd)


def _input_projections(x, mod, T, lw, tabs):
    N = x.shape[0]
    tm = min(512, T)
    tps = T // tm
    grid = (N // tm,)
    x_spec, mod_spec, g_spec = _token_specs(tm, tps)
    tab_spec = pl.BlockSpec((tm, ATT_Q_WIDTH), lambda i: (i % tps, 0))
    cos, sin = tabs["cos"], tabs["sin"]

    ret = pl.pallas_call(
        _proj_ret_kernel, grid=grid,
        in_specs=[x_spec, mod_spec, g_spec, _full(lw["w_ret"].shape), tab_spec, tab_spec],
        out_specs=pl.BlockSpec((tm, RET_COLS), lambda i: (i, 0)),
        out_shape=jax.ShapeDtypeStruct((N, RET_COLS), BF16),
        compiler_params=_params("parallel"), name="proj_ret",
    )(x, mod, lw["norm1_g"], lw["w_ret"], cos, sin)

    q, k, v = pl.pallas_call(
        _proj_att_kernel, grid=grid,
        in_specs=[x_spec, mod_spec, g_spec, _full(lw["w_att"].shape), tab_spec, tab_spec,
                  _full(lw["qk_gain"].shape), _full(tabs["ones512"].shape)],
        out_specs=[pl.BlockSpec((tm, ATT_Q_WIDTH), lambda i: (i, 0)),
                   pl.BlockSpec((ATT_KV_HEADS, tm, HEAD_DIM), lambda i: (0, i, 0)),
                   pl.BlockSpec((ATT_KV_HEADS, tm, HEAD_DIM), lambda i: (0, i, 0))],
        out_shape=[jax.ShapeDtypeStruct((N, ATT_Q_WIDTH), BF16),
                   jax.ShapeDtypeStruct((ATT_KV_HEADS, N, HEAD_DIM), BF16),
                   jax.ShapeDtypeStruct((ATT_KV_HEADS, N, HEAD_DIM), BF16)],
        compiler_params=_params("parallel"), name="proj_att",
    )(x, mod, lw["norm1_g"], lw["w_att"], cos, sin, lw["qk_gain"], tabs["ones512"])

    gates = pl.pallas_call(
        _proj_gate_kernel, grid=grid,
        in_specs=[x_spec, mod_spec, g_spec, _full(lw["w_gate"].shape)],
        out_specs=pl.BlockSpec((tm, GATE_COLS), lambda i: (i, 0)),
        out_shape=jax.ShapeDtypeStruct((N, GATE_COLS), BF16),
        compiler_params=_params("parallel"), name="proj_gate",
    )(x, mod, lw["norm1_g"], lw["w_gate"])

    tm_rw = min(256, T)
    tps_rw = T // tm_rw
    x_spec, mod_spec, g_spec = _token_specs(tm_rw, tps_rw)
    rows8 = tm_rw // 8
    last8 = N // 8 - 1
    rwp = pl.pallas_call(
        functools.partial(_proj_rw_kernel, tm=tm_rw, tiles_per_seq=tps_rw), grid=(N // tm_rw,),
        in_specs=[x_spec,
                  pl.BlockSpec((8, D_MODEL), lambda i: (jnp.maximum(i * rows8 - 1, 0), 0)),
                  pl.BlockSpec((8, D_MODEL), lambda i: (jnp.minimum((i + 1) * rows8, last8), 0)),
                  mod_spec, g_spec, _full(lw["w_rw"].shape), _full(lw["shift_mu"].shape),
                  _full(lw["rw_vec"].shape), _full(lw["rw_w2"].shape), _full(lw["rw_a2"].shape),
                  _full(lw["rw_g2"].shape), _full(tabs["ones256"].shape)],
        out_specs=pl.BlockSpec((tm_rw, RW_FIELDS * RWKV_WIDTH), lambda i: (i, 0)),
        out_shape=jax.ShapeDtypeStruct((N, RW_FIELDS * RWKV_WIDTH), F32),
        scratch_shapes=[pltpu.VMEM((tm_rw + 16, RW_COLS), F32)],
        compiler_params=_params("parallel"), name="proj_rw",
    )(x, x, x, mod, lw["norm1_g"], lw["w_rw"], lw["shift_mu"], lw["rw_vec"], lw["rw_w2"], lw["rw_a2"],
      lw["rw_g2"], tabs["ones256"])
    return ret, q, k, v, gates, rwp


def _retention_tables(reverse):
    lg = np.log1p(-np.exp2(-5.0 - np.arange(RET_HEADS, dtype=np.float64)))
    if reverse:
        lg = lg[::-1]
    C = RET_CHUNK
    pos = np.arange(C, dtype=np.float64)
    diff = pos[:, None] - pos[None, :]
    if reverse:
        dec = np.where(diff < 0, np.exp(lg[:, None, None] * np.maximum(-diff, 0.0)[None]), 0.0)
        xi = np.exp(lg[:, None] * (C - pos)[None, :])
        zeta = np.exp(lg[:, None] * pos[None, :])
    else:
        dec = np.where(diff >= 0, np.exp(lg[:, None, None] * np.maximum(diff, 0.0)[None]), 0.0)
        xi = np.exp(lg[:, None] * (pos + 1.0)[None, :])
        zeta = np.exp(lg[:, None] * (C - 1.0 - pos)[None, :])
    chunk_decay = tuple(float(v) for v in np.exp(lg * C))
    widen = lambda a: jnp.asarray(np.repeat(a.T, HEAD_DIM, axis=1), F32)
    return jnp.asarray(dec, F32), widen(xi), widen(zeta), chunk_decay


def _retention_kernel(ret_ref, dec_ref, xi_ref, zeta_ref, *rest, chunk_decay, final):
    if final:
        yf_ref, gn_ref, ones_ref, o_ref, state, y_scr = rest
    else:
        o_ref, state, y_scr = rest

    @pl.when(pl.program_id(1) == 0)
    def _():
        state[...] = jnp.zeros_like(state)

    W = RET_WIDTH
    q = ret_ref[:, 0:W]
    k = ret_ref[:, W:2 * W]
    v = ret_ref[:, 2 * W:3 * W]
    qx = (q.astype(F32) * xi_ref[...]).astype(BF16)
    kzt = (k.astype(F32) * zeta_ref[...]).T
    for h in range(RET_HEADS):
        sl = slice(h * HEAD_DIM, (h + 1) * HEAD_DIM)
        s = _dot_nt(q[:, sl], k[:, sl]) * dec_ref[h]
        inner = _dot(s.astype(BF16), v[:, sl])
        r_prev = state[h]
        cross = _dot(qx[:, sl], r_prev.astype(BF16))
        kv = _dot(kzt[sl, :].astype(BF16), v[:, sl])
        state[h] = r_prev * chunk_decay[h] + kv
        y_scr[:, sl] = inner + cross
    y = y_scr[...]
    if final:
        y = _head_layer_norm(y + yf_ref[...], ones_ref[...], RET_EPS) * gn_ref[...]
        o_ref[...] = (y * ret_ref[:, 3 * W:4 * W].astype(F32)).astype(BF16)
    else:
        o_ref[...] = y


def _retention(ret, B, T, ret_gn, ones256):
    N = ret.shape[0]
    C = RET_CHUNK
    nc = T // C
    scratch = [pltpu.VMEM((RET_HEADS, HEAD_DIM, HEAD_DIM), F32), pltpu.VMEM((C, RET_WIDTH), F32)]

    dec, xi, zeta, cd = _retention_tables(False)
    fwd_row = lambda b, c: (b * nc + c, 0)
    yf = pl.pallas_call(
        functools.partial(_retention_kernel, chunk_decay=cd, final=False), grid=(B, nc),
        in_specs=[pl.BlockSpec((C, RET_COLS), fwd_row), _full(dec.shape), _full(xi.shape), _full(zeta.shape)],
        out_specs=pl.BlockSpec((C, RET_WIDTH), fwd_row),
        out_shape=jax.ShapeDtypeStruct((N, RET_WIDTH), F32),
        scratch_shapes=scratch, compiler_params=_params("parallel", "arbitrary"), name="retention_fwd",
    )(ret, dec, xi, zeta)

    dec, xi, zeta, cd = _retention_tables(True)
    bwd_row = lambda b, c: (b * nc + nc - 1 - c, 0)
    return pl.pallas_call(
        functools.partial(_retention_kernel, chunk_decay=cd, final=True), grid=(B, nc),
        in_specs=[pl.BlockSpec((C, RET_COLS), bwd_row), _full(dec.shape), _full(xi.shape), _full(zeta.shape),
                  pl.BlockSpec((C, RET_WIDTH), bwd_row), _full(ret_gn.shape), _full(ones256.shape)],
        out_specs=pl.BlockSpec((C, RET_WIDTH), bwd_row),
        out_shape=jax.ShapeDtypeStruct((N, RET_WIDTH), BF16),
        scratch_shapes=scratch, compiler_params=_params("parallel", "arbitrary"), name="retention_bwd",
    )(ret, dec, xi, zeta, yf, ret_gn, ones256)


def _flash_kernel(q_ref, k_ref, v_ref, o_ref, q_scr, m_scr, l_scr, acc_scr, *, tq, tk):
    j = pl.program_id(3)

    @pl.when(j == 0)
    def _():
        for g in range(ATT_GROUP):
            q_scr[g * tq:(g + 1) * tq, :] = q_ref[:, g * HEAD_DIM:(g + 1) * HEAD_DIM]
        m_scr[...] = jnp.full_like(m_scr, -jnp.inf)
        l_scr[...] = jnp.zeros_like(l_scr)
        acc_scr[...] = jnp.zeros_like(acc_scr)

    s = _dot_nt(q_scr[...], k_ref[...])
    m_prev = m_scr[...]
    m_next = jnp.maximum(m_prev, jnp.max(s, axis=1, keepdims=True))
    p = jnp.exp(s - pltpu.repeat(m_next, tk // V7X_LANES, axis=1))
    alpha = jnp.exp(m_prev - m_next)
    l_scr[...] = alpha * l_scr[...] + jnp.sum(p, axis=1, keepdims=True)
    m_scr[...] = m_next
    acc_scr[...] = alpha[:, 0:HEAD_DIM] * acc_scr[...] + _dot(p.astype(BF16), v_ref[...])

    @pl.when(j == pl.num_programs(3) - 1)
    def _():
        o = acc_scr[...] / l_scr[:, 0:HEAD_DIM]
        for g in range(ATT_GROUP):
            o_ref[:, g * HEAD_DIM:(g + 1) * HEAD_DIM] = o[g * tq:(g + 1) * tq, :].astype(BF16)


def _attention(q, k, v, B, T):
    N = q.shape[0]
    tq = min(256, T)
    tk = min(512, T)
    nq, nk = T // tq, T // tk
    GW = ATT_GROUP * HEAD_DIM
    rows = ATT_GROUP * tq
    return pl.pallas_call(
        functools.partial(_flash_kernel, tq=tq, tk=tk), grid=(B, ATT_KV_HEADS, nq, nk),
        in_specs=[pl.BlockSpec((tq, GW), lambda b, h, i, j: (b * nq + i, h)),
                  pl.BlockSpec((None, tk, HEAD_DIM), lambda b, h, i, j: (h, b * nk + j, 0)),
                  pl.BlockSpec((None, tk, HEAD_DIM), lambda b, h, i, j: (h, b * nk + j, 0))],
        out_specs=pl.BlockSpec((tq, GW), lambda b, h, i, j: (b * nq + i, h)),
        out_shape=jax.ShapeDtypeStruct((N, ATT_Q_WIDTH), BF16),
        scratch_shapes=[pltpu.VMEM((rows, HEAD_DIM), BF16), pltpu.VMEM((rows, V7X_LANES), F32),
                        pltpu.VMEM((rows, V7X_LANES), F32), pltpu.VMEM((rows, HEAD_DIM), F32)],
        compiler_params=_params("parallel", "parallel", "parallel", "arbitrary"), name="flash_attention",
    )(q, k, v)


def _rwkv_scan_kernel(kk_f, r_f, w_f, k_f, b_f, vt_f, kk_b, r_b, w_b, k_b, b_b, vt_b, ones_ref,
                      yf_ref, yb_ref, state, vk_scr):
    C = RW_CHUNK
    L = V7X_LANES

    @pl.when(pl.program_id(1) == 0)
    def _():
        state[...] = jnp.zeros_like(state)

    ones = ones_ref[...]
    lane = lax.broadcasted_iota(jnp.int32, (1, L), 1) & (HEAD_DIM - 1)
    same_head = (lax.broadcasted_iota(jnp.int32, (L, L), 0) // HEAD_DIM
                 == lax.broadcasted_iota(jnp.int32, (L, L), 1) // HEAD_DIM)
    step_lane = (lax.broadcasted_iota(jnp.int32, (C, HEAD_DIM, L), 0)
                 == (lax.broadcasted_iota(jnp.int32, (C, HEAD_DIM, L), 2) & (HEAD_DIM - 1)))
    dirs = ((kk_f, r_f, w_f, k_f, b_f, vt_f), (kk_b, r_b, w_b, k_b, b_b, vt_b))

    for d in range(2):
        for p in range(2):
            kc = dirs[d][3][:, p * L:(p + 1) * L]
            kbd = jnp.where(same_head, jnp.concatenate([kc, kc], axis=0), 0.0).astype(BF16)
            vt = dirs[d][5][p]
            sel = jnp.where(step_lane, vt[None], 0.0).reshape(C * HEAD_DIM, L).astype(BF16)
            vk_scr[d * 2 + p] = _dot(sel, kbd).reshape(C, HEAD_DIM, L)

    SUB = 8

    def group(g, carry):
        states, ycols = list(carry[0]), list(carry[1])
        bases = (g * SUB, (C // SUB - 1 - g) * SUB)
        rows = [[ref[pl.ds(pl.multiple_of(bases[d], SUB), SUB), :] for ref in dirs[d][:5]] for d in range(2)]
        for j in range(SUB):
            for d in range(2):
                kk8, r8, w8, _, b8 = rows[d]
                jj = j if d == 0 else SUB - 1 - j
                tt = bases[d] + jj
                at_t = lane == tt
                for p in range(2):
                    c = d * 2 + p
                    sl = slice(p * L, (p + 1) * L)
                    s_old = states[c]
                    sa = _dot((s_old * kk8[jj:jj + 1, sl]).astype(BF16), ones)
                    s_new = s_old * w8[jj:jj + 1, sl] - sa * b8[jj:jj + 1, sl] + vk_scr[c, tt]
                    s_read = s_new if d == 0 else s_old
                    y = _dot((s_read * r8[jj:jj + 1, sl]).astype(BF16), ones)
                    states[c] = s_new
                    ycols[c] = jnp.where(at_t, y, ycols[c])
        return tuple(states), tuple(ycols)

    init = (tuple(state[c] for c in range(4)), tuple(jnp.zeros((HEAD_DIM, L), F32) for _ in range(4)))
    states, ycols = lax.fori_loop(0, C // SUB, group, init)
    for c in range(4):
        state[c] = states[c]
    for p in range(2):
        yf_ref[p] = ycols[p]
        yb_ref[p] = ycols[2 + p]


def _rwkv_scan(rwp, B, T, ones128):
    W, C, L = RWKV_WIDTH, RW_CHUNK, V7X_LANES
    nc = T // C
    v = rwp[:, 2 * W:3 * W].reshape(B, nc, C, 2, 2, HEAD_DIM)
    vt = v.transpose(0, 1, 3, 5, 4, 2).reshape(B, nc, 2, HEAD_DIM, L)

    fwd = lambda f: pl.BlockSpec((C, W), lambda b, c: (b * nc + c, f))
    bwd = lambda f: pl.BlockSpec((C, W), lambda b, c: (b * nc + nc - 1 - c, f))
    vt_fwd = pl.BlockSpec((None, None, 2, HEAD_DIM, L), lambda b, c: (b, c, 0, 0, 0))
    vt_bwd = pl.BlockSpec((None, None, 2, HEAD_DIM, L), lambda b, c: (b, nc - 1 - c, 0, 0, 0))
    yt_shape = jax.ShapeDtypeStruct((B, nc, 2, HEAD_DIM, L), F32)
    yt_f, yt_b = pl.pallas_call(
        _rwkv_scan_kernel, grid=(B, nc),
        in_specs=[fwd(1), fwd(0), fwd(3), fwd(5), fwd(7), vt_fwd,
                  bwd(1), bwd(0), bwd(4), bwd(6), bwd(8), vt_bwd, _full(ones128.shape)],
        out_specs=[vt_fwd, vt_bwd], out_shape=[yt_shape, yt_shape],
        scratch_shapes=[pltpu.VMEM((4, HEAD_DIM, L), F32), pltpu.VMEM((4, C, HEAD_DIM, L), F32)],
        compiler_params=_params("parallel", "arbitrary"), name="rwkv_scan",
    )(rwp, rwp, rwp, rwp, rwp, vt, rwp, rwp, rwp, rwp, rwp, vt, ones128)

    untranspose = lambda yt: yt.reshape(B, nc, 2, HEAD_DIM, 2, C).transpose(0, 1, 5, 2, 4, 3).reshape(B * T, W)
    return untranspose(yt_f), untranspose(yt_b)


def _merge_kernel(x_ref, mod_ref, ya_ref, yb_ref, yf_ref, ybk_ref, rg_ref, bonus_ref, gn_ref, gates_ref,
                  ones_ref, wa_ref, wb_ref, wc_ref, wo_ref, o_ref):
    y = _head_layer_norm(yf_ref[...] + ybk_ref[...], ones_ref[...], RWKV_GN_EPS) * gn_ref[...]
    yc = ((y + bonus_ref[...]) * rg_ref[...]).astype(BF16)
    D = D_MODEL
    merged = (gates_ref[:, 0:D].astype(F32) * _dot(ya_ref[...], wa_ref[...])
              + gates_ref[:, D:2 * D].astype(F32) * _dot(yb_ref[...], wb_ref[...])
              + gates_ref[:, 2 * D:3 * D].astype(F32) * _dot(yc, wc_ref[...]))
    o_ref[...] = x_ref[...] + mod_ref[2:3, :] * _dot(merged.astype(BF16), wo_ref[...])


def _merge(x, mod, T, ya, yb, yf, ybk, rwp, gates, lw, ones256):
    N = x.shape[0]
    tm = min(512, T)
    tps = T // tm
    x_spec, mod_spec, _ = _token_specs(tm, tps)
    row = lambda w: pl.BlockSpec((tm, w), lambda i: (i, 0))
    field = lambda f: pl.BlockSpec((tm, RWKV_WIDTH), lambda i: (i, f))
    return pl.pallas_call(
        _merge_kernel, grid=(N // tm,),
        in_specs=[x_spec, mod_spec, row(RET_WIDTH), row(ATT_Q_WIDTH), row(RWKV_WIDTH), row(RWKV_WIDTH),
                  field(9), field(10), _full(lw["rw_gn"].shape), row(GATE_COLS), _full(ones256.shape),
                  _full(lw["w_branch_a"].shape), _full(lw["w_branch_b"].shape), _full(lw["w_branch_c"].shape),
                  _full(lw["w_out"].shape)],
        out_specs=x_spec, out_shape=jax.ShapeDtypeStruct((N, D_MODEL), F32),
        compiler_params=_params("parallel"), name="merge_out",
    )(x, mod, ya, yb, yf, ybk, rwp, rwp, lw["rw_gn"], gates, ones256,
      lw["w_branch_a"], lw["w_branch_b"], lw["w_branch_c"], lw["w_out"])


def _route(h, hb, rwt_ref, rbias_ref, tm):
    h_lo = (h - hb.astype(F32)).astype(BF16)
    rw_hi, rw_lo = _split(rwt_ref[...])
    logits = _dot_nt(rw_hi, hb) + _dot_nt(rw_hi, h_lo) + _dot_nt(rw_lo, hb)
    scores = _sigmoid(logits)
    choice = scores + jnp.concatenate([rbias_ref[...]] * (tm // V7X_LANES), axis=1)
    per_group = N_EXPERTS // N_GROUPS
    sub = lax.broadcasted_iota(jnp.int32, (per_group, tm), 0)
    groups, gscore = [], []
    for g in range(N_GROUPS):
        cg = choice[g * per_group:(g + 1) * per_group, :]
        m1 = jnp.max(cg, axis=0, keepdims=True)
        first = jnp.min(jnp.where(cg == m1, sub, per_group), axis=0, keepdims=True)
        m2 = jnp.max(jnp.where(sub == first, -jnp.inf, cg), axis=0, keepdims=True)
        groups.append(cg)
        gscore.append(m1 + m2)
    masked = []
    for g in range(N_GROUPS):
        beaten = jnp.zeros((1, tm), F32)
        for o in range(N_GROUPS):
            if o == g:
                continue
            wins = (gscore[o] >= gscore[g]) if o < g else (gscore[o] > gscore[g])
            beaten = beaten + jnp.where(wins, 1.0, 0.0)
        keep = jnp.where(beaten < TOPK_GROUPS, 1.0, 0.0)
        masked.append(jnp.where(jnp.broadcast_to(keep, (per_group, tm)) > 0.5, groups[g], -jnp.inf))
    mc = jnp.concatenate(masked, axis=0)
    eidx = lax.broadcasted_iota(jnp.int32, (N_EXPERTS, tm), 0)
    beaten = jnp.zeros((N_EXPERTS, tm), F32)
    for o in range(N_EXPERTS):
        row = jnp.broadcast_to(mc[o:o + 1, :], (N_EXPERTS, tm))
        tie = jnp.where(eidx > o, 1.0, 0.0)
        beaten = beaten + jnp.where(row > mc, 1.0, jnp.where(row == mc, tie, 0.0))
    wts = jnp.where(beaten < TOP_K, scores, 0.0)
    return wts / jnp.sum(wts, axis=0, keepdims=True) * ROUTED_SCALE


def _moe_kernel(x_ref, mod_ref, g_ref, rwt_ref, rbias_ref, shgu_ref, shd_ref, wgu_ref, wd_ref, fin_ref,
                o_ref, h_scr, comb_scr, acc_scr, *, tm, final):
    e = pl.program_id(1)
    F = EXPERT_FF

    @pl.when(e == 0)
    def _():
        h = _norm_mod(x_ref[...], g_ref[...], mod_ref[3:4, :], mod_ref[4:5, :])
        hb = h.astype(BF16)
        h_scr[...] = hb
        comb_t = _route(h, hb, rwt_ref, rbias_ref, tm)
        comb_scr[...] = jnp.concatenate([comb_t, jnp.zeros_like(comb_t)], axis=0).T
        gu = _dot(hb, shgu_ref[...])
        acc_scr[...] = _dot((_silu(gu[:, 0:F]) * gu[:, F:2 * F]).astype(BF16), shd_ref[...])

    gu = _dot(h_scr[...], wgu_ref[...])
    pick = jnp.where(lax.broadcasted_iota(jnp.int32, (2 * N_EXPERTS, F), 0) == e, 1.0, 0.0).astype(BF16)
    cw = _dot_split_lhs(comb_scr[...], pick)
    act = (_silu(gu[:, 0:F]) * gu[:, F:2 * F] * cw).astype(BF16)
    acc_scr[...] += _dot(act, wd_ref[...])

    @pl.when(e == N_EXPERTS - 1)
    def _():
        out = x_ref[...] + mod_ref[5:6, :] * acc_scr[...]
        if final:
            ms = jnp.mean(out * out, axis=-1, keepdims=True)
            out = out * lax.rsqrt(ms + NORM_EPS) * fin_ref[...]
        o_ref[...] = out


def _moe(x, mod, T, lw, final_g, final):
    N = x.shape[0]
    tm = min(1024, T)
    tps = T // tm
    D, F, E = D_MODEL, EXPERT_FF, N_EXPERTS
    const2 = lambda shape: pl.BlockSpec(shape, lambda i, e: (0, 0))
    return pl.pallas_call(
        functools.partial(_moe_kernel, tm=tm, final=final), grid=(N // tm, E),
        in_specs=[pl.BlockSpec((tm, D), lambda i, e: (i, 0)),
                  pl.BlockSpec((None, 6, D), lambda i, e: (i // tps, 0, 0)),
                  const2((1, D)), const2((E, D)), const2((E, V7X_LANES)), const2((D, 2 * F)), const2((F, D)),
                  pl.BlockSpec((None, D, 2 * F), lambda i, e: (e, 0, 0)),
                  pl.BlockSpec((None, F, D), lambda i, e: (e, 0, 0)),
                  const2((1, D))],
        out_specs=pl.BlockSpec((tm, D), lambda i, e: (i, 0)),
        out_shape=jax.ShapeDtypeStruct((N, D), F32),
        scratch_shapes=[pltpu.VMEM((tm, D), BF16), pltpu.VMEM((tm, 2 * E), F32), pltpu.VMEM((tm, D), F32)],
        compiler_params=_params("parallel", "arbitrary"), name="moe",
    )(x, mod, lw["norm2_g"], lw["router_wt"], lw["router_bias"], lw["sh_gu"], lw["sh_d"],
      lw["exp_gu"], lw["exp_d"], final_g)


def _swap_halves(w, heads):
    shp = w.shape
    w = w.reshape(shp[:-1] + (heads, 2, HEAD_DIM // 2))
    return jnp.flip(w, axis=-2).reshape(shp)


def _block_diag2(a, b):
    za = jnp.zeros((a.shape[0], b.shape[1]), a.dtype)
    zb = jnp.zeros((b.shape[0], a.shape[1]), a.dtype)
    return jnp.concatenate([jnp.concatenate([a, za], 1), jnp.concatenate([zb, b], 1)], 0)


def _layer_weights(P, l):
    w_in = P["w_in"][l]
    o1, o2, o3 = RET_COLS, RET_COLS + ATT_COLS, RET_COLS + ATT_COLS + RW_COLS
    w_ret, w_att, w_rw, w_gate = w_in[:, :o1], w_in[:, o1:o2], w_in[:, o2:o3], w_in[:, o3:]
    W = RET_WIDTH
    w_ret = jnp.concatenate([w_ret, _swap_halves(w_ret[:, 0:W], RET_HEADS),
                             _swap_halves(w_ret[:, W:2 * W], RET_HEADS)], axis=1)
    QW, KW = ATT_Q_WIDTH, ATT_KV_WIDTH
    w_att = jnp.concatenate([w_att, _swap_halves(w_att[:, 0:QW], ATT_Q_HEADS),
                             _swap_halves(w_att[:, QW:QW + KW], ATT_KV_HEADS)], axis=1)
    qg, kg = P["q_norm_g"][l], P["k_norm_g"][l]
    pad = lambda v: jnp.zeros((QW,), F32).at[:v.shape[0]].set(v)
    qk_gain = jnp.zeros((8, QW), F32)
    qk_gain = qk_gain.at[0].set(jnp.tile(qg, ATT_Q_HEADS))
    qk_gain = qk_gain.at[1].set(jnp.tile(_swap_halves(qg, 1), ATT_Q_HEADS))
    qk_gain = qk_gain.at[2].set(pad(jnp.tile(kg, ATT_KV_HEADS)))
    qk_gain = qk_gain.at[3].set(pad(jnp.tile(_swap_halves(kg, 1), ATT_KV_HEADS)))
    rw_vec = jnp.zeros((8, RWKV_WIDTH), F32)
    for j, name in enumerate(("rw_w0_f", "rw_w0_b", "rw_a0_f", "rw_a0_b", "rw_k_k", "rw_k_a")):
        rw_vec = rw_vec.at[j].set(P[name][l])
    rw_vec = rw_vec.at[6].set(P["rw_r_k"][l].reshape(RWKV_WIDTH))
    row = lambda v: v.reshape(1, -1)
    return {
        "norm1_g": row(P["norm1_g"][l]), "norm2_g": row(P["norm2_g"][l]),
        "w_ret": w_ret.astype(BF16), "w_att": w_att.astype(BF16), "w_rw": w_rw.astype(BF16),
        "w_gate": w_gate.astype(BF16), "qk_gain": qk_gain, "ret_gn": row(P["ret_gn"][l]),
        "shift_mu": P["shift_mu"][l], "rw_vec": rw_vec,
        "rw_w2": _block_diag2(P["rw_w2_f"][l], P["rw_w2_b"][l]).astype(BF16),
        "rw_a2": _block_diag2(P["rw_a2_f"][l], P["rw_a2_b"][l]).astype(BF16),
        "rw_g2": P["rw_g2"][l].astype(BF16), "rw_gn": row(P["rw_gn"][l]),
        "w_branch_a": P["w_branch_a"][l].astype(BF16), "w_branch_b": P["w_branch_b"][l].astype(BF16),
        "w_branch_c": P["w_branch_c"][l].astype(BF16), "w_out": P["w_out"][l].astype(BF16),
        "router_wt": P["router_w"][l].T,
        "router_bias": jnp.broadcast_to(P["router_bias"][l][:, None], (N_EXPERTS, V7X_LANES)),
        "sh_gu": jnp.concatenate([P["sh_w_gate"][l], P["sh_w_up"][l]], axis=1).astype(BF16),
        "sh_d": P["sh_w_down"][l].astype(BF16),
        "exp_gu": jnp.concatenate([P["exp_w_gate"][l], P["exp_w_up"][l]], axis=2).astype(BF16),
        "exp_d": P["exp_w_down"][l].astype(BF16),
    }


def _ones_block_diag(n):
    idx = np.arange(n) // HEAD_DIM
    return jnp.asarray(idx[:, None] == idx[None, :], BF16)


def _tables(T):
    rows = T // GRID_W
    row = jnp.repeat(jnp.arange(rows, dtype=F32), GRID_W)
    col = jnp.tile(jnp.arange(GRID_W, dtype=F32), rows)
    freqs = ROPE_THETA ** (-jnp.arange(ROPE_PAIRS_PER_AXIS, dtype=F32) / ROPE_PAIRS_PER_AXIS)
    ang = jnp.concatenate([row[:, None] * freqs, col[:, None] * freqs], axis=-1)
    cos, sin = jnp.cos(ang), jnp.sin(ang)
    return {
        "cos": jnp.tile(jnp.concatenate([cos, cos], axis=-1), (1, ATT_Q_HEADS)),
        "sin": jnp.tile(jnp.concatenate([-sin, sin], axis=-1), (1, ATT_Q_HEADS)),
        "ones128": _ones_block_diag(128), "ones256": _ones_block_diag(256), "ones512": _ones_block_diag(512),
    }


def _trunk(x, c, P, layer_weights):
    B, T, D = x.shape
    tabs = _tables(T)
    mods = _ada_mod(c, P["ada_w"], P["ada_b"])
    x = x.reshape(B * T, D)
    final_g = P["final_g"].reshape(1, D)
    depth = len(layer_weights)
    for l, lw in enumerate(layer_weights):
        mod = mods[l]
        ret, q, k, v, gates, rwp = _input_projections(x, mod, T, lw, tabs)
        ya = _retention(ret, B, T, lw["ret_gn"], tabs["ones256"])
        yb = _attention(q, k, v, B, T)
        yf, ybk = _rwkv_scan(rwp, B, T, tabs["ones128"])
        x = _merge(x, mod, T, ya, yb, yf, ybk, rwp, gates, lw, tabs["ones256"])
        x = _moe(x, mod, T, lw, final_g, final=(l == depth - 1))
    return x.reshape(B, T, D)


def kernel(x_prompt, x_sample, c_prompt, c_sample, norm1_g, norm2_g, final_g, ada_w, ada_b, w_in, ret_gn, q_norm_g, k_norm_g, shift_mu, rw_w0_f, rw_w2_f, rw_w0_b, rw_w2_b, rw_a0_f, rw_a2_f, rw_a0_b, rw_a2_b, rw_g2, rw_k_k, rw_k_a, rw_r_k, rw_gn, w_branch_a, w_branch_b, w_branch_c, w_out, router_w, router_bias, exp_w_gate, exp_w_up, exp_w_down, sh_w_gate, sh_w_up, sh_w_down):
    P = {
        "norm1_g": norm1_g, "norm2_g": norm2_g, "final_g": final_g, "ada_w": ada_w, "ada_b": ada_b,
        "w_in": w_in, "ret_gn": ret_gn, "q_norm_g": q_norm_g, "k_norm_g": k_norm_g, "shift_mu": shift_mu,
        "rw_w0_f": rw_w0_f, "rw_w2_f": rw_w2_f, "rw_w0_b": rw_w0_b, "rw_w2_b": rw_w2_b,
        "rw_a0_f": rw_a0_f, "rw_a2_f": rw_a2_f, "rw_a0_b": rw_a0_b, "rw_a2_b": rw_a2_b,
        "rw_g2": rw_g2, "rw_k_k": rw_k_k, "rw_k_a": rw_k_a, "rw_r_k": rw_r_k, "rw_gn": rw_gn,
        "w_branch_a": w_branch_a, "w_branch_b": w_branch_b, "w_branch_c": w_branch_c, "w_out": w_out,
        "router_w": router_w, "router_bias": router_bias,
        "exp_w_gate": exp_w_gate, "exp_w_up": exp_w_up, "exp_w_down": exp_w_down,
        "sh_w_gate": sh_w_gate, "sh_w_up": sh_w_up, "sh_w_down": sh_w_down,
    }
    layer_weights = [_layer_weights(P, l) for l in range(w_in.shape[0])]
    y_prompt = _trunk(x_prompt, c_prompt, P, layer_weights)
    y_sample = _trunk(x_sample, c_sample, P, layer_weights)
    return (y_prompt, y_sample)
```

```python
import functools

import numpy as np
import jax
import jax.numpy as jnp
from jax import lax
from jax.experimental import pallas as pl
from jax.experimental.pallas import tpu as pltpu

F32 = jnp.float32
BF16 = jnp.bfloat16

D_MODEL = 1024
DEPTH = 2
GRID_W = 64
HEAD_DIM = 64
ROPE_THETA = 10000.0
ROPE_PAIRS_PER_AXIS = HEAD_DIM // 4

RET_HEADS = 4
RET_WIDTH = RET_HEADS * HEAD_DIM
RET_CHUNK = 128
RET_EPS = 1e-5

ATT_Q_HEADS = 8
ATT_KV_HEADS = 2
ATT_GROUP = ATT_Q_HEADS // ATT_KV_HEADS
ATT_Q_WIDTH = ATT_Q_HEADS * HEAD_DIM
ATT_KV_WIDTH = ATT_KV_HEADS * HEAD_DIM

RWKV_HEADS = 4
RWKV_WIDTH = RWKV_HEADS * HEAD_DIM
DECAY_LORA = 64
AAA_LORA = 64
GATE_LORA = 128
RWKV_GN_EPS = 64e-5
RW_CHUNK = 64
RW_FIELDS = 11
RW_BATCH_ROWS = 4

FLASH_Q_ROWS = 512
FLASH_KV_ROWS = 512
FLASH_SUB_ROWS = 1024

MOE_TOKENS = 1024
MOE_EXPERTS_PER_STEP = 4
MOE_SUB_ROWS = 256

N_EXPERTS = 64
TOP_K = 8
N_GROUPS = 8
TOPK_GROUPS = 4
EXPERT_FF = 256
ROUTED_SCALE = 2.5
NORM_EPS = 1e-6

RET_COLS = 4 * RET_WIDTH
ATT_COLS = ATT_Q_WIDTH + 2 * ATT_KV_WIDTH
RW_COLS = 3 * RWKV_WIDTH + 2 * DECAY_LORA + 2 * AAA_LORA + GATE_LORA
GATE_COLS = 3 * D_MODEL

V7X_LANES = 128
VMEM_LIMIT_BYTES = 48 * 1024 * 1024


def _params(*dims):
    return pltpu.CompilerParams(dimension_semantics=dims, vmem_limit_bytes=VMEM_LIMIT_BYTES)


def _sigmoid(x):
    return 1.0 / (1.0 + jnp.exp(-x))


def _silu(x):
    return x * _sigmoid(x)


def _dot(a, b):
    return jnp.dot(a, b, preferred_element_type=F32)


def _dot_nt(a, b):
    return lax.dot_general(a, b, (((1,), (1,)), ((), ())), preferred_element_type=F32)


def _split(a):
    hi = a.astype(BF16)
    lo = (a - hi.astype(F32)).astype(BF16)
    return hi, lo


def _dot_split_lhs(a, b_bf16):
    hi, lo = _split(a)
    return _dot(hi, b_bf16) + _dot(lo, b_bf16)


def _dot3(a, b):
    ah, al = _split(a)
    bh, bl = _split(b)
    return _dot(ah, bh) + _dot(ah, bl) + _dot(al, bh)


def _norm_mod(x, gain, shift, scale):
    ms = jnp.mean(x * x, axis=-1, keepdims=True)
    return x * lax.rsqrt(ms + NORM_EPS) * gain * (1.0 + scale) + shift


def _head_layer_norm(y, ones_bd, eps):
    mean = _dot_split_lhs(y, ones_bd) * (1.0 / HEAD_DIM)
    yc = y - mean
    var = _dot_split_lhs(yc * yc, ones_bd) * (1.0 / HEAD_DIM)
    return yc * lax.rsqrt(var + eps)


def _ada_kernel(c_ref, w_ref, b_ref, o_ref):
    c = c_ref[...]
    o_ref[...] = _dot3(_silu(c), w_ref[...]) + b_ref[...]


def _ada_mod(c, ada_w, ada_b):
    B = c.shape[0]
    Bp = max(8, B)
    cp = jnp.zeros((Bp, D_MODEL), F32).at[:B].set(c)
    L = ada_w.shape[0]
    tn = 1536
    out = pl.pallas_call(
        _ada_kernel,
        grid=(L, 6 * D_MODEL // tn),
        in_specs=[pl.BlockSpec((Bp, D_MODEL), lambda l, j: (0, 0)),
                  pl.BlockSpec((None, D_MODEL, tn), lambda l, j: (l, 0, j)),
                  pl.BlockSpec((None, 1, tn), lambda l, j: (l, 0, j))],
        out_specs=pl.BlockSpec((None, Bp, tn), lambda l, j: (l, 0, j)),
        out_shape=jax.ShapeDtypeStruct((L, Bp, 6 * D_MODEL), F32),
        compiler_params=_params("parallel", "parallel"),
        name="ada_mod",
    )(cp, ada_w, ada_b.reshape(L, 1, 6 * D_MODEL))
    return out[:, :B].reshape(L, B, 6, D_MODEL)


def _proj_ret_kernel(x_ref, mod_ref, g_ref, w_ref, cos_ref, sin_ref, o_ref):
    h = _norm_mod(x_ref[...], g_ref[...], mod_ref[0:1, :], mod_ref[1:2, :]).astype(BF16)
    p = _dot(h, w_ref[...])
    W = RET_WIDTH
    cos = cos_ref[:, 0:W]
    sin = sin_ref[:, 0:W]
    o_ref[:, 0:W] = (p[:, 0:W] * cos + p[:, 4 * W:5 * W] * sin).astype(BF16)
    o_ref[:, W:2 * W] = ((p[:, W:2 * W] * cos + p[:, 5 * W:6 * W] * sin) * (HEAD_DIM ** -0.5)).astype(BF16)
    o_ref[:, 2 * W:3 * W] = p[:, 2 * W:3 * W].astype(BF16)
    o_ref[:, 3 * W:4 * W] = _silu(p[:, 3 * W:4 * W]).astype(BF16)


def _proj_att_kernel(x_ref, mod_ref, g_ref, w_ref, cos_ref, sin_ref, qkg_ref, ones_ref, q_ref, k_ref, v_ref):
    h = _norm_mod(x_ref[...], g_ref[...], mod_ref[0:1, :], mod_ref[1:2, :]).astype(BF16)
    p = _dot(h, w_ref[...])
    QW, KW = ATT_Q_WIDTH, ATT_KV_WIDTH
    q = p[:, 0:QW]
    k = p[:, QW:QW + KW]
    v = p[:, QW + KW:QW + 2 * KW]
    qs = p[:, QW + 2 * KW:2 * QW + 2 * KW]
    ks = p[:, 2 * QW + 2 * KW:2 * QW + 3 * KW]
    ones = ones_ref[...]
    rq = lax.rsqrt(_dot_split_lhs(q * q, ones) * (1.0 / HEAD_DIM) + NORM_EPS)
    rk = lax.rsqrt(_dot_split_lhs(k * k, ones[0:KW, 0:KW]) * (1.0 / HEAD_DIM) + NORM_EPS)
    cos = cos_ref[...]
    sin = sin_ref[...]
    qo = (q * qkg_ref[0:1, :] * cos + qs * qkg_ref[1:2, :] * sin) * (rq * (HEAD_DIM ** -0.5))
    ko = (k * qkg_ref[2:3, 0:KW] * cos[:, 0:KW] + ks * qkg_ref[3:4, 0:KW] * sin[:, 0:KW]) * rk
    q_ref[...] = qo.astype(BF16)
    kb = ko.astype(BF16)
    vb = v.astype(BF16)
    for hk in range(ATT_KV_HEADS):
        k_ref[hk] = kb[:, hk * HEAD_DIM:(hk + 1) * HEAD_DIM]
        v_ref[hk] = vb[:, hk * HEAD_DIM:(hk + 1) * HEAD_DIM]


def _proj_gate_kernel(x_ref, mod_ref, g_ref, w_ref, o_ref):
    h = _norm_mod(x_ref[...], g_ref[...], mod_ref[0:1, :], mod_ref[1:2, :]).astype(BF16)
    o_ref[...] = _sigmoid(_dot(h, w_ref[...])).astype(BF16)


def _proj_rw_kernel(x_ref, xp_ref, xn_ref, mod_ref, g_ref, w_ref, mu_ref, vec_ref, w2_ref, a2_ref, g2_ref,
                    ones_ref, o_ref, p_scr, *, tm, tiles_per_seq):
    i = pl.program_id(0)
    gain, shift, scale = g_ref[...], mod_ref[0:1, :], mod_ref[1:2, :]
    w = w_ref[...]
    h = _norm_mod(x_ref[...], gain, shift, scale).astype(BF16)
    p = _dot(h, w)
    hp = _norm_mod(xp_ref[...], gain, shift, scale).astype(BF16)
    hn = _norm_mod(xn_ref[...], gain, shift, scale).astype(BF16)
    first = (i % tiles_per_seq) == 0
    last = (i % tiles_per_seq) == tiles_per_seq - 1
    prev_row = jnp.where(first, 0.0, _dot(hp, w)[7:8, :])
    next_row = jnp.where(last, 0.0, _dot(hn, w)[0:1, :])
    p_scr[8:8 + tm, :] = p
    p_scr[7:8, :] = prev_row
    p_scr[8 + tm:9 + tm, :] = next_row
    prev = p_scr[7:7 + tm, :]
    nxt = p_scr[9:9 + tm, :]
    x = p + mu_ref[0:1, :] * (prev - p) + mu_ref[1:2, :] * (nxt - p)

    W = RWKV_WIDTH
    r, k, v = x[:, 0:W], x[:, W:2 * W], x[:, 2 * W:3 * W]
    xw = x[:, 3 * W:3 * W + 2 * DECAY_LORA]
    xa = x[:, 3 * W + 2 * DECAY_LORA:3 * W + 2 * DECAY_LORA + 2 * AAA_LORA]
    xg = x[:, 3 * W + 2 * DECAY_LORA + 2 * AAA_LORA:]
    w0_f, w0_b = vec_ref[0:1, :], vec_ref[1:2, :]
    a0_f, a0_b = vec_ref[2:3, :], vec_ref[3:4, :]
    k_k, k_a, r_k = vec_ref[4:5, :], vec_ref[5:6, :], vec_ref[6:7, :]
    wl = _dot(jnp.tanh(xw).astype(BF16), w2_ref[...])
    al = _dot(xa.astype(BF16), a2_ref[...])
    dec_c = float(np.exp(-0.5))
    w_f = jnp.exp(-dec_c * _sigmoid(w0_f + wl[:, 0:W]))
    w_b = jnp.exp(-dec_c * _sigmoid(w0_b + wl[:, W:2 * W]))
    a_f = _sigmoid(a0_f + al[:, 0:W])
    a_b = _sigmoid(a0_b + al[:, W:2 * W])
    gate = _dot(_sigmoid(xg).astype(BF16), g2_ref[...])
    ones = ones_ref[...]
    kk = k * k_k
    kk = kk * lax.rsqrt(_dot_split_lhs(kk * kk, ones) + 1e-12)
    k_f = k * (1.0 + (a_f - 1.0) * k_a)
    k_b = k * (1.0 + (a_b - 1.0) * k_a)
    bonus = _dot_split_lhs(r * k_f * r_k, ones) * v
    for j, val in enumerate((r, kk, v, w_f, w_b, k_f, k_b, kk * a_f, kk * a_b, gate, bonus)):
        o_ref[:, j * W:(j + 1) * W] = val


def _token_specs(tm, tiles_per_seq):
    x_spec = pl.BlockSpec((tm, D_MODEL), lambda i: (i, 0))
    mod_spec = pl.BlockSpec((None, 6, D_MODEL), lambda i: (i // tiles_per_seq, 0, 0))
    g_spec = pl.BlockSpec((1, D_MODEL), lambda i: (0, 0))
    return x_spec, mod_spec, g_spec


def _full(shape):
    nd = len(shape)
    return pl.BlockSpec(shape, lambda *_: (0,) * nd)


def _input_projections(x, mod, T, lw, tabs):
    N = x.shape[0]
    tm = min(512, T)
    tps = T // tm
    grid = (N // tm,)
    x_spec, mod_spec, g_spec = _token_specs(tm, tps)
    tab_spec = pl.BlockSpec((tm, ATT_Q_WIDTH), lambda i: (i % tps, 0))
    cos, sin = tabs["cos"], tabs["sin"]

    ret = pl.pallas_call(
        _proj_ret_kernel, grid=grid,
        in_specs=[x_spec, mod_spec, g_spec, _full(lw["w_ret"].shape), tab_spec, tab_spec],
        out_specs=pl.BlockSpec((tm, RET_COLS), lambda i: (i, 0)),
        out_shape=jax.ShapeDtypeStruct((N, RET_COLS), BF16),
        compiler_params=_params("parallel"), name="proj_ret",
    )(x, mod, lw["norm1_g"], lw["w_ret"], cos, sin)

    q, k, v = pl.pallas_call(
        _proj_att_kernel, grid=grid,
        in_specs=[x_spec, mod_spec, g_spec, _full(lw["w_att"].shape), tab_spec, tab_spec,
                  _full(lw["qk_gain"].shape), _full(tabs["ones512"].shape)],
        out_specs=[pl.BlockSpec((tm, ATT_Q_WIDTH), lambda i: (i, 0)),
                   pl.BlockSpec((ATT_KV_HEADS, tm, HEAD_DIM), lambda i: (0, i, 0)),
                   pl.BlockSpec((ATT_KV_HEADS, tm, HEAD_DIM), lambda i: (0, i, 0))],
        out_shape=[jax.ShapeDtypeStruct((N, ATT_Q_WIDTH), BF16),
                   jax.ShapeDtypeStruct((ATT_KV_HEADS, N, HEAD_DIM), BF16),
                   jax.ShapeDtypeStruct((ATT_KV_HEADS, N, HEAD_DIM), BF16)],
        compiler_params=_params("parallel"), name="proj_att",
    )(x, mod, lw["norm1_g"], lw["w_att"], cos, sin, lw["qk_gain"], tabs["ones512"])

    gates = pl.pallas_call(
        _proj_gate_kernel, grid=grid,
        in_specs=[x_spec, mod_spec, g_spec, _full(lw["w_gate"].shape)],
        out_specs=pl.BlockSpec((tm, GATE_COLS), lambda i: (i, 0)),
        out_shape=jax.ShapeDtypeStruct((N, GATE_COLS), BF16),
        compiler_params=_params("parallel"), name="proj_gate",
    )(x, mod, lw["norm1_g"], lw["w_gate"])

    tm_rw = min(256, T)
    tps_rw = T // tm_rw
    x_spec, mod_spec, g_spec = _token_specs(tm_rw, tps_rw)
    rows8 = tm_rw // 8
    last8 = N // 8 - 1
    rwp = pl.pallas_call(
        functools.partial(_proj_rw_kernel, tm=tm_rw, tiles_per_seq=tps_rw), grid=(N // tm_rw,),
        in_specs=[x_spec,
                  pl.BlockSpec((8, D_MODEL), lambda i: (jnp.maximum(i * rows8 - 1, 0), 0)),
                  pl.BlockSpec((8, D_MODEL), lambda i: (jnp.minimum((i + 1) * rows8, last8), 0)),
                  mod_spec, g_spec, _full(lw["w_rw"].shape), _full(lw["shift_mu"].shape),
                  _full(lw["rw_vec"].shape), _full(lw["rw_w2"].shape), _full(lw["rw_a2"].shape),
                  _full(lw["rw_g2"].shape), _full(tabs["ones256"].shape)],
        out_specs=pl.BlockSpec((tm_rw, RW_FIELDS * RWKV_WIDTH), lambda i: (i, 0)),
        out_shape=jax.ShapeDtypeStruct((N, RW_FIELDS * RWKV_WIDTH), F32),
        scratch_shapes=[pltpu.VMEM((tm_rw + 16, RW_COLS), F32)],
        compiler_params=_params("parallel"), name="proj_rw",
    )(x, x, x, mod, lw["norm1_g"], lw["w_rw"], lw["shift_mu"], lw["rw_vec"], lw["rw_w2"], lw["rw_a2"],
      lw["rw_g2"], tabs["ones256"])
    return ret, q, k, v, gates, rwp


def _retention_tables(reverse):
    lg = np.log1p(-np.exp2(-5.0 - np.arange(RET_HEADS, dtype=np.float64)))
    if reverse:
        lg = lg[::-1]
    C = RET_CHUNK
    pos = np.arange(C, dtype=np.float64)
    diff = pos[:, None] - pos[None, :]
    if reverse:
        dec = np.where(diff < 0, np.exp(lg[:, None, None] * np.maximum(-diff, 0.0)[None]), 0.0)
        xi = np.exp(lg[:, None] * (C - pos)[None, :])
        zeta = np.exp(lg[:, None] * pos[None, :])
    else:
        dec = np.where(diff >= 0, np.exp(lg[:, None, None] * np.maximum(diff, 0.0)[None]), 0.0)
        xi = np.exp(lg[:, None] * (pos + 1.0)[None, :])
        zeta = np.exp(lg[:, None] * (C - 1.0 - pos)[None, :])
    chunk_decay = tuple(float(v) for v in np.exp(lg * C))
    widen = lambda a: jnp.asarray(np.repeat(a.T, HEAD_DIM, axis=1), F32)
    return jnp.asarray(dec, F32), widen(xi), widen(zeta), chunk_decay


def _retention_kernel(ret_ref, dec_ref, xi_ref, zeta_ref, *rest, chunk_decay, final):
    if final:
        yf_ref, gn_ref, ones_ref, o_ref, state, y_scr = rest
    else:
        o_ref, state, y_scr = rest

    @pl.when(pl.program_id(1) == 0)
    def _():
        state[...] = jnp.zeros_like(state)

    W = RET_WIDTH
    q = ret_ref[:, 0:W]
    k = ret_ref[:, W:2 * W]
    v = ret_ref[:, 2 * W:3 * W]
    qx = (q.astype(F32) * xi_ref[...]).astype(BF16)
    kzt = (k.astype(F32) * zeta_ref[...]).T
    for h in range(RET_HEADS):
        sl = slice(h * HEAD_DIM, (h + 1) * HEAD_DIM)
        s = _dot_nt(q[:, sl], k[:, sl]) * dec_ref[h]
        inner = _dot(s.astype(BF16), v[:, sl])
        r_prev = state[h]
        cross = _dot(qx[:, sl], r_prev.astype(BF16))
        kv = _dot(kzt[sl, :].astype(BF16), v[:, sl])
        state[h] = r_prev * chunk_decay[h] + kv
        y_scr[:, sl] = inner + cross
    y = y_scr[...]
    if final:
        y = _head_layer_norm(y + yf_ref[...], ones_ref[...], RET_EPS) * gn_ref[...]
        o_ref[...] = (y * ret_ref[:, 3 * W:4 * W].astype(F32)).astype(BF16)
    else:
        o_ref[...] = y


def _retention(ret, B, T, ret_gn, ones256):
    N = ret.shape[0]
    C = RET_CHUNK
    nc = T // C
    scratch = [pltpu.VMEM((RET_HEADS, HEAD_DIM, HEAD_DIM), F32), pltpu.VMEM((C, RET_WIDTH), F32)]

    dec, xi, zeta, cd = _retention_tables(False)
    fwd_row = lambda b, c: (b * nc + c, 0)
    yf = pl.pallas_call(
        functools.partial(_retention_kernel, chunk_decay=cd, final=False), grid=(B, nc),
        in_specs=[pl.BlockSpec((C, RET_COLS), fwd_row), _full(dec.shape), _full(xi.shape), _full(zeta.shape)],
        out_specs=pl.BlockSpec((C, RET_WIDTH), fwd_row),
        out_shape=jax.ShapeDtypeStruct((N, RET_WIDTH), F32),
        scratch_shapes=scratch, compiler_params=_params("parallel", "arbitrary"), name="retention_fwd",
    )(ret, dec, xi, zeta)

    dec, xi, zeta, cd = _retention_tables(True)
    bwd_row = lambda b, c: (b * nc + nc - 1 - c, 0)
    return pl.pallas_call(
        functools.partial(_retention_kernel, chunk_decay=cd, final=True), grid=(B, nc),
        in_specs=[pl.BlockSpec((C, RET_COLS), bwd_row), _full(dec.shape), _full(xi.shape), _full(zeta.shape),
                  pl.BlockSpec((C, RET_WIDTH), bwd_row), _full(ret_gn.shape), _full(ones256.shape)],
        out_specs=pl.BlockSpec((C, RET_WIDTH), bwd_row),
        out_shape=jax.ShapeDtypeStruct((N, RET_WIDTH), BF16),
        scratch_shapes=scratch, compiler_params=_params("parallel", "arbitrary"), name="retention_bwd",
    )(ret, dec, xi, zeta, yf, ret_gn, ones256)


def _flash_kernel(q_ref, k_ref, v_ref, o_ref, q_scr, m_scr, l_scr, acc_scr, *, tq, tk):
    j = pl.program_id(3)

    @pl.when(j == 0)
    def _():
        for g in range(ATT_GROUP):
            q_scr[g * tq:(g + 1) * tq, :] = q_ref[:, g * HEAD_DIM:(g + 1) * HEAD_DIM]
        m_scr[...] = jnp.full_like(m_scr, -jnp.inf)
        l_scr[...] = jnp.zeros_like(l_scr)
        acc_scr[...] = jnp.zeros_like(acc_scr)

    k = k_ref[...]
    v = v_ref[...]
    for r0 in range(0, ATT_GROUP * tq, FLASH_SUB_ROWS):
        rows = slice(r0, r0 + FLASH_SUB_ROWS)
        s = _dot_nt(q_scr[rows, :], k)
        m_prev = m_scr[rows, :]
        m_next = jnp.maximum(m_prev, jnp.max(s, axis=1, keepdims=True))
        p = jnp.exp(s - pltpu.repeat(m_next, tk // V7X_LANES, axis=1))
        alpha = jnp.exp(m_prev - m_next)
        l_scr[rows, :] = alpha * l_scr[rows, :] + jnp.sum(p, axis=1, keepdims=True)
        m_scr[rows, :] = m_next
        acc_scr[rows, :] = alpha[:, 0:HEAD_DIM] * acc_scr[rows, :] + _dot(p.astype(BF16), v)

    @pl.when(j == pl.num_programs(3) - 1)
    def _():
        o = acc_scr[...] / l_scr[:, 0:HEAD_DIM]
        for g in range(ATT_GROUP):
            o_ref[:, g * HEAD_DIM:(g + 1) * HEAD_DIM] = o[g * tq:(g + 1) * tq, :].astype(BF16)


def _attention(q, k, v, B, T):
    N = q.shape[0]
    tq = min(FLASH_Q_ROWS, T)
    tk = min(FLASH_KV_ROWS, T)
    nq, nk = T // tq, T // tk
    GW = ATT_GROUP * HEAD_DIM
    rows = ATT_GROUP * tq
    return pl.pallas_call(
        functools.partial(_flash_kernel, tq=tq, tk=tk), grid=(B, ATT_KV_HEADS, nq, nk),
        in_specs=[pl.BlockSpec((tq, GW), lambda b, h, i, j: (b * nq + i, h)),
                  pl.BlockSpec((None, tk, HEAD_DIM), lambda b, h, i, j: (h, b * nk + j, 0)),
                  pl.BlockSpec((None, tk, HEAD_DIM), lambda b, h, i, j: (h, b * nk + j, 0))],
        out_specs=pl.BlockSpec((tq, GW), lambda b, h, i, j: (b * nq + i, h)),
        out_shape=jax.ShapeDtypeStruct((N, ATT_Q_WIDTH), BF16),
        scratch_shapes=[pltpu.VMEM((rows, HEAD_DIM), BF16), pltpu.VMEM((rows, V7X_LANES), F32),
                        pltpu.VMEM((rows, V7X_LANES), F32), pltpu.VMEM((rows, HEAD_DIM), F32)],
        compiler_params=_params("parallel", "parallel", "parallel", "arbitrary"), name="flash_attention",
    )(q, k, v)


def _rwkv_scan_kernel(kk_f, r_f, w_f, k_f, b_f, vt_f, kk_b, r_b, w_b, k_b, b_b, vt_b, ones_ref,
                      yf_ref, yb_ref, state, vk_scr, *, nb):
    C = RW_CHUNK
    H = C // 2
    W = RWKV_WIDTH
    SUB = 8

    @pl.when(pl.program_id(1) == 0)
    def _():
        state[...] = jnp.zeros_like(state)

    ones = ones_ref[...]
    lane = lax.broadcasted_iota(jnp.int32, (1, W), 1) & (HEAD_DIM - 1)
    same_head = (lax.broadcasted_iota(jnp.int32, (W, W), 0) // HEAD_DIM
                 == lax.broadcasted_iota(jnp.int32, (W, W), 1) // HEAD_DIM)
    step_iota = lax.broadcasted_iota(jnp.int32, (H, HEAD_DIM, W), 0)
    lane_iota = lax.broadcasted_iota(jnp.int32, (H, HEAD_DIM, W), 2) & (HEAD_DIM - 1)
    dirs = ((kk_f, r_f, w_f, k_f, b_f, vt_f), (kk_b, r_b, w_b, k_b, b_b, vt_b))
    chains = [(n, d) for n in range(nb) for d in range(2)]

    def head_sums(vals):
        stacked = jnp.concatenate([v.astype(BF16) for v in vals], axis=0)
        out = _dot(stacked, ones)
        return [out[c * HEAD_DIM:(c + 1) * HEAD_DIM, :] for c in range(len(vals))]

    carry = (tuple(state[c] for c in range(len(chains))),
             tuple(jnp.zeros((HEAD_DIM, W), F32) for _ in chains))
    for half in range(2):
        lo = (half * H, (1 - half) * H)
        for c, (n, d) in enumerate(chains):
            kc = dirs[d][3][n]
            kbd = jnp.where(same_head, jnp.concatenate([kc] * (W // C), axis=0), 0.0).astype(BF16)
            sel = jnp.where(step_iota + lo[d] == lane_iota, dirs[d][5][n][None], 0.0).reshape(H * HEAD_DIM, W)
            vk_scr[c] = _dot(sel.astype(BF16), kbd).reshape(H, HEAD_DIM, W)

        def group(g, carry, lo=lo):
            states, ycols = list(carry[0]), list(carry[1])
            bases = (lo[0] + g * SUB, lo[1] + (H // SUB - 1 - g) * SUB)
            rows = [[ref[n, pl.ds(pl.multiple_of(bases[d], SUB), SUB), :] for ref in dirs[d][:5]]
                    for (n, d) in chains]
            for j in range(SUB):
                jj = (j, SUB - 1 - j)
                row = lambda c, f: rows[c][f][jj[chains[c][1]]:jj[chains[c][1]] + 1, :]
                sa = head_sums([states[c] * row(c, 0) for c in range(len(chains))])
                new = [states[c] * row(c, 2) - sa[c] * row(c, 4)
                       + vk_scr[c, bases[chains[c][1]] + jj[chains[c][1]] - lo[chains[c][1]]]
                       for c in range(len(chains))]
                read = [new[c] if chains[c][1] == 0 else states[c] for c in range(len(chains))]
                y = head_sums([read[c] * row(c, 1) for c in range(len(chains))])
                for c, (n, d) in enumerate(chains):
                    ycols[c] = jnp.where(lane == bases[d] + jj[d], y[c], ycols[c])
                states = new
            return tuple(states), tuple(ycols)

        carry = lax.fori_loop(0, H // SUB, group, carry)

    states, ycols = carry
    for c, (n, d) in enumerate(chains):
        state[c] = states[c]
        (yf_ref if d == 0 else yb_ref)[n] = ycols[c]


def _rwkv_scan(rwp, B, T, ones256):
    W, C = RWKV_WIDTH, RW_CHUNK
    nc = T // C
    nb = RW_BATCH_ROWS if B % RW_BATCH_ROWS == 0 else 1
    fields = rwp.reshape(B, T, RW_FIELDS * W)
    v = rwp[:, 2 * W:3 * W].reshape(B, nc, C, RWKV_HEADS, HEAD_DIM)
    vt = v.transpose(0, 1, 4, 3, 2).reshape(B, nc, HEAD_DIM, W)

    fwd = lambda f: pl.BlockSpec((nb, C, W), lambda b, c: (b, c, f))
    bwd = lambda f: pl.BlockSpec((nb, C, W), lambda b, c: (b, nc - 1 - c, f))
    vt_fwd = pl.BlockSpec((nb, None, HEAD_DIM, W), lambda b, c: (b, c, 0, 0))
    vt_bwd = pl.BlockSpec((nb, None, HEAD_DIM, W), lambda b, c: (b, nc - 1 - c, 0, 0))
    yt_shape = jax.ShapeDtypeStruct((B, nc, HEAD_DIM, W), F32)
    yt_f, yt_b = pl.pallas_call(
        functools.partial(_rwkv_scan_kernel, nb=nb), grid=(B // nb, nc),
        in_specs=[fwd(1), fwd(0), fwd(3), fwd(5), fwd(7), vt_fwd,
                  bwd(1), bwd(0), bwd(4), bwd(6), bwd(8), vt_bwd, _full(ones256.shape)],
        out_specs=[vt_fwd, vt_bwd], out_shape=[yt_shape, yt_shape],
        scratch_shapes=[pltpu.VMEM((nb * 2, HEAD_DIM, W), F32),
                        pltpu.VMEM((nb * 2, C // 2, HEAD_DIM, W), F32)],
        compiler_params=_params("parallel", "arbitrary"), name="rwkv_scan",
    )(fields, fields, fields, fields, fields, vt, fields, fields, fields, fields, fields, vt, ones256)

    untranspose = lambda yt: yt.reshape(B, nc, HEAD_DIM, RWKV_HEADS, C).transpose(0, 1, 4, 3, 2).reshape(B * T, W)
    return untranspose(yt_f), untranspose(yt_b)


def _merge_kernel(x_ref, mod_ref, ya_ref, yb_ref, yf_ref, ybk_ref, rg_ref, bonus_ref, gn_ref, gates_ref,
                  ones_ref, wa_ref, wb_ref, wc_ref, wo_ref, o_ref):
    y = _head_layer_norm(yf_ref[...] + ybk_ref[...], ones_ref[...], RWKV_GN_EPS) * gn_ref[...]
    yc = ((y + bonus_ref[...]) * rg_ref[...]).astype(BF16)
    D = D_MODEL
    merged = (gates_ref[:, 0:D].astype(F32) * _dot(ya_ref[...], wa_ref[...])
              + gates_ref[:, D:2 * D].astype(F32) * _dot(yb_ref[...], wb_ref[...])
              + gates_ref[:, 2 * D:3 * D].astype(F32) * _dot(yc, wc_ref[...]))
    o_ref[...] = x_ref[...] + mod_ref[2:3, :] * _dot(merged.astype(BF16), wo_ref[...])


def _merge(x, mod, T, ya, yb, yf, ybk, rwp, gates, lw, ones256):
    N = x.shape[0]
    tm = min(512, T)
    tps = T // tm
    x_spec, mod_spec, _ = _token_specs(tm, tps)
    row = lambda w: pl.BlockSpec((tm, w), lambda i: (i, 0))
    field = lambda f: pl.BlockSpec((tm, RWKV_WIDTH), lambda i: (i, f))
    return pl.pallas_call(
        _merge_kernel, grid=(N // tm,),
        in_specs=[x_spec, mod_spec, row(RET_WIDTH), row(ATT_Q_WIDTH), row(RWKV_WIDTH), row(RWKV_WIDTH),
                  field(9), field(10), _full(lw["rw_gn"].shape), row(GATE_COLS), _full(ones256.shape),
                  _full(lw["w_branch_a"].shape), _full(lw["w_branch_b"].shape), _full(lw["w_branch_c"].shape),
                  _full(lw["w_out"].shape)],
        out_specs=x_spec, out_shape=jax.ShapeDtypeStruct((N, D_MODEL), F32),
        compiler_params=_params("parallel"), name="merge_out",
    )(x, mod, ya, yb, yf, ybk, rwp, rwp, lw["rw_gn"], gates, ones256,
      lw["w_branch_a"], lw["w_branch_b"], lw["w_branch_c"], lw["w_out"])


def _route(h, hb, rwt_ref, rbias_ref, tm):
    h_lo = (h - hb.astype(F32)).astype(BF16)
    rw_hi, rw_lo = _split(rwt_ref[...])
    logits = _dot_nt(rw_hi, hb) + _dot_nt(rw_hi, h_lo) + _dot_nt(rw_lo, hb)
    scores = _sigmoid(logits)
    choice = scores + jnp.concatenate([rbias_ref[...]] * (tm // V7X_LANES), axis=1)
    per_group = N_EXPERTS // N_GROUPS
    sub = lax.broadcasted_iota(jnp.int32, (per_group, tm), 0)
    groups, gscore = [], []
    for g in range(N_GROUPS):
        cg = choice[g * per_group:(g + 1) * per_group, :]
        m1 = jnp.max(cg, axis=0, keepdims=True)
        first = jnp.min(jnp.where(cg == m1, sub, per_group), axis=0, keepdims=True)
        m2 = jnp.max(jnp.where(sub == first, -jnp.inf, cg), axis=0, keepdims=True)
        groups.append(cg)
        gscore.append(m1 + m2)
    masked = []
    for g in range(N_GROUPS):
        beaten = jnp.zeros((1, tm), F32)
        for o in range(N_GROUPS):
            if o == g:
                continue
            wins = (gscore[o] >= gscore[g]) if o < g else (gscore[o] > gscore[g])
            beaten = beaten + jnp.where(wins, 1.0, 0.0)
        keep = jnp.where(beaten < TOPK_GROUPS, 1.0, 0.0)
        masked.append(jnp.where(jnp.broadcast_to(keep, (per_group, tm)) > 0.5, groups[g], -jnp.inf))
    mc = jnp.concatenate(masked, axis=0)
    eidx = lax.broadcasted_iota(jnp.int32, (N_EXPERTS, tm), 0)
    beaten = jnp.zeros((N_EXPERTS, tm), F32)
    for o in range(N_EXPERTS):
        row = jnp.broadcast_to(mc[o:o + 1, :], (N_EXPERTS, tm))
        tie = jnp.where(eidx > o, 1.0, 0.0)
        beaten = beaten + jnp.where(row > mc, 1.0, jnp.where(row == mc, tie, 0.0))
    wts = jnp.where(beaten < TOP_K, scores, 0.0)
    return wts / jnp.sum(wts, axis=0, keepdims=True) * ROUTED_SCALE


def _moe_kernel(x_ref, mod_ref, g_ref, rwt_ref, rbias_ref, shgu_ref, shd_ref, wgu_ref, wd_ref, fin_ref,
                o_ref, h_scr, comb_scr, acc_scr, act_scr, *, tm, final):
    step = pl.program_id(1)
    F = EXPERT_FF
    G = MOE_EXPERTS_PER_STEP
    SUB = min(MOE_SUB_ROWS, tm)

    @pl.when(step == 0)
    def _():
        h = _norm_mod(x_ref[...], g_ref[...], mod_ref[3:4, :], mod_ref[4:5, :])
        hb = h.astype(BF16)
        h_scr[...] = hb
        comb_t = _route(h, hb, rwt_ref, rbias_ref, tm)
        comb_scr[...] = jnp.concatenate([comb_t, jnp.zeros_like(comb_t)], axis=0).T
        gu = _dot(hb, shgu_ref[...])
        acc_scr[...] = _dot((_silu(gu[:, 0:F]) * gu[:, F:2 * F]).astype(BF16), shd_ref[...])

    expert_row = lax.broadcasted_iota(jnp.int32, (2 * N_EXPERTS, F), 0)
    wd = wd_ref[...].reshape(G * F, D_MODEL)

    def row_block(rb, _):
        rows = pl.ds(pl.multiple_of(rb * SUB, SUB), SUB)
        hb = h_scr[rows, :]
        c_hi, c_lo = _split(comb_scr[rows, :])
        for g in range(G):
            gu = _dot(hb, wgu_ref[g])
            pick = jnp.where(expert_row == step * G + g, 1.0, 0.0).astype(BF16)
            cw = _dot(c_hi, pick) + _dot(c_lo, pick)
            act_scr[:, g * F:(g + 1) * F] = (_silu(gu[:, 0:F]) * gu[:, F:2 * F] * cw).astype(BF16)
        acc_scr[rows, :] += _dot(act_scr[...], wd)
        return 0

    lax.fori_loop(0, tm // SUB, row_block, 0)

    @pl.when(step == N_EXPERTS // G - 1)
    def _():
        out = x_ref[...] + mod_ref[5:6, :] * acc_scr[...]
        if final:
            ms = jnp.mean(out * out, axis=-1, keepdims=True)
            out = out * lax.rsqrt(ms + NORM_EPS) * fin_ref[...]
        o_ref[...] = out


def _moe(x, mod, T, lw, final_g, final):
    N = x.shape[0]
    tm = min(MOE_TOKENS, T)
    tps = T // tm
    D, F, E, G = D_MODEL, EXPERT_FF, N_EXPERTS, MOE_EXPERTS_PER_STEP
    sub = min(MOE_SUB_ROWS, tm)
    const2 = lambda shape: pl.BlockSpec(shape, lambda i, e: (0, 0))
    return pl.pallas_call(
        functools.partial(_moe_kernel, tm=tm, final=final), grid=(N // tm, E // G),
        in_specs=[pl.BlockSpec((tm, D), lambda i, e: (i, 0)),
                  pl.BlockSpec((None, 6, D), lambda i, e: (i // tps, 0, 0)),
                  const2((1, D)), const2((E, D)), const2((E, V7X_LANES)), const2((D, 2 * F)), const2((F, D)),
                  pl.BlockSpec((G, D, 2 * F), lambda i, e: (e, 0, 0)),
                  pl.BlockSpec((G, F, D), lambda i, e: (e, 0, 0)),
                  const2((1, D))],
        out_specs=pl.BlockSpec((tm, D), lambda i, e: (i, 0)),
        out_shape=jax.ShapeDtypeStruct((N, D), F32),
        scratch_shapes=[pltpu.VMEM((tm, D), BF16), pltpu.VMEM((tm, 2 * E), F32), pltpu.VMEM((tm, D), F32),
                        pltpu.VMEM((sub, G * F), BF16)],
        compiler_params=_params("parallel", "arbitrary"), name="moe",
    )(x, mod, lw["norm2_g"], lw["router_wt"], lw["router_bias"], lw["sh_gu"], lw["sh_d"],
      lw["exp_gu"], lw["exp_d"], final_g)


def _swap_halves(w, heads):
    shp = w.shape
    w = w.reshape(shp[:-1] + (heads, 2, HEAD_DIM // 2))
    return jnp.flip(w, axis=-2).reshape(shp)


def _block_diag2(a, b):
    za = jnp.zeros((a.shape[0], b.shape[1]), a.dtype)
    zb = jnp.zeros((b.shape[0], a.shape[1]), a.dtype)
    return jnp.concatenate([jnp.concatenate([a, za], 1), jnp.concatenate([zb, b], 1)], 0)


def _layer_weights(P, l):
    w_in = P["w_in"][l]
    o1, o2, o3 = RET_COLS, RET_COLS + ATT_COLS, RET_COLS + ATT_COLS + RW_COLS
    w_ret, w_att, w_rw, w_gate = w_in[:, :o1], w_in[:, o1:o2], w_in[:, o2:o3], w_in[:, o3:]
    W = RET_WIDTH
    w_ret = jnp.concatenate([w_ret, _swap_halves(w_ret[:, 0:W], RET_HEADS),
                             _swap_halves(w_ret[:, W:2 * W], RET_HEADS)], axis=1)
    QW, KW = ATT_Q_WIDTH, ATT_KV_WIDTH
    w_att = jnp.concatenate([w_att, _swap_halves(w_att[:, 0:QW], ATT_Q_HEADS),
                             _swap_halves(w_att[:, QW:QW + KW], ATT_KV_HEADS)], axis=1)
    qg, kg = P["q_norm_g"][l], P["k_norm_g"][l]
    pad = lambda v: jnp.zeros((QW,), F32).at[:v.shape[0]].set(v)
    qk_gain = jnp.zeros((8, QW), F32)
    qk_gain = qk_gain.at[0].set(jnp.tile(qg, ATT_Q_HEADS))
    qk_gain = qk_gain.at[1].set(jnp.tile(_swap_halves(qg, 1), ATT_Q_HEADS))
    qk_gain = qk_gain.at[2].set(pad(jnp.tile(kg, ATT_KV_HEADS)))
    qk_gain = qk_gain.at[3].set(pad(jnp.tile(_swap_halves(kg, 1), ATT_KV_HEADS)))
    rw_vec = jnp.zeros((8, RWKV_WIDTH), F32)
    for j, name in enumerate(("rw_w0_f", "rw_w0_b", "rw_a0_f", "rw_a0_b", "rw_k_k", "rw_k_a")):
        rw_vec = rw_vec.at[j].set(P[name][l])
    rw_vec = rw_vec.at[6].set(P["rw_r_k"][l].reshape(RWKV_WIDTH))
    row = lambda v: v.reshape(1, -1)
    return {
        "norm1_g": row(P["norm1_g"][l]), "norm2_g": row(P["norm2_g"][l]),
        "w_ret": w_ret.astype(BF16), "w_att": w_att.astype(BF16), "w_rw": w_rw.astype(BF16),
        "w_gate": w_gate.astype(BF16), "qk_gain": qk_gain, "ret_gn": row(P["ret_gn"][l]),
        "shift_mu": P["shift_mu"][l], "rw_vec": rw_vec,
        "rw_w2": _block_diag2(P["rw_w2_f"][l], P["rw_w2_b"][l]).astype(BF16),
        "rw_a2": _block_diag2(P["rw_a2_f"][l], P["rw_a2_b"][l]).astype(BF16),
        "rw_g2": P["rw_g2"][l].astype(BF16), "rw_gn": row(P["rw_gn"][l]),
        "w_branch_a": P["w_branch_a"][l].astype(BF16), "w_branch_b": P["w_branch_b"][l].astype(BF16),
        "w_branch_c": P["w_branch_c"][l].astype(BF16), "w_out": P["w_out"][l].astype(BF16),
        "router_wt": P["router_w"][l].T,
        "router_bias": jnp.broadcast_to(P["router_bias"][l][:, None], (N_EXPERTS, V7X_LANES)),
        "sh_gu": jnp.concatenate([P["sh_w_gate"][l], P["sh_w_up"][l]], axis=1).astype(BF16),
        "sh_d": P["sh_w_down"][l].astype(BF16),
        "exp_gu": jnp.concatenate([P["exp_w_gate"][l], P["exp_w_up"][l]], axis=2).astype(BF16),
        "exp_d": P["exp_w_down"][l].astype(BF16),
    }


def _ones_block_diag(n):
    idx = np.arange(n) // HEAD_DIM
    return jnp.asarray(idx[:, None] == idx[None, :], BF16)


def _tables(T):
    rows = T // GRID_W
    row = jnp.repeat(jnp.arange(rows, dtype=F32), GRID_W)
    col = jnp.tile(jnp.arange(GRID_W, dtype=F32), rows)
    freqs = ROPE_THETA ** (-jnp.arange(ROPE_PAIRS_PER_AXIS, dtype=F32) / ROPE_PAIRS_PER_AXIS)
    ang = jnp.concatenate([row[:, None] * freqs, col[:, None] * freqs], axis=-1)
    cos, sin = jnp.cos(ang), jnp.sin(ang)
    return {
        "cos": jnp.tile(jnp.concatenate([cos, cos], axis=-1), (1, ATT_Q_HEADS)),
        "sin": jnp.tile(jnp.concatenate([-sin, sin], axis=-1), (1, ATT_Q_HEADS)),
        "ones128": _ones_block_diag(128), "ones256": _ones_block_diag(256), "ones512": _ones_block_diag(512),
    }


def _trunk(x, c, P, layer_weights):
    B, T, D = x.shape
    tabs = _tables(T)
    mods = _ada_mod(c, P["ada_w"], P["ada_b"])
    x = x.reshape(B * T, D)
    final_g = P["final_g"].reshape(1, D)
    depth = len(layer_weights)
    for l, lw in enumerate(layer_weights):
        mod = mods[l]
        ret, q, k, v, gates, rwp = _input_projections(x, mod, T, lw, tabs)
        ya = _retention(ret, B, T, lw["ret_gn"], tabs["ones256"])
        yb = _attention(q, k, v, B, T)
        yf, ybk = _rwkv_scan(rwp, B, T, tabs["ones256"])
        x = _merge(x, mod, T, ya, yb, yf, ybk, rwp, gates, lw, tabs["ones256"])
        x = _moe(x, mod, T, lw, final_g, final=(l == depth - 1))
    return x.reshape(B, T, D)


def kernel(x_prompt, x_sample, c_prompt, c_sample, norm1_g, norm2_g, final_g, ada_w, ada_b, w_in, ret_gn, q_norm_g, k_norm_g, shift_mu, rw_w0_f, rw_w2_f, rw_w0_b, rw_w2_b, rw_a0_f, rw_a2_f, rw_a0_b, rw_a2_b, rw_g2, rw_k_k, rw_k_a, rw_r_k, rw_gn, w_branch_a, w_branch_b, w_branch_c, w_out, router_w, router_bias, exp_w_gate, exp_w_up, exp_w_down, sh_w_gate, sh_w_up, sh_w_down):
    P = {
        "norm1_g": norm1_g, "norm2_g": norm2_g, "final_g": final_g, "ada_w": ada_w, "ada_b": ada_b,
        "w_in": w_in, "ret_gn": ret_gn, "q_norm_g": q_norm_g, "k_norm_g": k_norm_g, "shift_mu": shift_mu,
        "rw_w0_f": rw_w0_f, "rw_w2_f": rw_w2_f, "rw_w0_b": rw_w0_b, "rw_w2_b": rw_w2_b,
        "rw_a0_f": rw_a0_f, "rw_a2_f": rw_a2_f, "rw_a0_b": rw_a0_b, "rw_a2_b": rw_a2_b,
        "rw_g2": rw_g2, "rw_k_k": rw_k_k, "rw_k_a": rw_k_a, "rw_r_k": rw_r_k, "rw_gn": rw_gn,
        "w_branch_a": w_branch_a, "w_branch_b": w_branch_b, "w_branch_c": w_branch_c, "w_out": w_out,
        "router_w": router_w, "router_bias": router_bias,
        "exp_w_gate": exp_w_gate, "exp_w_up": exp_w_up, "exp_w_down": exp_w_down,
        "sh_w_gate": sh_w_gate, "sh_w_up": sh_w_up, "sh_w_down": sh_w_down,
    }
    layer_weights = [_layer_weights(P, l) for l in range(w_in.shape[0])]
    y_prompt = _trunk(x_prompt, c_prompt, P, layer_weights)
    y_sample = _trunk(x_sample, c_sample, P, layer_weights)
    return (y_prompt, y_sample)
```

```python
import functools

import numpy as np
import jax
import jax.numpy as jnp
from jax import lax
from jax.experimental import pallas as pl
from jax.experimental.pallas import tpu as pltpu
from jax.experimental.pallas import tpu_sc as plsc

F32 = jnp.float32
BF16 = jnp.bfloat16

D_MODEL = 1024
DEPTH = 2
GRID_W = 64
HEAD_DIM = 64
ROPE_THETA = 10000.0
ROPE_PAIRS_PER_AXIS = HEAD_DIM // 4

RET_HEADS = 4
RET_WIDTH = RET_HEADS * HEAD_DIM
RET_CHUNK = 128
RET_EPS = 1e-5

ATT_Q_HEADS = 8
ATT_KV_HEADS = 2
ATT_GROUP = ATT_Q_HEADS // ATT_KV_HEADS
ATT_Q_WIDTH = ATT_Q_HEADS * HEAD_DIM
ATT_KV_WIDTH = ATT_KV_HEADS * HEAD_DIM

RWKV_HEADS = 4
RWKV_WIDTH = RWKV_HEADS * HEAD_DIM
DECAY_LORA = 64
AAA_LORA = 64
GATE_LORA = 128
RWKV_GN_EPS = 64e-5
RW_CHUNK = 64
RW_FIELDS = 11
RW_BATCH_ROWS = 4

FLASH_Q_ROWS = 512
FLASH_KV_ROWS = 512
FLASH_SUB_ROWS = 1024

MOE_TOKENS = 512
MOE_GROUP_ROWS = 256
SC_GATHER_ROWS = 128

N_EXPERTS = 64
TOP_K = 8
N_GROUPS = 8
TOPK_GROUPS = 4
EXPERT_FF = 256
ROUTED_SCALE = 2.5
NORM_EPS = 1e-6

RET_COLS = 4 * RET_WIDTH
ATT_COLS = ATT_Q_WIDTH + 2 * ATT_KV_WIDTH
RW_COLS = 3 * RWKV_WIDTH + 2 * DECAY_LORA + 2 * AAA_LORA + GATE_LORA
GATE_COLS = 3 * D_MODEL

V7X_LANES = 128
VMEM_LIMIT_BYTES = 48 * 1024 * 1024


def _params(*dims):
    return pltpu.CompilerParams(dimension_semantics=dims, vmem_limit_bytes=VMEM_LIMIT_BYTES)


def _sigmoid(x):
    return 1.0 / (1.0 + jnp.exp(-x))


def _silu(x):
    return x * _sigmoid(x)


def _dot(a, b):
    return jnp.dot(a, b, preferred_element_type=F32)


def _dot_nt(a, b):
    return lax.dot_general(a, b, (((1,), (1,)), ((), ())), preferred_element_type=F32)


def _split(a):
    hi = a.astype(BF16)
    lo = (a - hi.astype(F32)).astype(BF16)
    return hi, lo


def _dot_split_lhs(a, b_bf16):
    hi, lo = _split(a)
    return _dot(hi, b_bf16) + _dot(lo, b_bf16)


def _dot3(a, b):
    ah, al = _split(a)
    bh, bl = _split(b)
    return _dot(ah, bh) + _dot(ah, bl) + _dot(al, bh)


def _norm_mod(x, gain, shift, scale):
    ms = jnp.mean(x * x, axis=-1, keepdims=True)
    return x * lax.rsqrt(ms + NORM_EPS) * gain * (1.0 + scale) + shift


def _head_layer_norm(y, ones_bd, eps):
    mean = _dot_split_lhs(y, ones_bd) * (1.0 / HEAD_DIM)
    yc = y - mean
    var = _dot_split_lhs(yc * yc, ones_bd) * (1.0 / HEAD_DIM)
    return yc * lax.rsqrt(var + eps)


def _ada_kernel(c_ref, w_ref, b_ref, o_ref):
    c = c_ref[...]
    o_ref[...] = _dot3(_silu(c), w_ref[...]) + b_ref[...]


def _ada_mod(c, ada_w, ada_b):
    B = c.shape[0]
    Bp = max(8, B)
    cp = jnp.zeros((Bp, D_MODEL), F32).at[:B].set(c)
    L = ada_w.shape[0]
    tn = 1536
    out = pl.pallas_call(
        _ada_kernel,
        grid=(L, 6 * D_MODEL // tn),
        in_specs=[pl.BlockSpec((Bp, D_MODEL), lambda l, j: (0, 0)),
                  pl.BlockSpec((None, D_MODEL, tn), lambda l, j: (l, 0, j)),
                  pl.BlockSpec((None, 1, tn), lambda l, j: (l, 0, j))],
        out_specs=pl.BlockSpec((None, Bp, tn), lambda l, j: (l, 0, j)),
        out_shape=jax.ShapeDtypeStruct((L, Bp, 6 * D_MODEL), F32),
        compiler_params=_params("parallel", "parallel"),
        name="ada_mod",
    )(cp, ada_w, ada_b.reshape(L, 1, 6 * D_MODEL))
    return out[:, :B].reshape(L, B, 6, D_MODEL)


def _proj_ret_kernel(x_ref, mod_ref, g_ref, w_ref, cos_ref, sin_ref, o_ref):
    h = _norm_mod(x_ref[...], g_ref[...], mod_ref[0:1, :], mod_ref[1:2, :]).astype(BF16)
    p = _dot(h, w_ref[...])
    W = RET_WIDTH
    cos = cos_ref[:, 0:W]
    sin = sin_ref[:, 0:W]
    o_ref[:, 0:W] = (p[:, 0:W] * cos + p[:, 4 * W:5 * W] * sin).astype(BF16)
    o_ref[:, W:2 * W] = ((p[:, W:2 * W] * cos + p[:, 5 * W:6 * W] * sin) * (HEAD_DIM ** -0.5)).astype(BF16)
    o_ref[:, 2 * W:3 * W] = p[:, 2 * W:3 * W].astype(BF16)
    o_ref[:, 3 * W:4 * W] = _silu(p[:, 3 * W:4 * W]).astype(BF16)


def _proj_att_kernel(x_ref, mod_ref, g_ref, w_ref, cos_ref, sin_ref, qkg_ref, ones_ref, q_ref, k_ref, v_ref):
    h = _norm_mod(x_ref[...], g_ref[...], mod_ref[0:1, :], mod_ref[1:2, :]).astype(BF16)
    p = _dot(h, w_ref[...])
    QW, KW = ATT_Q_WIDTH, ATT_KV_WIDTH
    q = p[:, 0:QW]
    k = p[:, QW:QW + KW]
    v = p[:, QW + KW:QW + 2 * KW]
    qs = p[:, QW + 2 * KW:2 * QW + 2 * KW]
    ks = p[:, 2 * QW + 2 * KW:2 * QW + 3 * KW]
    ones = ones_ref[...]
    rq = lax.rsqrt(_dot_split_lhs(q * q, ones) * (1.0 / HEAD_DIM) + NORM_EPS)
    rk = lax.rsqrt(_dot_split_lhs(k * k, ones[0:KW, 0:KW]) * (1.0 / HEAD_DIM) + NORM_EPS)
    cos = cos_ref[...]
    sin = sin_ref[...]
    qo = (q * qkg_ref[0:1, :] * cos + qs * qkg_ref[1:2, :] * sin) * (rq * (HEAD_DIM ** -0.5))
    ko = (k * qkg_ref[2:3, 0:KW] * cos[:, 0:KW] + ks * qkg_ref[3:4, 0:KW] * sin[:, 0:KW]) * rk
    q_ref[...] = qo.astype(BF16)
    kb = ko.astype(BF16)
    vb = v.astype(BF16)
    for hk in range(ATT_KV_HEADS):
        k_ref[hk] = kb[:, hk * HEAD_DIM:(hk + 1) * HEAD_DIM]
        v_ref[hk] = vb[:, hk * HEAD_DIM:(hk + 1) * HEAD_DIM]


def _proj_gate_kernel(x_ref, mod_ref, g_ref, w_ref, o_ref):
    h = _norm_mod(x_ref[...], g_ref[...], mod_ref[0:1, :], mod_ref[1:2, :]).astype(BF16)
    o_ref[...] = _sigmoid(_dot(h, w_ref[...])).astype(BF16)


def _proj_rw_kernel(x_ref, xp_ref, xn_ref, mod_ref, g_ref, w_ref, mu_ref, vec_ref, w2_ref, a2_ref, g2_ref,
                    ones_ref, o_ref, p_scr, *, tm, tiles_per_seq):
    i = pl.program_id(0)
    gain, shift, scale = g_ref[...], mod_ref[0:1, :], mod_ref[1:2, :]
    w = w_ref[...]
    h = _norm_mod(x_ref[...], gain, shift, scale).astype(BF16)
    p = _dot(h, w)
    hp = _norm_mod(xp_ref[...], gain, shift, scale).astype(BF16)
    hn = _norm_mod(xn_ref[...], gain, shift, scale).astype(BF16)
    first = (i % tiles_per_seq) == 0
    last = (i % tiles_per_seq) == tiles_per_seq - 1
    prev_row = jnp.where(first, 0.0, _dot(hp, w)[7:8, :])
    next_row = jnp.where(last, 0.0, _dot(hn, w)[0:1, :])
    p_scr[8:8 + tm, :] = p
    p_scr[7:8, :] = prev_row
    p_scr[8 + tm:9 + tm, :] = next_row
    prev = p_scr[7:7 + tm, :]
    nxt = p_scr[9:9 + tm, :]
    x = p + mu_ref[0:1, :] * (prev - p) + mu_ref[1:2, :] * (nxt - p)

    W = RWKV_WIDTH
    r, k, v = x[:, 0:W], x[:, W:2 * W], x[:, 2 * W:3 * W]
    xw = x[:, 3 * W:3 * W + 2 * DECAY_LORA]
    xa = x[:, 3 * W + 2 * DECAY_LORA:3 * W + 2 * DECAY_LORA + 2 * AAA_LORA]
    xg = x[:, 3 * W + 2 * DECAY_LORA + 2 * AAA_LORA:]
    w0_f, w0_b = vec_ref[0:1, :], vec_ref[1:2, :]
    a0_f, a0_b = vec_ref[2:3, :], vec_ref[3:4, :]
    k_k, k_a, r_k = vec_ref[4:5, :], vec_ref[5:6, :], vec_ref[6:7, :]
    wl = _dot(jnp.tanh(xw).astype(BF16), w2_ref[...])
    al = _dot(xa.astype(BF16), a2_ref[...])
    dec_c = float(np.exp(-0.5))
    w_f = jnp.exp(-dec_c * _sigmoid(w0_f + wl[:, 0:W]))
    w_b = jnp.exp(-dec_c * _sigmoid(w0_b + wl[:, W:2 * W]))
    a_f = _sigmoid(a0_f + al[:, 0:W])
    a_b = _sigmoid(a0_b + al[:, W:2 * W])
    gate = _dot(_sigmoid(xg).astype(BF16), g2_ref[...])
    ones = ones_ref[...]
    kk = k * k_k
    kk = kk * lax.rsqrt(_dot_split_lhs(kk * kk, ones) + 1e-12)
    k_f = k * (1.0 + (a_f - 1.0) * k_a)
    k_b = k * (1.0 + (a_b - 1.0) * k_a)
    bonus = _dot_split_lhs(r * k_f * r_k, ones) * v
    for j, val in enumerate((r, kk, v, w_f, w_b, k_f, k_b, kk * a_f, kk * a_b, gate, bonus)):
        o_ref[:, j * W:(j + 1) * W] = val


def _token_specs(tm, tiles_per_seq):
    x_spec = pl.BlockSpec((tm, D_MODEL), lambda i: (i, 0))
    mod_spec = pl.BlockSpec((None, 6, D_MODEL), lambda i: (i // tiles_per_seq, 0, 0))
    g_spec = pl.BlockSpec((1, D_MODEL), lambda i: (0, 0))
    return x_spec, mod_spec, g_spec


def _full(shape):
    nd = len(shape)
    return pl.BlockSpec(shape, lambda *_: (0,) * nd)


def _input_projections(x, mod, T, lw, tabs):
    N = x.shape[0]
    tm = min(512, T)
    tps = T // tm
    grid = (N // tm,)
    x_spec, mod_spec, g_spec = _token_specs(tm, tps)
    tab_spec = pl.BlockSpec((tm, ATT_Q_WIDTH), lambda i: (i % tps, 0))
    cos, sin = tabs["cos"], tabs["sin"]

    ret = pl.pallas_call(
        _proj_ret_kernel, grid=grid,
        in_specs=[x_spec, mod_spec, g_spec, _full(lw["w_ret"].shape), tab_spec, tab_spec],
        out_specs=pl.BlockSpec((tm, RET_COLS), lambda i: (i, 0)),
        out_shape=jax.ShapeDtypeStruct((N, RET_COLS), BF16),
        compiler_params=_params("parallel"), name="proj_ret",
    )(x, mod, lw["norm1_g"], lw["w_ret"], cos, sin)

    q, k, v = pl.pallas_call(
        _proj_att_kernel, grid=grid,
        in_specs=[x_spec, mod_spec, g_spec, _full(lw["w_att"].shape), tab_spec, tab_spec,
                  _full(lw["qk_gain"].shape), _full(tabs["ones512"].shape)],
        out_specs=[pl.BlockSpec((tm, ATT_Q_WIDTH), lambda i: (i, 0)),
                   pl.BlockSpec((ATT_KV_HEADS, tm, HEAD_DIM), lambda i: (0, i, 0)),
                   pl.BlockSpec((ATT_KV_HEADS, tm, HEAD_DIM), lambda i: (0, i, 0))],
        out_shape=[jax.ShapeDtypeStruct((N, ATT_Q_WIDTH), BF16),
                   jax.ShapeDtypeStruct((ATT_KV_HEADS, N, HEAD_DIM), BF16),
                   jax.ShapeDtypeStruct((ATT_KV_HEADS, N, HEAD_DIM), BF16)],
        compiler_params=_params("parallel"), name="proj_att",
    )(x, mod, lw["norm1_g"], lw["w_att"], cos, sin, lw["qk_gain"], tabs["ones512"])

    gates = pl.pallas_call(
        _proj_gate_kernel, grid=grid,
        in_specs=[x_spec, mod_spec, g_spec, _full(lw["w_gate"].shape)],
        out_specs=pl.BlockSpec((tm, GATE_COLS), lambda i: (i, 0)),
        out_shape=jax.ShapeDtypeStruct((N, GATE_COLS), BF16),
        compiler_params=_params("parallel"), name="proj_gate",
    )(x, mod, lw["norm1_g"], lw["w_gate"])

    tm_rw = min(256, T)
    tps_rw = T // tm_rw
    x_spec, mod_spec, g_spec = _token_specs(tm_rw, tps_rw)
    rows8 = tm_rw // 8
    last8 = N // 8 - 1
    rwp = pl.pallas_call(
        functools.partial(_proj_rw_kernel, tm=tm_rw, tiles_per_seq=tps_rw), grid=(N // tm_rw,),
        in_specs=[x_spec,
                  pl.BlockSpec((8, D_MODEL), lambda i: (jnp.maximum(i * rows8 - 1, 0), 0)),
                  pl.BlockSpec((8, D_MODEL), lambda i: (jnp.minimum((i + 1) * rows8, last8), 0)),
                  mod_spec, g_spec, _full(lw["w_rw"].shape), _full(lw["shift_mu"].shape),
                  _full(lw["rw_vec"].shape), _full(lw["rw_w2"].shape), _full(lw["rw_a2"].shape),
                  _full(lw["rw_g2"].shape), _full(tabs["ones256"].shape)],
        out_specs=pl.BlockSpec((tm_rw, RW_FIELDS * RWKV_WIDTH), lambda i: (i, 0)),
        out_shape=jax.ShapeDtypeStruct((N, RW_FIELDS * RWKV_WIDTH), F32),
        scratch_shapes=[pltpu.VMEM((tm_rw + 16, RW_COLS), F32)],
        compiler_params=_params("parallel"), name="proj_rw",
    )(x, x, x, mod, lw["norm1_g"], lw["w_rw"], lw["shift_mu"], lw["rw_vec"], lw["rw_w2"], lw["rw_a2"],
      lw["rw_g2"], tabs["ones256"])
    return ret, q, k, v, gates, rwp


def _retention_tables(reverse):
    lg = np.log1p(-np.exp2(-5.0 - np.arange(RET_HEADS, dtype=np.float64)))
    if reverse:
        lg = lg[::-1]
    C = RET_CHUNK
    pos = np.arange(C, dtype=np.float64)
    diff = pos[:, None] - pos[None, :]
    if reverse:
        dec = np.where(diff < 0, np.exp(lg[:, None, None] * np.maximum(-diff, 0.0)[None]), 0.0)
        xi = np.exp(lg[:, None] * (C - pos)[None, :])
        zeta = np.exp(lg[:, None] * pos[None, :])
    else:
        dec = np.where(diff >= 0, np.exp(lg[:, None, None] * np.maximum(diff, 0.0)[None]), 0.0)
        xi = np.exp(lg[:, None] * (pos + 1.0)[None, :])
        zeta = np.exp(lg[:, None] * (C - 1.0 - pos)[None, :])
    chunk_decay = tuple(float(v) for v in np.exp(lg * C))
    widen = lambda a: jnp.asarray(np.repeat(a.T, HEAD_DIM, axis=1), F32)
    return jnp.asarray(dec, F32), widen(xi), widen(zeta), chunk_decay


def _retention_kernel(ret_ref, dec_ref, xi_ref, zeta_ref, *rest, chunk_decay, final):
    if final:
        yf_ref, gn_ref, ones_ref, o_ref, state, y_scr = rest
    else:
        o_ref, state, y_scr = rest

    @pl.when(pl.program_id(1) == 0)
    def _():
        state[...] = jnp.zeros_like(state)

    W = RET_WIDTH
    q = ret_ref[:, 0:W]
    k = ret_ref[:, W:2 * W]
    v = ret_ref[:, 2 * W:3 * W]
    qx = (q.astype(F32) * xi_ref[...]).astype(BF16)
    kzt = (k.astype(F32) * zeta_ref[...]).T
    for h in range(RET_HEADS):
        sl = slice(h * HEAD_DIM, (h + 1) * HEAD_DIM)
        s = _dot_nt(q[:, sl], k[:, sl]) * dec_ref[h]
        inner = _dot(s.astype(BF16), v[:, sl])
        r_prev = state[h]
        cross = _dot(qx[:, sl], r_prev.astype(BF16))
        kv = _dot(kzt[sl, :].astype(BF16), v[:, sl])
        state[h] = r_prev * chunk_decay[h] + kv
        y_scr[:, sl] = inner + cross
    y = y_scr[...]
    if final:
        y = _head_layer_norm(y + yf_ref[...], ones_ref[...], RET_EPS) * gn_ref[...]
        o_ref[...] = (y * ret_ref[:, 3 * W:4 * W].astype(F32)).astype(BF16)
    else:
        o_ref[...] = y


def _retention(ret, B, T, ret_gn, ones256):
    N = ret.shape[0]
    C = RET_CHUNK
    nc = T // C
    scratch = [pltpu.VMEM((RET_HEADS, HEAD_DIM, HEAD_DIM), F32), pltpu.VMEM((C, RET_WIDTH), F32)]

    dec, xi, zeta, cd = _retention_tables(False)
    fwd_row = lambda b, c: (b * nc + c, 0)
    yf = pl.pallas_call(
        functools.partial(_retention_kernel, chunk_decay=cd, final=False), grid=(B, nc),
        in_specs=[pl.BlockSpec((C, RET_COLS), fwd_row), _full(dec.shape), _full(xi.shape), _full(zeta.shape)],
        out_specs=pl.BlockSpec((C, RET_WIDTH), fwd_row),
        out_shape=jax.ShapeDtypeStruct((N, RET_WIDTH), F32),
        scratch_shapes=scratch, compiler_params=_params("parallel", "arbitrary"), name="retention_fwd",
    )(ret, dec, xi, zeta)

    dec, xi, zeta, cd = _retention_tables(True)
    bwd_row = lambda b, c: (b * nc + nc - 1 - c, 0)
    return pl.pallas_call(
        functools.partial(_retention_kernel, chunk_decay=cd, final=True), grid=(B, nc),
        in_specs=[pl.BlockSpec((C, RET_COLS), bwd_row), _full(dec.shape), _full(xi.shape), _full(zeta.shape),
                  pl.BlockSpec((C, RET_WIDTH), bwd_row), _full(ret_gn.shape), _full(ones256.shape)],
        out_specs=pl.BlockSpec((C, RET_WIDTH), bwd_row),
        out_shape=jax.ShapeDtypeStruct((N, RET_WIDTH), BF16),
        scratch_shapes=scratch, compiler_params=_params("parallel", "arbitrary"), name="retention_bwd",
    )(ret, dec, xi, zeta, yf, ret_gn, ones256)


def _flash_kernel(q_ref, k_ref, v_ref, o_ref, q_scr, m_scr, l_scr, acc_scr, *, tq, tk):
    j = pl.program_id(3)

    @pl.when(j == 0)
    def _():
        for g in range(ATT_GROUP):
            q_scr[g * tq:(g + 1) * tq, :] = q_ref[:, g * HEAD_DIM:(g + 1) * HEAD_DIM]
        m_scr[...] = jnp.full_like(m_scr, -jnp.inf)
        l_scr[...] = jnp.zeros_like(l_scr)
        acc_scr[...] = jnp.zeros_like(acc_scr)

    k = k_ref[...]
    v = v_ref[...]
    for r0 in range(0, ATT_GROUP * tq, FLASH_SUB_ROWS):
        rows = slice(r0, r0 + FLASH_SUB_ROWS)
        s = _dot_nt(q_scr[rows, :], k)
        m_prev = m_scr[rows, :]
        m_next = jnp.maximum(m_prev, jnp.max(s, axis=1, keepdims=True))
        p = jnp.exp(s - jnp.concatenate([m_next] * (tk // V7X_LANES), axis=1))
        alpha = jnp.exp(m_prev - m_next)
        l_scr[rows, :] = alpha * l_scr[rows, :] + jnp.sum(p, axis=1, keepdims=True)
        m_scr[rows, :] = m_next
        acc_scr[rows, :] = alpha[:, 0:HEAD_DIM] * acc_scr[rows, :] + _dot(p.astype(BF16), v)

    @pl.when(j == pl.num_programs(3) - 1)
    def _():
        o = acc_scr[...] / l_scr[:, 0:HEAD_DIM]
        for g in range(ATT_GROUP):
            o_ref[:, g * HEAD_DIM:(g + 1) * HEAD_DIM] = o[g * tq:(g + 1) * tq, :].astype(BF16)


def _attention(q, k, v, B, T):
    N = q.shape[0]
    tq = min(FLASH_Q_ROWS, T)
    tk = min(FLASH_KV_ROWS, T)
    nq, nk = T // tq, T // tk
    GW = ATT_GROUP * HEAD_DIM
    rows = ATT_GROUP * tq
    return pl.pallas_call(
        functools.partial(_flash_kernel, tq=tq, tk=tk), grid=(B, ATT_KV_HEADS, nq, nk),
        in_specs=[pl.BlockSpec((tq, GW), lambda b, h, i, j: (b * nq + i, h)),
                  pl.BlockSpec((None, tk, HEAD_DIM), lambda b, h, i, j: (h, b * nk + j, 0)),
                  pl.BlockSpec((None, tk, HEAD_DIM), lambda b, h, i, j: (h, b * nk + j, 0))],
        out_specs=pl.BlockSpec((tq, GW), lambda b, h, i, j: (b * nq + i, h)),
        out_shape=jax.ShapeDtypeStruct((N, ATT_Q_WIDTH), BF16),
        scratch_shapes=[pltpu.VMEM((rows, HEAD_DIM), BF16), pltpu.VMEM((rows, V7X_LANES), F32),
                        pltpu.VMEM((rows, V7X_LANES), F32), pltpu.VMEM((rows, HEAD_DIM), F32)],
        compiler_params=_params("parallel", "parallel", "parallel", "arbitrary"), name="flash_attention",
    )(q, k, v)


def _rwkv_scan_kernel(kk_f, r_f, w_f, k_f, b_f, vt_f, kk_b, r_b, w_b, k_b, b_b, vt_b, ones_ref,
                      yf_ref, yb_ref, state, vk_scr, *, nb):
    C = RW_CHUNK
    H = C // 2
    W = RWKV_WIDTH
    SUB = 8

    @pl.when(pl.program_id(1) == 0)
    def _():
        state[...] = jnp.zeros_like(state)

    ones = ones_ref[...]
    lane = lax.broadcasted_iota(jnp.int32, (1, W), 1) & (HEAD_DIM - 1)
    same_head = (lax.broadcasted_iota(jnp.int32, (W, W), 0) // HEAD_DIM
                 == lax.broadcasted_iota(jnp.int32, (W, W), 1) // HEAD_DIM)
    step_iota = lax.broadcasted_iota(jnp.int32, (H, HEAD_DIM, W), 0)
    lane_iota = lax.broadcasted_iota(jnp.int32, (H, HEAD_DIM, W), 2) & (HEAD_DIM - 1)
    dirs = ((kk_f, r_f, w_f, k_f, b_f, vt_f), (kk_b, r_b, w_b, k_b, b_b, vt_b))
    chains = [(n, d) for n in range(nb) for d in range(2)]

    def head_sums(vals):
        stacked = jnp.concatenate([v.astype(BF16) for v in vals], axis=0)
        out = _dot(stacked, ones)
        return [out[c * HEAD_DIM:(c + 1) * HEAD_DIM, :] for c in range(len(vals))]

    carry = (tuple(state[c] for c in range(len(chains))),
             tuple(jnp.zeros((HEAD_DIM, W), F32) for _ in chains))
    for half in range(2):
        lo = (half * H, (1 - half) * H)
        for c, (n, d) in enumerate(chains):
            kc = dirs[d][3][n]
            kbd = jnp.where(same_head, jnp.concatenate([kc] * (W // C), axis=0), 0.0).astype(BF16)
            sel = jnp.where(step_iota + lo[d] == lane_iota, dirs[d][5][n][None], 0.0).reshape(H * HEAD_DIM, W)
            vk_scr[c] = _dot(sel.astype(BF16), kbd).reshape(H, HEAD_DIM, W)

        def group(g, carry, lo=lo):
            states, ycols = list(carry[0]), list(carry[1])
            bases = (lo[0] + g * SUB, lo[1] + (H // SUB - 1 - g) * SUB)
            rows = [[ref[n, pl.ds(pl.multiple_of(bases[d], SUB), SUB), :] for ref in dirs[d][:5]]
                    for (n, d) in chains]
            for j in range(SUB):
                jj = (j, SUB - 1 - j)
                row = lambda c, f: rows[c][f][jj[chains[c][1]]:jj[chains[c][1]] + 1, :]
                sa = head_sums([states[c] * row(c, 0) for c in range(len(chains))])
                new = [states[c] * row(c, 2) - sa[c] * row(c, 4)
                       + vk_scr[c, bases[chains[c][1]] + jj[chains[c][1]] - lo[chains[c][1]]]
                       for c in range(len(chains))]
                read = [new[c] if chains[c][1] == 0 else states[c] for c in range(len(chains))]
                y = head_sums([read[c] * row(c, 1) for c in range(len(chains))])
                for c, (n, d) in enumerate(chains):
                    ycols[c] = jnp.where(lane == bases[d] + jj[d], y[c], ycols[c])
                states = new
            return tuple(states), tuple(ycols)

        carry = lax.fori_loop(0, H // SUB, group, carry)

    states, ycols = carry
    for c, (n, d) in enumerate(chains):
        state[c] = states[c]
        (yf_ref if d == 0 else yb_ref)[n] = ycols[c]


def _rwkv_scan(rwp, B, T, ones256):
    W, C = RWKV_WIDTH, RW_CHUNK
    nc = T // C
    nb = RW_BATCH_ROWS if B % RW_BATCH_ROWS == 0 else 1
    fields = rwp.reshape(B, T, RW_FIELDS * W)
    v = rwp[:, 2 * W:3 * W].reshape(B, nc, C, RWKV_HEADS, HEAD_DIM)
    vt = v.transpose(0, 1, 4, 3, 2).reshape(B, nc, HEAD_DIM, W)

    fwd = lambda f: pl.BlockSpec((nb, C, W), lambda b, c: (b, c, f))
    bwd = lambda f: pl.BlockSpec((nb, C, W), lambda b, c: (b, nc - 1 - c, f))
    vt_fwd = pl.BlockSpec((nb, None, HEAD_DIM, W), lambda b, c: (b, c, 0, 0))
    vt_bwd = pl.BlockSpec((nb, None, HEAD_DIM, W), lambda b, c: (b, nc - 1 - c, 0, 0))
    yt_shape = jax.ShapeDtypeStruct((B, nc, HEAD_DIM, W), F32)
    yt_f, yt_b = pl.pallas_call(
        functools.partial(_rwkv_scan_kernel, nb=nb), grid=(B // nb, nc),
        in_specs=[fwd(1), fwd(0), fwd(3), fwd(5), fwd(7), vt_fwd,
                  bwd(1), bwd(0), bwd(4), bwd(6), bwd(8), vt_bwd, _full(ones256.shape)],
        out_specs=[vt_fwd, vt_bwd], out_shape=[yt_shape, yt_shape],
        scratch_shapes=[pltpu.VMEM((nb * 2, HEAD_DIM, W), F32),
                        pltpu.VMEM((nb * 2, C // 2, HEAD_DIM, W), F32)],
        compiler_params=_params("parallel", "arbitrary"), name="rwkv_scan",
    )(fields, fields, fields, fields, fields, vt, fields, fields, fields, fields, fields, vt, ones256)

    untranspose = lambda yt: yt.reshape(B, nc, HEAD_DIM, RWKV_HEADS, C).transpose(0, 1, 4, 3, 2).reshape(B * T, W)
    return untranspose(yt_f), untranspose(yt_b)


def _merge_kernel(x_ref, mod_ref, ya_ref, yb_ref, yf_ref, ybk_ref, rg_ref, bonus_ref, gn_ref, gates_ref,
                  ones_ref, wa_ref, wb_ref, wc_ref, wo_ref, o_ref):
    y = _head_layer_norm(yf_ref[...] + ybk_ref[...], ones_ref[...], RWKV_GN_EPS) * gn_ref[...]
    yc = ((y + bonus_ref[...]) * rg_ref[...]).astype(BF16)
    D = D_MODEL
    merged = (gates_ref[:, 0:D].astype(F32) * _dot(ya_ref[...], wa_ref[...])
              + gates_ref[:, D:2 * D].astype(F32) * _dot(yb_ref[...], wb_ref[...])
              + gates_ref[:, 2 * D:3 * D].astype(F32) * _dot(yc, wc_ref[...]))
    o_ref[...] = x_ref[...] + mod_ref[2:3, :] * _dot(merged.astype(BF16), wo_ref[...])


def _merge(x, mod, T, ya, yb, yf, ybk, rwp, gates, lw, ones256):
    N = x.shape[0]
    tm = min(512, T)
    tps = T // tm
    x_spec, mod_spec, _ = _token_specs(tm, tps)
    row = lambda w: pl.BlockSpec((tm, w), lambda i: (i, 0))
    field = lambda f: pl.BlockSpec((tm, RWKV_WIDTH), lambda i: (i, f))
    return pl.pallas_call(
        _merge_kernel, grid=(N // tm,),
        in_specs=[x_spec, mod_spec, row(RET_WIDTH), row(ATT_Q_WIDTH), row(RWKV_WIDTH), row(RWKV_WIDTH),
                  field(9), field(10), _full(lw["rw_gn"].shape), row(GATE_COLS), _full(ones256.shape),
                  _full(lw["w_branch_a"].shape), _full(lw["w_branch_b"].shape), _full(lw["w_branch_c"].shape),
                  _full(lw["w_out"].shape)],
        out_specs=x_spec, out_shape=jax.ShapeDtypeStruct((N, D_MODEL), F32),
        compiler_params=_params("parallel"), name="merge_out",
    )(x, mod, ya, yb, yf, ybk, rwp, rwp, lw["rw_gn"], gates, ones256,
      lw["w_branch_a"], lw["w_branch_b"], lw["w_branch_c"], lw["w_out"])


def _route(h, hb, rwt_ref, rbias_ref, tm):
    h_lo = (h - hb.astype(F32)).astype(BF16)
    rw_hi, rw_lo = _split(rwt_ref[...])
    logits = _dot_nt(rw_hi, hb) + _dot_nt(rw_hi, h_lo) + _dot_nt(rw_lo, hb)
    scores = _sigmoid(logits)
    choice = scores + jnp.concatenate([rbias_ref[...]] * (tm // V7X_LANES), axis=1)
    per_group = N_EXPERTS // N_GROUPS
    sub = lax.broadcasted_iota(jnp.int32, (per_group, tm), 0)
    groups, gscore = [], []
    for g in range(N_GROUPS):
        cg = choice[g * per_group:(g + 1) * per_group, :]
        m1 = jnp.max(cg, axis=0, keepdims=True)
        first = jnp.min(jnp.where(cg == m1, sub, per_group), axis=0, keepdims=True)
        m2 = jnp.max(jnp.where(sub == first, -jnp.inf, cg), axis=0, keepdims=True)
        groups.append(cg)
        gscore.append(m1 + m2)
    masked = []
    for g in range(N_GROUPS):
        beaten = jnp.zeros((1, tm), F32)
        for o in range(N_GROUPS):
            if o == g:
                continue
            wins = (gscore[o] >= gscore[g]) if o < g else (gscore[o] > gscore[g])
            beaten = beaten + jnp.where(wins, 1.0, 0.0)
        keep = jnp.where(beaten < TOPK_GROUPS, 1.0, 0.0)
        masked.append(jnp.where(jnp.broadcast_to(keep, (per_group, tm)) > 0.5, groups[g], -jnp.inf))
    mc = jnp.concatenate(masked, axis=0)
    eidx = lax.broadcasted_iota(jnp.int32, (N_EXPERTS, tm), 0)
    beaten = jnp.zeros((N_EXPERTS, tm), F32)
    for o in range(N_EXPERTS):
        row = jnp.broadcast_to(mc[o:o + 1, :], (N_EXPERTS, tm))
        tie = jnp.where(eidx > o, 1.0, 0.0)
        beaten = beaten + jnp.where(row > mc, 1.0, jnp.where(row == mc, tie, 0.0))
    sel = jnp.where(beaten < TOP_K, 1.0, 0.0)
    wts = jnp.where(beaten < TOP_K, scores, 0.0)
    return wts / jnp.sum(wts, axis=0, keepdims=True) * ROUTED_SCALE, sel


def _selected_lists(comb, sel):
    eidx = lax.broadcasted_iota(jnp.int32, comb.shape, 0).astype(F32)
    prev = jnp.full((1, comb.shape[1]), -1.0, F32)
    ids, wts = [], []
    for _ in range(TOP_K):
        cand = jnp.where(sel > 0.5, jnp.where(eidx > prev, eidx, float(N_EXPERTS)), float(N_EXPERTS))
        prev = jnp.min(cand, axis=0, keepdims=True)
        wts.append(jnp.sum(jnp.where(eidx == prev, comb, 0.0), axis=0, keepdims=True))
        ids.append(jnp.minimum(prev, N_EXPERTS - 1.0))
    return jnp.concatenate(ids, axis=0).astype(jnp.int32), jnp.concatenate(wts, axis=0)


def _pack_halves(y):
    n = y.shape[1] // 2
    hi = pltpu.bitcast(y[:, :n].astype(BF16).astype(F32), jnp.int32)
    lo = pltpu.bitcast(y[:, n:].astype(BF16).astype(F32), jnp.int32)
    return hi | lax.shift_right_logical(lo, 16)


def _unpack_halves(w):
    hi = pltpu.bitcast(w & jnp.int32(-65536), F32)
    lo = pltpu.bitcast(lax.shift_left(w, 16), F32)
    return hi, lo


def _moe_route_kernel(x_ref, mod_ref, g_ref, rwt_ref, rbias_ref, shgu_ref, shd_ref,
                      hp_ref, ids_ref, wts_ref, base_ref, *, tm):
    F = EXPERT_FF
    x = x_ref[...]
    h = _norm_mod(x, g_ref[...], mod_ref[3:4, :], mod_ref[4:5, :])
    hb = h.astype(BF16)
    hp_ref[...] = _pack_halves(hb.astype(F32))
    comb, sel = _route(h, hb, rwt_ref, rbias_ref, tm)
    ids, wts = _selected_lists(comb, sel)
    ids_ref[...] = ids
    wts_ref[...] = wts
    gu = _dot(hb, shgu_ref[...])
    shared = _dot((_silu(gu[:, 0:F]) * gu[:, F:2 * F]).astype(BF16), shd_ref[...])
    base_ref[...] = x + mod_ref[5:6, :] * shared


def _moe_expert_kernel(tile_expert_ref, xs_ref, wgu_ref, wd_ref, o_ref):
    del tile_expert_ref
    F = EXPERT_FF
    half = D_MODEL // 2
    hi, lo = _unpack_halves(xs_ref[...])
    gu = _dot(hi.astype(BF16), wgu_ref[0:half, :]) + _dot(lo.astype(BF16), wgu_ref[half:, :])
    act = (_silu(gu[:, 0:F]) * gu[:, F:2 * F]).astype(BF16)
    o_ref[...] = _pack_halves(_dot(act, wd_ref[...]))


def _moe_combine_kernel(base_ref, mod_ref, y_ref, w_ref, fin_ref, o_ref, *, final):
    half = D_MODEL // 2
    acc_hi = jnp.zeros((base_ref.shape[0], half), F32)
    acc_lo = jnp.zeros((base_ref.shape[0], half), F32)
    for k in range(TOP_K):
        hi, lo = _unpack_halves(y_ref[k])
        wk = w_ref[:, k:k + 1]
        acc_hi = acc_hi + wk * hi
        acc_lo = acc_lo + wk * lo
    out_hi = base_ref[:, 0:half] + mod_ref[5:6, 0:half] * acc_hi
    out_lo = base_ref[:, half:] + mod_ref[5:6, half:] * acc_lo
    if final:
        ms = (jnp.sum(out_hi * out_hi, axis=-1, keepdims=True)
              + jnp.sum(out_lo * out_lo, axis=-1, keepdims=True)) * (1.0 / D_MODEL)
        scale = lax.rsqrt(ms + NORM_EPS)
        out_hi = out_hi * scale * fin_ref[:, 0:half]
        out_lo = out_lo * scale * fin_ref[:, half:]
    o_ref[:, 0:half] = out_hi
    o_ref[:, half:] = out_lo


def _gather_rows(table, idx):
    rows, width = idx.shape[0], table.shape[1]
    info = plsc.get_sparse_core_info()
    workers = info.num_cores * info.num_subcores
    windows = rows // (workers * SC_GATHER_ROWS)
    assert windows * workers * SC_GATHER_ROWS == rows
    mesh = plsc.VectorSubcoreMesh(core_axis_name="core", subcore_axis_name="subcore")

    @functools.partial(pl.kernel, out_type=jax.ShapeDtypeStruct((rows, width), table.dtype), mesh=mesh,
                       scratch_types=[pltpu.VMEM((SC_GATHER_ROWS,), jnp.int32),
                                      pltpu.VMEM((SC_GATHER_ROWS, width), table.dtype)],
                       name="moe_gather")
    def gather(table_hbm, idx_hbm, out_hbm, idx_vmem, rows_vmem):
        worker = lax.axis_index("subcore") * info.num_cores + lax.axis_index("core")
        base = worker * (windows * SC_GATHER_ROWS)

        @pl.loop(0, windows)
        def _(j):
            span = pl.ds(pl.multiple_of(base + j * SC_GATHER_ROWS, SC_GATHER_ROWS), SC_GATHER_ROWS)
            pltpu.sync_copy(idx_hbm.at[span], idx_vmem)
            pltpu.sync_copy(table_hbm.at[idx_vmem], rows_vmem)
            pltpu.sync_copy(rows_vmem, out_hbm.at[span])

    return gather(table, idx)


def _dispatch_plan(ids, n_tokens):
    E, tg = N_EXPERTS, MOE_GROUP_ROWS
    onehot = (ids[:, :, None] == jnp.arange(E, dtype=jnp.int32)[None, None, :]).astype(jnp.int32)
    cnt = jnp.sum(onehot, axis=0)
    csum = jnp.cumsum(cnt, axis=0)
    rank = csum - cnt
    padded = (csum[-1] + tg - 1) // tg * tg
    ends = jnp.cumsum(padded)
    off = ends - padded
    dup_before = jnp.stack([sum((ids[j] == ids[k]).astype(jnp.int32) for j in range(k)) if k else
                            jnp.zeros_like(ids[0]) for k in range(TOP_K)])
    dest = off[ids] + jnp.take_along_axis(rank, ids.T, axis=1).T + dup_before
    total = n_tokens * TOP_K + E * tg
    src = jnp.zeros((total,), jnp.int32).at[dest.reshape(-1)].set(
        jnp.tile(jnp.arange(n_tokens, dtype=jnp.int32), TOP_K))
    tile_start = jnp.arange(total // tg, dtype=jnp.int32) * tg
    tile_expert = jnp.minimum(jnp.searchsorted(ends, tile_start, side="right"), E - 1).astype(jnp.int32)
    return dest.reshape(-1), src, tile_expert


def _moe(x, mod, T, lw, final_g, final):
    N = x.shape[0]
    tm = min(MOE_TOKENS, T)
    tps = T // tm
    D, F, E, tg = D_MODEL, EXPERT_FF, N_EXPERTS, MOE_GROUP_ROWS
    half = D // 2
    x_spec, mod_spec, g_spec = _token_specs(tm, tps)
    list_spec = pl.BlockSpec((TOP_K, tm), lambda i: (0, i))

    hp, ids, wts, base = pl.pallas_call(
        functools.partial(_moe_route_kernel, tm=tm), grid=(N // tm,),
        in_specs=[x_spec, mod_spec, g_spec, _full((E, D)), _full((E, V7X_LANES)), _full((D, 2 * F)), _full((F, D))],
        out_specs=[pl.BlockSpec((tm, half), lambda i: (i, 0)), list_spec, list_spec, x_spec],
        out_shape=[jax.ShapeDtypeStruct((N, half), jnp.int32), jax.ShapeDtypeStruct((TOP_K, N), jnp.int32),
                   jax.ShapeDtypeStruct((TOP_K, N), F32), jax.ShapeDtypeStruct((N, D), F32)],
        compiler_params=_params("parallel"), name="moe_route",
    )(x, mod, lw["norm2_g"], lw["router_wt"], lw["router_bias"], lw["sh_gu"], lw["sh_d"])

    dest, src, tile_expert = _dispatch_plan(ids, N)
    xs = _gather_rows(hp, src)
    n_tiles = xs.shape[0] // tg
    ys = pl.pallas_call(
        _moe_expert_kernel,
        grid_spec=pltpu.PrefetchScalarGridSpec(
            num_scalar_prefetch=1, grid=(n_tiles,),
            in_specs=[pl.BlockSpec((tg, half), lambda i, te: (i, 0)),
                      pl.BlockSpec((None, D, 2 * F), lambda i, te: (te[i], 0, 0)),
                      pl.BlockSpec((None, F, D), lambda i, te: (te[i], 0, 0))],
            out_specs=pl.BlockSpec((tg, half), lambda i, te: (i, 0))),
        out_shape=jax.ShapeDtypeStruct(xs.shape, jnp.int32),
        compiler_params=_params("parallel"), name="moe_experts",
    )(tile_expert, xs, lw["exp_gu"], lw["exp_d"])

    yk = _gather_rows(ys, dest).reshape(TOP_K, N, half)
    return pl.pallas_call(
        functools.partial(_moe_combine_kernel, final=final), grid=(N // tm,),
        in_specs=[x_spec, mod_spec, pl.BlockSpec((TOP_K, tm, half), lambda i: (0, i, 0)),
                  pl.BlockSpec((tm, TOP_K), lambda i: (i, 0)), g_spec],
        out_specs=x_spec, out_shape=jax.ShapeDtypeStruct((N, D), F32),
        compiler_params=_params("parallel"), name="moe_combine",
    )(base, mod, yk, wts.T, final_g)


def _swap_halves(w, heads):
    shp = w.shape
    w = w.reshape(shp[:-1] + (heads, 2, HEAD_DIM // 2))
    return jnp.flip(w, axis=-2).reshape(shp)


def _block_diag2(a, b):
    za = jnp.zeros((a.shape[0], b.shape[1]), a.dtype)
    zb = jnp.zeros((b.shape[0], a.shape[1]), a.dtype)
    return jnp.concatenate([jnp.concatenate([a, za], 1), jnp.concatenate([zb, b], 1)], 0)


def _layer_weights(P, l):
    w_in = P["w_in"][l]
    o1, o2, o3 = RET_COLS, RET_COLS + ATT_COLS, RET_COLS + ATT_COLS + RW_COLS
    w_ret, w_att, w_rw, w_gate = w_in[:, :o1], w_in[:, o1:o2], w_in[:, o2:o3], w_in[:, o3:]
    W = RET_WIDTH
    w_ret = jnp.concatenate([w_ret, _swap_halves(w_ret[:, 0:W], RET_HEADS),
                             _swap_halves(w_ret[:, W:2 * W], RET_HEADS)], axis=1)
    QW, KW = ATT_Q_WIDTH, ATT_KV_WIDTH
    w_att = jnp.concatenate([w_att, _swap_halves(w_att[:, 0:QW], ATT_Q_HEADS),
                             _swap_halves(w_att[:, QW:QW + KW], ATT_KV_HEADS)], axis=1)
    qg, kg = P["q_norm_g"][l], P["k_norm_g"][l]
    pad = lambda v: jnp.zeros((QW,), F32).at[:v.shape[0]].set(v)
    qk_gain = jnp.zeros((8, QW), F32)
    qk_gain = qk_gain.at[0].set(jnp.tile(qg, ATT_Q_HEADS))
    qk_gain = qk_gain.at[1].set(jnp.tile(_swap_halves(qg, 1), ATT_Q_HEADS))
    qk_gain = qk_gain.at[2].set(pad(jnp.tile(kg, ATT_KV_HEADS)))
    qk_gain = qk_gain.at[3].set(pad(jnp.tile(_swap_halves(kg, 1), ATT_KV_HEADS)))
    rw_vec = jnp.zeros((8, RWKV_WIDTH), F32)
    for j, name in enumerate(("rw_w0_f", "rw_w0_b", "rw_a0_f", "rw_a0_b", "rw_k_k", "rw_k_a")):
        rw_vec = rw_vec.at[j].set(P[name][l])
    rw_vec = rw_vec.at[6].set(P["rw_r_k"][l].reshape(RWKV_WIDTH))
    row = lambda v: v.reshape(1, -1)
    return {
        "norm1_g": row(P["norm1_g"][l]), "norm2_g": row(P["norm2_g"][l]),
        "w_ret": w_ret.astype(BF16), "w_att": w_att.astype(BF16), "w_rw": w_rw.astype(BF16),
        "w_gate": w_gate.astype(BF16), "qk_gain": qk_gain, "ret_gn": row(P["ret_gn"][l]),
        "shift_mu": P["shift_mu"][l], "rw_vec": rw_vec,
        "rw_w2": _block_diag2(P["rw_w2_f"][l], P["rw_w2_b"][l]).astype(BF16),
        "rw_a2": _block_diag2(P["rw_a2_f"][l], P["rw_a2_b"][l]).astype(BF16),
        "rw_g2": P["rw_g2"][l].astype(BF16), "rw_gn": row(P["rw_gn"][l]),
        "w_branch_a": P["w_branch_a"][l].astype(BF16), "w_branch_b": P["w_branch_b"][l].astype(BF16),
        "w_branch_c": P["w_branch_c"][l].astype(BF16), "w_out": P["w_out"][l].astype(BF16),
        "router_wt": P["router_w"][l].T,
        "router_bias": jnp.broadcast_to(P["router_bias"][l][:, None], (N_EXPERTS, V7X_LANES)),
        "sh_gu": jnp.concatenate([P["sh_w_gate"][l], P["sh_w_up"][l]], axis=1).astype(BF16),
        "sh_d": P["sh_w_down"][l].astype(BF16),
        "exp_gu": jnp.concatenate([P["exp_w_gate"][l], P["exp_w_up"][l]], axis=2).astype(BF16),
        "exp_d": P["exp_w_down"][l].astype(BF16),
    }


def _ones_block_diag(n):
    idx = np.arange(n) // HEAD_DIM
    return jnp.asarray(idx[:, None] == idx[None, :], BF16)


def _tables(T):
    rows = T // GRID_W
    row = jnp.repeat(jnp.arange(rows, dtype=F32), GRID_W)
    col = jnp.tile(jnp.arange(GRID_W, dtype=F32), rows)
    freqs = ROPE_THETA ** (-jnp.arange(ROPE_PAIRS_PER_AXIS, dtype=F32) / ROPE_PAIRS_PER_AXIS)
    ang = jnp.concatenate([row[:, None] * freqs, col[:, None] * freqs], axis=-1)
    cos, sin = jnp.cos(ang), jnp.sin(ang)
    return {
        "cos": jnp.tile(jnp.concatenate([cos, cos], axis=-1), (1, ATT_Q_HEADS)),
        "sin": jnp.tile(jnp.concatenate([-sin, sin], axis=-1), (1, ATT_Q_HEADS)),
        "ones128": _ones_block_diag(128), "ones256": _ones_block_diag(256), "ones512": _ones_block_diag(512),
    }


def _trunk(x, c, P, layer_weights):
    B, T, D = x.shape
    tabs = _tables(T)
    mods = _ada_mod(c, P["ada_w"], P["ada_b"])
    x = x.reshape(B * T, D)
    final_g = P["final_g"].reshape(1, D)
    depth = len(layer_weights)
    for l, lw in enumerate(layer_weights):
        mod = mods[l]
        ret, q, k, v, gates, rwp = _input_projections(x, mod, T, lw, tabs)
        ya = _retention(ret, B, T, lw["ret_gn"], tabs["ones256"])
        yb = _attention(q, k, v, B, T)
        yf, ybk = _rwkv_scan(rwp, B, T, tabs["ones256"])
        x = _merge(x, mod, T, ya, yb, yf, ybk, rwp, gates, lw, tabs["ones256"])
        x = _moe(x, mod, T, lw, final_g, final=(l == depth - 1))
    return x.reshape(B, T, D)


def kernel(x_prompt, x_sample, c_prompt, c_sample, norm1_g, norm2_g, final_g, ada_w, ada_b, w_in, ret_gn, q_norm_g, k_norm_g, shift_mu, rw_w0_f, rw_w2_f, rw_w0_b, rw_w2_b, rw_a0_f, rw_a2_f, rw_a0_b, rw_a2_b, rw_g2, rw_k_k, rw_k_a, rw_r_k, rw_gn, w_branch_a, w_branch_b, w_branch_c, w_out, router_w, router_bias, exp_w_gate, exp_w_up, exp_w_down, sh_w_gate, sh_w_up, sh_w_down):
    P = {
        "norm1_g": norm1_g, "norm2_g": norm2_g, "final_g": final_g, "ada_w": ada_w, "ada_b": ada_b,
        "w_in": w_in, "ret_gn": ret_gn, "q_norm_g": q_norm_g, "k_norm_g": k_norm_g, "shift_mu": shift_mu,
        "rw_w0_f": rw_w0_f, "rw_w2_f": rw_w2_f, "rw_w0_b": rw_w0_b, "rw_w2_b": rw_w2_b,
        "rw_a0_f": rw_a0_f, "rw_a2_f": rw_a2_f, "rw_a0_b": rw_a0_b, "rw_a2_b": rw_a2_b,
        "rw_g2": rw_g2, "rw_k_k": rw_k_k, "rw_k_a": rw_k_a, "rw_r_k": rw_r_k, "rw_gn": rw_gn,
        "w_branch_a": w_branch_a, "w_branch_b": w_branch_b, "w_branch_c": w_branch_c, "w_out": w_out,
        "router_w": router_w, "router_bias": router_bias,
        "exp_w_gate": exp_w_gate, "exp_w_up": exp_w_up, "exp_w_down": exp_w_down,
        "sh_w_gate": sh_w_gate, "sh_w_up": sh_w_up, "sh_w_down": sh_w_down,
    }
    layer_weights = [_layer_weights(P, l) for l in range(w_in.shape[0])]
    y_prompt = _trunk(x_prompt, c_prompt, P, layer_weights)
    y_sample = _trunk(x_sample, c_sample, P, layer_weights)
    return (y_prompt, y_sample)
```

```python
import functools

import numpy as np
import jax
import jax.numpy as jnp
from jax import lax
from jax.experimental import pallas as pl
from jax.experimental.pallas import tpu as pltpu
from jax.experimental.pallas import tpu_sc as plsc

F32 = jnp.float32
BF16 = jnp.bfloat16

D_MODEL = 1024
DEPTH = 2
GRID_W = 64
HEAD_DIM = 64
ROPE_THETA = 10000.0
ROPE_PAIRS_PER_AXIS = HEAD_DIM // 4

RET_HEADS = 4
RET_WIDTH = RET_HEADS * HEAD_DIM
RET_CHUNK = 128
RET_EPS = 1e-5

ATT_Q_HEADS = 8
ATT_KV_HEADS = 2
ATT_GROUP = ATT_Q_HEADS // ATT_KV_HEADS
ATT_Q_WIDTH = ATT_Q_HEADS * HEAD_DIM
ATT_KV_WIDTH = ATT_KV_HEADS * HEAD_DIM

RWKV_HEADS = 4
RWKV_WIDTH = RWKV_HEADS * HEAD_DIM
DECAY_LORA = 64
AAA_LORA = 64
GATE_LORA = 128
RWKV_GN_EPS = 64e-5
RW_CHUNK = 64
RW_FIELDS = 11
RW_BATCH_ROWS = 4

FLASH_Q_ROWS = 512
FLASH_KV_ROWS = 512
FLASH_SUB_ROWS = 1024

MOE_TOKENS = 512
MOE_GROUP_ROWS = 256
SC_GATHER_ROWS = 128

N_EXPERTS = 64
TOP_K = 8
N_GROUPS = 8
TOPK_GROUPS = 4
EXPERT_FF = 256
ROUTED_SCALE = 2.5
NORM_EPS = 1e-6

RET_COLS = 4 * RET_WIDTH
ATT_COLS = ATT_Q_WIDTH + 2 * ATT_KV_WIDTH
RW_COLS = 3 * RWKV_WIDTH + 2 * DECAY_LORA + 2 * AAA_LORA + GATE_LORA
GATE_COLS = 3 * D_MODEL

V7X_LANES = 128
VMEM_LIMIT_BYTES = 48 * 1024 * 1024


def _params(*dims):
    return pltpu.CompilerParams(dimension_semantics=dims, vmem_limit_bytes=VMEM_LIMIT_BYTES)


def _sigmoid(x):
    return 1.0 / (1.0 + jnp.exp(-x))


def _silu(x):
    return x * _sigmoid(x)


def _dot(a, b):
    return jnp.dot(a, b, preferred_element_type=F32)


def _dot_nt(a, b):
    return lax.dot_general(a, b, (((1,), (1,)), ((), ())), preferred_element_type=F32)


def _split(a):
    hi = a.astype(BF16)
    lo = (a - hi.astype(F32)).astype(BF16)
    return hi, lo


def _dot_split_lhs(a, b_bf16):
    hi, lo = _split(a)
    return _dot(hi, b_bf16) + _dot(lo, b_bf16)


def _dot3(a, b):
    ah, al = _split(a)
    bh, bl = _split(b)
    return _dot(ah, bh) + _dot(ah, bl) + _dot(al, bh)


def _norm_mod(x, gain, shift, scale):
    ms = jnp.mean(x * x, axis=-1, keepdims=True)
    return x * lax.rsqrt(ms + NORM_EPS) * gain * (1.0 + scale) + shift


def _head_layer_norm(y, ones_bd, eps):
    mean = _dot_split_lhs(y, ones_bd) * (1.0 / HEAD_DIM)
    yc = y - mean
    var = _dot_split_lhs(yc * yc, ones_bd) * (1.0 / HEAD_DIM)
    return yc * lax.rsqrt(var + eps)


def _ada_kernel(c_ref, w_ref, b_ref, o_ref):
    c = c_ref[...]
    o_ref[...] = _dot3(_silu(c), w_ref[...]) + b_ref[...]


def _ada_mod(c, ada_w, ada_b):
    B = c.shape[0]
    Bp = max(8, B)
    cp = jnp.zeros((Bp, D_MODEL), F32).at[:B].set(c)
    L = ada_w.shape[0]
    tn = 1536
    out = pl.pallas_call(
        _ada_kernel,
        grid=(L, 6 * D_MODEL // tn),
        in_specs=[pl.BlockSpec((Bp, D_MODEL), lambda l, j: (0, 0)),
                  pl.BlockSpec((None, D_MODEL, tn), lambda l, j: (l, 0, j)),
                  pl.BlockSpec((None, 1, tn), lambda l, j: (l, 0, j))],
        out_specs=pl.BlockSpec((None, Bp, tn), lambda l, j: (l, 0, j)),
        out_shape=jax.ShapeDtypeStruct((L, Bp, 6 * D_MODEL), F32),
        compiler_params=_params("parallel", "parallel"),
        name="ada_mod",
    )(cp, ada_w, ada_b.reshape(L, 1, 6 * D_MODEL))
    return out[:, :B].reshape(L, B, 6, D_MODEL)


def _proj_ret_kernel(x_ref, mod_ref, g_ref, w_ref, cos_ref, sin_ref, o_ref):
    h = _norm_mod(x_ref[...], g_ref[...], mod_ref[0:1, :], mod_ref[1:2, :]).astype(BF16)
    p = _dot(h, w_ref[...])
    W = RET_WIDTH
    cos = cos_ref[:, 0:W]
    sin = sin_ref[:, 0:W]
    o_ref[:, 0:W] = (p[:, 0:W] * cos + p[:, 4 * W:5 * W] * sin).astype(BF16)
    o_ref[:, W:2 * W] = ((p[:, W:2 * W] * cos + p[:, 5 * W:6 * W] * sin) * (HEAD_DIM ** -0.5)).astype(BF16)
    o_ref[:, 2 * W:3 * W] = p[:, 2 * W:3 * W].astype(BF16)
    o_ref[:, 3 * W:4 * W] = _silu(p[:, 3 * W:4 * W]).astype(BF16)


def _proj_att_kernel(x_ref, mod_ref, g_ref, w_ref, cos_ref, sin_ref, qkg_ref, ones_ref, q_ref, k_ref, v_ref):
    h = _norm_mod(x_ref[...], g_ref[...], mod_ref[0:1, :], mod_ref[1:2, :]).astype(BF16)
    p = _dot(h, w_ref[...])
    QW, KW = ATT_Q_WIDTH, ATT_KV_WIDTH
    q = p[:, 0:QW]
    k = p[:, QW:QW + KW]
    v = p[:, QW + KW:QW + 2 * KW]
    qs = p[:, QW + 2 * KW:2 * QW + 2 * KW]
    ks = p[:, 2 * QW + 2 * KW:2 * QW + 3 * KW]
    ones = ones_ref[...]
    rq = lax.rsqrt(_dot_split_lhs(q * q, ones) * (1.0 / HEAD_DIM) + NORM_EPS)
    rk = lax.rsqrt(_dot_split_lhs(k * k, ones[0:KW, 0:KW]) * (1.0 / HEAD_DIM) + NORM_EPS)
    cos = cos_ref[...]
    sin = sin_ref[...]
    qo = (q * qkg_ref[0:1, :] * cos + qs * qkg_ref[1:2, :] * sin) * (rq * (HEAD_DIM ** -0.5))
    ko = (k * qkg_ref[2:3, 0:KW] * cos[:, 0:KW] + ks * qkg_ref[3:4, 0:KW] * sin[:, 0:KW]) * rk
    q_ref[...] = qo.astype(BF16)
    kb = ko.astype(BF16)
    vb = v.astype(BF16)
    for hk in range(ATT_KV_HEADS):
        k_ref[hk] = kb[:, hk * HEAD_DIM:(hk + 1) * HEAD_DIM]
        v_ref[hk] = vb[:, hk * HEAD_DIM:(hk + 1) * HEAD_DIM]


def _proj_gate_kernel(x_ref, mod_ref, g_ref, w_ref, o_ref):
    h = _norm_mod(x_ref[...], g_ref[...], mod_ref[0:1, :], mod_ref[1:2, :]).astype(BF16)
    o_ref[...] = _sigmoid(_dot(h, w_ref[...])).astype(BF16)


def _proj_rw_kernel(x_ref, xp_ref, xn_ref, mod_ref, g_ref, w_ref, mu_ref, vec_ref, w2_ref, a2_ref, g2_ref,
                    ones_ref, o_ref, p_scr, *, tm, tiles_per_seq):
    i = pl.program_id(0)
    gain, shift, scale = g_ref[...], mod_ref[0:1, :], mod_ref[1:2, :]
    w = w_ref[...]
    h = _norm_mod(x_ref[...], gain, shift, scale).astype(BF16)
    p = _dot(h, w)
    hp = _norm_mod(xp_ref[...], gain, shift, scale).astype(BF16)
    hn = _norm_mod(xn_ref[...], gain, shift, scale).astype(BF16)
    first = (i % tiles_per_seq) == 0
    last = (i % tiles_per_seq) == tiles_per_seq - 1
    prev_row = jnp.where(first, 0.0, _dot(hp, w)[7:8, :])
    next_row = jnp.where(last, 0.0, _dot(hn, w)[0:1, :])
    p_scr[8:8 + tm, :] = p
    p_scr[7:8, :] = prev_row
    p_scr[8 + tm:9 + tm, :] = next_row
    prev = p_scr[7:7 + tm, :]
    nxt = p_scr[9:9 + tm, :]
    x = p + mu_ref[0:1, :] * (prev - p) + mu_ref[1:2, :] * (nxt - p)

    W = RWKV_WIDTH
    r, k, v = x[:, 0:W], x[:, W:2 * W], x[:, 2 * W:3 * W]
    xw = x[:, 3 * W:3 * W + 2 * DECAY_LORA]
    xa = x[:, 3 * W + 2 * DECAY_LORA:3 * W + 2 * DECAY_LORA + 2 * AAA_LORA]
    xg = x[:, 3 * W + 2 * DECAY_LORA + 2 * AAA_LORA:]
    w0_f, w0_b = vec_ref[0:1, :], vec_ref[1:2, :]
    a0_f, a0_b = vec_ref[2:3, :], vec_ref[3:4, :]
    k_k, k_a, r_k = vec_ref[4:5, :], vec_ref[5:6, :], vec_ref[6:7, :]
    wl = _dot(jnp.tanh(xw).astype(BF16), w2_ref[...])
    al = _dot(xa.astype(BF16), a2_ref[...])
    dec_c = float(np.exp(-0.5))
    w_f = jnp.exp(-dec_c * _sigmoid(w0_f + wl[:, 0:W]))
    w_b = jnp.exp(-dec_c * _sigmoid(w0_b + wl[:, W:2 * W]))
    a_f = _sigmoid(a0_f + al[:, 0:W])
    a_b = _sigmoid(a0_b + al[:, W:2 * W])
    gate = _dot(_sigmoid(xg).astype(BF16), g2_ref[...])
    ones = ones_ref[...]
    kk = k * k_k
    kk = kk * lax.rsqrt(_dot_split_lhs(kk * kk, ones) + 1e-12)
    k_f = k * (1.0 + (a_f - 1.0) * k_a)
    k_b = k * (1.0 + (a_b - 1.0) * k_a)
    bonus = _dot_split_lhs(r * k_f * r_k, ones) * v
    for j, val in enumerate((r, kk, v, w_f, w_b, k_f, k_b, kk * a_f, kk * a_b, gate, bonus)):
        o_ref[:, j * W:(j + 1) * W] = val


def _token_specs(tm, tiles_per_seq):
    x_spec = pl.BlockSpec((tm, D_MODEL), lambda i: (i, 0))
    mod_spec = pl.BlockSpec((None, 6, D_MODEL), lambda i: (i // tiles_per_seq, 0, 0))
    g_spec = pl.BlockSpec((1, D_MODEL), lambda i: (0, 0))
    return x_spec, mod_spec, g_spec


def _full(shape):
    nd = len(shape)
    return pl.BlockSpec(shape, lambda *_: (0,) * nd)


def _input_projections(x, mod, T, lw, tabs):
    N = x.shape[0]
    tm = min(512, T)
    tps = T // tm
    grid = (N // tm,)
    x_spec, mod_spec, g_spec = _token_specs(tm, tps)
    tab_spec = pl.BlockSpec((tm, ATT_Q_WIDTH), lambda i: (i % tps, 0))
    cos, sin = tabs["cos"], tabs["sin"]

    ret = pl.pallas_call(
        _proj_ret_kernel, grid=grid,
        in_specs=[x_spec, mod_spec, g_spec, _full(lw["w_ret"].shape), tab_spec, tab_spec],
        out_specs=pl.BlockSpec((tm, RET_COLS), lambda i: (i, 0)),
        out_shape=jax.ShapeDtypeStruct((N, RET_COLS), BF16),
        compiler_params=_params("parallel"), name="proj_ret",
    )(x, mod, lw["norm1_g"], lw["w_ret"], cos, sin)

    q, k, v = pl.pallas_call(
        _proj_att_kernel, grid=grid,
        in_specs=[x_spec, mod_spec, g_spec, _full(lw["w_att"].shape), tab_spec, tab_spec,
                  _full(lw["qk_gain"].shape), _full(tabs["ones512"].shape)],
        out_specs=[pl.BlockSpec((tm, ATT_Q_WIDTH), lambda i: (i, 0)),
                   pl.BlockSpec((ATT_KV_HEADS, tm, HEAD_DIM), lambda i: (0, i, 0)),
                   pl.BlockSpec((ATT_KV_HEADS, tm, HEAD_DIM), lambda i: (0, i, 0))],
        out_shape=[jax.ShapeDtypeStruct((N, ATT_Q_WIDTH), BF16),
                   jax.ShapeDtypeStruct((ATT_KV_HEADS, N, HEAD_DIM), BF16),
                   jax.ShapeDtypeStruct((ATT_KV_HEADS, N, HEAD_DIM), BF16)],
        compiler_params=_params("parallel"), name="proj_att",
    )(x, mod, lw["norm1_g"], lw["w_att"], cos, sin, lw["qk_gain"], tabs["ones512"])

    gates = pl.pallas_call(
        _proj_gate_kernel, grid=grid,
        in_specs=[x_spec, mod_spec, g_spec, _full(lw["w_gate"].shape)],
        out_specs=pl.BlockSpec((tm, GATE_COLS), lambda i: (i, 0)),
        out_shape=jax.ShapeDtypeStruct((N, GATE_COLS), BF16),
        compiler_params=_params("parallel"), name="proj_gate",
    )(x, mod, lw["norm1_g"], lw["w_gate"])

    tm_rw = min(256, T)
    tps_rw = T // tm_rw
    x_spec, mod_spec, g_spec = _token_specs(tm_rw, tps_rw)
    rows8 = tm_rw // 8
    last8 = N // 8 - 1
    rwp = pl.pallas_call(
        functools.partial(_proj_rw_kernel, tm=tm_rw, tiles_per_seq=tps_rw), grid=(N // tm_rw,),
        in_specs=[x_spec,
                  pl.BlockSpec((8, D_MODEL), lambda i: (jnp.maximum(i * rows8 - 1, 0), 0)),
                  pl.BlockSpec((8, D_MODEL), lambda i: (jnp.minimum((i + 1) * rows8, last8), 0)),
                  mod_spec, g_spec, _full(lw["w_rw"].shape), _full(lw["shift_mu"].shape),
                  _full(lw["rw_vec"].shape), _full(lw["rw_w2"].shape), _full(lw["rw_a2"].shape),
                  _full(lw["rw_g2"].shape), _full(tabs["ones256"].shape)],
        out_specs=pl.BlockSpec((tm_rw, RW_FIELDS * RWKV_WIDTH), lambda i: (i, 0)),
        out_shape=jax.ShapeDtypeStruct((N, RW_FIELDS * RWKV_WIDTH), F32),
        scratch_shapes=[pltpu.VMEM((tm_rw + 16, RW_COLS), F32)],
        compiler_params=_params("parallel"), name="proj_rw",
    )(x, x, x, mod, lw["norm1_g"], lw["w_rw"], lw["shift_mu"], lw["rw_vec"], lw["rw_w2"], lw["rw_a2"],
      lw["rw_g2"], tabs["ones256"])
    return ret, q, k, v, gates, rwp


def _retention_tables(reverse):
    lg = np.log1p(-np.exp2(-5.0 - np.arange(RET_HEADS, dtype=np.float64)))
    if reverse:
        lg = lg[::-1]
    C = RET_CHUNK
    pos = np.arange(C, dtype=np.float64)
    diff = pos[:, None] - pos[None, :]
    if reverse:
        dec = np.where(diff < 0, np.exp(lg[:, None, None] * np.maximum(-diff, 0.0)[None]), 0.0)
        xi = np.exp(lg[:, None] * (C - pos)[None, :])
        zeta = np.exp(lg[:, None] * pos[None, :])
    else:
        dec = np.where(diff >= 0, np.exp(lg[:, None, None] * np.maximum(diff, 0.0)[None]), 0.0)
        xi = np.exp(lg[:, None] * (pos + 1.0)[None, :])
        zeta = np.exp(lg[:, None] * (C - 1.0 - pos)[None, :])
    chunk_decay = tuple(float(v) for v in np.exp(lg * C))
    widen = lambda a: jnp.asarray(np.repeat(a.T, HEAD_DIM, axis=1), F32)
    return jnp.asarray(dec, F32), widen(xi), widen(zeta), chunk_decay


def _retention_kernel(ret_ref, dec_ref, xi_ref, zeta_ref, *rest, chunk_decay, final):
    if final:
        yf_ref, gn_ref, ones_ref, o_ref, state, y_scr = rest
    else:
        o_ref, state, y_scr = rest

    @pl.when(pl.program_id(1) == 0)
    def _():
        state[...] = jnp.zeros_like(state)

    W = RET_WIDTH
    q = ret_ref[:, 0:W]
    k = ret_ref[:, W:2 * W]
    v = ret_ref[:, 2 * W:3 * W]
    qx = (q.astype(F32) * xi_ref[...]).astype(BF16)
    kzt = (k.astype(F32) * zeta_ref[...]).T
    for h in range(RET_HEADS):
        sl = slice(h * HEAD_DIM, (h + 1) * HEAD_DIM)
        s = _dot_nt(q[:, sl], k[:, sl]) * dec_ref[h]
        inner = _dot(s.astype(BF16), v[:, sl])
        r_prev = state[h]
        cross = _dot(qx[:, sl], r_prev.astype(BF16))
        kv = _dot(kzt[sl, :].astype(BF16), v[:, sl])
        state[h] = r_prev * chunk_decay[h] + kv
        y_scr[:, sl] = inner + cross
    y = y_scr[...]
    if final:
        y = _head_layer_norm(y + yf_ref[...], ones_ref[...], RET_EPS) * gn_ref[...]
        o_ref[...] = (y * ret_ref[:, 3 * W:4 * W].astype(F32)).astype(BF16)
    else:
        o_ref[...] = y


def _retention(ret, B, T, ret_gn, ones256):
    N = ret.shape[0]
    C = RET_CHUNK
    nc = T // C
    scratch = [pltpu.VMEM((RET_HEADS, HEAD_DIM, HEAD_DIM), F32), pltpu.VMEM((C, RET_WIDTH), F32)]

    dec, xi, zeta, cd = _retention_tables(False)
    fwd_row = lambda b, c: (b * nc + c, 0)
    yf = pl.pallas_call(
        functools.partial(_retention_kernel, chunk_decay=cd, final=False), grid=(B, nc),
        in_specs=[pl.BlockSpec((C, RET_COLS), fwd_row), _full(dec.shape), _full(xi.shape), _full(zeta.shape)],
        out_specs=pl.BlockSpec((C, RET_WIDTH), fwd_row),
        out_shape=jax.ShapeDtypeStruct((N, RET_WIDTH), F32),
        scratch_shapes=scratch, compiler_params=_params("parallel", "arbitrary"), name="retention_fwd",
    )(ret, dec, xi, zeta)

    dec, xi, zeta, cd = _retention_tables(True)
    bwd_row = lambda b, c: (b * nc + nc - 1 - c, 0)
    return pl.pallas_call(
        functools.partial(_retention_kernel, chunk_decay=cd, final=True), grid=(B, nc),
        in_specs=[pl.BlockSpec((C, RET_COLS), bwd_row), _full(dec.shape), _full(xi.shape), _full(zeta.shape),
                  pl.BlockSpec((C, RET_WIDTH), bwd_row), _full(ret_gn.shape), _full(ones256.shape)],
        out_specs=pl.BlockSpec((C, RET_WIDTH), bwd_row),
        out_shape=jax.ShapeDtypeStruct((N, RET_WIDTH), BF16),
        scratch_shapes=scratch, compiler_params=_params("parallel", "arbitrary"), name="retention_bwd",
    )(ret, dec, xi, zeta, yf, ret_gn, ones256)


def _flash_kernel(q_ref, k_ref, v_ref, o_ref, q_scr, m_scr, l_scr, acc_scr, *, tq, tk):
    j = pl.program_id(3)

    @pl.when(j == 0)
    def _():
        for g in range(ATT_GROUP):
            q_scr[g * tq:(g + 1) * tq, :] = q_ref[:, g * HEAD_DIM:(g + 1) * HEAD_DIM]
        m_scr[...] = jnp.full_like(m_scr, -jnp.inf)
        l_scr[...] = jnp.zeros_like(l_scr)
        acc_scr[...] = jnp.zeros_like(acc_scr)

    k = k_ref[...]
    v = v_ref[...]
    for r0 in range(0, ATT_GROUP * tq, FLASH_SUB_ROWS):
        rows = slice(r0, r0 + FLASH_SUB_ROWS)
        s = _dot_nt(q_scr[rows, :], k)
        m_prev = m_scr[rows, :]
        m_next = jnp.maximum(m_prev, jnp.max(s, axis=1, keepdims=True))
        p = jnp.exp(s - jnp.concatenate([m_next] * (tk // V7X_LANES), axis=1))
        alpha = jnp.exp(m_prev - m_next)
        l_scr[rows, :] = alpha * l_scr[rows, :] + jnp.sum(p, axis=1, keepdims=True)
        m_scr[rows, :] = m_next
        acc_scr[rows, :] = alpha[:, 0:HEAD_DIM] * acc_scr[rows, :] + _dot(p.astype(BF16), v)

    @pl.when(j == pl.num_programs(3) - 1)
    def _():
        o = acc_scr[...] / l_scr[:, 0:HEAD_DIM]
        for g in range(ATT_GROUP):
            o_ref[:, g * HEAD_DIM:(g + 1) * HEAD_DIM] = o[g * tq:(g + 1) * tq, :].astype(BF16)


def _attention(q, k, v, B, T):
    N = q.shape[0]
    tq = min(FLASH_Q_ROWS, T)
    tk = min(FLASH_KV_ROWS, T)
    nq, nk = T // tq, T // tk
    GW = ATT_GROUP * HEAD_DIM
    rows = ATT_GROUP * tq
    return pl.pallas_call(
        functools.partial(_flash_kernel, tq=tq, tk=tk), grid=(B, ATT_KV_HEADS, nq, nk),
        in_specs=[pl.BlockSpec((tq, GW), lambda b, h, i, j: (b * nq + i, h)),
                  pl.BlockSpec((None, tk, HEAD_DIM), lambda b, h, i, j: (h, b * nk + j, 0)),
                  pl.BlockSpec((None, tk, HEAD_DIM), lambda b, h, i, j: (h, b * nk + j, 0))],
        out_specs=pl.BlockSpec((tq, GW), lambda b, h, i, j: (b * nq + i, h)),
        out_shape=jax.ShapeDtypeStruct((N, ATT_Q_WIDTH), BF16),
        scratch_shapes=[pltpu.VMEM((rows, HEAD_DIM), BF16), pltpu.VMEM((rows, V7X_LANES), F32),
                        pltpu.VMEM((rows, V7X_LANES), F32), pltpu.VMEM((rows, HEAD_DIM), F32)],
        compiler_params=_params("parallel", "parallel", "parallel", "arbitrary"), name="flash_attention",
    )(q, k, v)


def _rwkv_scan_kernel(kk_f, r_f, w_f, k_f, b_f, vt_f, kk_b, r_b, w_b, k_b, b_b, vt_b, ones_ref,
                      yf_ref, yb_ref, state, vk_scr, *, nb):
    C = RW_CHUNK
    H = C // 2
    W = RWKV_WIDTH
    SUB = 8

    @pl.when(pl.program_id(1) == 0)
    def _():
        state[...] = jnp.zeros_like(state)

    ones = ones_ref[...]
    lane = lax.broadcasted_iota(jnp.int32, (1, W), 1) & (HEAD_DIM - 1)
    same_head = (lax.broadcasted_iota(jnp.int32, (W, W), 0) // HEAD_DIM
                 == lax.broadcasted_iota(jnp.int32, (W, W), 1) // HEAD_DIM)
    step_iota = lax.broadcasted_iota(jnp.int32, (H, HEAD_DIM, W), 0)
    lane_iota = lax.broadcasted_iota(jnp.int32, (H, HEAD_DIM, W), 2) & (HEAD_DIM - 1)
    dirs = ((kk_f, r_f, w_f, k_f, b_f, vt_f), (kk_b, r_b, w_b, k_b, b_b, vt_b))
    chains = [(n, d) for n in range(nb) for d in range(2)]

    def head_sums(vals):
        stacked = jnp.concatenate([v.astype(BF16) for v in vals], axis=0)
        out = _dot(stacked, ones)
        return [out[c * HEAD_DIM:(c + 1) * HEAD_DIM, :] for c in range(len(vals))]

    carry = (tuple(state[c] for c in range(len(chains))),
             tuple(jnp.zeros((HEAD_DIM, W), F32) for _ in chains))
    for half in range(2):
        lo = (half * H, (1 - half) * H)
        for c, (n, d) in enumerate(chains):
            kc = dirs[d][3][n]
            kbd = jnp.where(same_head, jnp.concatenate([kc] * (W // C), axis=0), 0.0).astype(BF16)
            sel = jnp.where(step_iota + lo[d] == lane_iota, dirs[d][5][n][None], 0.0).reshape(H * HEAD_DIM, W)
            vk_scr[c] = _dot(sel.astype(BF16), kbd).reshape(H, HEAD_DIM, W)

        def group(g, carry, lo=lo):
            states, ycols = list(carry[0]), list(carry[1])
            bases = (lo[0] + g * SUB, lo[1] + (H // SUB - 1 - g) * SUB)
            rows = [[ref[n, pl.ds(pl.multiple_of(bases[d], SUB), SUB), :] for ref in dirs[d][:5]]
                    for (n, d) in chains]
            for j in range(SUB):
                jj = (j, SUB - 1 - j)
                row = lambda c, f: rows[c][f][jj[chains[c][1]]:jj[chains[c][1]] + 1, :]
                sa = head_sums([states[c] * row(c, 0) for c in range(len(chains))])
                new = [states[c] * row(c, 2) - sa[c] * row(c, 4)
                       + vk_scr[c, bases[chains[c][1]] + jj[chains[c][1]] - lo[chains[c][1]]]
                       for c in range(len(chains))]
                read = [new[c] if chains[c][1] == 0 else states[c] for c in range(len(chains))]
                y = head_sums([read[c] * row(c, 1) for c in range(len(chains))])
                for c, (n, d) in enumerate(chains):
                    ycols[c] = jnp.where(lane == bases[d] + jj[d], y[c], ycols[c])
                states = new
            return tuple(states), tuple(ycols)

        carry = lax.fori_loop(0, H // SUB, group, carry)

    states, ycols = carry
    for c, (n, d) in enumerate(chains):
        state[c] = states[c]
        (yf_ref if d == 0 else yb_ref)[n] = ycols[c]


def _rwkv_scan(rwp, B, T, ones256):
    W, C = RWKV_WIDTH, RW_CHUNK
    nc = T // C
    nb = RW_BATCH_ROWS if B % RW_BATCH_ROWS == 0 else 1
    fields = rwp.reshape(B, T, RW_FIELDS * W)
    v = rwp[:, 2 * W:3 * W].reshape(B, nc, C, RWKV_HEADS, HEAD_DIM)
    vt = v.transpose(0, 1, 4, 3, 2).reshape(B, nc, HEAD_DIM, W)

    fwd = lambda f: pl.BlockSpec((nb, C, W), lambda b, c: (b, c, f))
    bwd = lambda f: pl.BlockSpec((nb, C, W), lambda b, c: (b, nc - 1 - c, f))
    vt_fwd = pl.BlockSpec((nb, None, HEAD_DIM, W), lambda b, c: (b, c, 0, 0))
    vt_bwd = pl.BlockSpec((nb, None, HEAD_DIM, W), lambda b, c: (b, nc - 1 - c, 0, 0))
    yt_shape = jax.ShapeDtypeStruct((B, nc, HEAD_DIM, W), F32)
    yt_f, yt_b = pl.pallas_call(
        functools.partial(_rwkv_scan_kernel, nb=nb), grid=(B // nb, nc),
        in_specs=[fwd(1), fwd(0), fwd(3), fwd(5), fwd(7), vt_fwd,
                  bwd(1), bwd(0), bwd(4), bwd(6), bwd(8), vt_bwd, _full(ones256.shape)],
        out_specs=[vt_fwd, vt_bwd], out_shape=[yt_shape, yt_shape],
        scratch_shapes=[pltpu.VMEM((nb * 2, HEAD_DIM, W), F32),
                        pltpu.VMEM((nb * 2, C // 2, HEAD_DIM, W), F32)],
        compiler_params=_params("parallel", "arbitrary"), name="rwkv_scan",
    )(fields, fields, fields, fields, fields, vt, fields, fields, fields, fields, fields, vt, ones256)

    untranspose = lambda yt: yt.reshape(B, nc, HEAD_DIM, RWKV_HEADS, C).transpose(0, 1, 4, 3, 2).reshape(B * T, W)
    return untranspose(yt_f), untranspose(yt_b)


def _merge_kernel(x_ref, mod_ref, ya_ref, yb_ref, yf_ref, ybk_ref, rg_ref, bonus_ref, gn_ref, gates_ref,
                  ones_ref, wa_ref, wb_ref, wc_ref, wo_ref, o_ref):
    y = _head_layer_norm(yf_ref[...] + ybk_ref[...], ones_ref[...], RWKV_GN_EPS) * gn_ref[...]
    yc = ((y + bonus_ref[...]) * rg_ref[...]).astype(BF16)
    D = D_MODEL
    merged = (gates_ref[:, 0:D].astype(F32) * _dot(ya_ref[...], wa_ref[...])
              + gates_ref[:, D:2 * D].astype(F32) * _dot(yb_ref[...], wb_ref[...])
              + gates_ref[:, 2 * D:3 * D].astype(F32) * _dot(yc, wc_ref[...]))
    o_ref[...] = x_ref[...] + mod_ref[2:3, :] * _dot(merged.astype(BF16), wo_ref[...])


def _merge(x, mod, T, ya, yb, yf, ybk, rwp, gates, lw, ones256):
    N = x.shape[0]
    tm = min(512, T)
    tps = T // tm
    x_spec, mod_spec, _ = _token_specs(tm, tps)
    row = lambda w: pl.BlockSpec((tm, w), lambda i: (i, 0))
    field = lambda f: pl.BlockSpec((tm, RWKV_WIDTH), lambda i: (i, f))
    return pl.pallas_call(
        _merge_kernel, grid=(N // tm,),
        in_specs=[x_spec, mod_spec, row(RET_WIDTH), row(ATT_Q_WIDTH), row(RWKV_WIDTH), row(RWKV_WIDTH),
                  field(9), field(10), _full(lw["rw_gn"].shape), row(GATE_COLS), _full(ones256.shape),
                  _full(lw["w_branch_a"].shape), _full(lw["w_branch_b"].shape), _full(lw["w_branch_c"].shape),
                  _full(lw["w_out"].shape)],
        out_specs=x_spec, out_shape=jax.ShapeDtypeStruct((N, D_MODEL), F32),
        compiler_params=_params("parallel"), name="merge_out",
    )(x, mod, ya, yb, yf, ybk, rwp, rwp, lw["rw_gn"], gates, ones256,
      lw["w_branch_a"], lw["w_branch_b"], lw["w_branch_c"], lw["w_out"])


def _route(h, hb, rwt_ref, rbias_ref, tm):
    h_lo = (h - hb.astype(F32)).astype(BF16)
    rw_hi, rw_lo = _split(rwt_ref[...])
    logits = _dot_nt(rw_hi, hb) + _dot_nt(rw_hi, h_lo) + _dot_nt(rw_lo, hb)
    scores = _sigmoid(logits)
    choice = scores + jnp.concatenate([rbias_ref[...]] * (tm // V7X_LANES), axis=1)
    per_group = N_EXPERTS // N_GROUPS
    sub = lax.broadcasted_iota(jnp.int32, (per_group, tm), 0)
    groups, gscore = [], []
    for g in range(N_GROUPS):
        cg = choice[g * per_group:(g + 1) * per_group, :]
        m1 = jnp.max(cg, axis=0, keepdims=True)
        first = jnp.min(jnp.where(cg == m1, sub, per_group), axis=0, keepdims=True)
        m2 = jnp.max(jnp.where(sub == first, -jnp.inf, cg), axis=0, keepdims=True)
        groups.append(cg)
        gscore.append(m1 + m2)
    masked = []
    for g in range(N_GROUPS):
        beaten = jnp.zeros((1, tm), F32)
        for o in range(N_GROUPS):
            if o == g:
                continue
            wins = (gscore[o] >= gscore[g]) if o < g else (gscore[o] > gscore[g])
            beaten = beaten + jnp.where(wins, 1.0, 0.0)
        keep = jnp.where(beaten < TOPK_GROUPS, 1.0, 0.0)
        masked.append(jnp.where(jnp.broadcast_to(keep, (per_group, tm)) > 0.5, groups[g], -jnp.inf))
    mc = jnp.concatenate(masked, axis=0)
    eidx = lax.broadcasted_iota(jnp.int32, (N_EXPERTS, tm), 0)
    beaten = jnp.zeros((N_EXPERTS, tm), F32)
    for o in range(N_EXPERTS):
        row = jnp.broadcast_to(mc[o:o + 1, :], (N_EXPERTS, tm))
        tie = jnp.where(eidx > o, 1.0, 0.0)
        beaten = beaten + jnp.where(row > mc, 1.0, jnp.where(row == mc, tie, 0.0))
    sel = jnp.where(beaten < TOP_K, 1.0, 0.0)
    wts = jnp.where(beaten < TOP_K, scores, 0.0)
    return wts / jnp.sum(wts, axis=0, keepdims=True) * ROUTED_SCALE, sel


def _selected_lists(comb, sel, rank):
    eidx = lax.broadcasted_iota(jnp.int32, comb.shape, 0).astype(F32)
    prev = jnp.full((1, comb.shape[1]), -1.0, F32)
    ids, wts, pos = [], [], []
    for _ in range(TOP_K):
        cand = jnp.where(sel > 0.5, jnp.where(eidx > prev, eidx, float(N_EXPERTS)), float(N_EXPERTS))
        prev = jnp.min(cand, axis=0, keepdims=True)
        wts.append(jnp.sum(jnp.where(eidx == prev, comb, 0.0), axis=0, keepdims=True))
        pos.append(jnp.sum(jnp.where(eidx == prev, rank, 0.0), axis=0, keepdims=True))
        ids.append(jnp.minimum(prev, N_EXPERTS - 1.0))
    as_int = lambda rows: jnp.concatenate(rows, axis=0).astype(jnp.int32)
    return as_int(ids), jnp.concatenate(wts, axis=0), as_int(pos)


def _pack_halves(y):
    n = y.shape[1] // 2
    hi = pltpu.bitcast(y[:, :n].astype(BF16).astype(F32), jnp.int32)
    lo = pltpu.bitcast(y[:, n:].astype(BF16).astype(F32), jnp.int32)
    return hi | lax.shift_right_logical(lo, 16)


def _unpack_halves(w):
    hi = pltpu.bitcast(w & jnp.int32(-65536), F32)
    lo = pltpu.bitcast(lax.shift_left(w, 16), F32)
    return hi, lo


def _moe_route_kernel(x_ref, mod_ref, g_ref, rwt_ref, rbias_ref, shgu_ref, shd_ref, tri_ref,
                      hp_ref, ids_ref, wts_ref, pos_ref, base_ref, count_ref, *, tm):
    F = EXPERT_FF

    @pl.when(pl.program_id(0) == 0)
    def _():
        count_ref[...] = jnp.zeros_like(count_ref)

    x = x_ref[...]
    h = _norm_mod(x, g_ref[...], mod_ref[3:4, :], mod_ref[4:5, :])
    hb = h.astype(BF16)
    hp_ref[...] = _pack_halves(hb.astype(F32))
    comb, sel = _route(h, hb, rwt_ref, rbias_ref, tm)
    before = count_ref[...]
    inclusive = _dot(sel.astype(BF16), tri_ref[...])
    rank = inclusive - sel + jnp.concatenate([before] * (tm // V7X_LANES), axis=1)
    count_ref[...] = before + jnp.sum(sel, axis=1, keepdims=True)
    ids, wts, pos = _selected_lists(comb, sel, rank)
    ids_ref[...] = ids
    wts_ref[...] = wts
    pos_ref[...] = pos
    gu = _dot(hb, shgu_ref[...])
    shared = _dot((_silu(gu[:, 0:F]) * gu[:, F:2 * F]).astype(BF16), shd_ref[...])
    base_ref[...] = x + mod_ref[5:6, :] * shared


def _moe_expert_kernel(tile_expert_ref, xs_ref, wgu_ref, wd_ref, o_ref):
    del tile_expert_ref
    F = EXPERT_FF
    half = D_MODEL // 2
    hi, lo = _unpack_halves(xs_ref[...])
    gu = _dot(hi.astype(BF16), wgu_ref[0:half, :]) + _dot(lo.astype(BF16), wgu_ref[half:, :])
    act = (_silu(gu[:, 0:F]) * gu[:, F:2 * F]).astype(BF16)
    o_ref[...] = _pack_halves(_dot(act, wd_ref[...]))


def _moe_combine_kernel(base_ref, mod_ref, y_ref, w_ref, fin_ref, o_ref, *, final):
    half = D_MODEL // 2
    acc_hi = jnp.zeros((base_ref.shape[0], half), F32)
    acc_lo = jnp.zeros((base_ref.shape[0], half), F32)
    for k in range(TOP_K):
        hi, lo = _unpack_halves(y_ref[k])
        wk = w_ref[:, k:k + 1]
        acc_hi = acc_hi + wk * hi
        acc_lo = acc_lo + wk * lo
    out_hi = base_ref[:, 0:half] + mod_ref[5:6, 0:half] * acc_hi
    out_lo = base_ref[:, half:] + mod_ref[5:6, half:] * acc_lo
    if final:
        ms = (jnp.sum(out_hi * out_hi, axis=-1, keepdims=True)
              + jnp.sum(out_lo * out_lo, axis=-1, keepdims=True)) * (1.0 / D_MODEL)
        scale = lax.rsqrt(ms + NORM_EPS)
        out_hi = out_hi * scale * fin_ref[:, 0:half]
        out_lo = out_lo * scale * fin_ref[:, half:]
    o_ref[:, 0:half] = out_hi
    o_ref[:, half:] = out_lo


def _gather_rows(table, idx):
    rows, width = idx.shape[0], table.shape[1]
    info = plsc.get_sparse_core_info()
    workers = info.num_cores * info.num_subcores
    windows = rows // (workers * SC_GATHER_ROWS)
    assert windows * workers * SC_GATHER_ROWS == rows
    mesh = plsc.VectorSubcoreMesh(core_axis_name="core", subcore_axis_name="subcore")

    @functools.partial(pl.kernel, out_type=jax.ShapeDtypeStruct((rows, width), table.dtype), mesh=mesh,
                       scratch_types=[pltpu.VMEM((SC_GATHER_ROWS,), jnp.int32),
                                      pltpu.VMEM((SC_GATHER_ROWS, width), table.dtype)],
                       name="moe_gather")
    def gather(table_hbm, idx_hbm, out_hbm, idx_vmem, rows_vmem):
        worker = lax.axis_index("subcore") * info.num_cores + lax.axis_index("core")
        base = worker * (windows * SC_GATHER_ROWS)

        @pl.loop(0, windows)
        def _(j):
            span = pl.ds(pl.multiple_of(base + j * SC_GATHER_ROWS, SC_GATHER_ROWS), SC_GATHER_ROWS)
            pltpu.sync_copy(idx_hbm.at[span], idx_vmem)
            pltpu.sync_copy(table_hbm.at[idx_vmem], rows_vmem)
            pltpu.sync_copy(rows_vmem, out_hbm.at[span])

    return gather(table, idx)


def _scatter_rows(table, dest, total_rows):
    n_lists, n_rows = dest.shape
    width = table.shape[1]
    info = plsc.get_sparse_core_info()
    workers = info.num_cores * info.num_subcores
    windows = n_rows // (workers * SC_GATHER_ROWS)
    assert windows * workers * SC_GATHER_ROWS == n_rows
    windows_per_list = n_rows // SC_GATHER_ROWS
    mesh = plsc.VectorSubcoreMesh(core_axis_name="core", subcore_axis_name="subcore")

    @functools.partial(pl.kernel, out_type=jax.ShapeDtypeStruct((total_rows, width), table.dtype), mesh=mesh,
                       scratch_types=[pltpu.VMEM((1, SC_GATHER_ROWS), jnp.int32),
                                      pltpu.VMEM((SC_GATHER_ROWS, width), table.dtype)],
                       name="moe_scatter")
    def scatter(table_hbm, dest_hbm, out_hbm, idx_vmem, rows_vmem):
        worker = lax.axis_index("subcore") * info.num_cores + lax.axis_index("core")

        @pl.loop(0, windows)
        def _(j):
            window = worker * windows + j
            span = pl.ds(pl.multiple_of(window * SC_GATHER_ROWS, SC_GATHER_ROWS), SC_GATHER_ROWS)
            pltpu.sync_copy(table_hbm.at[span], rows_vmem)
            for k in range(n_lists):
                pltpu.sync_copy(dest_hbm.at[pl.ds(k * windows_per_list + window, 1)], idx_vmem)
                pltpu.sync_copy(rows_vmem, out_hbm.at[idx_vmem.at[0]])

    return scatter(table, dest.reshape(n_lists * windows_per_list, SC_GATHER_ROWS))


def _dispatch_plan(ids, pos, counts, n_tokens):
    E, tg = N_EXPERTS, MOE_GROUP_ROWS
    padded = (counts + tg - 1) // tg * tg
    ends = jnp.cumsum(padded)
    off = ends - padded
    onehot = ids[:, :, None] == jnp.arange(E, dtype=jnp.int32)[None, None, :]
    dest = pos + jnp.sum(jnp.where(onehot, off[None, None, :], 0), axis=-1)
    total = n_tokens * TOP_K + E * tg
    tile_start = jnp.arange(total // tg, dtype=jnp.int32) * tg
    tile_expert = jnp.minimum(jnp.searchsorted(ends, tile_start, side="right"), E - 1).astype(jnp.int32)
    return dest, total, tile_expert


def _moe(x, mod, T, lw, final_g, final):
    N = x.shape[0]
    tm = min(MOE_TOKENS, T)
    tps = T // tm
    D, F, E, tg = D_MODEL, EXPERT_FF, N_EXPERTS, MOE_GROUP_ROWS
    half = D // 2
    x_spec, mod_spec, g_spec = _token_specs(tm, tps)
    list_spec = pl.BlockSpec((TOP_K, tm), lambda i: (0, i))

    tri = jnp.asarray(np.triu(np.ones((tm, tm), np.float32)), BF16)
    hp, ids, wts, pos, base, counts = pl.pallas_call(
        functools.partial(_moe_route_kernel, tm=tm), grid=(N // tm,),
        in_specs=[x_spec, mod_spec, g_spec, _full((E, D)), _full((E, V7X_LANES)), _full((D, 2 * F)), _full((F, D)),
                  _full((tm, tm))],
        out_specs=[pl.BlockSpec((tm, half), lambda i: (i, 0)), list_spec, list_spec, list_spec, x_spec,
                   _full((E, V7X_LANES))],
        out_shape=[jax.ShapeDtypeStruct((N, half), jnp.int32), jax.ShapeDtypeStruct((TOP_K, N), jnp.int32),
                   jax.ShapeDtypeStruct((TOP_K, N), F32), jax.ShapeDtypeStruct((TOP_K, N), jnp.int32),
                   jax.ShapeDtypeStruct((N, D), F32), jax.ShapeDtypeStruct((E, V7X_LANES), F32)],
        compiler_params=_params("arbitrary"), name="moe_route",
    )(x, mod, lw["norm2_g"], lw["router_wt"], lw["router_bias"], lw["sh_gu"], lw["sh_d"], tri)

    dest, total, tile_expert = _dispatch_plan(ids, pos, counts[:, 0].astype(jnp.int32), N)
    xs = _scatter_rows(hp, dest, total)
    n_tiles = total // tg
    ys = pl.pallas_call(
        _moe_expert_kernel,
        grid_spec=pltpu.PrefetchScalarGridSpec(
            num_scalar_prefetch=1, grid=(n_tiles,),
            in_specs=[pl.BlockSpec((tg, half), lambda i, te: (i, 0)),
                      pl.BlockSpec((None, D, 2 * F), lambda i, te: (te[i], 0, 0)),
                      pl.BlockSpec((None, F, D), lambda i, te: (te[i], 0, 0))],
            out_specs=pl.BlockSpec((tg, half), lambda i, te: (i, 0))),
        out_shape=jax.ShapeDtypeStruct(xs.shape, jnp.int32),
        compiler_params=_params("parallel"), name="moe_experts",
    )(tile_expert, xs, lw["exp_gu"], lw["exp_d"])

    yk = _gather_rows(ys, dest.reshape(-1)).reshape(TOP_K, N, half)
    return pl.pallas_call(
        functools.partial(_moe_combine_kernel, final=final), grid=(N // tm,),
        in_specs=[x_spec, mod_spec, pl.BlockSpec((TOP_K, tm, half), lambda i: (0, i, 0)),
                  pl.BlockSpec((tm, TOP_K), lambda i: (i, 0)), g_spec],
        out_specs=x_spec, out_shape=jax.ShapeDtypeStruct((N, D), F32),
        compiler_params=_params("parallel"), name="moe_combine",
    )(base, mod, yk, wts.T, final_g)


def _swap_halves(w, heads):
    shp = w.shape
    w = w.reshape(shp[:-1] + (heads, 2, HEAD_DIM // 2))
    return jnp.flip(w, axis=-2).reshape(shp)


def _block_diag2(a, b):
    za = jnp.zeros((a.shape[0], b.shape[1]), a.dtype)
    zb = jnp.zeros((b.shape[0], a.shape[1]), a.dtype)
    return jnp.concatenate([jnp.concatenate([a, za], 1), jnp.concatenate([zb, b], 1)], 0)


def _layer_weights(P, l):
    w_in = P["w_in"][l]
    o1, o2, o3 = RET_COLS, RET_COLS + ATT_COLS, RET_COLS + ATT_COLS + RW_COLS
    w_ret, w_att, w_rw, w_gate = w_in[:, :o1], w_in[:, o1:o2], w_in[:, o2:o3], w_in[:, o3:]
    W = RET_WIDTH
    w_ret = jnp.concatenate([w_ret, _swap_halves(w_ret[:, 0:W], RET_HEADS),
                             _swap_halves(w_ret[:, W:2 * W], RET_HEADS)], axis=1)
    QW, KW = ATT_Q_WIDTH, ATT_KV_WIDTH
    w_att = jnp.concatenate([w_att, _swap_halves(w_att[:, 0:QW], ATT_Q_HEADS),
                             _swap_halves(w_att[:, QW:QW + KW], ATT_KV_HEADS)], axis=1)
    qg, kg = P["q_norm_g"][l], P["k_norm_g"][l]
    pad = lambda v: jnp.zeros((QW,), F32).at[:v.shape[0]].set(v)
    qk_gain = jnp.zeros((8, QW), F32)
    qk_gain = qk_gain.at[0].set(jnp.tile(qg, ATT_Q_HEADS))
    qk_gain = qk_gain.at[1].set(jnp.tile(_swap_halves(qg, 1), ATT_Q_HEADS))
    qk_gain = qk_gain.at[2].set(pad(jnp.tile(kg, ATT_KV_HEADS)))
    qk_gain = qk_gain.at[3].set(pad(jnp.tile(_swap_halves(kg, 1), ATT_KV_HEADS)))
    rw_vec = jnp.zeros((8, RWKV_WIDTH), F32)
    for j, name in enumerate(("rw_w0_f", "rw_w0_b", "rw_a0_f", "rw_a0_b", "rw_k_k", "rw_k_a")):
        rw_vec = rw_vec.at[j].set(P[name][l])
    rw_vec = rw_vec.at[6].set(P["rw_r_k"][l].reshape(RWKV_WIDTH))
    row = lambda v: v.reshape(1, -1)
    return {
        "norm1_g": row(P["norm1_g"][l]), "norm2_g": row(P["norm2_g"][l]),
        "w_ret": w_ret.astype(BF16), "w_att": w_att.astype(BF16), "w_rw": w_rw.astype(BF16),
        "w_gate": w_gate.astype(BF16), "qk_gain": qk_gain, "ret_gn": row(P["ret_gn"][l]),
        "shift_mu": P["shift_mu"][l], "rw_vec": rw_vec,
        "rw_w2": _block_diag2(P["rw_w2_f"][l], P["rw_w2_b"][l]).astype(BF16),
        "rw_a2": _block_diag2(P["rw_a2_f"][l], P["rw_a2_b"][l]).astype(BF16),
        "rw_g2": P["rw_g2"][l].astype(BF16), "rw_gn": row(P["rw_gn"][l]),
        "w_branch_a": P["w_branch_a"][l].astype(BF16), "w_branch_b": P["w_branch_b"][l].astype(BF16),
        "w_branch_c": P["w_branch_c"][l].astype(BF16), "w_out": P["w_out"][l].astype(BF16),
        "router_wt": P["router_w"][l].T,
        "router_bias": jnp.broadcast_to(P["router_bias"][l][:, None], (N_EXPERTS, V7X_LANES)),
        "sh_gu": jnp.concatenate([P["sh_w_gate"][l], P["sh_w_up"][l]], axis=1).astype(BF16),
        "sh_d": P["sh_w_down"][l].astype(BF16),
        "exp_gu": jnp.concatenate([P["exp_w_gate"][l], P["exp_w_up"][l]], axis=2).astype(BF16),
        "exp_d": P["exp_w_down"][l].astype(BF16),
    }


def _ones_block_diag(n):
    idx = np.arange(n) // HEAD_DIM
    return jnp.asarray(idx[:, None] == idx[None, :], BF16)


def _tables(T):
    rows = T // GRID_W
    row = jnp.repeat(jnp.arange(rows, dtype=F32), GRID_W)
    col = jnp.tile(jnp.arange(GRID_W, dtype=F32), rows)
    freqs = ROPE_THETA ** (-jnp.arange(ROPE_PAIRS_PER_AXIS, dtype=F32) / ROPE_PAIRS_PER_AXIS)
    ang = jnp.concatenate([row[:, None] * freqs, col[:, None] * freqs], axis=-1)
    cos, sin = jnp.cos(ang), jnp.sin(ang)
    return {
        "cos": jnp.tile(jnp.concatenate([cos, cos], axis=-1), (1, ATT_Q_HEADS)),
        "sin": jnp.tile(jnp.concatenate([-sin, sin], axis=-1), (1, ATT_Q_HEADS)),
        "ones128": _ones_block_diag(128), "ones256": _ones_block_diag(256), "ones512": _ones_block_diag(512),
    }


def _trunk(x, c, P, layer_weights):
    B, T, D = x.shape
    tabs = _tables(T)
    mods = _ada_mod(c, P["ada_w"], P["ada_b"])
    x = x.reshape(B * T, D)
    final_g = P["final_g"].reshape(1, D)
    depth = len(layer_weights)
    for l, lw in enumerate(layer_weights):
        mod = mods[l]
        ret, q, k, v, gates, rwp = _input_projections(x, mod, T, lw, tabs)
        ya = _retention(ret, B, T, lw["ret_gn"], tabs["ones256"])
        yb = _attention(q, k, v, B, T)
        yf, ybk = _rwkv_scan(rwp, B, T, tabs["ones256"])
        x = _merge(x, mod, T, ya, yb, yf, ybk, rwp, gates, lw, tabs["ones256"])
        x = _moe(x, mod, T, lw, final_g, final=(l == depth - 1))
    return x.reshape(B, T, D)


def kernel(x_prompt, x_sample, c_prompt, c_sample, norm1_g, norm2_g, final_g, ada_w, ada_b, w_in, ret_gn, q_norm_g, k_norm_g, shift_mu, rw_w0_f, rw_w2_f, rw_w0_b, rw_w2_b, rw_a0_f, rw_a2_f, rw_a0_b, rw_a2_b, rw_g2, rw_k_k, rw_k_a, rw_r_k, rw_gn, w_branch_a, w_branch_b, w_branch_c, w_out, router_w, router_bias, exp_w_gate, exp_w_up, exp_w_down, sh_w_gate, sh_w_up, sh_w_down):
    P = {
        "norm1_g": norm1_g, "norm2_g": norm2_g, "final_g": final_g, "ada_w": ada_w, "ada_b": ada_b,
        "w_in": w_in, "ret_gn": ret_gn, "q_norm_g": q_norm_g, "k_norm_g": k_norm_g, "shift_mu": shift_mu,
        "rw_w0_f": rw_w0_f, "rw_w2_f": rw_w2_f, "rw_w0_b": rw_w0_b, "rw_w2_b": rw_w2_b,
        "rw_a0_f": rw_a0_f, "rw_a2_f": rw_a2_f, "rw_a0_b": rw_a0_b, "rw_a2_b": rw_a2_b,
        "rw_g2": rw_g2, "rw_k_k": rw_k_k, "rw_k_a": rw_k_a, "rw_r_k": rw_r_k, "rw_gn": rw_gn,
        "w_branch_a": w_branch_a, "w_branch_b": w_branch_b, "w_branch_c": w_branch_c, "w_out": w_out,
        "router_w": router_w, "router_bias": router_bias,
        "exp_w_gate": exp_w_gate, "exp_w_up": exp_w_up, "exp_w_down": exp_w_down,
        "sh_w_gate": sh_w_gate, "sh_w_up": sh_w_up, "sh_w_down": sh_w_down,
    }
    layer_weights = [_layer_weights(P, l) for l in range(w_in.shape[0])]
    y_prompt = _trunk(x_prompt, c_prompt, P, layer_weights)
    y_sample = _trunk(x_sample, c_sample, P, layer_weights)
    return (y_prompt, y_sample)
```

```python
import functools

import numpy as np
import jax
import jax.numpy as jnp
from jax import lax
from jax.experimental import pallas as pl
from jax.experimental.pallas import tpu as pltpu
from jax.experimental.pallas import tpu_sc as plsc

F32 = jnp.float32
BF16 = jnp.bfloat16

D_MODEL = 1024
DEPTH = 2
GRID_W = 64
HEAD_DIM = 64
ROPE_THETA = 10000.0
ROPE_PAIRS_PER_AXIS = HEAD_DIM // 4

RET_HEADS = 4
RET_WIDTH = RET_HEADS * HEAD_DIM
RET_CHUNK = 128
RET_EPS = 1e-5

ATT_Q_HEADS = 8
ATT_KV_HEADS = 2
ATT_GROUP = ATT_Q_HEADS // ATT_KV_HEADS
ATT_Q_WIDTH = ATT_Q_HEADS * HEAD_DIM
ATT_KV_WIDTH = ATT_KV_HEADS * HEAD_DIM

RWKV_HEADS = 4
RWKV_WIDTH = RWKV_HEADS * HEAD_DIM
DECAY_LORA = 64
AAA_LORA = 64
GATE_LORA = 128
RWKV_GN_EPS = 64e-5
RW_CHUNK = 64
RW_FIELDS = 11
RW_BATCH_ROWS = 4

FLASH_Q_ROWS = 512
FLASH_KV_ROWS = 512
FLASH_SUB_ROWS = 1024

MOE_TOKENS = 512
MOE_GROUP_ROWS_MIN = 256
MOE_GROUP_ROWS_MAX = 512
SC_GATHER_ROWS = 128

N_EXPERTS = 64
TOP_K = 8
N_GROUPS = 8
TOPK_GROUPS = 4
EXPERT_FF = 256
ROUTED_SCALE = 2.5
NORM_EPS = 1e-6

RET_COLS = 4 * RET_WIDTH
ATT_COLS = ATT_Q_WIDTH + 2 * ATT_KV_WIDTH
RW_COLS = 3 * RWKV_WIDTH + 2 * DECAY_LORA + 2 * AAA_LORA + GATE_LORA
GATE_COLS = 3 * D_MODEL

V7X_LANES = 128
VMEM_LIMIT_BYTES = 48 * 1024 * 1024


def _params(*dims):
    return pltpu.CompilerParams(dimension_semantics=dims, vmem_limit_bytes=VMEM_LIMIT_BYTES)


def _sigmoid(x):
    return 1.0 / (1.0 + jnp.exp(-x))


def _silu(x):
    return x * _sigmoid(x)


def _dot(a, b):
    return jnp.dot(a, b, preferred_element_type=F32)


def _dot_nt(a, b):
    return lax.dot_general(a, b, (((1,), (1,)), ((), ())), preferred_element_type=F32)


def _split(a):
    hi = a.astype(BF16)
    lo = (a - hi.astype(F32)).astype(BF16)
    return hi, lo


def _dot_split_lhs(a, b_bf16):
    hi, lo = _split(a)
    return _dot(hi, b_bf16) + _dot(lo, b_bf16)


def _dot3(a, b):
    ah, al = _split(a)
    bh, bl = _split(b)
    return _dot(ah, bh) + _dot(ah, bl) + _dot(al, bh)


def _norm_mod(x, gain, shift, scale):
    ms = jnp.mean(x * x, axis=-1, keepdims=True)
    return x * lax.rsqrt(ms + NORM_EPS) * gain * (1.0 + scale) + shift


def _head_layer_norm(y, ones_bd, eps):
    mean = _dot_split_lhs(y, ones_bd) * (1.0 / HEAD_DIM)
    yc = y - mean
    var = _dot_split_lhs(yc * yc, ones_bd) * (1.0 / HEAD_DIM)
    return yc * lax.rsqrt(var + eps)


def _ada_kernel(c_ref, w_ref, b_ref, o_ref):
    c = c_ref[...]
    o_ref[...] = _dot3(_silu(c), w_ref[...]) + b_ref[...]


def _ada_mod(c, ada_w, ada_b):
    B = c.shape[0]
    Bp = max(8, B)
    cp = jnp.zeros((Bp, D_MODEL), F32).at[:B].set(c)
    L = ada_w.shape[0]
    tn = 1536
    out = pl.pallas_call(
        _ada_kernel,
        grid=(L, 6 * D_MODEL // tn),
        in_specs=[pl.BlockSpec((Bp, D_MODEL), lambda l, j: (0, 0)),
                  pl.BlockSpec((None, D_MODEL, tn), lambda l, j: (l, 0, j)),
                  pl.BlockSpec((None, 1, tn), lambda l, j: (l, 0, j))],
        out_specs=pl.BlockSpec((None, Bp, tn), lambda l, j: (l, 0, j)),
        out_shape=jax.ShapeDtypeStruct((L, Bp, 6 * D_MODEL), F32),
        compiler_params=_params("parallel", "parallel"),
        name="ada_mod",
    )(cp, ada_w, ada_b.reshape(L, 1, 6 * D_MODEL))
    return out[:, :B].reshape(L, B, 6, D_MODEL)


def _proj_ret_kernel(x_ref, mod_ref, g_ref, w_ref, cos_ref, sin_ref, o_ref):
    h = _norm_mod(x_ref[...], g_ref[...], mod_ref[0:1, :], mod_ref[1:2, :]).astype(BF16)
    p = _dot(h, w_ref[...])
    W = RET_WIDTH
    cos = cos_ref[:, 0:W]
    sin = sin_ref[:, 0:W]
    o_ref[:, 0:W] = (p[:, 0:W] * cos + p[:, 4 * W:5 * W] * sin).astype(BF16)
    o_ref[:, W:2 * W] = ((p[:, W:2 * W] * cos + p[:, 5 * W:6 * W] * sin) * (HEAD_DIM ** -0.5)).astype(BF16)
    o_ref[:, 2 * W:3 * W] = p[:, 2 * W:3 * W].astype(BF16)
    o_ref[:, 3 * W:4 * W] = _silu(p[:, 3 * W:4 * W]).astype(BF16)


def _proj_att_kernel(x_ref, mod_ref, g_ref, w_ref, cos_ref, sin_ref, qkg_ref, ones_ref, q_ref, k_ref, v_ref):
    h = _norm_mod(x_ref[...], g_ref[...], mod_ref[0:1, :], mod_ref[1:2, :]).astype(BF16)
    p = _dot(h, w_ref[...])
    QW, KW = ATT_Q_WIDTH, ATT_KV_WIDTH
    q = p[:, 0:QW]
    k = p[:, QW:QW + KW]
    v = p[:, QW + KW:QW + 2 * KW]
    qs = p[:, QW + 2 * KW:2 * QW + 2 * KW]
    ks = p[:, 2 * QW + 2 * KW:2 * QW + 3 * KW]
    ones = ones_ref[...]
    rq = lax.rsqrt(_dot_split_lhs(q * q, ones) * (1.0 / HEAD_DIM) + NORM_EPS)
    rk = lax.rsqrt(_dot_split_lhs(k * k, ones[0:KW, 0:KW]) * (1.0 / HEAD_DIM) + NORM_EPS)
    cos = cos_ref[...]
    sin = sin_ref[...]
    qo = (q * qkg_ref[0:1, :] * cos + qs * qkg_ref[1:2, :] * sin) * (rq * (HEAD_DIM ** -0.5))
    ko = (k * qkg_ref[2:3, 0:KW] * cos[:, 0:KW] + ks * qkg_ref[3:4, 0:KW] * sin[:, 0:KW]) * rk
    q_ref[...] = qo.astype(BF16)
    kb = ko.astype(BF16)
    vb = v.astype(BF16)
    for hk in range(ATT_KV_HEADS):
        k_ref[hk] = kb[:, hk * HEAD_DIM:(hk + 1) * HEAD_DIM]
        v_ref[hk] = vb[:, hk * HEAD_DIM:(hk + 1) * HEAD_DIM]


def _proj_gate_kernel(x_ref, mod_ref, g_ref, w_ref, o_ref):
    h = _norm_mod(x_ref[...], g_ref[...], mod_ref[0:1, :], mod_ref[1:2, :]).astype(BF16)
    o_ref[...] = _sigmoid(_dot(h, w_ref[...])).astype(BF16)


def _proj_rw_kernel(x_ref, xp_ref, xn_ref, mod_ref, g_ref, w_ref, mu_ref, vec_ref, w2_ref, a2_ref, g2_ref,
                    ones_ref, o_ref, p_scr, *, tm, tiles_per_seq):
    i = pl.program_id(0)
    gain, shift, scale = g_ref[...], mod_ref[0:1, :], mod_ref[1:2, :]
    w = w_ref[...]
    h = _norm_mod(x_ref[...], gain, shift, scale).astype(BF16)
    p = _dot(h, w)
    hp = _norm_mod(xp_ref[...], gain, shift, scale).astype(BF16)
    hn = _norm_mod(xn_ref[...], gain, shift, scale).astype(BF16)
    first = (i % tiles_per_seq) == 0
    last = (i % tiles_per_seq) == tiles_per_seq - 1
    prev_row = jnp.where(first, 0.0, _dot(hp, w)[7:8, :])
    next_row = jnp.where(last, 0.0, _dot(hn, w)[0:1, :])
    p_scr[8:8 + tm, :] = p
    p_scr[7:8, :] = prev_row
    p_scr[8 + tm:9 + tm, :] = next_row
    prev = p_scr[7:7 + tm, :]
    nxt = p_scr[9:9 + tm, :]
    x = p + mu_ref[0:1, :] * (prev - p) + mu_ref[1:2, :] * (nxt - p)

    W = RWKV_WIDTH
    r, k, v = x[:, 0:W], x[:, W:2 * W], x[:, 2 * W:3 * W]
    xw = x[:, 3 * W:3 * W + 2 * DECAY_LORA]
    xa = x[:, 3 * W + 2 * DECAY_LORA:3 * W + 2 * DECAY_LORA + 2 * AAA_LORA]
    xg = x[:, 3 * W + 2 * DECAY_LORA + 2 * AAA_LORA:]
    w0_f, w0_b = vec_ref[0:1, :], vec_ref[1:2, :]
    a0_f, a0_b = vec_ref[2:3, :], vec_ref[3:4, :]
    k_k, k_a, r_k = vec_ref[4:5, :], vec_ref[5:6, :], vec_ref[6:7, :]
    wl = _dot(jnp.tanh(xw).astype(BF16), w2_ref[...])
    al = _dot(xa.astype(BF16), a2_ref[...])
    dec_c = float(np.exp(-0.5))
    w_f = jnp.exp(-dec_c * _sigmoid(w0_f + wl[:, 0:W]))
    w_b = jnp.exp(-dec_c * _sigmoid(w0_b + wl[:, W:2 * W]))
    a_f = _sigmoid(a0_f + al[:, 0:W])
    a_b = _sigmoid(a0_b + al[:, W:2 * W])
    gate = _dot(_sigmoid(xg).astype(BF16), g2_ref[...])
    ones = ones_ref[...]
    kk = k * k_k
    kk = kk * lax.rsqrt(_dot_split_lhs(kk * kk, ones) + 1e-12)
    k_f = k * (1.0 + (a_f - 1.0) * k_a)
    k_b = k * (1.0 + (a_b - 1.0) * k_a)
    bonus = _dot_split_lhs(r * k_f * r_k, ones) * v
    for j, val in enumerate((r, kk, v, w_f, w_b, k_f, k_b, kk * a_f, kk * a_b, gate, bonus)):
        o_ref[:, j * W:(j + 1) * W] = val


def _token_specs(tm, tiles_per_seq):
    x_spec = pl.BlockSpec((tm, D_MODEL), lambda i: (i, 0))
    mod_spec = pl.BlockSpec((None, 6, D_MODEL), lambda i: (i // tiles_per_seq, 0, 0))
    g_spec = pl.BlockSpec((1, D_MODEL), lambda i: (0, 0))
    return x_spec, mod_spec, g_spec


def _full(shape):
    nd = len(shape)
    return pl.BlockSpec(shape, lambda *_: (0,) * nd)


def _input_projections(x, mod, T, lw, tabs):
    N = x.shape[0]
    tm = min(512, T)
    tps = T // tm
    grid = (N // tm,)
    x_spec, mod_spec, g_spec = _token_specs(tm, tps)
    tab_spec = pl.BlockSpec((tm, ATT_Q_WIDTH), lambda i: (i % tps, 0))
    cos, sin = tabs["cos"], tabs["sin"]

    ret = pl.pallas_call(
        _proj_ret_kernel, grid=grid,
        in_specs=[x_spec, mod_spec, g_spec, _full(lw["w_ret"].shape), tab_spec, tab_spec],
        out_specs=pl.BlockSpec((tm, RET_COLS), lambda i: (i, 0)),
        out_shape=jax.ShapeDtypeStruct((N, RET_COLS), BF16),
        compiler_params=_params("parallel"), name="proj_ret",
    )(x, mod, lw["norm1_g"], lw["w_ret"], cos, sin)

    q, k, v = pl.pallas_call(
        _proj_att_kernel, grid=grid,
        in_specs=[x_spec, mod_spec, g_spec, _full(lw["w_att"].shape), tab_spec, tab_spec,
                  _full(lw["qk_gain"].shape), _full(tabs["ones512"].shape)],
        out_specs=[pl.BlockSpec((tm, ATT_Q_WIDTH), lambda i: (i, 0)),
                   pl.BlockSpec((ATT_KV_HEADS, tm, HEAD_DIM), lambda i: (0, i, 0)),
                   pl.BlockSpec((ATT_KV_HEADS, tm, HEAD_DIM), lambda i: (0, i, 0))],
        out_shape=[jax.ShapeDtypeStruct((N, ATT_Q_WIDTH), BF16),
                   jax.ShapeDtypeStruct((ATT_KV_HEADS, N, HEAD_DIM), BF16),
                   jax.ShapeDtypeStruct((ATT_KV_HEADS, N, HEAD_DIM), BF16)],
        compiler_params=_params("parallel"), name="proj_att",
    )(x, mod, lw["norm1_g"], lw["w_att"], cos, sin, lw["qk_gain"], tabs["ones512"])

    gates = pl.pallas_call(
        _proj_gate_kernel, grid=grid,
        in_specs=[x_spec, mod_spec, g_spec, _full(lw["w_gate"].shape)],
        out_specs=pl.BlockSpec((tm, GATE_COLS), lambda i: (i, 0)),
        out_shape=jax.ShapeDtypeStruct((N, GATE_COLS), BF16),
        compiler_params=_params("parallel"), name="proj_gate",
    )(x, mod, lw["norm1_g"], lw["w_gate"])

    tm_rw = min(256, T)
    tps_rw = T // tm_rw
    x_spec, mod_spec, g_spec = _token_specs(tm_rw, tps_rw)
    rows8 = tm_rw // 8
    last8 = N // 8 - 1
    rwp = pl.pallas_call(
        functools.partial(_proj_rw_kernel, tm=tm_rw, tiles_per_seq=tps_rw), grid=(N // tm_rw,),
        in_specs=[x_spec,
                  pl.BlockSpec((8, D_MODEL), lambda i: (jnp.maximum(i * rows8 - 1, 0), 0)),
                  pl.BlockSpec((8, D_MODEL), lambda i: (jnp.minimum((i + 1) * rows8, last8), 0)),
                  mod_spec, g_spec, _full(lw["w_rw"].shape), _full(lw["shift_mu"].shape),
                  _full(lw["rw_vec"].shape), _full(lw["rw_w2"].shape), _full(lw["rw_a2"].shape),
                  _full(lw["rw_g2"].shape), _full(tabs["ones256"].shape)],
        out_specs=pl.BlockSpec((tm_rw, RW_FIELDS * RWKV_WIDTH), lambda i: (i, 0)),
        out_shape=jax.ShapeDtypeStruct((N, RW_FIELDS * RWKV_WIDTH), F32),
        scratch_shapes=[pltpu.VMEM((tm_rw + 16, RW_COLS), F32)],
        compiler_params=_params("parallel"), name="proj_rw",
    )(x, x, x, mod, lw["norm1_g"], lw["w_rw"], lw["shift_mu"], lw["rw_vec"], lw["rw_w2"], lw["rw_a2"],
      lw["rw_g2"], tabs["ones256"])
    return ret, q, k, v, gates, rwp


def _retention_tables(reverse):
    lg = np.log1p(-np.exp2(-5.0 - np.arange(RET_HEADS, dtype=np.float64)))
    if reverse:
        lg = lg[::-1]
    C = RET_CHUNK
    pos = np.arange(C, dtype=np.float64)
    diff = pos[:, None] - pos[None, :]
    if reverse:
        dec = np.where(diff < 0, np.exp(lg[:, None, None] * np.maximum(-diff, 0.0)[None]), 0.0)
        xi = np.exp(lg[:, None] * (C - pos)[None, :])
        zeta = np.exp(lg[:, None] * pos[None, :])
    else:
        dec = np.where(diff >= 0, np.exp(lg[:, None, None] * np.maximum(diff, 0.0)[None]), 0.0)
        xi = np.exp(lg[:, None] * (pos + 1.0)[None, :])
        zeta = np.exp(lg[:, None] * (C - 1.0 - pos)[None, :])
    chunk_decay = tuple(float(v) for v in np.exp(lg * C))
    widen = lambda a: jnp.asarray(np.repeat(a.T, HEAD_DIM, axis=1), F32)
    return jnp.asarray(dec, F32), widen(xi), widen(zeta), chunk_decay


def _retention_kernel(ret_ref, dec_ref, xi_ref, zeta_ref, *rest, chunk_decay, final):
    if final:
        yf_ref, gn_ref, ones_ref, o_ref, state, y_scr = rest
    else:
        o_ref, state, y_scr = rest

    @pl.when(pl.program_id(1) == 0)
    def _():
        state[...] = jnp.zeros_like(state)

    W = RET_WIDTH
    q = ret_ref[:, 0:W]
    k = ret_ref[:, W:2 * W]
    v = ret_ref[:, 2 * W:3 * W]
    qx = (q.astype(F32) * xi_ref[...]).astype(BF16)
    kzt = (k.astype(F32) * zeta_ref[...]).T
    for h in range(RET_HEADS):
        sl = slice(h * HEAD_DIM, (h + 1) * HEAD_DIM)
        s = _dot_nt(q[:, sl], k[:, sl]) * dec_ref[h]
        inner = _dot(s.astype(BF16), v[:, sl])
        r_prev = state[h]
        cross = _dot(qx[:, sl], r_prev.astype(BF16))
        kv = _dot(kzt[sl, :].astype(BF16), v[:, sl])
        state[h] = r_prev * chunk_decay[h] + kv
        y_scr[:, sl] = inner + cross
    y = y_scr[...]
    if final:
        y = _head_layer_norm(y + yf_ref[...], ones_ref[...], RET_EPS) * gn_ref[...]
        o_ref[...] = (y * ret_ref[:, 3 * W:4 * W].astype(F32)).astype(BF16)
    else:
        o_ref[...] = y


def _retention(ret, B, T, ret_gn, ones256):
    N = ret.shape[0]
    C = RET_CHUNK
    nc = T // C
    scratch = [pltpu.VMEM((RET_HEADS, HEAD_DIM, HEAD_DIM), F32), pltpu.VMEM((C, RET_WIDTH), F32)]

    dec, xi, zeta, cd = _retention_tables(False)
    fwd_row = lambda b, c: (b * nc + c, 0)
    yf = pl.pallas_call(
        functools.partial(_retention_kernel, chunk_decay=cd, final=False), grid=(B, nc),
        in_specs=[pl.BlockSpec((C, RET_COLS), fwd_row), _full(dec.shape), _full(xi.shape), _full(zeta.shape)],
        out_specs=pl.BlockSpec((C, RET_WIDTH), fwd_row),
        out_shape=jax.ShapeDtypeStruct((N, RET_WIDTH), F32),
        scratch_shapes=scratch, compiler_params=_params("parallel", "arbitrary"), name="retention_fwd",
    )(ret, dec, xi, zeta)

    dec, xi, zeta, cd = _retention_tables(True)
    bwd_row = lambda b, c: (b * nc + nc - 1 - c, 0)
    return pl.pallas_call(
        functools.partial(_retention_kernel, chunk_decay=cd, final=True), grid=(B, nc),
        in_specs=[pl.BlockSpec((C, RET_COLS), bwd_row), _full(dec.shape), _full(xi.shape), _full(zeta.shape),
                  pl.BlockSpec((C, RET_WIDTH), bwd_row), _full(ret_gn.shape), _full(ones256.shape)],
        out_specs=pl.BlockSpec((C, RET_WIDTH), bwd_row),
        out_shape=jax.ShapeDtypeStruct((N, RET_WIDTH), BF16),
        scratch_shapes=scratch, compiler_params=_params("parallel", "arbitrary"), name="retention_bwd",
    )(ret, dec, xi, zeta, yf, ret_gn, ones256)


def _flash_kernel(q_ref, k_ref, v_ref, o_ref, q_scr, m_scr, l_scr, acc_scr, *, tq, tk):
    j = pl.program_id(3)

    @pl.when(j == 0)
    def _():
        for g in range(ATT_GROUP):
            q_scr[g * tq:(g + 1) * tq, :] = q_ref[:, g * HEAD_DIM:(g + 1) * HEAD_DIM]
        m_scr[...] = jnp.full_like(m_scr, -jnp.inf)
        l_scr[...] = jnp.zeros_like(l_scr)
        acc_scr[...] = jnp.zeros_like(acc_scr)

    k = k_ref[...]
    v = v_ref[...]
    sub = min(FLASH_SUB_ROWS, ATT_GROUP * tq)
    for r0 in range(0, ATT_GROUP * tq, sub):
        rows = slice(r0, r0 + sub)
        s = _dot_nt(q_scr[rows, :], k)
        m_prev = m_scr[rows, :]
        m_next = jnp.maximum(m_prev, jnp.max(s, axis=1, keepdims=True))
        p = jnp.exp(s - jnp.concatenate([m_next] * (tk // V7X_LANES), axis=1))
        alpha = jnp.exp(m_prev - m_next)
        l_scr[rows, :] = alpha * l_scr[rows, :] + jnp.sum(p, axis=1, keepdims=True)
        m_scr[rows, :] = m_next
        acc_scr[rows, :] = alpha[:, 0:HEAD_DIM] * acc_scr[rows, :] + _dot(p.astype(BF16), v)

    @pl.when(j == pl.num_programs(3) - 1)
    def _():
        o = acc_scr[...] / l_scr[:, 0:HEAD_DIM]
        for g in range(ATT_GROUP):
            o_ref[:, g * HEAD_DIM:(g + 1) * HEAD_DIM] = o[g * tq:(g + 1) * tq, :].astype(BF16)


def _attention(q, k, v, B, T):
    N = q.shape[0]
    tq = min(FLASH_Q_ROWS, T)
    tk = min(FLASH_KV_ROWS, T)
    nq, nk = T // tq, T // tk
    GW = ATT_GROUP * HEAD_DIM
    rows = ATT_GROUP * tq
    return pl.pallas_call(
        functools.partial(_flash_kernel, tq=tq, tk=tk), grid=(B, ATT_KV_HEADS, nq, nk),
        in_specs=[pl.BlockSpec((tq, GW), lambda b, h, i, j: (b * nq + i, h)),
                  pl.BlockSpec((None, tk, HEAD_DIM), lambda b, h, i, j: (h, b * nk + j, 0)),
                  pl.BlockSpec((None, tk, HEAD_DIM), lambda b, h, i, j: (h, b * nk + j, 0))],
        out_specs=pl.BlockSpec((tq, GW), lambda b, h, i, j: (b * nq + i, h)),
        out_shape=jax.ShapeDtypeStruct((N, ATT_Q_WIDTH), BF16),
        scratch_shapes=[pltpu.VMEM((rows, HEAD_DIM), BF16), pltpu.VMEM((rows, V7X_LANES), F32),
                        pltpu.VMEM((rows, V7X_LANES), F32), pltpu.VMEM((rows, HEAD_DIM), F32)],
        compiler_params=_params("parallel", "parallel", "parallel", "arbitrary"), name="flash_attention",
    )(q, k, v)


def _rwkv_scan_kernel(*refs, rows_per_set):
    C = RW_CHUNK
    H = C // 2
    W = RWKV_WIDTH
    SUB = 8
    n_sets = len(rows_per_set)
    ones_ref = refs[12 * n_sets]
    outs = refs[12 * n_sets + 1:12 * n_sets + 1 + 2 * n_sets]
    state, vk_scr = refs[-2:]
    dirs = [(refs[12 * s:12 * s + 6], refs[12 * s + 6:12 * s + 12]) for s in range(n_sets)]
    chains = [(s, n, d) for s in range(n_sets) for n in range(rows_per_set[s]) for d in range(2)]

    first_chunk = pl.program_id(1) == 0

    @pl.when(first_chunk)
    def _():
        for c, (s, n, d) in enumerate(chains):
            if s == 0:
                state[c] = jnp.zeros((HEAD_DIM, W), F32)

    @pl.when(jnp.logical_and(first_chunk, pl.program_id(0) == 0))
    def _():
        for c, (s, n, d) in enumerate(chains):
            if s > 0:
                state[c] = jnp.zeros((HEAD_DIM, W), F32)

    ones = ones_ref[...]
    lane = lax.broadcasted_iota(jnp.int32, (1, W), 1) & (HEAD_DIM - 1)
    same_head = (lax.broadcasted_iota(jnp.int32, (W, W), 0) // HEAD_DIM
                 == lax.broadcasted_iota(jnp.int32, (W, W), 1) // HEAD_DIM)
    step_iota = lax.broadcasted_iota(jnp.int32, (H, HEAD_DIM, W), 0)
    lane_iota = lax.broadcasted_iota(jnp.int32, (H, HEAD_DIM, W), 2) & (HEAD_DIM - 1)

    def head_sums(vals):
        stacked = jnp.concatenate([v.astype(BF16) for v in vals], axis=0)
        out = _dot(stacked, ones)
        return [out[c * HEAD_DIM:(c + 1) * HEAD_DIM, :] for c in range(len(vals))]

    carry = (tuple(state[c] for c in range(len(chains))),
             tuple(jnp.zeros((HEAD_DIM, W), F32) for _ in chains))
    for half in range(2):
        lo = (half * H, (1 - half) * H)
        for c, (s, n, d) in enumerate(chains):
            kc = dirs[s][d][3][n]
            kbd = jnp.where(same_head, jnp.concatenate([kc] * (W // C), axis=0), 0.0).astype(BF16)
            sel = jnp.where(step_iota + lo[d] == lane_iota, dirs[s][d][5][n][None], 0.0).reshape(H * HEAD_DIM, W)
            vk_scr[c] = _dot(sel.astype(BF16), kbd).reshape(H, HEAD_DIM, W)

        def group(g, carry, lo=lo):
            states, ycols = list(carry[0]), list(carry[1])
            bases = (lo[0] + g * SUB, lo[1] + (H // SUB - 1 - g) * SUB)
            rows = [[ref[n, pl.ds(pl.multiple_of(bases[d], SUB), SUB), :] for ref in dirs[s][d][:5]]
                    for (s, n, d) in chains]
            for j in range(SUB):
                jj = (j, SUB - 1 - j)
                row = lambda c, f: rows[c][f][jj[chains[c][2]]:jj[chains[c][2]] + 1, :]
                sa = head_sums([states[c] * row(c, 0) for c in range(len(chains))])
                new = [states[c] * row(c, 2) - sa[c] * row(c, 4)
                       + vk_scr[c, bases[chains[c][2]] + jj[chains[c][2]] - lo[chains[c][2]]]
                       for c in range(len(chains))]
                read = [new[c] if chains[c][2] == 0 else states[c] for c in range(len(chains))]
                y = head_sums([read[c] * row(c, 1) for c in range(len(chains))])
                for c, (s, n, d) in enumerate(chains):
                    ycols[c] = jnp.where(lane == bases[d] + jj[d], y[c], ycols[c])
                states = new
            return tuple(states), tuple(ycols)

        carry = lax.fori_loop(0, H // SUB, group, carry)

    states, ycols = carry
    for c, (s, n, d) in enumerate(chains):
        state[c] = states[c]
        outs[2 * s + d][n] = ycols[c]


def _rwkv_scan_call(sets, ones256):
    W, C = RWKV_WIDTH, RW_CHUNK
    _, B0, T0, nb0 = sets[0]
    nc0 = T0 // C
    operands, in_specs, out_specs, out_shapes = [], [], [], []
    for s, (rwp, B, T, nb) in enumerate(sets):
        nc = T // C
        fields = rwp.reshape(B, T, RW_FIELDS * W)
        v = rwp[:, 2 * W:3 * W].reshape(B, nc, C, RWKV_HEADS, HEAD_DIM)
        vt = v.transpose(0, 1, 4, 3, 2).reshape(B, nc, HEAD_DIM, W)

        def block(b, c, reverse, s=s, nc=nc):
            bi, ci = (b, c) if s == 0 else (0, b * nc0 + c)
            return bi, (nc - 1 - ci if reverse else ci)

        field = lambda f, rev, nb=nb, block=block: pl.BlockSpec(
            (nb, C, W), lambda b, c: block(b, c, rev) + (f,))
        vt_spec = lambda rev, nb=nb, block=block: pl.BlockSpec(
            (nb, None, HEAD_DIM, W), lambda b, c: block(b, c, rev) + (0, 0))
        in_specs += [field(1, False), field(0, False), field(3, False), field(5, False), field(7, False),
                     vt_spec(False),
                     field(1, True), field(0, True), field(4, True), field(6, True), field(8, True),
                     vt_spec(True)]
        operands += [fields] * 5 + [vt] + [fields] * 5 + [vt]
        out_specs += [vt_spec(False), vt_spec(True)]
        out_shapes += [jax.ShapeDtypeStruct((B, nc, HEAD_DIM, W), F32)] * 2
    n_chains = 2 * sum(nb for _, _, _, nb in sets)
    outs = pl.pallas_call(
        functools.partial(_rwkv_scan_kernel, rows_per_set=tuple(nb for _, _, _, nb in sets)),
        grid=(B0 // nb0, nc0), in_specs=in_specs + [_full(ones256.shape)],
        out_specs=out_specs, out_shape=out_shapes,
        scratch_shapes=[pltpu.VMEM((n_chains, HEAD_DIM, W), F32),
                        pltpu.VMEM((n_chains, C // 2, HEAD_DIM, W), F32)],
        compiler_params=_params("arbitrary", "arbitrary"), name="rwkv_scan",
    )(*operands, ones256)
    results = []
    for s, (rwp, B, T, nb) in enumerate(sets):
        untranspose = lambda yt: yt.reshape(B, T // C, HEAD_DIM, RWKV_HEADS, C).transpose(0, 1, 4, 3, 2).reshape(
            B * T, W)
        results.append((untranspose(outs[2 * s]), untranspose(outs[2 * s + 1])))
    return results


def _rwkv_scan(groups, ones256):
    rows = lambda B: RW_BATCH_ROWS if B % RW_BATCH_ROWS == 0 else 1
    if len(groups) == 2:
        (_, B0, T0), (_, B1, T1) = groups
        if B1 == 1 and (B0 // rows(B0)) * (T0 // RW_CHUNK) == T1 // RW_CHUNK:
            return _rwkv_scan_call([groups[0] + (rows(B0),), groups[1] + (1,)], ones256)
    return [_rwkv_scan_call([g + (rows(g[1]),)], ones256)[0] for g in groups]


def _merge_kernel(x_ref, mod_ref, ya_ref, yb_ref, yf_ref, ybk_ref, rg_ref, bonus_ref, gn_ref, gates_ref,
                  ones_ref, wa_ref, wb_ref, wc_ref, wo_ref, o_ref):
    y = _head_layer_norm(yf_ref[...] + ybk_ref[...], ones_ref[...], RWKV_GN_EPS) * gn_ref[...]
    yc = ((y + bonus_ref[...]) * rg_ref[...]).astype(BF16)
    D = D_MODEL
    merged = (gates_ref[:, 0:D].astype(F32) * _dot(ya_ref[...], wa_ref[...])
              + gates_ref[:, D:2 * D].astype(F32) * _dot(yb_ref[...], wb_ref[...])
              + gates_ref[:, 2 * D:3 * D].astype(F32) * _dot(yc, wc_ref[...]))
    o_ref[...] = x_ref[...] + mod_ref[2:3, :] * _dot(merged.astype(BF16), wo_ref[...])


def _merge(x, mod, T, ya, yb, yf, ybk, rwp, gates, lw, ones256):
    N = x.shape[0]
    tm = min(512, T)
    tps = T // tm
    x_spec, mod_spec, _ = _token_specs(tm, tps)
    row = lambda w: pl.BlockSpec((tm, w), lambda i: (i, 0))
    field = lambda f: pl.BlockSpec((tm, RWKV_WIDTH), lambda i: (i, f))
    return pl.pallas_call(
        _merge_kernel, grid=(N // tm,),
        in_specs=[x_spec, mod_spec, row(RET_WIDTH), row(ATT_Q_WIDTH), row(RWKV_WIDTH), row(RWKV_WIDTH),
                  field(9), field(10), _full(lw["rw_gn"].shape), row(GATE_COLS), _full(ones256.shape),
                  _full(lw["w_branch_a"].shape), _full(lw["w_branch_b"].shape), _full(lw["w_branch_c"].shape),
                  _full(lw["w_out"].shape)],
        out_specs=x_spec, out_shape=jax.ShapeDtypeStruct((N, D_MODEL), F32),
        compiler_params=_params("parallel"), name="merge_out",
    )(x, mod, ya, yb, yf, ybk, rwp, rwp, lw["rw_gn"], gates, ones256,
      lw["w_branch_a"], lw["w_branch_b"], lw["w_branch_c"], lw["w_out"])


def _route(h, hb, rwt_ref, rbias_ref, tm):
    h_lo = (h - hb.astype(F32)).astype(BF16)
    rw_hi, rw_lo = _split(rwt_ref[...])
    logits = _dot_nt(rw_hi, hb) + _dot_nt(rw_hi, h_lo) + _dot_nt(rw_lo, hb)
    scores = _sigmoid(logits)
    choice = scores + jnp.concatenate([rbias_ref[...]] * (tm // V7X_LANES), axis=1)
    per_group = N_EXPERTS // N_GROUPS
    sub = lax.broadcasted_iota(jnp.int32, (per_group, tm), 0)
    groups, gscore = [], []
    for g in range(N_GROUPS):
        cg = choice[g * per_group:(g + 1) * per_group, :]
        m1 = jnp.max(cg, axis=0, keepdims=True)
        first = jnp.min(jnp.where(cg == m1, sub, per_group), axis=0, keepdims=True)
        m2 = jnp.max(jnp.where(sub == first, -jnp.inf, cg), axis=0, keepdims=True)
        groups.append(cg)
        gscore.append(m1 + m2)
    masked = []
    for g in range(N_GROUPS):
        beaten = jnp.zeros((1, tm), F32)
        for o in range(N_GROUPS):
            if o == g:
                continue
            wins = (gscore[o] >= gscore[g]) if o < g else (gscore[o] > gscore[g])
            beaten = beaten + jnp.where(wins, 1.0, 0.0)
        keep = jnp.where(beaten < TOPK_GROUPS, 1.0, 0.0)
        masked.append(jnp.where(jnp.broadcast_to(keep, (per_group, tm)) > 0.5, groups[g], -jnp.inf))
    mc = jnp.concatenate(masked, axis=0)
    eidx = lax.broadcasted_iota(jnp.int32, (N_EXPERTS, tm), 0)
    beaten = jnp.zeros((N_EXPERTS, tm), F32)
    for o in range(N_EXPERTS):
        row = jnp.broadcast_to(mc[o:o + 1, :], (N_EXPERTS, tm))
        tie = jnp.where(eidx > o, 1.0, 0.0)
        beaten = beaten + jnp.where(row > mc, 1.0, jnp.where(row == mc, tie, 0.0))
    sel = jnp.where(beaten < TOP_K, 1.0, 0.0)
    wts = jnp.where(beaten < TOP_K, scores, 0.0)
    return wts / jnp.sum(wts, axis=0, keepdims=True) * ROUTED_SCALE, sel


def _selected_lists(comb, sel, rank):
    eidx = lax.broadcasted_iota(jnp.int32, comb.shape, 0).astype(F32)
    prev = jnp.full((1, comb.shape[1]), -1.0, F32)
    ids, wts, pos = [], [], []
    for _ in range(TOP_K):
        cand = jnp.where(sel > 0.5, jnp.where(eidx > prev, eidx, float(N_EXPERTS)), float(N_EXPERTS))
        prev = jnp.min(cand, axis=0, keepdims=True)
        wts.append(jnp.sum(jnp.where(eidx == prev, comb, 0.0), axis=0, keepdims=True))
        pos.append(jnp.sum(jnp.where(eidx == prev, rank, 0.0), axis=0, keepdims=True))
        ids.append(jnp.minimum(prev, N_EXPERTS - 1.0))
    as_int = lambda rows: jnp.concatenate(rows, axis=0).astype(jnp.int32)
    return as_int(ids), jnp.concatenate(wts, axis=0), as_int(pos)


def _pack_halves(y):
    n = y.shape[1] // 2
    hi = pltpu.bitcast(y[:, :n].astype(BF16).astype(F32), jnp.int32)
    lo = pltpu.bitcast(y[:, n:].astype(BF16).astype(F32), jnp.int32)
    return hi | lax.shift_right_logical(lo, 16)


def _unpack_halves(w):
    hi = pltpu.bitcast(w & jnp.int32(-65536), F32)
    lo = pltpu.bitcast(lax.shift_left(w, 16), F32)
    return hi, lo


def _moe_route_kernel(x_ref, mod_ref, g_ref, rwt_ref, rbias_ref, shgu_ref, shd_ref, tri_ref,
                      hp_ref, ids_ref, wts_ref, pos_ref, base_ref, count_ref, *, tm):
    F = EXPERT_FF

    @pl.when(pl.program_id(0) == 0)
    def _():
        count_ref[...] = jnp.zeros_like(count_ref)

    x = x_ref[...]
    h = _norm_mod(x, g_ref[...], mod_ref[3:4, :], mod_ref[4:5, :])
    hb = h.astype(BF16)
    hp_ref[...] = _pack_halves(hb.astype(F32))
    comb, sel = _route(h, hb, rwt_ref, rbias_ref, tm)
    before = count_ref[...]
    inclusive = _dot(sel.astype(BF16), tri_ref[...])
    rank = inclusive - sel + jnp.concatenate([before] * (tm // V7X_LANES), axis=1)
    count_ref[...] = before + jnp.sum(sel, axis=1, keepdims=True)
    ids, wts, pos = _selected_lists(comb, sel, rank)
    ids_ref[...] = ids
    wts_ref[...] = wts
    pos_ref[...] = pos
    gu = _dot(hb, shgu_ref[...])
    shared = _dot((_silu(gu[:, 0:F]) * gu[:, F:2 * F]).astype(BF16), shd_ref[...])
    base_ref[...] = x + mod_ref[5:6, :] * shared


def _moe_expert_kernel(tile_expert_ref, xs_ref, wgu_ref, wd_ref, o_ref):
    del tile_expert_ref
    F = EXPERT_FF
    half = D_MODEL // 2
    hi, lo = _unpack_halves(xs_ref[...])
    gu = _dot(hi.astype(BF16), wgu_ref[0:half, :]) + _dot(lo.astype(BF16), wgu_ref[half:, :])
    act = (_silu(gu[:, 0:F]) * gu[:, F:2 * F]).astype(BF16)
    o_ref[...] = _pack_halves(_dot(act, wd_ref[...]))


def _moe_combine_kernel(base_ref, mod_ref, y_ref, w_ref, fin_ref, o_ref, *, final):
    half = D_MODEL // 2
    acc_hi = jnp.zeros((base_ref.shape[0], half), F32)
    acc_lo = jnp.zeros((base_ref.shape[0], half), F32)
    for k in range(TOP_K):
        hi, lo = _unpack_halves(y_ref[k])
        wk = w_ref[:, k:k + 1]
        acc_hi = acc_hi + wk * hi
        acc_lo = acc_lo + wk * lo
    out_hi = base_ref[:, 0:half] + mod_ref[5:6, 0:half] * acc_hi
    out_lo = base_ref[:, half:] + mod_ref[5:6, half:] * acc_lo
    if final:
        ms = (jnp.sum(out_hi * out_hi, axis=-1, keepdims=True)
              + jnp.sum(out_lo * out_lo, axis=-1, keepdims=True)) * (1.0 / D_MODEL)
        scale = lax.rsqrt(ms + NORM_EPS)
        out_hi = out_hi * scale * fin_ref[:, 0:half]
        out_lo = out_lo * scale * fin_ref[:, half:]
    o_ref[:, 0:half] = out_hi
    o_ref[:, half:] = out_lo


def _gather_rows(table, idx):
    rows, width = idx.shape[0], table.shape[1]
    info = plsc.get_sparse_core_info()
    workers = info.num_cores * info.num_subcores
    windows = rows // (workers * SC_GATHER_ROWS)
    assert windows * workers * SC_GATHER_ROWS == rows
    mesh = plsc.VectorSubcoreMesh(core_axis_name="core", subcore_axis_name="subcore")

    @functools.partial(pl.kernel, out_type=jax.ShapeDtypeStruct((rows, width), table.dtype), mesh=mesh,
                       scratch_types=[pltpu.VMEM((SC_GATHER_ROWS,), jnp.int32),
                                      pltpu.VMEM((SC_GATHER_ROWS, width), table.dtype)],
                       name="moe_gather")
    def gather(table_hbm, idx_hbm, out_hbm, idx_vmem, rows_vmem):
        worker = lax.axis_index("subcore") * info.num_cores + lax.axis_index("core")
        base = worker * (windows * SC_GATHER_ROWS)

        @pl.loop(0, windows)
        def _(j):
            span = pl.ds(pl.multiple_of(base + j * SC_GATHER_ROWS, SC_GATHER_ROWS), SC_GATHER_ROWS)
            pltpu.sync_copy(idx_hbm.at[span], idx_vmem)
            pltpu.sync_copy(table_hbm.at[idx_vmem], rows_vmem)
            pltpu.sync_copy(rows_vmem, out_hbm.at[span])

    return gather(table, idx)


def _scatter_rows(table, dest, total_rows):
    n_lists, n_rows = dest.shape
    width = table.shape[1]
    info = plsc.get_sparse_core_info()
    workers = info.num_cores * info.num_subcores
    windows = n_rows // (workers * SC_GATHER_ROWS)
    assert windows * workers * SC_GATHER_ROWS == n_rows
    windows_per_list = n_rows // SC_GATHER_ROWS
    mesh = plsc.VectorSubcoreMesh(core_axis_name="core", subcore_axis_name="subcore")

    @functools.partial(pl.kernel, out_type=jax.ShapeDtypeStruct((total_rows, width), table.dtype), mesh=mesh,
                       scratch_types=[pltpu.VMEM((1, SC_GATHER_ROWS), jnp.int32),
                                      pltpu.VMEM((SC_GATHER_ROWS, width), table.dtype)],
                       name="moe_scatter")
    def scatter(table_hbm, dest_hbm, out_hbm, idx_vmem, rows_vmem):
        worker = lax.axis_index("subcore") * info.num_cores + lax.axis_index("core")

        @pl.loop(0, windows)
        def _(j):
            window = worker * windows + j
            span = pl.ds(pl.multiple_of(window * SC_GATHER_ROWS, SC_GATHER_ROWS), SC_GATHER_ROWS)
            pltpu.sync_copy(table_hbm.at[span], rows_vmem)
            for k in range(n_lists):
                pltpu.sync_copy(dest_hbm.at[pl.ds(k * windows_per_list + window, 1)], idx_vmem)
                pltpu.sync_copy(rows_vmem, out_hbm.at[idx_vmem.at[0]])

    return scatter(table, dest.reshape(n_lists * windows_per_list, SC_GATHER_ROWS))


def _dispatch_plan(ids, pos, counts, n_tokens, tg):
    E = N_EXPERTS
    padded = (counts + tg - 1) // tg * tg
    ends = jnp.cumsum(padded)
    off = ends - padded
    onehot = ids[:, :, None] == jnp.arange(E, dtype=jnp.int32)[None, None, :]
    dest = pos + jnp.sum(jnp.where(onehot, off[None, None, :], 0), axis=-1)
    total = n_tokens * TOP_K + E * tg
    tile_start = jnp.arange(total // tg, dtype=jnp.int32) * tg
    tile_expert = jnp.sum((ends[None, :] <= tile_start[:, None]).astype(jnp.int32), axis=1)
    return dest, total, jnp.minimum(tile_expert, E - 1)


def _moe(x, mod, T, lw, final_g, final):
    N = x.shape[0]
    tm = min(MOE_TOKENS, T)
    tps = T // tm
    D, F, E = D_MODEL, EXPERT_FF, N_EXPERTS
    tg = max(MOE_GROUP_ROWS_MIN, min(MOE_GROUP_ROWS_MAX, N * TOP_K // (E * 16)))
    half = D // 2
    x_spec, mod_spec, g_spec = _token_specs(tm, tps)
    list_spec = pl.BlockSpec((TOP_K, tm), lambda i: (0, i))

    tri = jnp.asarray(np.triu(np.ones((tm, tm), np.float32)), BF16)
    hp, ids, wts, pos, base, counts = pl.pallas_call(
        functools.partial(_moe_route_kernel, tm=tm), grid=(N // tm,),
        in_specs=[x_spec, mod_spec, g_spec, _full((E, D)), _full((E, V7X_LANES)), _full((D, 2 * F)), _full((F, D)),
                  _full((tm, tm))],
        out_specs=[pl.BlockSpec((tm, half), lambda i: (i, 0)), list_spec, list_spec, list_spec, x_spec,
                   _full((E, V7X_LANES))],
        out_shape=[jax.ShapeDtypeStruct((N, half), jnp.int32), jax.ShapeDtypeStruct((TOP_K, N), jnp.int32),
                   jax.ShapeDtypeStruct((TOP_K, N), F32), jax.ShapeDtypeStruct((TOP_K, N), jnp.int32),
                   jax.ShapeDtypeStruct((N, D), F32), jax.ShapeDtypeStruct((E, V7X_LANES), F32)],
        compiler_params=_params("arbitrary"), name="moe_route",
    )(x, mod, lw["norm2_g"], lw["router_wt"], lw["router_bias"], lw["sh_gu"], lw["sh_d"], tri)

    dest, total, tile_expert = _dispatch_plan(ids, pos, counts[:, 0].astype(jnp.int32), N, tg)
    xs = _scatter_rows(hp, dest, total)
    n_tiles = total // tg
    ys = pl.pallas_call(
        _moe_expert_kernel,
        grid_spec=pltpu.PrefetchScalarGridSpec(
            num_scalar_prefetch=1, grid=(n_tiles,),
            in_specs=[pl.BlockSpec((tg, half), lambda i, te: (i, 0)),
                      pl.BlockSpec((None, D, 2 * F), lambda i, te: (te[i], 0, 0)),
                      pl.BlockSpec((None, F, D), lambda i, te: (te[i], 0, 0))],
            out_specs=pl.BlockSpec((tg, half), lambda i, te: (i, 0))),
        out_shape=jax.ShapeDtypeStruct(xs.shape, jnp.int32),
        compiler_params=_params("parallel"), name="moe_experts",
    )(tile_expert, xs, lw["exp_gu"], lw["exp_d"])

    yk = _gather_rows(ys, dest.reshape(-1)).reshape(TOP_K, N, half)
    return pl.pallas_call(
        functools.partial(_moe_combine_kernel, final=final), grid=(N // tm,),
        in_specs=[x_spec, mod_spec, pl.BlockSpec((TOP_K, tm, half), lambda i: (0, i, 0)),
                  pl.BlockSpec((tm, TOP_K), lambda i: (i, 0)), g_spec],
        out_specs=x_spec, out_shape=jax.ShapeDtypeStruct((N, D), F32),
        compiler_params=_params("parallel"), name="moe_combine",
    )(base, mod, yk, wts.T, final_g)


def _swap_halves(w, heads):
    shp = w.shape
    w = w.reshape(shp[:-1] + (heads, 2, HEAD_DIM // 2))
    return jnp.flip(w, axis=-2).reshape(shp)


def _block_diag2(a, b):
    za = jnp.zeros((a.shape[0], b.shape[1]), a.dtype)
    zb = jnp.zeros((b.shape[0], a.shape[1]), a.dtype)
    return jnp.concatenate([jnp.concatenate([a, za], 1), jnp.concatenate([zb, b], 1)], 0)


def _layer_weights(P, l):
    w_in = P["w_in"][l]
    o1, o2, o3 = RET_COLS, RET_COLS + ATT_COLS, RET_COLS + ATT_COLS + RW_COLS
    w_ret, w_att, w_rw, w_gate = w_in[:, :o1], w_in[:, o1:o2], w_in[:, o2:o3], w_in[:, o3:]
    W = RET_WIDTH
    w_ret = jnp.concatenate([w_ret, _swap_halves(w_ret[:, 0:W], RET_HEADS),
                             _swap_halves(w_ret[:, W:2 * W], RET_HEADS)], axis=1)
    QW, KW = ATT_Q_WIDTH, ATT_KV_WIDTH
    w_att = jnp.concatenate([w_att, _swap_halves(w_att[:, 0:QW], ATT_Q_HEADS),
                             _swap_halves(w_att[:, QW:QW + KW], ATT_KV_HEADS)], axis=1)
    qg, kg = P["q_norm_g"][l], P["k_norm_g"][l]
    pad = lambda v: jnp.zeros((QW,), F32).at[:v.shape[0]].set(v)
    qk_gain = jnp.zeros((8, QW), F32)
    qk_gain = qk_gain.at[0].set(jnp.tile(qg, ATT_Q_HEADS))
    qk_gain = qk_gain.at[1].set(jnp.tile(_swap_halves(qg, 1), ATT_Q_HEADS))
    qk_gain = qk_gain.at[2].set(pad(jnp.tile(kg, ATT_KV_HEADS)))
    qk_gain = qk_gain.at[3].set(pad(jnp.tile(_swap_halves(kg, 1), ATT_KV_HEADS)))
    rw_vec = jnp.zeros((8, RWKV_WIDTH), F32)
    for j, name in enumerate(("rw_w0_f", "rw_w0_b", "rw_a0_f", "rw_a0_b", "rw_k_k", "rw_k_a")):
        rw_vec = rw_vec.at[j].set(P[name][l])
    rw_vec = rw_vec.at[6].set(P["rw_r_k"][l].reshape(RWKV_WIDTH))
    row = lambda v: v.reshape(1, -1)
    return {
        "norm1_g": row(P["norm1_g"][l]), "norm2_g": row(P["norm2_g"][l]),
        "w_ret": w_ret.astype(BF16), "w_att": w_att.astype(BF16), "w_rw": w_rw.astype(BF16),
        "w_gate": w_gate.astype(BF16), "qk_gain": qk_gain, "ret_gn": row(P["ret_gn"][l]),
        "shift_mu": P["shift_mu"][l], "rw_vec": rw_vec,
        "rw_w2": _block_diag2(P["rw_w2_f"][l], P["rw_w2_b"][l]).astype(BF16),
        "rw_a2": _block_diag2(P["rw_a2_f"][l], P["rw_a2_b"][l]).astype(BF16),
        "rw_g2": P["rw_g2"][l].astype(BF16), "rw_gn": row(P["rw_gn"][l]),
        "w_branch_a": P["w_branch_a"][l].astype(BF16), "w_branch_b": P["w_branch_b"][l].astype(BF16),
        "w_branch_c": P["w_branch_c"][l].astype(BF16), "w_out": P["w_out"][l].astype(BF16),
        "router_wt": P["router_w"][l].T,
        "router_bias": jnp.broadcast_to(P["router_bias"][l][:, None], (N_EXPERTS, V7X_LANES)),
        "sh_gu": jnp.concatenate([P["sh_w_gate"][l], P["sh_w_up"][l]], axis=1).astype(BF16),
        "sh_d": P["sh_w_down"][l].astype(BF16),
        "exp_gu": jnp.concatenate([P["exp_w_gate"][l], P["exp_w_up"][l]], axis=2).astype(BF16),
        "exp_d": P["exp_w_down"][l].astype(BF16),
    }


def _ones_block_diag(n):
    idx = np.arange(n) // HEAD_DIM
    return jnp.asarray(idx[:, None] == idx[None, :], BF16)


def _tables(T):
    rows = T // GRID_W
    row = jnp.repeat(jnp.arange(rows, dtype=F32), GRID_W)
    col = jnp.tile(jnp.arange(GRID_W, dtype=F32), rows)
    freqs = ROPE_THETA ** (-jnp.arange(ROPE_PAIRS_PER_AXIS, dtype=F32) / ROPE_PAIRS_PER_AXIS)
    ang = jnp.concatenate([row[:, None] * freqs, col[:, None] * freqs], axis=-1)
    cos, sin = jnp.cos(ang), jnp.sin(ang)
    return {
        "cos": jnp.tile(jnp.concatenate([cos, cos], axis=-1), (1, ATT_Q_HEADS)),
        "sin": jnp.tile(jnp.concatenate([-sin, sin], axis=-1), (1, ATT_Q_HEADS)),
        "ones128": _ones_block_diag(128), "ones256": _ones_block_diag(256), "ones512": _ones_block_diag(512),
    }


def _trunks(requests, P, layer_weights):
    groups = []
    for x, c in requests:
        B, T, D = x.shape
        groups.append({"B": B, "T": T, "x": x.reshape(B * T, D), "tabs": _tables(T),
                       "mods": _ada_mod(c, P["ada_w"], P["ada_b"])})
    final_g = P["final_g"].reshape(1, D_MODEL)
    depth = len(layer_weights)
    for l, lw in enumerate(layer_weights):
        mixed = []
        for g in groups:
            B, T, tabs, mod = g["B"], g["T"], g["tabs"], g["mods"][l]
            ret, q, k, v, gates, rwp = _input_projections(g["x"], mod, T, lw, tabs)
            ya = _retention(ret, B, T, lw["ret_gn"], tabs["ones256"])
            yb = _attention(q, k, v, B, T)
            mixed.append((mod, ya, yb, gates, rwp))
        scans = _rwkv_scan([(m[4], g["B"], g["T"]) for m, g in zip(mixed, groups)], groups[0]["tabs"]["ones256"])
        for g, (mod, ya, yb, gates, rwp), (yf, ybk) in zip(groups, mixed, scans):
            x = _merge(g["x"], mod, g["T"], ya, yb, yf, ybk, rwp, gates, lw, g["tabs"]["ones256"])
            g["x"] = _moe(x, mod, g["T"], lw, final_g, final=(l == depth - 1))
    return [g["x"].reshape(g["B"], g["T"], D_MODEL) for g in groups]


def kernel(x_prompt, x_sample, c_prompt, c_sample, norm1_g, norm2_g, final_g, ada_w, ada_b, w_in, ret_gn, q_norm_g, k_norm_g, shift_mu, rw_w0_f, rw_w2_f, rw_w0_b, rw_w2_b, rw_a0_f, rw_a2_f, rw_a0_b, rw_a2_b, rw_g2, rw_k_k, rw_k_a, rw_r_k, rw_gn, w_branch_a, w_branch_b, w_branch_c, w_out, router_w, router_bias, exp_w_gate, exp_w_up, exp_w_down, sh_w_gate, sh_w_up, sh_w_down):
    P = {
        "norm1_g": norm1_g, "norm2_g": norm2_g, "final_g": final_g, "ada_w": ada_w, "ada_b": ada_b,
        "w_in": w_in, "ret_gn": ret_gn, "q_norm_g": q_norm_g, "k_norm_g": k_norm_g, "shift_mu": shift_mu,
        "rw_w0_f": rw_w0_f, "rw_w2_f": rw_w2_f, "rw_w0_b": rw_w0_b, "rw_w2_b": rw_w2_b,
        "rw_a0_f": rw_a0_f, "rw_a2_f": rw_a2_f, "rw_a0_b": rw_a0_b, "rw_a2_b": rw_a2_b,
        "rw_g2": rw_g2, "rw_k_k": rw_k_k, "rw_k_a": rw_k_a, "rw_r_k": rw_r_k, "rw_gn": rw_gn,
        "w_branch_a": w_branch_a, "w_branch_b": w_branch_b, "w_branch_c": w_branch_c, "w_out": w_out,
        "router_w": router_w, "router_bias": router_bias,
        "exp_w_gate": exp_w_gate, "exp_w_up": exp_w_up, "exp_w_down": exp_w_down,
        "sh_w_gate": sh_w_gate, "sh_w_up": sh_w_up, "sh_w_down": sh_w_down,
    }
    layer_weights = [_layer_weights(P, l) for l in range(w_in.shape[0])]
    y_prompt, y_sample = _trunks([(x_prompt, c_prompt), (x_sample, c_sample)], P, layer_weights)
    return (y_prompt, y_sample)
```

```python
import functools

import numpy as np
import jax
import jax.numpy as jnp
from jax import lax
from jax.experimental import pallas as pl
from jax.experimental.pallas import tpu as pltpu
from jax.experimental.pallas import tpu_sc as plsc

F32 = jnp.float32
BF16 = jnp.bfloat16

D_MODEL = 1024
DEPTH = 2
GRID_W = 64
HEAD_DIM = 64
ROPE_THETA = 10000.0
ROPE_PAIRS_PER_AXIS = HEAD_DIM // 4

RET_HEADS = 4
RET_WIDTH = RET_HEADS * HEAD_DIM
RET_CHUNK = 128
RET_CHUNKS_PER_STEP = 4
RET_EPS = 1e-5

ATT_Q_HEADS = 8
ATT_KV_HEADS = 2
ATT_GROUP = ATT_Q_HEADS // ATT_KV_HEADS
ATT_Q_WIDTH = ATT_Q_HEADS * HEAD_DIM
ATT_KV_WIDTH = ATT_KV_HEADS * HEAD_DIM

RWKV_HEADS = 4
RWKV_WIDTH = RWKV_HEADS * HEAD_DIM
DECAY_LORA = 64
AAA_LORA = 64
GATE_LORA = 128
RWKV_GN_EPS = 64e-5
RW_CHUNK = 64
RW_FIELDS = 11
RW_BATCH_ROWS = 4

FLASH_Q_ROWS = 512
FLASH_KV_ROWS = 512
FLASH_SUB_ROWS = 1024

MOE_TOKENS = 512
MOE_GROUP_ROWS_MIN = 256
MOE_GROUP_ROWS_MAX = 512
SC_GATHER_ROWS = 128

N_EXPERTS = 64
TOP_K = 8
N_GROUPS = 8
TOPK_GROUPS = 4
EXPERT_FF = 256
ROUTED_SCALE = 2.5
NORM_EPS = 1e-6

RET_COLS = 4 * RET_WIDTH
ATT_COLS = ATT_Q_WIDTH + 2 * ATT_KV_WIDTH
RW_COLS = 3 * RWKV_WIDTH + 2 * DECAY_LORA + 2 * AAA_LORA + GATE_LORA
GATE_COLS = 3 * D_MODEL

V7X_LANES = 128
VMEM_LIMIT_BYTES = 48 * 1024 * 1024


def _params(*dims):
    return pltpu.CompilerParams(dimension_semantics=dims, vmem_limit_bytes=VMEM_LIMIT_BYTES)


def _sigmoid(x):
    return 1.0 / (1.0 + jnp.exp(-x))


def _silu(x):
    return x * _sigmoid(x)


def _dot(a, b):
    return jnp.dot(a, b, preferred_element_type=F32)


def _dot_nt(a, b):
    return lax.dot_general(a, b, (((1,), (1,)), ((), ())), preferred_element_type=F32)


def _split(a):
    hi = a.astype(BF16)
    lo = (a - hi.astype(F32)).astype(BF16)
    return hi, lo


def _dot_split_lhs(a, b_bf16):
    hi, lo = _split(a)
    return _dot(hi, b_bf16) + _dot(lo, b_bf16)


def _dot3(a, b):
    ah, al = _split(a)
    bh, bl = _split(b)
    return _dot(ah, bh) + _dot(ah, bl) + _dot(al, bh)


def _norm_mod(x, gain, shift, scale):
    ms = jnp.mean(x * x, axis=-1, keepdims=True)
    return x * lax.rsqrt(ms + NORM_EPS) * gain * (1.0 + scale) + shift


def _head_layer_norm(y, ones_bd, eps):
    mean = _dot_split_lhs(y, ones_bd) * (1.0 / HEAD_DIM)
    yc = y - mean
    var = _dot_split_lhs(yc * yc, ones_bd) * (1.0 / HEAD_DIM)
    return yc * lax.rsqrt(var + eps)


def _ada_kernel(c_ref, w_ref, b_ref, o_ref):
    c = c_ref[...]
    o_ref[...] = _dot3(_silu(c), w_ref[...]) + b_ref[...]


def _ada_mod(c, ada_w, ada_b):
    B = c.shape[0]
    Bp = max(8, B)
    cp = jnp.zeros((Bp, D_MODEL), F32).at[:B].set(c)
    L = ada_w.shape[0]
    tn = 1536
    out = pl.pallas_call(
        _ada_kernel,
        grid=(L, 6 * D_MODEL // tn),
        in_specs=[pl.BlockSpec((Bp, D_MODEL), lambda l, j: (0, 0)),
                  pl.BlockSpec((None, D_MODEL, tn), lambda l, j: (l, 0, j)),
                  pl.BlockSpec((None, 1, tn), lambda l, j: (l, 0, j))],
        out_specs=pl.BlockSpec((None, Bp, tn), lambda l, j: (l, 0, j)),
        out_shape=jax.ShapeDtypeStruct((L, Bp, 6 * D_MODEL), F32),
        compiler_params=_params("parallel", "parallel"),
        name="ada_mod",
    )(cp, ada_w, ada_b.reshape(L, 1, 6 * D_MODEL))
    return out[:, :B].reshape(L, B, 6, D_MODEL)


def _proj_ret_kernel(x_ref, mod_ref, g_ref, w_ref, cos_ref, sin_ref, o_ref):
    h = _norm_mod(x_ref[...], g_ref[...], mod_ref[0:1, :], mod_ref[1:2, :]).astype(BF16)
    p = _dot(h, w_ref[...])
    W = RET_WIDTH
    cos = cos_ref[:, 0:W]
    sin = sin_ref[:, 0:W]
    o_ref[:, 0:W] = (p[:, 0:W] * cos + p[:, 4 * W:5 * W] * sin).astype(BF16)
    o_ref[:, W:2 * W] = ((p[:, W:2 * W] * cos + p[:, 5 * W:6 * W] * sin) * (HEAD_DIM ** -0.5)).astype(BF16)
    o_ref[:, 2 * W:3 * W] = p[:, 2 * W:3 * W].astype(BF16)
    o_ref[:, 3 * W:4 * W] = _silu(p[:, 3 * W:4 * W]).astype(BF16)


def _proj_att_kernel(x_ref, mod_ref, g_ref, w_ref, cos_ref, sin_ref, qkg_ref, ones_ref, q_ref, k_ref, v_ref):
    h = _norm_mod(x_ref[...], g_ref[...], mod_ref[0:1, :], mod_ref[1:2, :]).astype(BF16)
    p = _dot(h, w_ref[...])
    QW, KW = ATT_Q_WIDTH, ATT_KV_WIDTH
    q = p[:, 0:QW]
    k = p[:, QW:QW + KW]
    v = p[:, QW + KW:QW + 2 * KW]
    qs = p[:, QW + 2 * KW:2 * QW + 2 * KW]
    ks = p[:, 2 * QW + 2 * KW:2 * QW + 3 * KW]
    ones = ones_ref[...]
    rq = lax.rsqrt(_dot_split_lhs(q * q, ones) * (1.0 / HEAD_DIM) + NORM_EPS)
    rk = lax.rsqrt(_dot_split_lhs(k * k, ones[0:KW, 0:KW]) * (1.0 / HEAD_DIM) + NORM_EPS)
    cos = cos_ref[...]
    sin = sin_ref[...]
    qo = (q * qkg_ref[0:1, :] * cos + qs * qkg_ref[1:2, :] * sin) * (rq * (HEAD_DIM ** -0.5))
    ko = (k * qkg_ref[2:3, 0:KW] * cos[:, 0:KW] + ks * qkg_ref[3:4, 0:KW] * sin[:, 0:KW]) * rk
    q_ref[...] = qo.astype(BF16)
    kb = ko.astype(BF16)
    vb = v.astype(BF16)
    for hk in range(ATT_KV_HEADS):
        k_ref[hk] = kb[:, hk * HEAD_DIM:(hk + 1) * HEAD_DIM]
        v_ref[hk, :, 0:HEAD_DIM] = vb[:, hk * HEAD_DIM:(hk + 1) * HEAD_DIM]
        v_ref[hk, :, HEAD_DIM:2 * HEAD_DIM] = jnp.ones((vb.shape[0], HEAD_DIM), BF16)


def _proj_gate_kernel(x_ref, mod_ref, g_ref, w_ref, o_ref):
    h = _norm_mod(x_ref[...], g_ref[...], mod_ref[0:1, :], mod_ref[1:2, :]).astype(BF16)
    o_ref[...] = _sigmoid(_dot(h, w_ref[...])).astype(BF16)


def _proj_rw_kernel(x_ref, xp_ref, xn_ref, mod_ref, g_ref, w_ref, mu_ref, vec_ref, w2_ref, a2_ref, g2_ref,
                    ones_ref, o_ref, p_scr, *, tm, tiles_per_seq):
    i = pl.program_id(0)
    gain, shift, scale = g_ref[...], mod_ref[0:1, :], mod_ref[1:2, :]
    w = w_ref[...]
    h = _norm_mod(x_ref[...], gain, shift, scale).astype(BF16)
    p = _dot(h, w)
    hp = _norm_mod(xp_ref[...], gain, shift, scale).astype(BF16)
    hn = _norm_mod(xn_ref[...], gain, shift, scale).astype(BF16)
    first = (i % tiles_per_seq) == 0
    last = (i % tiles_per_seq) == tiles_per_seq - 1
    prev_row = jnp.where(first, 0.0, _dot(hp, w)[7:8, :])
    next_row = jnp.where(last, 0.0, _dot(hn, w)[0:1, :])
    p_scr[8:8 + tm, :] = p
    p_scr[7:8, :] = prev_row
    p_scr[8 + tm:9 + tm, :] = next_row
    prev = p_scr[7:7 + tm, :]
    nxt = p_scr[9:9 + tm, :]
    x = p + mu_ref[0:1, :] * (prev - p) + mu_ref[1:2, :] * (nxt - p)

    W = RWKV_WIDTH
    r, k, v = x[:, 0:W], x[:, W:2 * W], x[:, 2 * W:3 * W]
    xw = x[:, 3 * W:3 * W + 2 * DECAY_LORA]
    xa = x[:, 3 * W + 2 * DECAY_LORA:3 * W + 2 * DECAY_LORA + 2 * AAA_LORA]
    xg = x[:, 3 * W + 2 * DECAY_LORA + 2 * AAA_LORA:]
    w0_f, w0_b = vec_ref[0:1, :], vec_ref[1:2, :]
    a0_f, a0_b = vec_ref[2:3, :], vec_ref[3:4, :]
    k_k, k_a, r_k = vec_ref[4:5, :], vec_ref[5:6, :], vec_ref[6:7, :]
    wl = _dot(jnp.tanh(xw).astype(BF16), w2_ref[...])
    al = _dot(xa.astype(BF16), a2_ref[...])
    dec_c = float(np.exp(-0.5))
    w_f = jnp.exp(-dec_c * _sigmoid(w0_f + wl[:, 0:W]))
    w_b = jnp.exp(-dec_c * _sigmoid(w0_b + wl[:, W:2 * W]))
    a_f = _sigmoid(a0_f + al[:, 0:W])
    a_b = _sigmoid(a0_b + al[:, W:2 * W])
    gate = _dot(_sigmoid(xg).astype(BF16), g2_ref[...])
    ones = ones_ref[...]
    kk = k * k_k
    kk = kk * lax.rsqrt(_dot_split_lhs(kk * kk, ones) + 1e-12)
    k_f = k * (1.0 + (a_f - 1.0) * k_a)
    k_b = k * (1.0 + (a_b - 1.0) * k_a)
    bonus = _dot_split_lhs(r * k_f * r_k, ones) * v
    for j, val in enumerate((r, kk, v, w_f, w_b, k_f, k_b, kk * a_f, kk * a_b, gate, bonus)):
        o_ref[:, j * W:(j + 1) * W] = val


def _token_specs(tm, tiles_per_seq):
    x_spec = pl.BlockSpec((tm, D_MODEL), lambda i: (i, 0))
    mod_spec = pl.BlockSpec((None, 6, D_MODEL), lambda i: (i // tiles_per_seq, 0, 0))
    g_spec = pl.BlockSpec((1, D_MODEL), lambda i: (0, 0))
    return x_spec, mod_spec, g_spec


def _full(shape):
    nd = len(shape)
    return pl.BlockSpec(shape, lambda *_: (0,) * nd)


def _input_projections(x, mod, T, lw, tabs):
    N = x.shape[0]
    tm = min(512, T)
    tps = T // tm
    grid = (N // tm,)
    x_spec, mod_spec, g_spec = _token_specs(tm, tps)
    tab_spec = pl.BlockSpec((tm, ATT_Q_WIDTH), lambda i: (i % tps, 0))
    cos, sin = tabs["cos"], tabs["sin"]

    ret = pl.pallas_call(
        _proj_ret_kernel, grid=grid,
        in_specs=[x_spec, mod_spec, g_spec, _full(lw["w_ret"].shape), tab_spec, tab_spec],
        out_specs=pl.BlockSpec((tm, RET_COLS), lambda i: (i, 0)),
        out_shape=jax.ShapeDtypeStruct((N, RET_COLS), BF16),
        compiler_params=_params("parallel"), name="proj_ret",
    )(x, mod, lw["norm1_g"], lw["w_ret"], cos, sin)

    q, k, v = pl.pallas_call(
        _proj_att_kernel, grid=grid,
        in_specs=[x_spec, mod_spec, g_spec, _full(lw["w_att"].shape), tab_spec, tab_spec,
                  _full(lw["qk_gain"].shape), _full(tabs["ones512"].shape)],
        out_specs=[pl.BlockSpec((tm, ATT_Q_WIDTH), lambda i: (i, 0)),
                   pl.BlockSpec((ATT_KV_HEADS, tm, HEAD_DIM), lambda i: (0, i, 0)),
                   pl.BlockSpec((ATT_KV_HEADS, tm, 2 * HEAD_DIM), lambda i: (0, i, 0))],
        out_shape=[jax.ShapeDtypeStruct((N, ATT_Q_WIDTH), BF16),
                   jax.ShapeDtypeStruct((ATT_KV_HEADS, N, HEAD_DIM), BF16),
                   jax.ShapeDtypeStruct((ATT_KV_HEADS, N, 2 * HEAD_DIM), BF16)],
        compiler_params=_params("parallel"), name="proj_att",
    )(x, mod, lw["norm1_g"], lw["w_att"], cos, sin, lw["qk_gain"], tabs["ones512"])

    gates = pl.pallas_call(
        _proj_gate_kernel, grid=grid,
        in_specs=[x_spec, mod_spec, g_spec, _full(lw["w_gate"].shape)],
        out_specs=pl.BlockSpec((tm, GATE_COLS), lambda i: (i, 0)),
        out_shape=jax.ShapeDtypeStruct((N, GATE_COLS), BF16),
        compiler_params=_params("parallel"), name="proj_gate",
    )(x, mod, lw["norm1_g"], lw["w_gate"])

    tm_rw = min(256, T)
    tps_rw = T // tm_rw
    x_spec, mod_spec, g_spec = _token_specs(tm_rw, tps_rw)
    rows8 = tm_rw // 8
    last8 = N // 8 - 1
    rwp = pl.pallas_call(
        functools.partial(_proj_rw_kernel, tm=tm_rw, tiles_per_seq=tps_rw), grid=(N // tm_rw,),
        in_specs=[x_spec,
                  pl.BlockSpec((8, D_MODEL), lambda i: (jnp.maximum(i * rows8 - 1, 0), 0)),
                  pl.BlockSpec((8, D_MODEL), lambda i: (jnp.minimum((i + 1) * rows8, last8), 0)),
                  mod_spec, g_spec, _full(lw["w_rw"].shape), _full(lw["shift_mu"].shape),
                  _full(lw["rw_vec"].shape), _full(lw["rw_w2"].shape), _full(lw["rw_a2"].shape),
                  _full(lw["rw_g2"].shape), _full(tabs["ones256"].shape)],
        out_specs=pl.BlockSpec((tm_rw, RW_FIELDS * RWKV_WIDTH), lambda i: (i, 0)),
        out_shape=jax.ShapeDtypeStruct((N, RW_FIELDS * RWKV_WIDTH), F32),
        scratch_shapes=[pltpu.VMEM((tm_rw + 16, RW_COLS), F32)],
        compiler_params=_params("parallel"), name="proj_rw",
    )(x, x, x, mod, lw["norm1_g"], lw["w_rw"], lw["shift_mu"], lw["rw_vec"], lw["rw_w2"], lw["rw_a2"],
      lw["rw_g2"], tabs["ones256"])
    return ret, q, k, v, gates, rwp


def _retention_tables(reverse):
    lg = np.log1p(-np.exp2(-5.0 - np.arange(RET_HEADS, dtype=np.float64)))
    if reverse:
        lg = lg[::-1]
    C = RET_CHUNK
    pos = np.arange(C, dtype=np.float64)
    diff = pos[:, None] - pos[None, :]
    if reverse:
        dec = np.where(diff < 0, np.exp(lg[:, None, None] * np.maximum(-diff, 0.0)[None]), 0.0)
        xi = np.exp(lg[:, None] * (C - pos)[None, :])
        zeta = np.exp(lg[:, None] * pos[None, :])
    else:
        dec = np.where(diff >= 0, np.exp(lg[:, None, None] * np.maximum(diff, 0.0)[None]), 0.0)
        xi = np.exp(lg[:, None] * (pos + 1.0)[None, :])
        zeta = np.exp(lg[:, None] * (C - 1.0 - pos)[None, :])
    chunk_decay = tuple(float(v) for v in np.exp(lg * C))
    widen = lambda a: jnp.asarray(np.repeat(a.T, HEAD_DIM, axis=1), F32)
    return jnp.asarray(dec, F32), widen(xi), widen(zeta), chunk_decay


def _retention_kernel(ret_ref, dec_ref, xi_ref, zeta_ref, *rest, chunk_decay, final, chunks):
    if final:
        yf_ref, gn_ref, ones_ref, o_ref, state, y_scr = rest
    else:
        o_ref, state, y_scr = rest

    @pl.when(pl.program_id(1) == 0)
    def _():
        state[...] = jnp.zeros_like(state)

    W, C = RET_WIDTH, RET_CHUNK
    for step in range(chunks):
        cb = chunks - 1 - step if final else step
        rows = slice(cb * C, (cb + 1) * C)
        q = ret_ref[rows, 0:W]
        k = ret_ref[rows, W:2 * W]
        v = ret_ref[rows, 2 * W:3 * W]
        qx = (q.astype(F32) * xi_ref[...]).astype(BF16)
        kzt = (k.astype(F32) * zeta_ref[...]).T
        for h in range(RET_HEADS):
            sl = slice(h * HEAD_DIM, (h + 1) * HEAD_DIM)
            s = _dot_nt(q[:, sl], k[:, sl]) * dec_ref[h]
            inner = _dot(s.astype(BF16), v[:, sl])
            r_prev = state[h]
            cross = _dot(qx[:, sl], r_prev.astype(BF16))
            kv = _dot(kzt[sl, :].astype(BF16), v[:, sl])
            state[h] = r_prev * chunk_decay[h] + kv
            y_scr[rows, sl] = inner + cross
    y = y_scr[...]
    if final:
        y = _head_layer_norm(y + yf_ref[...], ones_ref[...], RET_EPS) * gn_ref[...]
        o_ref[...] = (y * ret_ref[:, 3 * W:4 * W].astype(F32)).astype(BF16)
    else:
        o_ref[...] = y


def _retention(ret, B, T, ret_gn, ones256):
    N = ret.shape[0]
    chunks = min(RET_CHUNKS_PER_STEP, T // RET_CHUNK)
    R = chunks * RET_CHUNK
    nblk = T // R
    scratch = [pltpu.VMEM((RET_HEADS, HEAD_DIM, HEAD_DIM), F32), pltpu.VMEM((R, RET_WIDTH), F32)]

    dec, xi, zeta, cd = _retention_tables(False)
    fwd_row = lambda b, c: (b * nblk + c, 0)
    yf = pl.pallas_call(
        functools.partial(_retention_kernel, chunk_decay=cd, final=False, chunks=chunks), grid=(B, nblk),
        in_specs=[pl.BlockSpec((R, RET_COLS), fwd_row), _full(dec.shape), _full(xi.shape), _full(zeta.shape)],
        out_specs=pl.BlockSpec((R, RET_WIDTH), fwd_row),
        out_shape=jax.ShapeDtypeStruct((N, RET_WIDTH), F32),
        scratch_shapes=scratch, compiler_params=_params("parallel", "arbitrary"), name="retention_fwd",
    )(ret, dec, xi, zeta)

    dec, xi, zeta, cd = _retention_tables(True)
    bwd_row = lambda b, c: (b * nblk + nblk - 1 - c, 0)
    return pl.pallas_call(
        functools.partial(_retention_kernel, chunk_decay=cd, final=True, chunks=chunks), grid=(B, nblk),
        in_specs=[pl.BlockSpec((R, RET_COLS), bwd_row), _full(dec.shape), _full(xi.shape), _full(zeta.shape),
                  pl.BlockSpec((R, RET_WIDTH), bwd_row), _full(ret_gn.shape), _full(ones256.shape)],
        out_specs=pl.BlockSpec((R, RET_WIDTH), bwd_row),
        out_shape=jax.ShapeDtypeStruct((N, RET_WIDTH), BF16),
        scratch_shapes=scratch, compiler_params=_params("parallel", "arbitrary"), name="retention_bwd",
    )(ret, dec, xi, zeta, yf, ret_gn, ones256)


def _flash_kernel(q_ref, k_ref, v_ref, o_ref, q_scr, m_scr, acc_scr, *, tq, tk):
    j = pl.program_id(3)

    @pl.when(j == 0)
    def _():
        for g in range(ATT_GROUP):
            q_scr[g * tq:(g + 1) * tq, :] = q_ref[:, g * HEAD_DIM:(g + 1) * HEAD_DIM]
        m_scr[...] = jnp.full_like(m_scr, -jnp.inf)
        acc_scr[...] = jnp.zeros_like(acc_scr)

    k = k_ref[...]
    v = v_ref[...]
    sub = min(FLASH_SUB_ROWS, ATT_GROUP * tq)
    for r0 in range(0, ATT_GROUP * tq, sub):
        rows = slice(r0, r0 + sub)
        s = _dot_nt(q_scr[rows, :], k)
        m_prev = m_scr[rows, :]
        m_next = jnp.maximum(m_prev, jnp.max(s, axis=1, keepdims=True))
        p = jnp.exp(s - jnp.concatenate([m_next] * (tk // V7X_LANES), axis=1))
        alpha = jnp.exp(m_prev - m_next)
        m_scr[rows, :] = m_next
        acc_scr[rows, :] = alpha * acc_scr[rows, :] + _dot(p.astype(BF16), v)

    @pl.when(j == pl.num_programs(3) - 1)
    def _():
        o = acc_scr[:, 0:HEAD_DIM] / acc_scr[:, HEAD_DIM:2 * HEAD_DIM]
        for g in range(ATT_GROUP):
            o_ref[:, g * HEAD_DIM:(g + 1) * HEAD_DIM] = o[g * tq:(g + 1) * tq, :].astype(BF16)


def _attention(q, k, v, B, T):
    N = q.shape[0]
    tq = min(FLASH_Q_ROWS, T)
    tk = min(FLASH_KV_ROWS, T)
    nq, nk = T // tq, T // tk
    GW = ATT_GROUP * HEAD_DIM
    rows = ATT_GROUP * tq
    return pl.pallas_call(
        functools.partial(_flash_kernel, tq=tq, tk=tk), grid=(B, ATT_KV_HEADS, nq, nk),
        in_specs=[pl.BlockSpec((tq, GW), lambda b, h, i, j: (b * nq + i, h)),
                  pl.BlockSpec((None, tk, HEAD_DIM), lambda b, h, i, j: (h, b * nk + j, 0)),
                  pl.BlockSpec((None, tk, 2 * HEAD_DIM), lambda b, h, i, j: (h, b * nk + j, 0))],
        out_specs=pl.BlockSpec((tq, GW), lambda b, h, i, j: (b * nq + i, h)),
        out_shape=jax.ShapeDtypeStruct((N, ATT_Q_WIDTH), BF16),
        scratch_shapes=[pltpu.VMEM((rows, HEAD_DIM), BF16), pltpu.VMEM((rows, V7X_LANES), F32),
                        pltpu.VMEM((rows, 2 * HEAD_DIM), F32)],
        compiler_params=_params("parallel", "parallel", "parallel", "arbitrary"), name="flash_attention",
    )(q, k, v)


def _rwkv_scan_kernel(*refs, rows_per_set):
    C = RW_CHUNK
    H = C // 2
    W = RWKV_WIDTH
    SUB = 8
    n_sets = len(rows_per_set)
    ones_ref = refs[12 * n_sets]
    outs = refs[12 * n_sets + 1:12 * n_sets + 1 + 2 * n_sets]
    state, vk_scr = refs[-2:]
    dirs = [(refs[12 * s:12 * s + 6], refs[12 * s + 6:12 * s + 12]) for s in range(n_sets)]
    chains = [(s, n, d) for s in range(n_sets) for n in range(rows_per_set[s]) for d in range(2)]

    first_chunk = pl.program_id(1) == 0

    @pl.when(first_chunk)
    def _():
        for c, (s, n, d) in enumerate(chains):
            if s == 0:
                state[c] = jnp.zeros((HEAD_DIM, W), F32)

    @pl.when(jnp.logical_and(first_chunk, pl.program_id(0) == 0))
    def _():
        for c, (s, n, d) in enumerate(chains):
            if s > 0:
                state[c] = jnp.zeros((HEAD_DIM, W), F32)

    ones = ones_ref[...]
    lane = lax.broadcasted_iota(jnp.int32, (1, W), 1) & (HEAD_DIM - 1)
    same_head = (lax.broadcasted_iota(jnp.int32, (W, W), 0) // HEAD_DIM
                 == lax.broadcasted_iota(jnp.int32, (W, W), 1) // HEAD_DIM)
    step_iota = lax.broadcasted_iota(jnp.int32, (H, HEAD_DIM, W), 0)
    lane_iota = lax.broadcasted_iota(jnp.int32, (H, HEAD_DIM, W), 2) & (HEAD_DIM - 1)

    def head_sums(vals):
        stacked = jnp.concatenate([v.astype(BF16) for v in vals], axis=0)
        out = _dot(stacked, ones)
        return [out[c * HEAD_DIM:(c + 1) * HEAD_DIM, :] for c in range(len(vals))]

    carry = (tuple(state[c] for c in range(len(chains))),
             tuple(jnp.zeros((HEAD_DIM, W), F32) for _ in chains))
    for half in range(2):
        lo = (half * H, (1 - half) * H)
        for c, (s, n, d) in enumerate(chains):
            kc = dirs[s][d][3][n]
            kbd = jnp.where(same_head, jnp.concatenate([kc] * (W // C), axis=0), 0.0).astype(BF16)
            sel = jnp.where(step_iota + lo[d] == lane_iota, dirs[s][d][5][n][None], 0.0).reshape(H * HEAD_DIM, W)
            vk_scr[c] = _dot(sel.astype(BF16), kbd).reshape(H, HEAD_DIM, W)

        def group(g, carry, lo=lo):
            states, ycols = list(carry[0]), list(carry[1])
            bases = (lo[0] + g * SUB, lo[1] + (H // SUB - 1 - g) * SUB)
            rows = [[ref[n, pl.ds(pl.multiple_of(bases[d], SUB), SUB), :] for ref in dirs[s][d][:5]]
                    for (s, n, d) in chains]
            for j in range(SUB):
                jj = (j, SUB - 1 - j)
                row = lambda c, f: rows[c][f][jj[chains[c][2]]:jj[chains[c][2]] + 1, :]
                sa = head_sums([states[c] * row(c, 0) for c in range(len(chains))])
                new = [states[c] * row(c, 2) - sa[c] * row(c, 4)
                       + vk_scr[c, bases[chains[c][2]] + jj[chains[c][2]] - lo[chains[c][2]]]
                       for c in range(len(chains))]
                read = [new[c] if chains[c][2] == 0 else states[c] for c in range(len(chains))]
                y = head_sums([read[c] * row(c, 1) for c in range(len(chains))])
                for c, (s, n, d) in enumerate(chains):
                    ycols[c] = jnp.where(lane == bases[d] + jj[d], y[c], ycols[c])
                states = new
            return tuple(states), tuple(ycols)

        carry = lax.fori_loop(0, H // SUB, group, carry)

    states, ycols = carry
    for c, (s, n, d) in enumerate(chains):
        state[c] = states[c]
        outs[2 * s + d][n] = ycols[c]


def _rwkv_scan_call(sets, ones256):
    W, C = RWKV_WIDTH, RW_CHUNK
    _, B0, T0, nb0 = sets[0]
    nc0 = T0 // C
    operands, in_specs, out_specs, out_shapes = [], [], [], []
    for s, (rwp, B, T, nb) in enumerate(sets):
        nc = T // C
        fields = rwp.reshape(B, T, RW_FIELDS * W)
        v = rwp[:, 2 * W:3 * W].reshape(B, nc, C, RWKV_HEADS, HEAD_DIM)
        vt = v.transpose(0, 1, 4, 3, 2).reshape(B, nc, HEAD_DIM, W)

        def block(b, c, reverse, s=s, nc=nc):
            bi, ci = (b, c) if s == 0 else (0, b * nc0 + c)
            return bi, (nc - 1 - ci if reverse else ci)

        field = lambda f, rev, nb=nb, block=block: pl.BlockSpec(
            (nb, C, W), lambda b, c: block(b, c, rev) + (f,))
        vt_spec = lambda rev, nb=nb, block=block: pl.BlockSpec(
            (nb, None, HEAD_DIM, W), lambda b, c: block(b, c, rev) + (0, 0))
        in_specs += [field(1, False), field(0, False), field(3, False), field(5, False), field(7, False),
                     vt_spec(False),
                     field(1, True), field(0, True), field(4, True), field(6, True), field(8, True),
                     vt_spec(True)]
        operands += [fields] * 5 + [vt] + [fields] * 5 + [vt]
        out_specs += [vt_spec(False), vt_spec(True)]
        out_shapes += [jax.ShapeDtypeStruct((B, nc, HEAD_DIM, W), F32)] * 2
    n_chains = 2 * sum(nb for _, _, _, nb in sets)
    outs = pl.pallas_call(
        functools.partial(_rwkv_scan_kernel, rows_per_set=tuple(nb for _, _, _, nb in sets)),
        grid=(B0 // nb0, nc0), in_specs=in_specs + [_full(ones256.shape)],
        out_specs=out_specs, out_shape=out_shapes,
        scratch_shapes=[pltpu.VMEM((n_chains, HEAD_DIM, W), F32),
                        pltpu.VMEM((n_chains, C // 2, HEAD_DIM, W), F32)],
        compiler_params=_params("arbitrary", "arbitrary"), name="rwkv_scan",
    )(*operands, ones256)
    results = []
    for s, (rwp, B, T, nb) in enumerate(sets):
        untranspose = lambda yt: yt.reshape(B, T // C, HEAD_DIM, RWKV_HEADS, C).transpose(0, 1, 4, 3, 2).reshape(
            B * T, W)
        results.append((untranspose(outs[2 * s]), untranspose(outs[2 * s + 1])))
    return results


def _rwkv_scan(groups, ones256):
    rows = lambda B: RW_BATCH_ROWS if B % RW_BATCH_ROWS == 0 else 1
    if len(groups) == 2:
        (_, B0, T0), (_, B1, T1) = groups
        if B1 == 1 and (B0 // rows(B0)) * (T0 // RW_CHUNK) == T1 // RW_CHUNK:
            return _rwkv_scan_call([groups[0] + (rows(B0),), groups[1] + (1,)], ones256)
    return [_rwkv_scan_call([g + (rows(g[1]),)], ones256)[0] for g in groups]


def _merge_kernel(x_ref, mod_ref, ya_ref, yb_ref, yf_ref, ybk_ref, rg_ref, bonus_ref, gn_ref, gates_ref,
                  ones_ref, wa_ref, wb_ref, wc_ref, wo_ref, o_ref):
    y = _head_layer_norm(yf_ref[...] + ybk_ref[...], ones_ref[...], RWKV_GN_EPS) * gn_ref[...]
    yc = ((y + bonus_ref[...]) * rg_ref[...]).astype(BF16)
    D = D_MODEL
    merged = (gates_ref[:, 0:D].astype(F32) * _dot(ya_ref[...], wa_ref[...])
              + gates_ref[:, D:2 * D].astype(F32) * _dot(yb_ref[...], wb_ref[...])
              + gates_ref[:, 2 * D:3 * D].astype(F32) * _dot(yc, wc_ref[...]))
    o_ref[...] = x_ref[...] + mod_ref[2:3, :] * _dot(merged.astype(BF16), wo_ref[...])


def _merge(x, mod, T, ya, yb, yf, ybk, rwp, gates, lw, ones256):
    N = x.shape[0]
    tm = min(512, T)
    tps = T // tm
    x_spec, mod_spec, _ = _token_specs(tm, tps)
    row = lambda w: pl.BlockSpec((tm, w), lambda i: (i, 0))
    field = lambda f: pl.BlockSpec((tm, RWKV_WIDTH), lambda i: (i, f))
    return pl.pallas_call(
        _merge_kernel, grid=(N // tm,),
        in_specs=[x_spec, mod_spec, row(RET_WIDTH), row(ATT_Q_WIDTH), row(RWKV_WIDTH), row(RWKV_WIDTH),
                  field(9), field(10), _full(lw["rw_gn"].shape), row(GATE_COLS), _full(ones256.shape),
                  _full(lw["w_branch_a"].shape), _full(lw["w_branch_b"].shape), _full(lw["w_branch_c"].shape),
                  _full(lw["w_out"].shape)],
        out_specs=x_spec, out_shape=jax.ShapeDtypeStruct((N, D_MODEL), F32),
        compiler_params=_params("parallel"), name="merge_out",
    )(x, mod, ya, yb, yf, ybk, rwp, rwp, lw["rw_gn"], gates, ones256,
      lw["w_branch_a"], lw["w_branch_b"], lw["w_branch_c"], lw["w_out"])


def _route(h, hb, rwt_ref, rbias_ref, tm):
    h_lo = (h - hb.astype(F32)).astype(BF16)
    rw_hi, rw_lo = _split(rwt_ref[...])
    logits = _dot_nt(rw_hi, hb) + _dot_nt(rw_hi, h_lo) + _dot_nt(rw_lo, hb)
    scores = _sigmoid(logits)
    choice = scores + jnp.concatenate([rbias_ref[...]] * (tm // V7X_LANES), axis=1)
    per_group = N_EXPERTS // N_GROUPS
    sub = lax.broadcasted_iota(jnp.int32, (per_group, tm), 0)
    groups, gscore = [], []
    for g in range(N_GROUPS):
        cg = choice[g * per_group:(g + 1) * per_group, :]
        m1 = jnp.max(cg, axis=0, keepdims=True)
        first = jnp.min(jnp.where(cg == m1, sub, per_group), axis=0, keepdims=True)
        m2 = jnp.max(jnp.where(sub == first, -jnp.inf, cg), axis=0, keepdims=True)
        groups.append(cg)
        gscore.append(m1 + m2)
    masked = []
    for g in range(N_GROUPS):
        beaten = jnp.zeros((1, tm), F32)
        for o in range(N_GROUPS):
            if o == g:
                continue
            wins = (gscore[o] >= gscore[g]) if o < g else (gscore[o] > gscore[g])
            beaten = beaten + jnp.where(wins, 1.0, 0.0)
        keep = jnp.where(beaten < TOPK_GROUPS, 1.0, 0.0)
        masked.append(jnp.where(jnp.broadcast_to(keep, (per_group, tm)) > 0.5, groups[g], -jnp.inf))
    mc = jnp.concatenate(masked, axis=0)
    eidx = lax.broadcasted_iota(jnp.int32, (N_EXPERTS, tm), 0)
    beaten = jnp.zeros((N_EXPERTS, tm), F32)
    for o in range(N_EXPERTS):
        row = jnp.broadcast_to(mc[o:o + 1, :], (N_EXPERTS, tm))
        tie = jnp.where(eidx > o, 1.0, 0.0)
        beaten = beaten + jnp.where(row > mc, 1.0, jnp.where(row == mc, tie, 0.0))
    sel = jnp.where(beaten < TOP_K, 1.0, 0.0)
    wts = jnp.where(beaten < TOP_K, scores, 0.0)
    return wts / jnp.sum(wts, axis=0, keepdims=True) * ROUTED_SCALE, sel


def _selected_lists(comb, sel, rank):
    eidx = lax.broadcasted_iota(jnp.int32, comb.shape, 0).astype(F32)
    prev = jnp.full((1, comb.shape[1]), -1.0, F32)
    ids, wts, pos = [], [], []
    for _ in range(TOP_K):
        cand = jnp.where(sel > 0.5, jnp.where(eidx > prev, eidx, float(N_EXPERTS)), float(N_EXPERTS))
        prev = jnp.min(cand, axis=0, keepdims=True)
        wts.append(jnp.sum(jnp.where(eidx == prev, comb, 0.0), axis=0, keepdims=True))
        pos.append(jnp.sum(jnp.where(eidx == prev, rank, 0.0), axis=0, keepdims=True))
        ids.append(jnp.minimum(prev, N_EXPERTS - 1.0))
    as_int = lambda rows: jnp.concatenate(rows, axis=0).astype(jnp.int32)
    return as_int(ids), jnp.concatenate(wts, axis=0), as_int(pos)


def _pack_halves(y):
    n = y.shape[1] // 2
    hi = pltpu.bitcast(y[:, :n].astype(BF16).astype(F32), jnp.int32)
    lo = pltpu.bitcast(y[:, n:].astype(BF16).astype(F32), jnp.int32)
    return hi | lax.shift_right_logical(lo, 16)


def _unpack_halves(w):
    hi = pltpu.bitcast(w & jnp.int32(-65536), F32)
    lo = pltpu.bitcast(lax.shift_left(w, 16), F32)
    return hi, lo


def _moe_route_kernel(x_ref, mod_ref, g_ref, rwt_ref, rbias_ref, shgu_ref, shd_ref, tri_ref,
                      hp_ref, ids_ref, wts_ref, pos_ref, base_ref, count_ref, *, tm):
    F = EXPERT_FF

    @pl.when(pl.program_id(0) == 0)
    def _():
        count_ref[...] = jnp.zeros_like(count_ref)

    x = x_ref[...]
    h = _norm_mod(x, g_ref[...], mod_ref[3:4, :], mod_ref[4:5, :])
    hb = h.astype(BF16)
    hp_ref[...] = _pack_halves(hb.astype(F32))
    comb, sel = _route(h, hb, rwt_ref, rbias_ref, tm)
    before = count_ref[...]
    inclusive = _dot(sel.astype(BF16), tri_ref[...])
    rank = inclusive - sel + jnp.concatenate([before] * (tm // V7X_LANES), axis=1)
    count_ref[...] = before + jnp.sum(sel, axis=1, keepdims=True)
    ids, wts, pos = _selected_lists(comb, sel, rank)
    ids_ref[...] = ids
    wts_ref[...] = wts
    pos_ref[...] = pos
    gu = _dot(hb, shgu_ref[...])
    shared = _dot((_silu(gu[:, 0:F]) * gu[:, F:2 * F]).astype(BF16), shd_ref[...])
    base_ref[...] = x + mod_ref[5:6, :] * shared


def _moe_expert_kernel(tile_expert_ref, xs_ref, wgu_ref, wd_ref, o_ref):
    del tile_expert_ref
    F = EXPERT_FF
    half = D_MODEL // 2
    hi, lo = _unpack_halves(xs_ref[...])
    gu = _dot(hi.astype(BF16), wgu_ref[0:half, :]) + _dot(lo.astype(BF16), wgu_ref[half:, :])
    act = (_silu(gu[:, 0:F]) * gu[:, F:2 * F]).astype(BF16)
    o_ref[...] = _pack_halves(_dot(act, wd_ref[...]))


def _moe_combine_kernel(base_ref, mod_ref, y_ref, w_ref, fin_ref, o_ref, *, final):
    half = D_MODEL // 2
    acc_hi = jnp.zeros((base_ref.shape[0], half), F32)
    acc_lo = jnp.zeros((base_ref.shape[0], half), F32)
    for k in range(TOP_K):
        hi, lo = _unpack_halves(y_ref[k])
        wk = w_ref[:, k:k + 1]
        acc_hi = acc_hi + wk * hi
        acc_lo = acc_lo + wk * lo
    out_hi = base_ref[:, 0:half] + mod_ref[5:6, 0:half] * acc_hi
    out_lo = base_ref[:, half:] + mod_ref[5:6, half:] * acc_lo
    if final:
        ms = (jnp.sum(out_hi * out_hi, axis=-1, keepdims=True)
              + jnp.sum(out_lo * out_lo, axis=-1, keepdims=True)) * (1.0 / D_MODEL)
        scale = lax.rsqrt(ms + NORM_EPS)
        out_hi = out_hi * scale * fin_ref[:, 0:half]
        out_lo = out_lo * scale * fin_ref[:, half:]
    o_ref[:, 0:half] = out_hi
    o_ref[:, half:] = out_lo


def _gather_rows(table, idx):
    rows, width = idx.shape[0], table.shape[1]
    info = plsc.get_sparse_core_info()
    workers = info.num_cores * info.num_subcores
    windows = rows // (workers * SC_GATHER_ROWS)
    assert windows * workers * SC_GATHER_ROWS == rows
    mesh = plsc.VectorSubcoreMesh(core_axis_name="core", subcore_axis_name="subcore")

    @functools.partial(pl.kernel, out_type=jax.ShapeDtypeStruct((rows, width), table.dtype), mesh=mesh,
                       scratch_types=[pltpu.VMEM((SC_GATHER_ROWS,), jnp.int32),
                                      pltpu.VMEM((SC_GATHER_ROWS, width), table.dtype)],
                       name="moe_gather")
    def gather(table_hbm, idx_hbm, out_hbm, idx_vmem, rows_vmem):
        worker = lax.axis_index("subcore") * info.num_cores + lax.axis_index("core")
        base = worker * (windows * SC_GATHER_ROWS)

        @pl.loop(0, windows)
        def _(j):
            span = pl.ds(pl.multiple_of(base + j * SC_GATHER_ROWS, SC_GATHER_ROWS), SC_GATHER_ROWS)
            pltpu.sync_copy(idx_hbm.at[span], idx_vmem)
            pltpu.sync_copy(table_hbm.at[idx_vmem], rows_vmem)
            pltpu.sync_copy(rows_vmem, out_hbm.at[span])

    return gather(table, idx)


def _scatter_rows(table, dest, total_rows):
    n_lists, n_rows = dest.shape
    width = table.shape[1]
    info = plsc.get_sparse_core_info()
    workers = info.num_cores * info.num_subcores
    windows = n_rows // (workers * SC_GATHER_ROWS)
    assert windows * workers * SC_GATHER_ROWS == n_rows
    windows_per_list = n_rows // SC_GATHER_ROWS
    mesh = plsc.VectorSubcoreMesh(core_axis_name="core", subcore_axis_name="subcore")

    @functools.partial(pl.kernel, out_type=jax.ShapeDtypeStruct((total_rows, width), table.dtype), mesh=mesh,
                       scratch_types=[pltpu.VMEM((1, SC_GATHER_ROWS), jnp.int32),
                                      pltpu.VMEM((SC_GATHER_ROWS, width), table.dtype)],
                       name="moe_scatter")
    def scatter(table_hbm, dest_hbm, out_hbm, idx_vmem, rows_vmem):
        worker = lax.axis_index("subcore") * info.num_cores + lax.axis_index("core")

        @pl.loop(0, windows)
        def _(j):
            window = worker * windows + j
            span = pl.ds(pl.multiple_of(window * SC_GATHER_ROWS, SC_GATHER_ROWS), SC_GATHER_ROWS)
            pltpu.sync_copy(table_hbm.at[span], rows_vmem)
            for k in range(n_lists):
                pltpu.sync_copy(dest_hbm.at[pl.ds(k * windows_per_list + window, 1)], idx_vmem)
                pltpu.sync_copy(rows_vmem, out_hbm.at[idx_vmem.at[0]])

    return scatter(table, dest.reshape(n_lists * windows_per_list, SC_GATHER_ROWS))


def _dispatch_plan(ids, pos, counts, n_tokens, tg):
    E = N_EXPERTS
    padded = (counts + tg - 1) // tg * tg
    ends = jnp.cumsum(padded)
    off = ends - padded
    onehot = ids[:, :, None] == jnp.arange(E, dtype=jnp.int32)[None, None, :]
    dest = pos + jnp.sum(jnp.where(onehot, off[None, None, :], 0), axis=-1)
    total = n_tokens * TOP_K + E * tg
    tile_start = jnp.arange(total // tg, dtype=jnp.int32) * tg
    tile_expert = jnp.sum((ends[None, :] <= tile_start[:, None]).astype(jnp.int32), axis=1)
    return dest, total, jnp.minimum(tile_expert, E - 1)


def _moe(x, mod, T, lw, final_g, final):
    N = x.shape[0]
    tm = min(MOE_TOKENS, T)
    tps = T // tm
    D, F, E = D_MODEL, EXPERT_FF, N_EXPERTS
    tg = max(MOE_GROUP_ROWS_MIN, min(MOE_GROUP_ROWS_MAX, N * TOP_K // (E * 16)))
    half = D // 2
    x_spec, mod_spec, g_spec = _token_specs(tm, tps)
    list_spec = pl.BlockSpec((TOP_K, tm), lambda i: (0, i))

    tri = jnp.asarray(np.triu(np.ones((tm, tm), np.float32)), BF16)
    hp, ids, wts, pos, base, counts = pl.pallas_call(
        functools.partial(_moe_route_kernel, tm=tm), grid=(N // tm,),
        in_specs=[x_spec, mod_spec, g_spec, _full((E, D)), _full((E, V7X_LANES)), _full((D, 2 * F)), _full((F, D)),
                  _full((tm, tm))],
        out_specs=[pl.BlockSpec((tm, half), lambda i: (i, 0)), list_spec, list_spec, list_spec, x_spec,
                   _full((E, V7X_LANES))],
        out_shape=[jax.ShapeDtypeStruct((N, half), jnp.int32), jax.ShapeDtypeStruct((TOP_K, N), jnp.int32),
                   jax.ShapeDtypeStruct((TOP_K, N), F32), jax.ShapeDtypeStruct((TOP_K, N), jnp.int32),
                   jax.ShapeDtypeStruct((N, D), F32), jax.ShapeDtypeStruct((E, V7X_LANES), F32)],
        compiler_params=_params("arbitrary"), name="moe_route",
    )(x, mod, lw["norm2_g"], lw["router_wt"], lw["router_bias"], lw["sh_gu"], lw["sh_d"], tri)

    dest, total, tile_expert = _dispatch_plan(ids, pos, counts[:, 0].astype(jnp.int32), N, tg)
    xs = _scatter_rows(hp, dest, total)
    n_tiles = total // tg
    ys = pl.pallas_call(
        _moe_expert_kernel,
        grid_spec=pltpu.PrefetchScalarGridSpec(
            num_scalar_prefetch=1, grid=(n_tiles,),
            in_specs=[pl.BlockSpec((tg, half), lambda i, te: (i, 0)),
                      pl.BlockSpec((None, D, 2 * F), lambda i, te: (te[i], 0, 0)),
                      pl.BlockSpec((None, F, D), lambda i, te: (te[i], 0, 0))],
            out_specs=pl.BlockSpec((tg, half), lambda i, te: (i, 0))),
        out_shape=jax.ShapeDtypeStruct(xs.shape, jnp.int32),
        compiler_params=_params("parallel"), name="moe_experts",
    )(tile_expert, xs, lw["exp_gu"], lw["exp_d"])

    yk = _gather_rows(ys, dest.reshape(-1)).reshape(TOP_K, N, half)
    return pl.pallas_call(
        functools.partial(_moe_combine_kernel, final=final), grid=(N // tm,),
        in_specs=[x_spec, mod_spec, pl.BlockSpec((TOP_K, tm, half), lambda i: (0, i, 0)),
                  pl.BlockSpec((tm, TOP_K), lambda i: (i, 0)), g_spec],
        out_specs=x_spec, out_shape=jax.ShapeDtypeStruct((N, D), F32),
        compiler_params=_params("parallel"), name="moe_combine",
    )(base, mod, yk, wts.T, final_g)


def _swap_halves(w, heads):
    shp = w.shape
    w = w.reshape(shp[:-1] + (heads, 2, HEAD_DIM // 2))
    return jnp.flip(w, axis=-2).reshape(shp)


def _block_diag2(a, b):
    za = jnp.zeros((a.shape[0], b.shape[1]), a.dtype)
    zb = jnp.zeros((b.shape[0], a.shape[1]), a.dtype)
    return jnp.concatenate([jnp.concatenate([a, za], 1), jnp.concatenate([zb, b], 1)], 0)


def _layer_weights(P, l):
    w_in = P["w_in"][l]
    o1, o2, o3 = RET_COLS, RET_COLS + ATT_COLS, RET_COLS + ATT_COLS + RW_COLS
    w_ret, w_att, w_rw, w_gate = w_in[:, :o1], w_in[:, o1:o2], w_in[:, o2:o3], w_in[:, o3:]
    W = RET_WIDTH
    w_ret = jnp.concatenate([w_ret, _swap_halves(w_ret[:, 0:W], RET_HEADS),
                             _swap_halves(w_ret[:, W:2 * W], RET_HEADS)], axis=1)
    QW, KW = ATT_Q_WIDTH, ATT_KV_WIDTH
    w_att = jnp.concatenate([w_att, _swap_halves(w_att[:, 0:QW], ATT_Q_HEADS),
                             _swap_halves(w_att[:, QW:QW + KW], ATT_KV_HEADS)], axis=1)
    qg, kg = P["q_norm_g"][l], P["k_norm_g"][l]
    pad = lambda v: jnp.zeros((QW,), F32).at[:v.shape[0]].set(v)
    qk_gain = jnp.zeros((8, QW), F32)
    qk_gain = qk_gain.at[0].set(jnp.tile(qg, ATT_Q_HEADS))
    qk_gain = qk_gain.at[1].set(jnp.tile(_swap_halves(qg, 1), ATT_Q_HEADS))
    qk_gain = qk_gain.at[2].set(pad(jnp.tile(kg, ATT_KV_HEADS)))
    qk_gain = qk_gain.at[3].set(pad(jnp.tile(_swap_halves(kg, 1), ATT_KV_HEADS)))
    rw_vec = jnp.zeros((8, RWKV_WIDTH), F32)
    for j, name in enumerate(("rw_w0_f", "rw_w0_b", "rw_a0_f", "rw_a0_b", "rw_k_k", "rw_k_a")):
        rw_vec = rw_vec.at[j].set(P[name][l])
    rw_vec = rw_vec.at[6].set(P["rw_r_k"][l].reshape(RWKV_WIDTH))
    row = lambda v: v.reshape(1, -1)
    return {
        "norm1_g": row(P["norm1_g"][l]), "norm2_g": row(P["norm2_g"][l]),
        "w_ret": w_ret.astype(BF16), "w_att": w_att.astype(BF16), "w_rw": w_rw.astype(BF16),
        "w_gate": w_gate.astype(BF16), "qk_gain": qk_gain, "ret_gn": row(P["ret_gn"][l]),
        "shift_mu": P["shift_mu"][l], "rw_vec": rw_vec,
        "rw_w2": _block_diag2(P["rw_w2_f"][l], P["rw_w2_b"][l]).astype(BF16),
        "rw_a2": _block_diag2(P["rw_a2_f"][l], P["rw_a2_b"][l]).astype(BF16),
        "rw_g2": P["rw_g2"][l].astype(BF16), "rw_gn": row(P["rw_gn"][l]),
        "w_branch_a": P["w_branch_a"][l].astype(BF16), "w_branch_b": P["w_branch_b"][l].astype(BF16),
        "w_branch_c": P["w_branch_c"][l].astype(BF16), "w_out": P["w_out"][l].astype(BF16),
        "router_wt": P["router_w"][l].T,
        "router_bias": jnp.broadcast_to(P["router_bias"][l][:, None], (N_EXPERTS, V7X_LANES)),
        "sh_gu": jnp.concatenate([P["sh_w_gate"][l], P["sh_w_up"][l]], axis=1).astype(BF16),
        "sh_d": P["sh_w_down"][l].astype(BF16),
        "exp_gu": jnp.concatenate([P["exp_w_gate"][l], P["exp_w_up"][l]], axis=2).astype(BF16),
        "exp_d": P["exp_w_down"][l].astype(BF16),
    }


def _ones_block_diag(n):
    idx = np.arange(n) // HEAD_DIM
    return jnp.asarray(idx[:, None] == idx[None, :], BF16)


def _tables(T):
    rows = T // GRID_W
    row = jnp.repeat(jnp.arange(rows, dtype=F32), GRID_W)
    col = jnp.tile(jnp.arange(GRID_W, dtype=F32), rows)
    freqs = ROPE_THETA ** (-jnp.arange(ROPE_PAIRS_PER_AXIS, dtype=F32) / ROPE_PAIRS_PER_AXIS)
    ang = jnp.concatenate([row[:, None] * freqs, col[:, None] * freqs], axis=-1)
    cos, sin = jnp.cos(ang), jnp.sin(ang)
    return {
        "cos": jnp.tile(jnp.concatenate([cos, cos], axis=-1), (1, ATT_Q_HEADS)),
        "sin": jnp.tile(jnp.concatenate([-sin, sin], axis=-1), (1, ATT_Q_HEADS)),
        "ones128": _ones_block_diag(128), "ones256": _ones_block_diag(256), "ones512": _ones_block_diag(512),
    }


def _trunks(requests, P, layer_weights):
    groups = []
    for x, c in requests:
        B, T, D = x.shape
        groups.append({"B": B, "T": T, "x": x.reshape(B * T, D), "tabs": _tables(T),
                       "mods": _ada_mod(c, P["ada_w"], P["ada_b"])})
    final_g = P["final_g"].reshape(1, D_MODEL)
    depth = len(layer_weights)
    for l, lw in enumerate(layer_weights):
        mixed = []
        for g in groups:
            B, T, tabs, mod = g["B"], g["T"], g["tabs"], g["mods"][l]
            ret, q, k, v, gates, rwp = _input_projections(g["x"], mod, T, lw, tabs)
            ya = _retention(ret, B, T, lw["ret_gn"], tabs["ones256"])
            yb = _attention(q, k, v, B, T)
            mixed.append((mod, ya, yb, gates, rwp))
        scans = _rwkv_scan([(m[4], g["B"], g["T"]) for m, g in zip(mixed, groups)], groups[0]["tabs"]["ones256"])
        for g, (mod, ya, yb, gates, rwp), (yf, ybk) in zip(groups, mixed, scans):
            x = _merge(g["x"], mod, g["T"], ya, yb, yf, ybk, rwp, gates, lw, g["tabs"]["ones256"])
            g["x"] = _moe(x, mod, g["T"], lw, final_g, final=(l == depth - 1))
    return [g["x"].reshape(g["B"], g["T"], D_MODEL) for g in groups]


def kernel(x_prompt, x_sample, c_prompt, c_sample, norm1_g, norm2_g, final_g, ada_w, ada_b, w_in, ret_gn, q_norm_g, k_norm_g, shift_mu, rw_w0_f, rw_w2_f, rw_w0_b, rw_w2_b, rw_a0_f, rw_a2_f, rw_a0_b, rw_a2_b, rw_g2, rw_k_k, rw_k_a, rw_r_k, rw_gn, w_branch_a, w_branch_b, w_branch_c, w_out, router_w, router_bias, exp_w_gate, exp_w_up, exp_w_down, sh_w_gate, sh_w_up, sh_w_down):
    P = {
        "norm1_g": norm1_g, "norm2_g": norm2_g, "final_g": final_g, "ada_w": ada_w, "ada_b": ada_b,
        "w_in": w_in, "ret_gn": ret_gn, "q_norm_g": q_norm_g, "k_norm_g": k_norm_g, "shift_mu": shift_mu,
        "rw_w0_f": rw_w0_f, "rw_w2_f": rw_w2_f, "rw_w0_b": rw_w0_b, "rw_w2_b": rw_w2_b,
        "rw_a0_f": rw_a0_f, "rw_a2_f": rw_a2_f, "rw_a0_b": rw_a0_b, "rw_a2_b": rw_a2_b,
        "rw_g2": rw_g2, "rw_k_k": rw_k_k, "rw_k_a": rw_k_a, "rw_r_k": rw_r_k, "rw_gn": rw_gn,
        "w_branch_a": w_branch_a, "w_branch_b": w_branch_b, "w_branch_c": w_branch_c, "w_out": w_out,
        "router_w": router_w, "router_bias": router_bias,
        "exp_w_gate": exp_w_gate, "exp_w_up": exp_w_up, "exp_w_down": exp_w_down,
        "sh_w_gate": sh_w_gate, "sh_w_up": sh_w_up, "sh_w_down": sh_w_down,
    }
    layer_weights = [_layer_weights(P, l) for l in range(w_in.shape[0])]
    y_prompt, y_sample = _trunks([(x_prompt, c_prompt), (x_sample, c_sample)], P, layer_weights)
    return (y_prompt, y_sample)
```

```python
import functools

import numpy as np
import jax
import jax.numpy as jnp
from jax import lax
from jax.experimental import pallas as pl
from jax.experimental.pallas import tpu as pltpu
from jax.experimental.pallas import tpu_sc as plsc

F32 = jnp.float32
BF16 = jnp.bfloat16

D_MODEL = 1024
DEPTH = 2
GRID_W = 64
HEAD_DIM = 64
ROPE_THETA = 10000.0
ROPE_PAIRS_PER_AXIS = HEAD_DIM // 4

RET_HEADS = 4
RET_WIDTH = RET_HEADS * HEAD_DIM
RET_CHUNK = 128
RET_CHUNKS_PER_STEP = 4
RET_EPS = 1e-5

ATT_Q_HEADS = 8
ATT_KV_HEADS = 2
ATT_GROUP = ATT_Q_HEADS // ATT_KV_HEADS
ATT_Q_WIDTH = ATT_Q_HEADS * HEAD_DIM
ATT_KV_WIDTH = ATT_KV_HEADS * HEAD_DIM

RWKV_HEADS = 4
RWKV_WIDTH = RWKV_HEADS * HEAD_DIM
DECAY_LORA = 64
AAA_LORA = 64
GATE_LORA = 128
RWKV_GN_EPS = 64e-5
RW_CHUNK = 64
RW_FIELDS = 11
RW_BATCH_ROWS = 4

FLASH_Q_ROWS = 512
FLASH_KV_ROWS = 512
FLASH_SUB_ROWS = 1024

MOE_TOKENS = 512
MOE_GROUP_ROWS_MIN = 256
MOE_GROUP_ROWS_MAX = 512
SC_GATHER_ROWS = 128

N_EXPERTS = 64
TOP_K = 8
N_GROUPS = 8
TOPK_GROUPS = 4
EXPERT_FF = 256
ROUTED_SCALE = 2.5
NORM_EPS = 1e-6

RET_COLS = 4 * RET_WIDTH
ATT_COLS = ATT_Q_WIDTH + 2 * ATT_KV_WIDTH
RW_COLS = 3 * RWKV_WIDTH + 2 * DECAY_LORA + 2 * AAA_LORA + GATE_LORA
GATE_COLS = 3 * D_MODEL

V7X_LANES = 128
VMEM_LIMIT_BYTES = 48 * 1024 * 1024


def _params(*dims):
    return pltpu.CompilerParams(dimension_semantics=dims, vmem_limit_bytes=VMEM_LIMIT_BYTES)


def _sigmoid(x):
    return 1.0 / (1.0 + jnp.exp(-x))


def _silu(x):
    return x * _sigmoid(x)


def _dot(a, b):
    return jnp.dot(a, b, preferred_element_type=F32)


def _dot_nt(a, b):
    return lax.dot_general(a, b, (((1,), (1,)), ((), ())), preferred_element_type=F32)


def _split(a):
    hi = a.astype(BF16)
    lo = (a - hi.astype(F32)).astype(BF16)
    return hi, lo


def _dot_split_lhs(a, b_bf16):
    hi, lo = _split(a)
    return _dot(hi, b_bf16) + _dot(lo, b_bf16)


def _dot3(a, b):
    ah, al = _split(a)
    bh, bl = _split(b)
    return _dot(ah, bh) + _dot(ah, bl) + _dot(al, bh)


def _norm_mod(x, gain, shift, scale):
    ms = jnp.mean(x * x, axis=-1, keepdims=True)
    return x * lax.rsqrt(ms + NORM_EPS) * gain * (1.0 + scale) + shift


def _head_layer_norm(y, ones_bd, eps):
    mean = _dot_split_lhs(y, ones_bd) * (1.0 / HEAD_DIM)
    yc = y - mean
    var = _dot_split_lhs(yc * yc, ones_bd) * (1.0 / HEAD_DIM)
    return yc * lax.rsqrt(var + eps)


def _ada_kernel(c_ref, w_ref, b_ref, o_ref):
    c = c_ref[...]
    o_ref[...] = _dot3(_silu(c), w_ref[...]) + b_ref[...]


def _ada_mod(c, ada_w, ada_b):
    B = c.shape[0]
    Bp = max(8, B)
    cp = jnp.zeros((Bp, D_MODEL), F32).at[:B].set(c)
    L = ada_w.shape[0]
    tn = 1536
    out = pl.pallas_call(
        _ada_kernel,
        grid=(L, 6 * D_MODEL // tn),
        in_specs=[pl.BlockSpec((Bp, D_MODEL), lambda l, j: (0, 0)),
                  pl.BlockSpec((None, D_MODEL, tn), lambda l, j: (l, 0, j)),
                  pl.BlockSpec((None, 1, tn), lambda l, j: (l, 0, j))],
        out_specs=pl.BlockSpec((None, Bp, tn), lambda l, j: (l, 0, j)),
        out_shape=jax.ShapeDtypeStruct((L, Bp, 6 * D_MODEL), F32),
        compiler_params=_params("parallel", "parallel"),
        name="ada_mod",
    )(cp, ada_w, ada_b.reshape(L, 1, 6 * D_MODEL))
    return out[:, :B].reshape(L, B, 6, D_MODEL)


def _proj_ret_kernel(x_ref, mod_ref, g_ref, w_ref, cos_ref, sin_ref, o_ref):
    h = _norm_mod(x_ref[...], g_ref[...], mod_ref[0:1, :], mod_ref[1:2, :]).astype(BF16)
    p = _dot(h, w_ref[...])
    W = RET_WIDTH
    cos = cos_ref[:, 0:W]
    sin = sin_ref[:, 0:W]
    o_ref[:, 0:W] = (p[:, 0:W] * cos + p[:, 4 * W:5 * W] * sin).astype(BF16)
    o_ref[:, W:2 * W] = ((p[:, W:2 * W] * cos + p[:, 5 * W:6 * W] * sin) * (HEAD_DIM ** -0.5)).astype(BF16)
    o_ref[:, 2 * W:3 * W] = p[:, 2 * W:3 * W].astype(BF16)
    o_ref[:, 3 * W:4 * W] = _silu(p[:, 3 * W:4 * W]).astype(BF16)


def _proj_att_kernel(x_ref, mod_ref, g_ref, w_ref, cos_ref, sin_ref, qkg_ref, ones_ref, q_ref, k_ref, v_ref):
    h = _norm_mod(x_ref[...], g_ref[...], mod_ref[0:1, :], mod_ref[1:2, :]).astype(BF16)
    p = _dot(h, w_ref[...])
    QW, KW = ATT_Q_WIDTH, ATT_KV_WIDTH
    q = p[:, 0:QW]
    k = p[:, QW:QW + KW]
    v = p[:, QW + KW:QW + 2 * KW]
    qs = p[:, QW + 2 * KW:2 * QW + 2 * KW]
    ks = p[:, 2 * QW + 2 * KW:2 * QW + 3 * KW]
    ones = ones_ref[...]
    rq = lax.rsqrt(_dot_split_lhs(q * q, ones) * (1.0 / HEAD_DIM) + NORM_EPS)
    rk = lax.rsqrt(_dot_split_lhs(k * k, ones[0:KW, 0:KW]) * (1.0 / HEAD_DIM) + NORM_EPS)
    cos = cos_ref[...]
    sin = sin_ref[...]
    qo = (q * qkg_ref[0:1, :] * cos + qs * qkg_ref[1:2, :] * sin) * (rq * (HEAD_DIM ** -0.5))
    ko = (k * qkg_ref[2:3, 0:KW] * cos[:, 0:KW] + ks * qkg_ref[3:4, 0:KW] * sin[:, 0:KW]) * rk
    q_ref[...] = qo.astype(BF16)
    kb = ko.astype(BF16)
    vb = v.astype(BF16)
    for hk in range(ATT_KV_HEADS):
        k_ref[hk] = kb[:, hk * HEAD_DIM:(hk + 1) * HEAD_DIM]
        v_ref[hk, :, 0:HEAD_DIM] = vb[:, hk * HEAD_DIM:(hk + 1) * HEAD_DIM]
        v_ref[hk, :, HEAD_DIM:2 * HEAD_DIM] = jnp.ones((vb.shape[0], HEAD_DIM), BF16)


def _proj_gate_kernel(x_ref, mod_ref, g_ref, w_ref, o_ref):
    h = _norm_mod(x_ref[...], g_ref[...], mod_ref[0:1, :], mod_ref[1:2, :]).astype(BF16)
    o_ref[...] = _sigmoid(_dot(h, w_ref[...])).astype(BF16)


def _proj_rw_kernel(x_ref, xp_ref, xn_ref, mod_ref, g_ref, w_ref, mu_ref, vec_ref, w2_ref, a2_ref, g2_ref,
                    ones_ref, o_ref, p_scr, *, tm, tiles_per_seq):
    i = pl.program_id(0)
    gain, shift, scale = g_ref[...], mod_ref[0:1, :], mod_ref[1:2, :]
    w = w_ref[...]
    h = _norm_mod(x_ref[...], gain, shift, scale).astype(BF16)
    p = _dot(h, w)
    hp = _norm_mod(xp_ref[...], gain, shift, scale).astype(BF16)
    hn = _norm_mod(xn_ref[...], gain, shift, scale).astype(BF16)
    first = (i % tiles_per_seq) == 0
    last = (i % tiles_per_seq) == tiles_per_seq - 1
    prev_row = jnp.where(first, 0.0, _dot(hp, w)[7:8, :])
    next_row = jnp.where(last, 0.0, _dot(hn, w)[0:1, :])
    p_scr[8:8 + tm, :] = p
    p_scr[7:8, :] = prev_row
    p_scr[8 + tm:9 + tm, :] = next_row
    prev = p_scr[7:7 + tm, :]
    nxt = p_scr[9:9 + tm, :]
    x = p + mu_ref[0:1, :] * (prev - p) + mu_ref[1:2, :] * (nxt - p)

    W = RWKV_WIDTH
    r, k, v = x[:, 0:W], x[:, W:2 * W], x[:, 2 * W:3 * W]
    xw = x[:, 3 * W:3 * W + 2 * DECAY_LORA]
    xa = x[:, 3 * W + 2 * DECAY_LORA:3 * W + 2 * DECAY_LORA + 2 * AAA_LORA]
    xg = x[:, 3 * W + 2 * DECAY_LORA + 2 * AAA_LORA:]
    w0_f, w0_b = vec_ref[0:1, :], vec_ref[1:2, :]
    a0_f, a0_b = vec_ref[2:3, :], vec_ref[3:4, :]
    k_k, k_a, r_k = vec_ref[4:5, :], vec_ref[5:6, :], vec_ref[6:7, :]
    wl = _dot(jnp.tanh(xw).astype(BF16), w2_ref[...])
    al = _dot(xa.astype(BF16), a2_ref[...])
    dec_c = float(np.exp(-0.5))
    w_f = jnp.exp(-dec_c * _sigmoid(w0_f + wl[:, 0:W]))
    w_b = jnp.exp(-dec_c * _sigmoid(w0_b + wl[:, W:2 * W]))
    a_f = _sigmoid(a0_f + al[:, 0:W])
    a_b = _sigmoid(a0_b + al[:, W:2 * W])
    gate = _dot(_sigmoid(xg).astype(BF16), g2_ref[...])
    ones = ones_ref[...]
    kk = k * k_k
    kk = kk * lax.rsqrt(_dot_split_lhs(kk * kk, ones) + 1e-12)
    k_f = k * (1.0 + (a_f - 1.0) * k_a)
    k_b = k * (1.0 + (a_b - 1.0) * k_a)
    bonus = _dot_split_lhs(r * k_f * r_k, ones) * v
    for j, val in enumerate((r, kk, v, w_f, w_b, k_f, k_b, kk * a_f, kk * a_b, gate, bonus)):
        o_ref[:, j * W:(j + 1) * W] = val


def _token_specs(tm, tiles_per_seq):
    x_spec = pl.BlockSpec((tm, D_MODEL), lambda i: (i, 0))
    mod_spec = pl.BlockSpec((None, 6, D_MODEL), lambda i: (i // tiles_per_seq, 0, 0))
    g_spec = pl.BlockSpec((1, D_MODEL), lambda i: (0, 0))
    return x_spec, mod_spec, g_spec


def _full(shape):
    nd = len(shape)
    return pl.BlockSpec(shape, lambda *_: (0,) * nd)


def _input_projections(x, mod, T, lw, tabs):
    N = x.shape[0]
    tm = min(512, T)
    tps = T // tm
    grid = (N // tm,)
    x_spec, mod_spec, g_spec = _token_specs(tm, tps)
    tab_spec = pl.BlockSpec((tm, ATT_Q_WIDTH), lambda i: (i % tps, 0))
    cos, sin = tabs["cos"], tabs["sin"]

    ret = pl.pallas_call(
        _proj_ret_kernel, grid=grid,
        in_specs=[x_spec, mod_spec, g_spec, _full(lw["w_ret"].shape), tab_spec, tab_spec],
        out_specs=pl.BlockSpec((tm, RET_COLS), lambda i: (i, 0)),
        out_shape=jax.ShapeDtypeStruct((N, RET_COLS), BF16),
        compiler_params=_params("parallel"), name="proj_ret",
    )(x, mod, lw["norm1_g"], lw["w_ret"], cos, sin)

    q, k, v = pl.pallas_call(
        _proj_att_kernel, grid=grid,
        in_specs=[x_spec, mod_spec, g_spec, _full(lw["w_att"].shape), tab_spec, tab_spec,
                  _full(lw["qk_gain"].shape), _full(tabs["ones512"].shape)],
        out_specs=[pl.BlockSpec((tm, ATT_Q_WIDTH), lambda i: (i, 0)),
                   pl.BlockSpec((ATT_KV_HEADS, tm, HEAD_DIM), lambda i: (0, i, 0)),
                   pl.BlockSpec((ATT_KV_HEADS, tm, 2 * HEAD_DIM), lambda i: (0, i, 0))],
        out_shape=[jax.ShapeDtypeStruct((N, ATT_Q_WIDTH), BF16),
                   jax.ShapeDtypeStruct((ATT_KV_HEADS, N, HEAD_DIM), BF16),
                   jax.ShapeDtypeStruct((ATT_KV_HEADS, N, 2 * HEAD_DIM), BF16)],
        compiler_params=_params("parallel"), name="proj_att",
    )(x, mod, lw["norm1_g"], lw["w_att"], cos, sin, lw["qk_gain"], tabs["ones512"])

    gates = pl.pallas_call(
        _proj_gate_kernel, grid=grid,
        in_specs=[x_spec, mod_spec, g_spec, _full(lw["w_gate"].shape)],
        out_specs=pl.BlockSpec((tm, GATE_COLS), lambda i: (i, 0)),
        out_shape=jax.ShapeDtypeStruct((N, GATE_COLS), BF16),
        compiler_params=_params("parallel"), name="proj_gate",
    )(x, mod, lw["norm1_g"], lw["w_gate"])

    tm_rw = min(256, T)
    tps_rw = T // tm_rw
    x_spec, mod_spec, g_spec = _token_specs(tm_rw, tps_rw)
    rows8 = tm_rw // 8
    last8 = N // 8 - 1
    rwp = pl.pallas_call(
        functools.partial(_proj_rw_kernel, tm=tm_rw, tiles_per_seq=tps_rw), grid=(N // tm_rw,),
        in_specs=[x_spec,
                  pl.BlockSpec((8, D_MODEL), lambda i: (jnp.maximum(i * rows8 - 1, 0), 0)),
                  pl.BlockSpec((8, D_MODEL), lambda i: (jnp.minimum((i + 1) * rows8, last8), 0)),
                  mod_spec, g_spec, _full(lw["w_rw"].shape), _full(lw["shift_mu"].shape),
                  _full(lw["rw_vec"].shape), _full(lw["rw_w2"].shape), _full(lw["rw_a2"].shape),
                  _full(lw["rw_g2"].shape), _full(tabs["ones256"].shape)],
        out_specs=pl.BlockSpec((tm_rw, RW_FIELDS * RWKV_WIDTH), lambda i: (i, 0)),
        out_shape=jax.ShapeDtypeStruct((N, RW_FIELDS * RWKV_WIDTH), F32),
        scratch_shapes=[pltpu.VMEM((tm_rw + 16, RW_COLS), F32)],
        compiler_params=_params("parallel"), name="proj_rw",
    )(x, x, x, mod, lw["norm1_g"], lw["w_rw"], lw["shift_mu"], lw["rw_vec"], lw["rw_w2"], lw["rw_a2"],
      lw["rw_g2"], tabs["ones256"])
    return ret, q, k, v, gates, rwp


def _retention_tables(reverse):
    lg = np.log1p(-np.exp2(-5.0 - np.arange(RET_HEADS, dtype=np.float64)))
    if reverse:
        lg = lg[::-1]
    C = RET_CHUNK
    pos = np.arange(C, dtype=np.float64)
    diff = pos[:, None] - pos[None, :]
    if reverse:
        dec = np.where(diff < 0, np.exp(lg[:, None, None] * np.maximum(-diff, 0.0)[None]), 0.0)
        xi = np.exp(lg[:, None] * (C - pos)[None, :])
        zeta = np.exp(lg[:, None] * pos[None, :])
    else:
        dec = np.where(diff >= 0, np.exp(lg[:, None, None] * np.maximum(diff, 0.0)[None]), 0.0)
        xi = np.exp(lg[:, None] * (pos + 1.0)[None, :])
        zeta = np.exp(lg[:, None] * (C - 1.0 - pos)[None, :])
    chunk_decay = tuple(float(v) for v in np.exp(lg * C))
    widen = lambda a: jnp.asarray(np.repeat(a.T, HEAD_DIM, axis=1), F32)
    return jnp.asarray(dec, F32), widen(xi), widen(zeta), chunk_decay


def _retention_kernel(ret_ref, dec_ref, xi_ref, zeta_ref, *rest, chunk_decay, final, chunks):
    if final:
        yf_ref, gn_ref, ones_ref, o_ref, state, y_scr = rest
    else:
        o_ref, state, y_scr = rest

    @pl.when(pl.program_id(1) == 0)
    def _():
        state[...] = jnp.zeros_like(state)

    W, C = RET_WIDTH, RET_CHUNK
    for step in range(chunks):
        cb = chunks - 1 - step if final else step
        rows = slice(cb * C, (cb + 1) * C)
        q = ret_ref[rows, 0:W]
        k = ret_ref[rows, W:2 * W]
        v = ret_ref[rows, 2 * W:3 * W]
        qx = (q.astype(F32) * xi_ref[...]).astype(BF16)
        kzt = (k.astype(F32) * zeta_ref[...]).T
        for h in range(RET_HEADS):
            sl = slice(h * HEAD_DIM, (h + 1) * HEAD_DIM)
            s = _dot_nt(q[:, sl], k[:, sl]) * dec_ref[h]
            inner = _dot(s.astype(BF16), v[:, sl])
            r_prev = state[h]
            cross = _dot(qx[:, sl], r_prev.astype(BF16))
            kv = _dot(kzt[sl, :].astype(BF16), v[:, sl])
            state[h] = r_prev * chunk_decay[h] + kv
            y_scr[rows, sl] = inner + cross
    y = y_scr[...]
    if final:
        y = _head_layer_norm(y + yf_ref[...], ones_ref[...], RET_EPS) * gn_ref[...]
        o_ref[...] = (y * ret_ref[:, 3 * W:4 * W].astype(F32)).astype(BF16)
    else:
        o_ref[...] = y


def _retention(ret, B, T, ret_gn, ones256):
    N = ret.shape[0]
    chunks = min(RET_CHUNKS_PER_STEP, T // RET_CHUNK)
    R = chunks * RET_CHUNK
    nblk = T // R
    scratch = [pltpu.VMEM((RET_HEADS, HEAD_DIM, HEAD_DIM), F32), pltpu.VMEM((R, RET_WIDTH), F32)]

    dec, xi, zeta, cd = _retention_tables(False)
    fwd_row = lambda b, c: (b * nblk + c, 0)
    yf = pl.pallas_call(
        functools.partial(_retention_kernel, chunk_decay=cd, final=False, chunks=chunks), grid=(B, nblk),
        in_specs=[pl.BlockSpec((R, RET_COLS), fwd_row), _full(dec.shape), _full(xi.shape), _full(zeta.shape)],
        out_specs=pl.BlockSpec((R, RET_WIDTH), fwd_row),
        out_shape=jax.ShapeDtypeStruct((N, RET_WIDTH), F32),
        scratch_shapes=scratch, compiler_params=_params("parallel", "arbitrary"), name="retention_fwd",
    )(ret, dec, xi, zeta)

    dec, xi, zeta, cd = _retention_tables(True)
    bwd_row = lambda b, c: (b * nblk + nblk - 1 - c, 0)
    return pl.pallas_call(
        functools.partial(_retention_kernel, chunk_decay=cd, final=True, chunks=chunks), grid=(B, nblk),
        in_specs=[pl.BlockSpec((R, RET_COLS), bwd_row), _full(dec.shape), _full(xi.shape), _full(zeta.shape),
                  pl.BlockSpec((R, RET_WIDTH), bwd_row), _full(ret_gn.shape), _full(ones256.shape)],
        out_specs=pl.BlockSpec((R, RET_WIDTH), bwd_row),
        out_shape=jax.ShapeDtypeStruct((N, RET_WIDTH), BF16),
        scratch_shapes=scratch, compiler_params=_params("parallel", "arbitrary"), name="retention_bwd",
    )(ret, dec, xi, zeta, yf, ret_gn, ones256)


def _flash_kernel(q_ref, k_ref, v_ref, o_ref, q_scr, m_scr, acc_scr, *, tq, tk):
    j = pl.program_id(3)

    @pl.when(j == 0)
    def _():
        for g in range(ATT_GROUP):
            q_scr[g * tq:(g + 1) * tq, :] = q_ref[:, g * HEAD_DIM:(g + 1) * HEAD_DIM]
        m_scr[...] = jnp.full_like(m_scr, -jnp.inf)
        acc_scr[...] = jnp.zeros_like(acc_scr)

    k = k_ref[...]
    v = v_ref[...]
    sub = min(FLASH_SUB_ROWS, ATT_GROUP * tq)
    for r0 in range(0, ATT_GROUP * tq, sub):
        rows = slice(r0, r0 + sub)
        s = _dot_nt(q_scr[rows, :], k)
        m_prev = m_scr[rows, :]
        m_next = jnp.maximum(m_prev, jnp.max(s, axis=1, keepdims=True))
        p = jnp.exp(s - jnp.concatenate([m_next] * (tk // V7X_LANES), axis=1))
        alpha = jnp.exp(m_prev - m_next)
        m_scr[rows, :] = m_next
        acc_scr[rows, :] = alpha * acc_scr[rows, :] + _dot(p.astype(BF16), v)

    @pl.when(j == pl.num_programs(3) - 1)
    def _():
        o = acc_scr[:, 0:HEAD_DIM] / acc_scr[:, HEAD_DIM:2 * HEAD_DIM]
        for g in range(ATT_GROUP):
            o_ref[:, g * HEAD_DIM:(g + 1) * HEAD_DIM] = o[g * tq:(g + 1) * tq, :].astype(BF16)


def _attention(q, k, v, B, T):
    N = q.shape[0]
    tq = min(FLASH_Q_ROWS, T)
    tk = min(FLASH_KV_ROWS, T)
    nq, nk = T // tq, T // tk
    GW = ATT_GROUP * HEAD_DIM
    rows = ATT_GROUP * tq
    return pl.pallas_call(
        functools.partial(_flash_kernel, tq=tq, tk=tk), grid=(B, ATT_KV_HEADS, nq, nk),
        in_specs=[pl.BlockSpec((tq, GW), lambda b, h, i, j: (b * nq + i, h)),
                  pl.BlockSpec((None, tk, HEAD_DIM), lambda b, h, i, j: (h, b * nk + j, 0)),
                  pl.BlockSpec((None, tk, 2 * HEAD_DIM), lambda b, h, i, j: (h, b * nk + j, 0))],
        out_specs=pl.BlockSpec((tq, GW), lambda b, h, i, j: (b * nq + i, h)),
        out_shape=jax.ShapeDtypeStruct((N, ATT_Q_WIDTH), BF16),
        scratch_shapes=[pltpu.VMEM((rows, HEAD_DIM), BF16), pltpu.VMEM((rows, V7X_LANES), F32),
                        pltpu.VMEM((rows, 2 * HEAD_DIM), F32)],
        compiler_params=_params("parallel", "parallel", "parallel", "arbitrary"), name="flash_attention",
    )(q, k, v)


def _rwkv_scan_kernel(*refs, rows_per_set):
    C = RW_CHUNK
    H = C // 2
    W = RWKV_WIDTH
    SUB = 8
    n_sets = len(rows_per_set)
    ones_ref = refs[12 * n_sets]
    outs = refs[12 * n_sets + 1:12 * n_sets + 1 + 2 * n_sets]
    state, vk_scr = refs[-2:]
    dirs = [(refs[12 * s:12 * s + 6], refs[12 * s + 6:12 * s + 12]) for s in range(n_sets)]
    chains = [(s, n, d) for s in range(n_sets) for n in range(rows_per_set[s]) for d in range(2)]

    first_chunk = pl.program_id(1) == 0

    @pl.when(first_chunk)
    def _():
        for c, (s, n, d) in enumerate(chains):
            if s == 0:
                state[c] = jnp.zeros((HEAD_DIM, W), F32)

    @pl.when(jnp.logical_and(first_chunk, pl.program_id(0) == 0))
    def _():
        for c, (s, n, d) in enumerate(chains):
            if s > 0:
                state[c] = jnp.zeros((HEAD_DIM, W), F32)

    ones = ones_ref[...]
    same_head = (lax.broadcasted_iota(jnp.int32, (W, W), 0) // HEAD_DIM
                 == lax.broadcasted_iota(jnp.int32, (W, W), 1) // HEAD_DIM)
    step_iota = lax.broadcasted_iota(jnp.int32, (H, HEAD_DIM, W), 0)
    lane_iota = lax.broadcasted_iota(jnp.int32, (H, HEAD_DIM, W), 2) & (HEAD_DIM - 1)

    def head_sums(vals):
        stacked = jnp.concatenate([v.astype(BF16) for v in vals], axis=0)
        out = _dot(stacked, ones)
        return [out[c * HEAD_DIM:(c + 1) * HEAD_DIM, :] for c in range(len(vals))]

    own_head = (lax.broadcasted_iota(jnp.int32, (SUB, W), 0)
                == lax.broadcasted_iota(jnp.int32, (SUB, W), 1) // HEAD_DIM)
    carry = tuple(state[c] for c in range(len(chains)))
    for half in range(2):
        lo = (half * H, (1 - half) * H)
        for c, (s, n, d) in enumerate(chains):
            kc = dirs[s][d][3][n]
            kbd = jnp.where(same_head, jnp.concatenate([kc] * (W // C), axis=0), 0.0).astype(BF16)
            sel = jnp.where(step_iota + lo[d] == lane_iota, dirs[s][d][5][n][None], 0.0).reshape(H * HEAD_DIM, W)
            vk_scr[c] = _dot(sel.astype(BF16), kbd).reshape(H, HEAD_DIM, W)

        def group(g, states, lo=lo):
            states = list(states)
            bases = (lo[0] + g * SUB, lo[1] + (H // SUB - 1 - g) * SUB)
            rows = [[ref[n, pl.ds(pl.multiple_of(bases[d], SUB), SUB), :] for ref in dirs[s][d][:5]]
                    for (s, n, d) in chains]
            for j in range(SUB):
                jj = (j, SUB - 1 - j)
                row = lambda c, f: rows[c][f][jj[chains[c][2]]:jj[chains[c][2]] + 1, :]
                sa = head_sums([states[c] * row(c, 0) for c in range(len(chains))])
                new = [states[c] * row(c, 2) - sa[c] * row(c, 4)
                       + vk_scr[c, bases[chains[c][2]] + jj[chains[c][2]] - lo[chains[c][2]]]
                       for c in range(len(chains))]
                for c, (s, n, d) in enumerate(chains):
                    read = new[c] if d == 0 else states[c]
                    r_heads = jnp.where(own_head, row(c, 1), 0.0).astype(BF16)
                    outs[2 * s + d][n, bases[d] + jj[d]] = _dot_nt(r_heads, read.astype(BF16))
                states = new
            return tuple(states)

        carry = lax.fori_loop(0, H // SUB, group, carry)

    for c in range(len(chains)):
        state[c] = carry[c]


def _rwkv_scan_call(sets, ones256):
    W, C = RWKV_WIDTH, RW_CHUNK
    _, B0, T0, nb0 = sets[0]
    nc0 = T0 // C
    operands, in_specs, out_specs, out_shapes = [], [], [], []
    for s, (rwp, B, T, nb) in enumerate(sets):
        nc = T // C
        fields = rwp.reshape(B, T, RW_FIELDS * W)
        v = rwp[:, 2 * W:3 * W].reshape(B, nc, C, RWKV_HEADS, HEAD_DIM)
        vt = v.transpose(0, 1, 4, 3, 2).reshape(B, nc, HEAD_DIM, W)

        def block(b, c, reverse, s=s, nc=nc):
            bi, ci = (b, c) if s == 0 else (0, b * nc0 + c)
            return bi, (nc - 1 - ci if reverse else ci)

        field = lambda f, rev, nb=nb, block=block: pl.BlockSpec(
            (nb, C, W), lambda b, c: block(b, c, rev) + (f,))
        vt_spec = lambda rev, nb=nb, block=block: pl.BlockSpec(
            (nb, None, HEAD_DIM, W), lambda b, c: block(b, c, rev) + (0, 0))
        in_specs += [field(1, False), field(0, False), field(3, False), field(5, False), field(7, False),
                     vt_spec(False),
                     field(1, True), field(0, True), field(4, True), field(6, True), field(8, True),
                     vt_spec(True)]
        operands += [fields] * 5 + [vt] + [fields] * 5 + [vt]
        y_spec = lambda rev, nb=nb, block=block: pl.BlockSpec(
            (nb, C, 8, HEAD_DIM), lambda b, c: block(b, c, rev) + (0, 0))
        out_specs += [y_spec(False), y_spec(True)]
        out_shapes += [jax.ShapeDtypeStruct((B, T, 8, HEAD_DIM), F32)] * 2
    n_chains = 2 * sum(nb for _, _, _, nb in sets)
    outs = pl.pallas_call(
        functools.partial(_rwkv_scan_kernel, rows_per_set=tuple(nb for _, _, _, nb in sets)),
        grid=(B0 // nb0, nc0), in_specs=in_specs + [_full(ones256.shape)],
        out_specs=out_specs, out_shape=out_shapes,
        scratch_shapes=[pltpu.VMEM((n_chains, HEAD_DIM, W), F32),
                        pltpu.VMEM((n_chains, C // 2, HEAD_DIM, W), F32)],
        compiler_params=_params("arbitrary", "arbitrary"), name="rwkv_scan",
    )(*operands, ones256)
    heads = lambda y, B, T: y[:, :, :RWKV_HEADS, :].reshape(B * T, W)
    return [(heads(outs[2 * s], B, T), heads(outs[2 * s + 1], B, T)) for s, (_, B, T, _) in enumerate(sets)]


def _rwkv_scan(groups, ones256):
    rows = lambda B: RW_BATCH_ROWS if B % RW_BATCH_ROWS == 0 else 1
    if len(groups) == 2:
        (_, B0, T0), (_, B1, T1) = groups
        if B1 == 1 and (B0 // rows(B0)) * (T0 // RW_CHUNK) == T1 // RW_CHUNK:
            return _rwkv_scan_call([groups[0] + (rows(B0),), groups[1] + (1,)], ones256)
    return [_rwkv_scan_call([g + (rows(g[1]),)], ones256)[0] for g in groups]


def _merge_kernel(x_ref, mod_ref, ya_ref, yb_ref, yf_ref, ybk_ref, rg_ref, bonus_ref, gn_ref, gates_ref,
                  ones_ref, wa_ref, wb_ref, wc_ref, wo_ref, o_ref):
    y = _head_layer_norm(yf_ref[...] + ybk_ref[...], ones_ref[...], RWKV_GN_EPS) * gn_ref[...]
    yc = ((y + bonus_ref[...]) * rg_ref[...]).astype(BF16)
    D = D_MODEL
    merged = (gates_ref[:, 0:D].astype(F32) * _dot(ya_ref[...], wa_ref[...])
              + gates_ref[:, D:2 * D].astype(F32) * _dot(yb_ref[...], wb_ref[...])
              + gates_ref[:, 2 * D:3 * D].astype(F32) * _dot(yc, wc_ref[...]))
    o_ref[...] = x_ref[...] + mod_ref[2:3, :] * _dot(merged.astype(BF16), wo_ref[...])


def _merge(x, mod, T, ya, yb, yf, ybk, rwp, gates, lw, ones256):
    N = x.shape[0]
    tm = min(512, T)
    tps = T // tm
    x_spec, mod_spec, _ = _token_specs(tm, tps)
    row = lambda w: pl.BlockSpec((tm, w), lambda i: (i, 0))
    field = lambda f: pl.BlockSpec((tm, RWKV_WIDTH), lambda i: (i, f))
    return pl.pallas_call(
        _merge_kernel, grid=(N // tm,),
        in_specs=[x_spec, mod_spec, row(RET_WIDTH), row(ATT_Q_WIDTH), row(RWKV_WIDTH), row(RWKV_WIDTH),
                  field(9), field(10), _full(lw["rw_gn"].shape), row(GATE_COLS), _full(ones256.shape),
                  _full(lw["w_branch_a"].shape), _full(lw["w_branch_b"].shape), _full(lw["w_branch_c"].shape),
                  _full(lw["w_out"].shape)],
        out_specs=x_spec, out_shape=jax.ShapeDtypeStruct((N, D_MODEL), F32),
        compiler_params=_params("parallel"), name="merge_out",
    )(x, mod, ya, yb, yf, ybk, rwp, rwp, lw["rw_gn"], gates, ones256,
      lw["w_branch_a"], lw["w_branch_b"], lw["w_branch_c"], lw["w_out"])


def _route(h, hb, rwt_ref, rbias_ref, tm):
    h_lo = (h - hb.astype(F32)).astype(BF16)
    rw_hi, rw_lo = _split(rwt_ref[...])
    logits = _dot_nt(rw_hi, hb) + _dot_nt(rw_hi, h_lo) + _dot_nt(rw_lo, hb)
    scores = _sigmoid(logits)
    choice = scores + jnp.concatenate([rbias_ref[...]] * (tm // V7X_LANES), axis=1)
    per_group = N_EXPERTS // N_GROUPS
    sub = lax.broadcasted_iota(jnp.int32, (per_group, tm), 0)
    groups, gscore = [], []
    for g in range(N_GROUPS):
        cg = choice[g * per_group:(g + 1) * per_group, :]
        m1 = jnp.max(cg, axis=0, keepdims=True)
        first = jnp.min(jnp.where(cg == m1, sub, per_group), axis=0, keepdims=True)
        m2 = jnp.max(jnp.where(sub == first, -jnp.inf, cg), axis=0, keepdims=True)
        groups.append(cg)
        gscore.append(m1 + m2)
    masked = []
    for g in range(N_GROUPS):
        beaten = jnp.zeros((1, tm), F32)
        for o in range(N_GROUPS):
            if o == g:
                continue
            wins = (gscore[o] >= gscore[g]) if o < g else (gscore[o] > gscore[g])
            beaten = beaten + jnp.where(wins, 1.0, 0.0)
        keep = jnp.where(beaten < TOPK_GROUPS, 1.0, 0.0)
        masked.append(jnp.where(jnp.broadcast_to(keep, (per_group, tm)) > 0.5, groups[g], -jnp.inf))
    mc = jnp.concatenate(masked, axis=0)
    eidx = lax.broadcasted_iota(jnp.int32, (N_EXPERTS, tm), 0)
    beaten = jnp.zeros((N_EXPERTS, tm), F32)
    for o in range(N_EXPERTS):
        row = jnp.broadcast_to(mc[o:o + 1, :], (N_EXPERTS, tm))
        tie = jnp.where(eidx > o, 1.0, 0.0)
        beaten = beaten + jnp.where(row > mc, 1.0, jnp.where(row == mc, tie, 0.0))
    sel = jnp.where(beaten < TOP_K, 1.0, 0.0)
    wts = jnp.where(beaten < TOP_K, scores, 0.0)
    return wts / jnp.sum(wts, axis=0, keepdims=True) * ROUTED_SCALE, sel


def _selected_lists(comb, sel, rank):
    eidx = lax.broadcasted_iota(jnp.int32, comb.shape, 0).astype(F32)
    prev = jnp.full((1, comb.shape[1]), -1.0, F32)
    ids, wts, pos = [], [], []
    for _ in range(TOP_K):
        cand = jnp.where(sel > 0.5, jnp.where(eidx > prev, eidx, float(N_EXPERTS)), float(N_EXPERTS))
        prev = jnp.min(cand, axis=0, keepdims=True)
        wts.append(jnp.sum(jnp.where(eidx == prev, comb, 0.0), axis=0, keepdims=True))
        pos.append(jnp.sum(jnp.where(eidx == prev, rank, 0.0), axis=0, keepdims=True))
        ids.append(jnp.minimum(prev, N_EXPERTS - 1.0))
    as_int = lambda rows: jnp.concatenate(rows, axis=0).astype(jnp.int32)
    return as_int(ids), jnp.concatenate(wts, axis=0), as_int(pos)


def _pack_halves(y):
    n = y.shape[1] // 2
    hi = pltpu.bitcast(y[:, :n].astype(BF16).astype(F32), jnp.int32)
    lo = pltpu.bitcast(y[:, n:].astype(BF16).astype(F32), jnp.int32)
    return hi | lax.shift_right_logical(lo, 16)


def _unpack_halves(w):
    hi = pltpu.bitcast(w & jnp.int32(-65536), F32)
    lo = pltpu.bitcast(lax.shift_left(w, 16), F32)
    return hi, lo


def _moe_route_kernel(x_ref, mod_ref, g_ref, rwt_ref, rbias_ref, shgu_ref, shd_ref, tri_ref,
                      hp_ref, ids_ref, wts_ref, pos_ref, base_ref, count_ref, *, tm):
    F = EXPERT_FF

    @pl.when(pl.program_id(0) == 0)
    def _():
        count_ref[...] = jnp.zeros_like(count_ref)

    x = x_ref[...]
    h = _norm_mod(x, g_ref[...], mod_ref[3:4, :], mod_ref[4:5, :])
    hb = h.astype(BF16)
    hp_ref[...] = _pack_halves(hb.astype(F32))
    comb, sel = _route(h, hb, rwt_ref, rbias_ref, tm)
    before = count_ref[...]
    inclusive = _dot(sel.astype(BF16), tri_ref[...])
    rank = inclusive - sel + jnp.concatenate([before] * (tm // V7X_LANES), axis=1)
    count_ref[...] = before + jnp.sum(sel, axis=1, keepdims=True)
    ids, wts, pos = _selected_lists(comb, sel, rank)
    ids_ref[...] = ids
    wts_ref[...] = wts
    pos_ref[...] = pos
    gu = _dot(hb, shgu_ref[...])
    shared = _dot((_silu(gu[:, 0:F]) * gu[:, F:2 * F]).astype(BF16), shd_ref[...])
    base_ref[...] = x + mod_ref[5:6, :] * shared


def _moe_expert_kernel(tile_expert_ref, xs_ref, wgu_ref, wd_ref, o_ref):
    del tile_expert_ref
    F = EXPERT_FF
    half = D_MODEL // 2
    hi, lo = _unpack_halves(xs_ref[...])
    gu = _dot(hi.astype(BF16), wgu_ref[0:half, :]) + _dot(lo.astype(BF16), wgu_ref[half:, :])
    act = (_silu(gu[:, 0:F]) * gu[:, F:2 * F]).astype(BF16)
    o_ref[...] = _pack_halves(_dot(act, wd_ref[...]))


def _moe_combine_kernel(base_ref, mod_ref, y_ref, w_ref, fin_ref, o_ref, *, final):
    half = D_MODEL // 2
    acc_hi = jnp.zeros((base_ref.shape[0], half), F32)
    acc_lo = jnp.zeros((base_ref.shape[0], half), F32)
    for k in range(TOP_K):
        hi, lo = _unpack_halves(y_ref[k])
        wk = w_ref[:, k:k + 1]
        acc_hi = acc_hi + wk * hi
        acc_lo = acc_lo + wk * lo
    out_hi = base_ref[:, 0:half] + mod_ref[5:6, 0:half] * acc_hi
    out_lo = base_ref[:, half:] + mod_ref[5:6, half:] * acc_lo
    if final:
        ms = (jnp.sum(out_hi * out_hi, axis=-1, keepdims=True)
              + jnp.sum(out_lo * out_lo, axis=-1, keepdims=True)) * (1.0 / D_MODEL)
        scale = lax.rsqrt(ms + NORM_EPS)
        out_hi = out_hi * scale * fin_ref[:, 0:half]
        out_lo = out_lo * scale * fin_ref[:, half:]
    o_ref[:, 0:half] = out_hi
    o_ref[:, half:] = out_lo


def _gather_rows(table, idx):
    rows, width = idx.shape[0], table.shape[1]
    info = plsc.get_sparse_core_info()
    workers = info.num_cores * info.num_subcores
    windows = rows // (workers * SC_GATHER_ROWS)
    assert windows * workers * SC_GATHER_ROWS == rows
    mesh = plsc.VectorSubcoreMesh(core_axis_name="core", subcore_axis_name="subcore")

    @functools.partial(pl.kernel, out_type=jax.ShapeDtypeStruct((rows, width), table.dtype), mesh=mesh,
                       scratch_types=[pltpu.VMEM((SC_GATHER_ROWS,), jnp.int32),
                                      pltpu.VMEM((SC_GATHER_ROWS, width), table.dtype)],
                       name="moe_gather")
    def gather(table_hbm, idx_hbm, out_hbm, idx_vmem, rows_vmem):
        worker = lax.axis_index("subcore") * info.num_cores + lax.axis_index("core")
        base = worker * (windows * SC_GATHER_ROWS)

        @pl.loop(0, windows)
        def _(j):
            span = pl.ds(pl.multiple_of(base + j * SC_GATHER_ROWS, SC_GATHER_ROWS), SC_GATHER_ROWS)
            pltpu.sync_copy(idx_hbm.at[span], idx_vmem)
            pltpu.sync_copy(table_hbm.at[idx_vmem], rows_vmem)
            pltpu.sync_copy(rows_vmem, out_hbm.at[span])

    return gather(table, idx)


def _scatter_rows(table, dest, total_rows):
    n_lists, n_rows = dest.shape
    width = table.shape[1]
    info = plsc.get_sparse_core_info()
    workers = info.num_cores * info.num_subcores
    windows = n_rows // (workers * SC_GATHER_ROWS)
    assert windows * workers * SC_GATHER_ROWS == n_rows
    windows_per_list = n_rows // SC_GATHER_ROWS
    mesh = plsc.VectorSubcoreMesh(core_axis_name="core", subcore_axis_name="subcore")

    @functools.partial(pl.kernel, out_type=jax.ShapeDtypeStruct((total_rows, width), table.dtype), mesh=mesh,
                       scratch_types=[pltpu.VMEM((1, SC_GATHER_ROWS), jnp.int32),
                                      pltpu.VMEM((SC_GATHER_ROWS, width), table.dtype)],
                       name="moe_scatter")
    def scatter(table_hbm, dest_hbm, out_hbm, idx_vmem, rows_vmem):
        worker = lax.axis_index("subcore") * info.num_cores + lax.axis_index("core")

        @pl.loop(0, windows)
        def _(j):
            window = worker * windows + j
            span = pl.ds(pl.multiple_of(window * SC_GATHER_ROWS, SC_GATHER_ROWS), SC_GATHER_ROWS)
            pltpu.sync_copy(table_hbm.at[span], rows_vmem)
            for k in range(n_lists):
                pltpu.sync_copy(dest_hbm.at[pl.ds(k * windows_per_list + window, 1)], idx_vmem)
                pltpu.sync_copy(rows_vmem, out_hbm.at[idx_vmem.at[0]])

    return scatter(table, dest.reshape(n_lists * windows_per_list, SC_GATHER_ROWS))


def _dispatch_plan(ids, pos, counts, n_tokens, tg):
    E = N_EXPERTS
    padded = (counts + tg - 1) // tg * tg
    ends = jnp.cumsum(padded)
    off = ends - padded
    onehot = ids[:, :, None] == jnp.arange(E, dtype=jnp.int32)[None, None, :]
    dest = pos + jnp.sum(jnp.where(onehot, off[None, None, :], 0), axis=-1)
    total = n_tokens * TOP_K + E * tg
    tile_start = jnp.arange(total // tg, dtype=jnp.int32) * tg
    tile_expert = jnp.sum((ends[None, :] <= tile_start[:, None]).astype(jnp.int32), axis=1)
    return dest, total, jnp.minimum(tile_expert, E - 1)


def _moe(x, mod, T, lw, final_g, final):
    N = x.shape[0]
    tm = min(MOE_TOKENS, T)
    tps = T // tm
    D, F, E = D_MODEL, EXPERT_FF, N_EXPERTS
    tg = max(MOE_GROUP_ROWS_MIN, min(MOE_GROUP_ROWS_MAX, N * TOP_K // (E * 16)))
    half = D // 2
    x_spec, mod_spec, g_spec = _token_specs(tm, tps)
    list_spec = pl.BlockSpec((TOP_K, tm), lambda i: (0, i))

    tri = jnp.asarray(np.triu(np.ones((tm, tm), np.float32)), BF16)
    hp, ids, wts, pos, base, counts = pl.pallas_call(
        functools.partial(_moe_route_kernel, tm=tm), grid=(N // tm,),
        in_specs=[x_spec, mod_spec, g_spec, _full((E, D)), _full((E, V7X_LANES)), _full((D, 2 * F)), _full((F, D)),
                  _full((tm, tm))],
        out_specs=[pl.BlockSpec((tm, half), lambda i: (i, 0)), list_spec, list_spec, list_spec, x_spec,
                   _full((E, V7X_LANES))],
        out_shape=[jax.ShapeDtypeStruct((N, half), jnp.int32), jax.ShapeDtypeStruct((TOP_K, N), jnp.int32),
                   jax.ShapeDtypeStruct((TOP_K, N), F32), jax.ShapeDtypeStruct((TOP_K, N), jnp.int32),
                   jax.ShapeDtypeStruct((N, D), F32), jax.ShapeDtypeStruct((E, V7X_LANES), F32)],
        compiler_params=_params("arbitrary"), name="moe_route",
    )(x, mod, lw["norm2_g"], lw["router_wt"], lw["router_bias"], lw["sh_gu"], lw["sh_d"], tri)

    dest, total, tile_expert = _dispatch_plan(ids, pos, counts[:, 0].astype(jnp.int32), N, tg)
    xs = _scatter_rows(hp, dest, total)
    n_tiles = total // tg
    ys = pl.pallas_call(
        _moe_expert_kernel,
        grid_spec=pltpu.PrefetchScalarGridSpec(
            num_scalar_prefetch=1, grid=(n_tiles,),
            in_specs=[pl.BlockSpec((tg, half), lambda i, te: (i, 0)),
                      pl.BlockSpec((None, D, 2 * F), lambda i, te: (te[i], 0, 0)),
                      pl.BlockSpec((None, F, D), lambda i, te: (te[i], 0, 0))],
            out_specs=pl.BlockSpec((tg, half), lambda i, te: (i, 0))),
        out_shape=jax.ShapeDtypeStruct(xs.shape, jnp.int32),
        compiler_params=_params("parallel"), name="moe_experts",
    )(tile_expert, xs, lw["exp_gu"], lw["exp_d"])

    yk = _gather_rows(ys, dest.reshape(-1)).reshape(TOP_K, N, half)
    return pl.pallas_call(
        functools.partial(_moe_combine_kernel, final=final), grid=(N // tm,),
        in_specs=[x_spec, mod_spec, pl.BlockSpec((TOP_K, tm, half), lambda i: (0, i, 0)),
                  pl.BlockSpec((tm, TOP_K), lambda i: (i, 0)), g_spec],
        out_specs=x_spec, out_shape=jax.ShapeDtypeStruct((N, D), F32),
        compiler_params=_params("parallel"), name="moe_combine",
    )(base, mod, yk, wts.T, final_g)


def _swap_halves(w, heads):
    shp = w.shape
    w = w.reshape(shp[:-1] + (heads, 2, HEAD_DIM // 2))
    return jnp.flip(w, axis=-2).reshape(shp)


def _block_diag2(a, b):
    za = jnp.zeros((a.shape[0], b.shape[1]), a.dtype)
    zb = jnp.zeros((b.shape[0], a.shape[1]), a.dtype)
    return jnp.concatenate([jnp.concatenate([a, za], 1), jnp.concatenate([zb, b], 1)], 0)


def _layer_weights(P, l):
    w_in = P["w_in"][l]
    o1, o2, o3 = RET_COLS, RET_COLS + ATT_COLS, RET_COLS + ATT_COLS + RW_COLS
    w_ret, w_att, w_rw, w_gate = w_in[:, :o1], w_in[:, o1:o2], w_in[:, o2:o3], w_in[:, o3:]
    W = RET_WIDTH
    w_ret = jnp.concatenate([w_ret, _swap_halves(w_ret[:, 0:W], RET_HEADS),
                             _swap_halves(w_ret[:, W:2 * W], RET_HEADS)], axis=1)
    QW, KW = ATT_Q_WIDTH, ATT_KV_WIDTH
    w_att = jnp.concatenate([w_att, _swap_halves(w_att[:, 0:QW], ATT_Q_HEADS),
                             _swap_halves(w_att[:, QW:QW + KW], ATT_KV_HEADS)], axis=1)
    qg, kg = P["q_norm_g"][l], P["k_norm_g"][l]
    pad = lambda v: jnp.zeros((QW,), F32).at[:v.shape[0]].set(v)
    qk_gain = jnp.zeros((8, QW), F32)
    qk_gain = qk_gain.at[0].set(jnp.tile(qg, ATT_Q_HEADS))
    qk_gain = qk_gain.at[1].set(jnp.tile(_swap_halves(qg, 1), ATT_Q_HEADS))
    qk_gain = qk_gain.at[2].set(pad(jnp.tile(kg, ATT_KV_HEADS)))
    qk_gain = qk_gain.at[3].set(pad(jnp.tile(_swap_halves(kg, 1), ATT_KV_HEADS)))
    rw_vec = jnp.zeros((8, RWKV_WIDTH), F32)
    for j, name in enumerate(("rw_w0_f", "rw_w0_b", "rw_a0_f", "rw_a0_b", "rw_k_k", "rw_k_a")):
        rw_vec = rw_vec.at[j].set(P[name][l])
    rw_vec = rw_vec.at[6].set(P["rw_r_k"][l].reshape(RWKV_WIDTH))
    row = lambda v: v.reshape(1, -1)
    return {
        "norm1_g": row(P["norm1_g"][l]), "norm2_g": row(P["norm2_g"][l]),
        "w_ret": w_ret.astype(BF16), "w_att": w_att.astype(BF16), "w_rw": w_rw.astype(BF16),
        "w_gate": w_gate.astype(BF16), "qk_gain": qk_gain, "ret_gn": row(P["ret_gn"][l]),
        "shift_mu": P["shift_mu"][l], "rw_vec": rw_vec,
        "rw_w2": _block_diag2(P["rw_w2_f"][l], P["rw_w2_b"][l]).astype(BF16),
        "rw_a2": _block_diag2(P["rw_a2_f"][l], P["rw_a2_b"][l]).astype(BF16),
        "rw_g2": P["rw_g2"][l].astype(BF16), "rw_gn": row(P["rw_gn"][l]),
        "w_branch_a": P["w_branch_a"][l].astype(BF16), "w_branch_b": P["w_branch_b"][l].astype(BF16),
        "w_branch_c": P["w_branch_c"][l].astype(BF16), "w_out": P["w_out"][l].astype(BF16),
        "router_wt": P["router_w"][l].T,
        "router_bias": jnp.broadcast_to(P["router_bias"][l][:, None], (N_EXPERTS, V7X_LANES)),
        "sh_gu": jnp.concatenate([P["sh_w_gate"][l], P["sh_w_up"][l]], axis=1).astype(BF16),
        "sh_d": P["sh_w_down"][l].astype(BF16),
        "exp_gu": jnp.concatenate([P["exp_w_gate"][l], P["exp_w_up"][l]], axis=2).astype(BF16),
        "exp_d": P["exp_w_down"][l].astype(BF16),
    }


def _ones_block_diag(n):
    idx = np.arange(n) // HEAD_DIM
    return jnp.asarray(idx[:, None] == idx[None, :], BF16)


def _tables(T):
    rows = T // GRID_W
    row = jnp.repeat(jnp.arange(rows, dtype=F32), GRID_W)
    col = jnp.tile(jnp.arange(GRID_W, dtype=F32), rows)
    freqs = ROPE_THETA ** (-jnp.arange(ROPE_PAIRS_PER_AXIS, dtype=F32) / ROPE_PAIRS_PER_AXIS)
    ang = jnp.concatenate([row[:, None] * freqs, col[:, None] * freqs], axis=-1)
    cos, sin = jnp.cos(ang), jnp.sin(ang)
    return {
        "cos": jnp.tile(jnp.concatenate([cos, cos], axis=-1), (1, ATT_Q_HEADS)),
        "sin": jnp.tile(jnp.concatenate([-sin, sin], axis=-1), (1, ATT_Q_HEADS)),
        "ones128": _ones_block_diag(128), "ones256": _ones_block_diag(256), "ones512": _ones_block_diag(512),
    }


def _trunks(requests, P, layer_weights):
    groups = []
    for x, c in requests:
        B, T, D = x.shape
        groups.append({"B": B, "T": T, "x": x.reshape(B * T, D), "tabs": _tables(T),
                       "mods": _ada_mod(c, P["ada_w"], P["ada_b"])})
    final_g = P["final_g"].reshape(1, D_MODEL)
    depth = len(layer_weights)
    for l, lw in enumerate(layer_weights):
        mixed = []
        for g in groups:
            B, T, tabs, mod = g["B"], g["T"], g["tabs"], g["mods"][l]
            ret, q, k, v, gates, rwp = _input_projections(g["x"], mod, T, lw, tabs)
            ya = _retention(ret, B, T, lw["ret_gn"], tabs["ones256"])
            yb = _attention(q, k, v, B, T)
            mixed.append((mod, ya, yb, gates, rwp))
        scans = _rwkv_scan([(m[4], g["B"], g["T"]) for m, g in zip(mixed, groups)], groups[0]["tabs"]["ones256"])
        for g, (mod, ya, yb, gates, rwp), (yf, ybk) in zip(groups, mixed, scans):
            x = _merge(g["x"], mod, g["T"], ya, yb, yf, ybk, rwp, gates, lw, g["tabs"]["ones256"])
            g["x"] = _moe(x, mod, g["T"], lw, final_g, final=(l == depth - 1))
    return [g["x"].reshape(g["B"], g["T"], D_MODEL) for g in groups]


def kernel(x_prompt, x_sample, c_prompt, c_sample, norm1_g, norm2_g, final_g, ada_w, ada_b, w_in, ret_gn, q_norm_g, k_norm_g, shift_mu, rw_w0_f, rw_w2_f, rw_w0_b, rw_w2_b, rw_a0_f, rw_a2_f, rw_a0_b, rw_a2_b, rw_g2, rw_k_k, rw_k_a, rw_r_k, rw_gn, w_branch_a, w_branch_b, w_branch_c, w_out, router_w, router_bias, exp_w_gate, exp_w_up, exp_w_down, sh_w_gate, sh_w_up, sh_w_down):
    P = {
        "norm1_g": norm1_g, "norm2_g": norm2_g, "final_g": final_g, "ada_w": ada_w, "ada_b": ada_b,
        "w_in": w_in, "ret_gn": ret_gn, "q_norm_g": q_norm_g, "k_norm_g": k_norm_g, "shift_mu": shift_mu,
        "rw_w0_f": rw_w0_f, "rw_w2_f": rw_w2_f, "rw_w0_b": rw_w0_b, "rw_w2_b": rw_w2_b,
        "rw_a0_f": rw_a0_f, "rw_a2_f": rw_a2_f, "rw_a0_b": rw_a0_b, "rw_a2_b": rw_a2_b,
        "rw_g2": rw_g2, "rw_k_k": rw_k_k, "rw_k_a": rw_k_a, "rw_r_k": rw_r_k, "rw_gn": rw_gn,
        "w_branch_a": w_branch_a, "w_branch_b": w_branch_b, "w_branch_c": w_branch_c, "w_out": w_out,
        "router_w": router_w, "router_bias": router_bias,
        "exp_w_gate": exp_w_gate, "exp_w_up": exp_w_up, "exp_w_down": exp_w_down,
        "sh_w_gate": sh_w_gate, "sh_w_up": sh_w_up, "sh_w_down": sh_w_down,
    }
    layer_weights = [_layer_weights(P, l) for l in range(w_in.shape[0])]
    y_prompt, y_sample = _trunks([(x_prompt, c_prompt), (x_sample, c_sample)], P, layer_weights)
    return (y_prompt, y_sample)
```

```python
import functools

import numpy as np
import jax
import jax.numpy as jnp
from jax import lax
from jax.experimental import pallas as pl
from jax.experimental.pallas import tpu as pltpu
from jax.experimental.pallas import tpu_sc as plsc

F32 = jnp.float32
BF16 = jnp.bfloat16

D_MODEL = 1024
DEPTH = 2
GRID_W = 64
HEAD_DIM = 64
ROPE_THETA = 10000.0
ROPE_PAIRS_PER_AXIS = HEAD_DIM // 4

RET_HEADS = 4
RET_WIDTH = RET_HEADS * HEAD_DIM
RET_CHUNK = 128
RET_CHUNKS_PER_STEP = 4
RET_EPS = 1e-5

ATT_Q_HEADS = 8
ATT_KV_HEADS = 2
ATT_GROUP = ATT_Q_HEADS // ATT_KV_HEADS
ATT_Q_WIDTH = ATT_Q_HEADS * HEAD_DIM
ATT_KV_WIDTH = ATT_KV_HEADS * HEAD_DIM

RWKV_HEADS = 4
RWKV_WIDTH = RWKV_HEADS * HEAD_DIM
DECAY_LORA = 64
AAA_LORA = 64
GATE_LORA = 128
RWKV_GN_EPS = 64e-5
RW_CHUNK = 64
RW_FIELDS = 11
RW_BATCH_ROWS = 4

FLASH_Q_ROWS = 512
FLASH_KV_ROWS = 512
FLASH_SUB_ROWS = 1024

MOE_TOKENS = 512
MOE_GROUP_ROWS_MIN = 256
MOE_GROUP_ROWS_MAX = 512
SC_GATHER_ROWS = 128

N_EXPERTS = 64
TOP_K = 8
N_GROUPS = 8
TOPK_GROUPS = 4
EXPERT_FF = 256
ROUTED_SCALE = 2.5
NORM_EPS = 1e-6

RET_COLS = 4 * RET_WIDTH
ATT_COLS = ATT_Q_WIDTH + 2 * ATT_KV_WIDTH
RW_COLS = 3 * RWKV_WIDTH + 2 * DECAY_LORA + 2 * AAA_LORA + GATE_LORA
GATE_COLS = 3 * D_MODEL

V7X_LANES = 128
VMEM_LIMIT_BYTES = 48 * 1024 * 1024


def _params(*dims):
    return pltpu.CompilerParams(dimension_semantics=dims, vmem_limit_bytes=VMEM_LIMIT_BYTES)


def _sigmoid(x):
    return 1.0 / (1.0 + jnp.exp(-x))


def _silu(x):
    return x * _sigmoid(x)


def _dot(a, b):
    return jnp.dot(a, b, preferred_element_type=F32)


def _dot_nt(a, b):
    return lax.dot_general(a, b, (((1,), (1,)), ((), ())), preferred_element_type=F32)


def _split(a):
    hi = a.astype(BF16)
    lo = (a - hi.astype(F32)).astype(BF16)
    return hi, lo


def _dot_split_lhs(a, b_bf16):
    hi, lo = _split(a)
    return _dot(hi, b_bf16) + _dot(lo, b_bf16)


def _dot3(a, b):
    ah, al = _split(a)
    bh, bl = _split(b)
    return _dot(ah, bh) + _dot(ah, bl) + _dot(al, bh)


def _norm_mod(x, gain, shift, scale):
    ms = jnp.mean(x * x, axis=-1, keepdims=True)
    return x * lax.rsqrt(ms + NORM_EPS) * gain * (1.0 + scale) + shift


def _head_layer_norm(y, ones_bd, eps):
    mean = _dot_split_lhs(y, ones_bd) * (1.0 / HEAD_DIM)
    yc = y - mean
    var = _dot_split_lhs(yc * yc, ones_bd) * (1.0 / HEAD_DIM)
    return yc * lax.rsqrt(var + eps)


def _ada_kernel(c_ref, w_ref, b_ref, o_ref):
    c = c_ref[...]
    o_ref[...] = _dot3(_silu(c), w_ref[...]) + b_ref[...]


def _ada_mod(c, ada_w, ada_b):
    B = c.shape[0]
    Bp = max(8, B)
    cp = jnp.zeros((Bp, D_MODEL), F32).at[:B].set(c)
    L = ada_w.shape[0]
    tn = 1536
    out = pl.pallas_call(
        _ada_kernel,
        grid=(L, 6 * D_MODEL // tn),
        in_specs=[pl.BlockSpec((Bp, D_MODEL), lambda l, j: (0, 0)),
                  pl.BlockSpec((None, D_MODEL, tn), lambda l, j: (l, 0, j)),
                  pl.BlockSpec((None, 1, tn), lambda l, j: (l, 0, j))],
        out_specs=pl.BlockSpec((None, Bp, tn), lambda l, j: (l, 0, j)),
        out_shape=jax.ShapeDtypeStruct((L, Bp, 6 * D_MODEL), F32),
        compiler_params=_params("parallel", "parallel"),
        name="ada_mod",
    )(cp, ada_w, ada_b.reshape(L, 1, 6 * D_MODEL))
    return out[:, :B].reshape(L, B, 6, D_MODEL)


def _proj_ret_kernel(x_ref, mod_ref, g_ref, w_ref, cos_ref, sin_ref, o_ref):
    h = _norm_mod(x_ref[...], g_ref[...], mod_ref[0:1, :], mod_ref[1:2, :]).astype(BF16)
    p = _dot(h, w_ref[...])
    W = RET_WIDTH
    cos = cos_ref[:, 0:W]
    sin = sin_ref[:, 0:W]
    o_ref[:, 0:W] = (p[:, 0:W] * cos + p[:, 4 * W:5 * W] * sin).astype(BF16)
    o_ref[:, W:2 * W] = ((p[:, W:2 * W] * cos + p[:, 5 * W:6 * W] * sin) * (HEAD_DIM ** -0.5)).astype(BF16)
    o_ref[:, 2 * W:3 * W] = p[:, 2 * W:3 * W].astype(BF16)
    o_ref[:, 3 * W:4 * W] = _silu(p[:, 3 * W:4 * W]).astype(BF16)


def _proj_att_kernel(x_ref, mod_ref, g_ref, w_ref, cos_ref, sin_ref, qkg_ref, ones_ref, q_ref, k_ref, v_ref):
    h = _norm_mod(x_ref[...], g_ref[...], mod_ref[0:1, :], mod_ref[1:2, :]).astype(BF16)
    p = _dot(h, w_ref[...])
    QW, KW = ATT_Q_WIDTH, ATT_KV_WIDTH
    q = p[:, 0:QW]
    k = p[:, QW:QW + KW]
    v = p[:, QW + KW:QW + 2 * KW]
    qs = p[:, QW + 2 * KW:2 * QW + 2 * KW]
    ks = p[:, 2 * QW + 2 * KW:2 * QW + 3 * KW]
    ones = ones_ref[...]
    rq = lax.rsqrt(_dot_split_lhs(q * q, ones) * (1.0 / HEAD_DIM) + NORM_EPS)
    rk = lax.rsqrt(_dot_split_lhs(k * k, ones[0:KW, 0:KW]) * (1.0 / HEAD_DIM) + NORM_EPS)
    cos = cos_ref[...]
    sin = sin_ref[...]
    qo = (q * qkg_ref[0:1, :] * cos + qs * qkg_ref[1:2, :] * sin) * (rq * (HEAD_DIM ** -0.5))
    ko = (k * qkg_ref[2:3, 0:KW] * cos[:, 0:KW] + ks * qkg_ref[3:4, 0:KW] * sin[:, 0:KW]) * rk
    q_ref[...] = qo.astype(BF16)
    kb = ko.astype(BF16)
    vb = v.astype(BF16)
    for hk in range(ATT_KV_HEADS):
        k_ref[hk] = kb[:, hk * HEAD_DIM:(hk + 1) * HEAD_DIM]
        v_ref[hk, :, 0:HEAD_DIM] = vb[:, hk * HEAD_DIM:(hk + 1) * HEAD_DIM]
        v_ref[hk, :, HEAD_DIM:2 * HEAD_DIM] = jnp.ones((vb.shape[0], HEAD_DIM), BF16)


def _proj_gate_kernel(x_ref, mod_ref, g_ref, w_ref, o_ref):
    h = _norm_mod(x_ref[...], g_ref[...], mod_ref[0:1, :], mod_ref[1:2, :]).astype(BF16)
    o_ref[...] = _sigmoid(_dot(h, w_ref[...])).astype(BF16)


def _proj_rw_kernel(x_ref, xp_ref, xn_ref, mod_ref, g_ref, w_ref, mu_ref, vec_ref, w2_ref, a2_ref, g2_ref,
                    ones_ref, o_ref, p_scr, *, tm, tiles_per_seq):
    i = pl.program_id(0)
    gain, shift, scale = g_ref[...], mod_ref[0:1, :], mod_ref[1:2, :]
    w = w_ref[...]
    h = _norm_mod(x_ref[...], gain, shift, scale).astype(BF16)
    p = _dot(h, w)
    hp = _norm_mod(xp_ref[...], gain, shift, scale).astype(BF16)
    hn = _norm_mod(xn_ref[...], gain, shift, scale).astype(BF16)
    first = (i % tiles_per_seq) == 0
    last = (i % tiles_per_seq) == tiles_per_seq - 1
    prev_row = jnp.where(first, 0.0, _dot(hp, w)[7:8, :])
    next_row = jnp.where(last, 0.0, _dot(hn, w)[0:1, :])
    p_scr[8:8 + tm, :] = p
    p_scr[7:8, :] = prev_row
    p_scr[8 + tm:9 + tm, :] = next_row
    prev = p_scr[7:7 + tm, :]
    nxt = p_scr[9:9 + tm, :]
    x = p + mu_ref[0:1, :] * (prev - p) + mu_ref[1:2, :] * (nxt - p)

    W = RWKV_WIDTH
    r, k, v = x[:, 0:W], x[:, W:2 * W], x[:, 2 * W:3 * W]
    xw = x[:, 3 * W:3 * W + 2 * DECAY_LORA]
    xa = x[:, 3 * W + 2 * DECAY_LORA:3 * W + 2 * DECAY_LORA + 2 * AAA_LORA]
    xg = x[:, 3 * W + 2 * DECAY_LORA + 2 * AAA_LORA:]
    w0_f, w0_b = vec_ref[0:1, :], vec_ref[1:2, :]
    a0_f, a0_b = vec_ref[2:3, :], vec_ref[3:4, :]
    k_k, k_a, r_k = vec_ref[4:5, :], vec_ref[5:6, :], vec_ref[6:7, :]
    wl = _dot(jnp.tanh(xw).astype(BF16), w2_ref[...])
    al = _dot(xa.astype(BF16), a2_ref[...])
    dec_c = float(np.exp(-0.5))
    w_f = jnp.exp(-dec_c * _sigmoid(w0_f + wl[:, 0:W]))
    w_b = jnp.exp(-dec_c * _sigmoid(w0_b + wl[:, W:2 * W]))
    a_f = _sigmoid(a0_f + al[:, 0:W])
    a_b = _sigmoid(a0_b + al[:, W:2 * W])
    gate = _dot(_sigmoid(xg).astype(BF16), g2_ref[...])
    ones = ones_ref[...]
    kk = k * k_k
    kk = kk * lax.rsqrt(_dot_split_lhs(kk * kk, ones) + 1e-12)
    k_f = k * (1.0 + (a_f - 1.0) * k_a)
    k_b = k * (1.0 + (a_b - 1.0) * k_a)
    bonus = _dot_split_lhs(r * k_f * r_k, ones) * v
    for j, val in enumerate((r, kk, v, w_f, w_b, k_f, k_b, kk * a_f, kk * a_b, gate, bonus)):
        o_ref[:, j * W:(j + 1) * W] = val


def _token_specs(tm, tiles_per_seq):
    x_spec = pl.BlockSpec((tm, D_MODEL), lambda i: (i, 0))
    mod_spec = pl.BlockSpec((None, 6, D_MODEL), lambda i: (i // tiles_per_seq, 0, 0))
    g_spec = pl.BlockSpec((1, D_MODEL), lambda i: (0, 0))
    return x_spec, mod_spec, g_spec


def _full(shape):
    nd = len(shape)
    return pl.BlockSpec(shape, lambda *_: (0,) * nd)


def _input_projections(x, mod, T, lw, tabs):
    N = x.shape[0]
    tm = min(512, T)
    tps = T // tm
    grid = (N // tm,)
    x_spec, mod_spec, g_spec = _token_specs(tm, tps)
    tab_spec = pl.BlockSpec((tm, ATT_Q_WIDTH), lambda i: (i % tps, 0))
    cos, sin = tabs["cos"], tabs["sin"]

    ret = pl.pallas_call(
        _proj_ret_kernel, grid=grid,
        in_specs=[x_spec, mod_spec, g_spec, _full(lw["w_ret"].shape), tab_spec, tab_spec],
        out_specs=pl.BlockSpec((tm, RET_COLS), lambda i: (i, 0)),
        out_shape=jax.ShapeDtypeStruct((N, RET_COLS), BF16),
        compiler_params=_params("parallel"), name="proj_ret",
    )(x, mod, lw["norm1_g"], lw["w_ret"], cos, sin)

    q, k, v = pl.pallas_call(
        _proj_att_kernel, grid=grid,
        in_specs=[x_spec, mod_spec, g_spec, _full(lw["w_att"].shape), tab_spec, tab_spec,
                  _full(lw["qk_gain"].shape), _full(tabs["ones512"].shape)],
        out_specs=[pl.BlockSpec((tm, ATT_Q_WIDTH), lambda i: (i, 0)),
                   pl.BlockSpec((ATT_KV_HEADS, tm, HEAD_DIM), lambda i: (0, i, 0)),
                   pl.BlockSpec((ATT_KV_HEADS, tm, 2 * HEAD_DIM), lambda i: (0, i, 0))],
        out_shape=[jax.ShapeDtypeStruct((N, ATT_Q_WIDTH), BF16),
                   jax.ShapeDtypeStruct((ATT_KV_HEADS, N, HEAD_DIM), BF16),
                   jax.ShapeDtypeStruct((ATT_KV_HEADS, N, 2 * HEAD_DIM), BF16)],
        compiler_params=_params("parallel"), name="proj_att",
    )(x, mod, lw["norm1_g"], lw["w_att"], cos, sin, lw["qk_gain"], tabs["ones512"])

    gates = pl.pallas_call(
        _proj_gate_kernel, grid=grid,
        in_specs=[x_spec, mod_spec, g_spec, _full(lw["w_gate"].shape)],
        out_specs=pl.BlockSpec((tm, GATE_COLS), lambda i: (i, 0)),
        out_shape=jax.ShapeDtypeStruct((N, GATE_COLS), BF16),
        compiler_params=_params("parallel"), name="proj_gate",
    )(x, mod, lw["norm1_g"], lw["w_gate"])

    tm_rw = min(256, T)
    tps_rw = T // tm_rw
    x_spec, mod_spec, g_spec = _token_specs(tm_rw, tps_rw)
    rows8 = tm_rw // 8
    last8 = N // 8 - 1
    rwp = pl.pallas_call(
        functools.partial(_proj_rw_kernel, tm=tm_rw, tiles_per_seq=tps_rw), grid=(N // tm_rw,),
        in_specs=[x_spec,
                  pl.BlockSpec((8, D_MODEL), lambda i: (jnp.maximum(i * rows8 - 1, 0), 0)),
                  pl.BlockSpec((8, D_MODEL), lambda i: (jnp.minimum((i + 1) * rows8, last8), 0)),
                  mod_spec, g_spec, _full(lw["w_rw"].shape), _full(lw["shift_mu"].shape),
                  _full(lw["rw_vec"].shape), _full(lw["rw_w2"].shape), _full(lw["rw_a2"].shape),
                  _full(lw["rw_g2"].shape), _full(tabs["ones256"].shape)],
        out_specs=pl.BlockSpec((tm_rw, RW_FIELDS * RWKV_WIDTH), lambda i: (i, 0)),
        out_shape=jax.ShapeDtypeStruct((N, RW_FIELDS * RWKV_WIDTH), F32),
        scratch_shapes=[pltpu.VMEM((tm_rw + 16, RW_COLS), F32)],
        compiler_params=_params("parallel"), name="proj_rw",
    )(x, x, x, mod, lw["norm1_g"], lw["w_rw"], lw["shift_mu"], lw["rw_vec"], lw["rw_w2"], lw["rw_a2"],
      lw["rw_g2"], tabs["ones256"])
    return ret, q, k, v, gates, rwp


def _retention_tables(reverse):
    lg = np.log1p(-np.exp2(-5.0 - np.arange(RET_HEADS, dtype=np.float64)))
    if reverse:
        lg = lg[::-1]
    C = RET_CHUNK
    pos = np.arange(C, dtype=np.float64)
    diff = pos[:, None] - pos[None, :]
    if reverse:
        dec = np.where(diff < 0, np.exp(lg[:, None, None] * np.maximum(-diff, 0.0)[None]), 0.0)
        xi = np.exp(lg[:, None] * (C - pos)[None, :])
        zeta = np.exp(lg[:, None] * pos[None, :])
    else:
        dec = np.where(diff >= 0, np.exp(lg[:, None, None] * np.maximum(diff, 0.0)[None]), 0.0)
        xi = np.exp(lg[:, None] * (pos + 1.0)[None, :])
        zeta = np.exp(lg[:, None] * (C - 1.0 - pos)[None, :])
    chunk_decay = tuple(float(v) for v in np.exp(lg * C))
    widen = lambda a: jnp.asarray(np.repeat(a.T, HEAD_DIM, axis=1), F32)
    return jnp.asarray(dec, F32), widen(xi), widen(zeta), chunk_decay


def _retention_kernel(ret_ref, dec_ref, xi_ref, zeta_ref, *rest, chunk_decay, final, chunks):
    if final:
        yf_ref, gn_ref, ones_ref, o_ref, state, y_scr = rest
    else:
        o_ref, state, y_scr = rest

    @pl.when(pl.program_id(1) == 0)
    def _():
        state[...] = jnp.zeros_like(state)

    W, C = RET_WIDTH, RET_CHUNK
    for step in range(chunks):
        cb = chunks - 1 - step if final else step
        rows = slice(cb * C, (cb + 1) * C)
        q = ret_ref[rows, 0:W]
        k = ret_ref[rows, W:2 * W]
        v = ret_ref[rows, 2 * W:3 * W]
        qx = (q.astype(F32) * xi_ref[...]).astype(BF16)
        kzt = (k.astype(F32) * zeta_ref[...]).T
        for h in range(RET_HEADS):
            sl = slice(h * HEAD_DIM, (h + 1) * HEAD_DIM)
            s = _dot_nt(q[:, sl], k[:, sl]) * dec_ref[h]
            inner = _dot(s.astype(BF16), v[:, sl])
            r_prev = state[h]
            cross = _dot(qx[:, sl], r_prev.astype(BF16))
            kv = _dot(kzt[sl, :].astype(BF16), v[:, sl])
            state[h] = r_prev * chunk_decay[h] + kv
            y_scr[rows, sl] = inner + cross
    y = y_scr[...]
    if final:
        y = _head_layer_norm(y + yf_ref[...], ones_ref[...], RET_EPS) * gn_ref[...]
        o_ref[...] = (y * ret_ref[:, 3 * W:4 * W].astype(F32)).astype(BF16)
    else:
        o_ref[...] = y


def _retention(ret, B, T, ret_gn, ones256):
    N = ret.shape[0]
    chunks = min(RET_CHUNKS_PER_STEP, T // RET_CHUNK)
    R = chunks * RET_CHUNK
    nblk = T // R
    scratch = [pltpu.VMEM((RET_HEADS, HEAD_DIM, HEAD_DIM), F32), pltpu.VMEM((R, RET_WIDTH), F32)]

    dec, xi, zeta, cd = _retention_tables(False)
    fwd_row = lambda b, c: (b * nblk + c, 0)
    yf = pl.pallas_call(
        functools.partial(_retention_kernel, chunk_decay=cd, final=False, chunks=chunks), grid=(B, nblk),
        in_specs=[pl.BlockSpec((R, RET_COLS), fwd_row), _full(dec.shape), _full(xi.shape), _full(zeta.shape)],
        out_specs=pl.BlockSpec((R, RET_WIDTH), fwd_row),
        out_shape=jax.ShapeDtypeStruct((N, RET_WIDTH), F32),
        scratch_shapes=scratch, compiler_params=_params("parallel", "arbitrary"), name="retention_fwd",
    )(ret, dec, xi, zeta)

    dec, xi, zeta, cd = _retention_tables(True)
    bwd_row = lambda b, c: (b * nblk + nblk - 1 - c, 0)
    return pl.pallas_call(
        functools.partial(_retention_kernel, chunk_decay=cd, final=True, chunks=chunks), grid=(B, nblk),
        in_specs=[pl.BlockSpec((R, RET_COLS), bwd_row), _full(dec.shape), _full(xi.shape), _full(zeta.shape),
                  pl.BlockSpec((R, RET_WIDTH), bwd_row), _full(ret_gn.shape), _full(ones256.shape)],
        out_specs=pl.BlockSpec((R, RET_WIDTH), bwd_row),
        out_shape=jax.ShapeDtypeStruct((N, RET_WIDTH), BF16),
        scratch_shapes=scratch, compiler_params=_params("parallel", "arbitrary"), name="retention_bwd",
    )(ret, dec, xi, zeta, yf, ret_gn, ones256)


def _flash_kernel(q_ref, k_ref, v_ref, o_ref, q_scr, m_scr, acc_scr, *, tq, tk):
    j = pl.program_id(3)

    @pl.when(j == 0)
    def _():
        for g in range(ATT_GROUP):
            q_scr[g * tq:(g + 1) * tq, :] = q_ref[:, g * HEAD_DIM:(g + 1) * HEAD_DIM]
        m_scr[...] = jnp.full_like(m_scr, -jnp.inf)
        acc_scr[...] = jnp.zeros_like(acc_scr)

    k = k_ref[...]
    v = v_ref[...]
    sub = min(FLASH_SUB_ROWS, ATT_GROUP * tq)
    for r0 in range(0, ATT_GROUP * tq, sub):
        rows = slice(r0, r0 + sub)
        s = _dot_nt(q_scr[rows, :], k)
        m_prev = m_scr[rows, :]
        m_next = jnp.maximum(m_prev, jnp.max(s, axis=1, keepdims=True))
        p = jnp.exp(s - jnp.concatenate([m_next] * (tk // V7X_LANES), axis=1))
        alpha = jnp.exp(m_prev - m_next)
        m_scr[rows, :] = m_next
        acc_scr[rows, :] = alpha * acc_scr[rows, :] + _dot(p.astype(BF16), v)

    @pl.when(j == pl.num_programs(3) - 1)
    def _():
        o = acc_scr[:, 0:HEAD_DIM] / acc_scr[:, HEAD_DIM:2 * HEAD_DIM]
        for g in range(ATT_GROUP):
            o_ref[:, g * HEAD_DIM:(g + 1) * HEAD_DIM] = o[g * tq:(g + 1) * tq, :].astype(BF16)


def _attention(q, k, v, B, T):
    N = q.shape[0]
    tq = min(FLASH_Q_ROWS, T)
    tk = min(FLASH_KV_ROWS, T)
    nq, nk = T // tq, T // tk
    GW = ATT_GROUP * HEAD_DIM
    rows = ATT_GROUP * tq
    return pl.pallas_call(
        functools.partial(_flash_kernel, tq=tq, tk=tk), grid=(B, ATT_KV_HEADS, nq, nk),
        in_specs=[pl.BlockSpec((tq, GW), lambda b, h, i, j: (b * nq + i, h)),
                  pl.BlockSpec((None, tk, HEAD_DIM), lambda b, h, i, j: (h, b * nk + j, 0)),
                  pl.BlockSpec((None, tk, 2 * HEAD_DIM), lambda b, h, i, j: (h, b * nk + j, 0))],
        out_specs=pl.BlockSpec((tq, GW), lambda b, h, i, j: (b * nq + i, h)),
        out_shape=jax.ShapeDtypeStruct((N, ATT_Q_WIDTH), BF16),
        scratch_shapes=[pltpu.VMEM((rows, HEAD_DIM), BF16), pltpu.VMEM((rows, V7X_LANES), F32),
                        pltpu.VMEM((rows, 2 * HEAD_DIM), F32)],
        compiler_params=_params("parallel", "parallel", "parallel", "arbitrary"), name="flash_attention",
    )(q, k, v)


def _rwkv_scan_kernel(*refs, rows_per_set):
    C = RW_CHUNK
    H = C // 2
    W = RWKV_WIDTH
    SUB = 8
    n_sets = len(rows_per_set)
    ones_ref = refs[12 * n_sets]
    outs = refs[12 * n_sets + 1:12 * n_sets + 1 + 2 * n_sets]
    state, vk_scr = refs[-2:]
    dirs = [(refs[12 * s:12 * s + 6], refs[12 * s + 6:12 * s + 12]) for s in range(n_sets)]
    chains = [(s, n, d) for s in range(n_sets) for n in range(rows_per_set[s]) for d in range(2)]

    first_chunk = pl.program_id(1) == 0

    @pl.when(first_chunk)
    def _():
        for c, (s, n, d) in enumerate(chains):
            if s == 0:
                state[c] = jnp.zeros((HEAD_DIM, W), F32)

    @pl.when(jnp.logical_and(first_chunk, pl.program_id(0) == 0))
    def _():
        for c, (s, n, d) in enumerate(chains):
            if s > 0:
                state[c] = jnp.zeros((HEAD_DIM, W), F32)

    ones = ones_ref[...]
    same_head = (lax.broadcasted_iota(jnp.int32, (W, W), 0) // HEAD_DIM
                 == lax.broadcasted_iota(jnp.int32, (W, W), 1) // HEAD_DIM)
    step_iota = lax.broadcasted_iota(jnp.int32, (H, HEAD_DIM, W), 0)
    lane_iota = lax.broadcasted_iota(jnp.int32, (H, HEAD_DIM, W), 2) & (HEAD_DIM - 1)

    def head_sums(vals):
        stacked = jnp.concatenate([v.astype(BF16) for v in vals], axis=0)
        out = _dot(stacked, ones)
        return [out[c * HEAD_DIM:(c + 1) * HEAD_DIM, :] for c in range(len(vals))]

    own_head = (lax.broadcasted_iota(jnp.int32, (SUB, W), 0)
                == lax.broadcasted_iota(jnp.int32, (SUB, W), 1) // HEAD_DIM)
    carry = tuple(state[c] for c in range(len(chains)))
    for half in range(2):
        lo = (half * H, (1 - half) * H)
        for c, (s, n, d) in enumerate(chains):
            kc = dirs[s][d][3][n]
            kbd = jnp.where(same_head, jnp.concatenate([kc] * (W // C), axis=0), 0.0).astype(BF16)
            sel = jnp.where(step_iota + lo[d] == lane_iota, dirs[s][d][5][n][None], 0.0).reshape(H * HEAD_DIM, W)
            vk_scr[c] = _dot(sel.astype(BF16), kbd).reshape(H, HEAD_DIM, W)

        def group(g, states, lo=lo):
            states = list(states)
            bases = (lo[0] + g * SUB, lo[1] + (H // SUB - 1 - g) * SUB)
            rows = [[ref[n, pl.ds(pl.multiple_of(bases[d], SUB), SUB), :] for ref in dirs[s][d][:5]]
                    for (s, n, d) in chains]
            ys = [[None] * SUB for _ in chains]
            for j in range(SUB):
                jj = (j, SUB - 1 - j)
                row = lambda c, f: rows[c][f][jj[chains[c][2]]:jj[chains[c][2]] + 1, :]
                sa = head_sums([states[c] * row(c, 0) for c in range(len(chains))])
                new = [states[c] * row(c, 2) - sa[c] * row(c, 4)
                       + vk_scr[c, bases[chains[c][2]] + jj[chains[c][2]] - lo[chains[c][2]]]
                       for c in range(len(chains))]
                for c, (s, n, d) in enumerate(chains):
                    read = new[c] if d == 0 else states[c]
                    r_heads = jnp.where(own_head, row(c, 1), 0.0).astype(BF16)
                    ys[c][jj[d]] = _dot_nt(r_heads, read.astype(BF16))
                states = new
            for c, (s, n, d) in enumerate(chains):
                span = pl.ds(pl.multiple_of(bases[d], SUB), SUB)
                for h in range(RWKV_HEADS):
                    col = jnp.concatenate([ys[c][r][h:h + 1, :] for r in range(SUB)], axis=0)
                    outs[2 * s + d][n, span, h * HEAD_DIM:(h + 1) * HEAD_DIM] = col
            return tuple(states)

        carry = lax.fori_loop(0, H // SUB, group, carry)

    for c in range(len(chains)):
        state[c] = carry[c]


def _rwkv_scan_call(sets, ones256):
    W, C = RWKV_WIDTH, RW_CHUNK
    _, B0, T0, nb0 = sets[0]
    nc0 = T0 // C
    operands, in_specs, out_specs, out_shapes = [], [], [], []
    for s, (rwp, B, T, nb) in enumerate(sets):
        nc = T // C
        fields = rwp.reshape(B, T, RW_FIELDS * W)
        v = rwp[:, 2 * W:3 * W].reshape(B, nc, C, RWKV_HEADS, HEAD_DIM)
        vt = v.transpose(0, 1, 4, 3, 2).reshape(B, nc, HEAD_DIM, W)

        def block(b, c, reverse, s=s, nc=nc):
            bi, ci = (b, c) if s == 0 else (0, b * nc0 + c)
            return bi, (nc - 1 - ci if reverse else ci)

        field = lambda f, rev, nb=nb, block=block: pl.BlockSpec(
            (nb, C, W), lambda b, c: block(b, c, rev) + (f,))
        vt_spec = lambda rev, nb=nb, block=block: pl.BlockSpec(
            (nb, None, HEAD_DIM, W), lambda b, c: block(b, c, rev) + (0, 0))
        in_specs += [field(1, False), field(0, False), field(3, False), field(5, False), field(7, False),
                     vt_spec(False),
                     field(1, True), field(0, True), field(4, True), field(6, True), field(8, True),
                     vt_spec(True)]
        operands += [fields] * 5 + [vt] + [fields] * 5 + [vt]
        y_spec = lambda rev, nb=nb, block=block: pl.BlockSpec(
            (nb, C, W), lambda b, c: block(b, c, rev) + (0,))
        out_specs += [y_spec(False), y_spec(True)]
        out_shapes += [jax.ShapeDtypeStruct((B, T, W), F32)] * 2
    n_chains = 2 * sum(nb for _, _, _, nb in sets)
    outs = pl.pallas_call(
        functools.partial(_rwkv_scan_kernel, rows_per_set=tuple(nb for _, _, _, nb in sets)),
        grid=(B0 // nb0, nc0), in_specs=in_specs + [_full(ones256.shape)],
        out_specs=out_specs, out_shape=out_shapes,
        scratch_shapes=[pltpu.VMEM((n_chains, HEAD_DIM, W), F32),
                        pltpu.VMEM((n_chains, C // 2, HEAD_DIM, W), F32)],
        compiler_params=_params("arbitrary", "arbitrary"), name="rwkv_scan",
    )(*operands, ones256)
    return [(outs[2 * s].reshape(B * T, W), outs[2 * s + 1].reshape(B * T, W))
            for s, (_, B, T, _) in enumerate(sets)]


def _rwkv_scan(groups, ones256):
    rows = lambda B: RW_BATCH_ROWS if B % RW_BATCH_ROWS == 0 else 1
    if len(groups) == 2:
        (_, B0, T0), (_, B1, T1) = groups
        if B1 == 1 and (B0 // rows(B0)) * (T0 // RW_CHUNK) == T1 // RW_CHUNK:
            return _rwkv_scan_call([groups[0] + (rows(B0),), groups[1] + (1,)], ones256)
    return [_rwkv_scan_call([g + (rows(g[1]),)], ones256)[0] for g in groups]


def _merge_kernel(x_ref, mod_ref, ya_ref, yb_ref, yf_ref, ybk_ref, rg_ref, bonus_ref, gn_ref, gates_ref,
                  ones_ref, wa_ref, wb_ref, wc_ref, wo_ref, o_ref):
    y = _head_layer_norm(yf_ref[...] + ybk_ref[...], ones_ref[...], RWKV_GN_EPS) * gn_ref[...]
    yc = ((y + bonus_ref[...]) * rg_ref[...]).astype(BF16)
    D = D_MODEL
    merged = (gates_ref[:, 0:D].astype(F32) * _dot(ya_ref[...], wa_ref[...])
              + gates_ref[:, D:2 * D].astype(F32) * _dot(yb_ref[...], wb_ref[...])
              + gates_ref[:, 2 * D:3 * D].astype(F32) * _dot(yc, wc_ref[...]))
    o_ref[...] = x_ref[...] + mod_ref[2:3, :] * _dot(merged.astype(BF16), wo_ref[...])


def _merge(x, mod, T, ya, yb, yf, ybk, rwp, gates, lw, ones256):
    N = x.shape[0]
    tm = min(512, T)
    tps = T // tm
    x_spec, mod_spec, _ = _token_specs(tm, tps)
    row = lambda w: pl.BlockSpec((tm, w), lambda i: (i, 0))
    field = lambda f: pl.BlockSpec((tm, RWKV_WIDTH), lambda i: (i, f))
    return pl.pallas_call(
        _merge_kernel, grid=(N // tm,),
        in_specs=[x_spec, mod_spec, row(RET_WIDTH), row(ATT_Q_WIDTH), row(RWKV_WIDTH), row(RWKV_WIDTH),
                  field(9), field(10), _full(lw["rw_gn"].shape), row(GATE_COLS), _full(ones256.shape),
                  _full(lw["w_branch_a"].shape), _full(lw["w_branch_b"].shape), _full(lw["w_branch_c"].shape),
                  _full(lw["w_out"].shape)],
        out_specs=x_spec, out_shape=jax.ShapeDtypeStruct((N, D_MODEL), F32),
        compiler_params=_params("parallel"), name="merge_out",
    )(x, mod, ya, yb, yf, ybk, rwp, rwp, lw["rw_gn"], gates, ones256,
      lw["w_branch_a"], lw["w_branch_b"], lw["w_branch_c"], lw["w_out"])


def _route(h, hb, rwt_ref, rbias_ref, tm):
    h_lo = (h - hb.astype(F32)).astype(BF16)
    rw_hi, rw_lo = _split(rwt_ref[...])
    logits = _dot_nt(rw_hi, hb) + _dot_nt(rw_hi, h_lo) + _dot_nt(rw_lo, hb)
    scores = _sigmoid(logits)
    choice = scores + jnp.concatenate([rbias_ref[...]] * (tm // V7X_LANES), axis=1)
    per_group = N_EXPERTS // N_GROUPS
    sub = lax.broadcasted_iota(jnp.int32, (per_group, tm), 0)
    groups, gscore = [], []
    for g in range(N_GROUPS):
        cg = choice[g * per_group:(g + 1) * per_group, :]
        m1 = jnp.max(cg, axis=0, keepdims=True)
        first = jnp.min(jnp.where(cg == m1, sub, per_group), axis=0, keepdims=True)
        m2 = jnp.max(jnp.where(sub == first, -jnp.inf, cg), axis=0, keepdims=True)
        groups.append(cg)
        gscore.append(m1 + m2)
    masked = []
    for g in range(N_GROUPS):
        beaten = jnp.zeros((1, tm), F32)
        for o in range(N_GROUPS):
            if o == g:
                continue
            wins = (gscore[o] >= gscore[g]) if o < g else (gscore[o] > gscore[g])
            beaten = beaten + jnp.where(wins, 1.0, 0.0)
        keep = jnp.where(beaten < TOPK_GROUPS, 1.0, 0.0)
        masked.append(jnp.where(jnp.broadcast_to(keep, (per_group, tm)) > 0.5, groups[g], -jnp.inf))
    mc = jnp.concatenate(masked, axis=0)
    eidx = lax.broadcasted_iota(jnp.int32, (N_EXPERTS, tm), 0).astype(F32)
    sel = jnp.zeros((N_EXPERTS, tm), F32)
    for _ in range(TOP_K):
        top = jnp.max(mc, axis=0, keepdims=True)
        first = jnp.min(jnp.where(mc == top, eidx, float(N_EXPERTS)), axis=0, keepdims=True)
        pick = eidx == first
        sel = jnp.where(pick, 1.0, sel)
        mc = jnp.where(pick, -jnp.inf, mc)
    wts = jnp.where(sel > 0.5, scores, 0.0)
    return wts / jnp.sum(wts, axis=0, keepdims=True) * ROUTED_SCALE, sel


def _selected_lists(comb, sel, rank):
    eidx = lax.broadcasted_iota(jnp.int32, comb.shape, 0).astype(F32)
    prev = jnp.full((1, comb.shape[1]), -1.0, F32)
    ids, wts, pos = [], [], []
    for _ in range(TOP_K):
        cand = jnp.where(sel > 0.5, jnp.where(eidx > prev, eidx, float(N_EXPERTS)), float(N_EXPERTS))
        prev = jnp.min(cand, axis=0, keepdims=True)
        wts.append(jnp.sum(jnp.where(eidx == prev, comb, 0.0), axis=0, keepdims=True))
        pos.append(jnp.sum(jnp.where(eidx == prev, rank, 0.0), axis=0, keepdims=True))
        ids.append(jnp.minimum(prev, N_EXPERTS - 1.0))
    as_int = lambda rows: jnp.concatenate(rows, axis=0).astype(jnp.int32)
    return as_int(ids), jnp.concatenate(wts, axis=0), as_int(pos)


def _pack_halves(y):
    n = y.shape[1] // 2
    hi = pltpu.bitcast(y[:, :n].astype(BF16).astype(F32), jnp.int32)
    lo = pltpu.bitcast(y[:, n:].astype(BF16).astype(F32), jnp.int32)
    return hi | lax.shift_right_logical(lo, 16)


def _unpack_halves(w):
    hi = pltpu.bitcast(w & jnp.int32(-65536), F32)
    lo = pltpu.bitcast(lax.shift_left(w, 16), F32)
    return hi, lo


def _moe_route_kernel(x_ref, mod_ref, g_ref, rwt_ref, rbias_ref, shgu_ref, shd_ref, tri_ref,
                      hp_ref, ids_ref, wts_ref, pos_ref, base_ref, count_ref, *, tm):
    F = EXPERT_FF

    @pl.when(pl.program_id(0) == 0)
    def _():
        count_ref[...] = jnp.zeros_like(count_ref)

    x = x_ref[...]
    h = _norm_mod(x, g_ref[...], mod_ref[3:4, :], mod_ref[4:5, :])
    hb = h.astype(BF16)
    hp_ref[...] = _pack_halves(hb.astype(F32))
    comb, sel = _route(h, hb, rwt_ref, rbias_ref, tm)
    before = count_ref[...]
    inclusive = _dot(sel.astype(BF16), tri_ref[...])
    rank = inclusive - sel + jnp.concatenate([before] * (tm // V7X_LANES), axis=1)
    count_ref[...] = before + jnp.sum(sel, axis=1, keepdims=True)
    ids, wts, pos = _selected_lists(comb, sel, rank)
    ids_ref[...] = ids
    wts_ref[...] = wts
    pos_ref[...] = pos
    gu = _dot(hb, shgu_ref[...])
    shared = _dot((_silu(gu[:, 0:F]) * gu[:, F:2 * F]).astype(BF16), shd_ref[...])
    base_ref[...] = x + mod_ref[5:6, :] * shared


def _moe_expert_kernel(tile_expert_ref, xs_ref, wgu_ref, wd_ref, o_ref):
    del tile_expert_ref
    F = EXPERT_FF
    half = D_MODEL // 2
    hi, lo = _unpack_halves(xs_ref[...])
    gu = _dot(hi.astype(BF16), wgu_ref[0:half, :]) + _dot(lo.astype(BF16), wgu_ref[half:, :])
    act = (_silu(gu[:, 0:F]) * gu[:, F:2 * F]).astype(BF16)
    o_ref[...] = _pack_halves(_dot(act, wd_ref[...]))


def _moe_combine_kernel(base_ref, mod_ref, y_ref, w_ref, fin_ref, o_ref, *, final):
    half = D_MODEL // 2
    acc_hi = jnp.zeros((base_ref.shape[0], half), F32)
    acc_lo = jnp.zeros((base_ref.shape[0], half), F32)
    for k in range(TOP_K):
        hi, lo = _unpack_halves(y_ref[k])
        wk = w_ref[:, k:k + 1]
        acc_hi = acc_hi + wk * hi
        acc_lo = acc_lo + wk * lo
    out_hi = base_ref[:, 0:half] + mod_ref[5:6, 0:half] * acc_hi
    out_lo = base_ref[:, half:] + mod_ref[5:6, half:] * acc_lo
    if final:
        ms = (jnp.sum(out_hi * out_hi, axis=-1, keepdims=True)
              + jnp.sum(out_lo * out_lo, axis=-1, keepdims=True)) * (1.0 / D_MODEL)
        scale = lax.rsqrt(ms + NORM_EPS)
        out_hi = out_hi * scale * fin_ref[:, 0:half]
        out_lo = out_lo * scale * fin_ref[:, half:]
    o_ref[:, 0:half] = out_hi
    o_ref[:, half:] = out_lo


def _gather_rows(table, idx):
    rows, width = idx.shape[0], table.shape[1]
    info = plsc.get_sparse_core_info()
    workers = info.num_cores * info.num_subcores
    windows = rows // (workers * SC_GATHER_ROWS)
    assert windows * workers * SC_GATHER_ROWS == rows
    mesh = plsc.VectorSubcoreMesh(core_axis_name="core", subcore_axis_name="subcore")

    @functools.partial(pl.kernel, out_type=jax.ShapeDtypeStruct((rows, width), table.dtype), mesh=mesh,
                       scratch_types=[pltpu.VMEM((SC_GATHER_ROWS,), jnp.int32),
                                      pltpu.VMEM((SC_GATHER_ROWS, width), table.dtype)],
                       name="moe_gather")
    def gather(table_hbm, idx_hbm, out_hbm, idx_vmem, rows_vmem):
        worker = lax.axis_index("subcore") * info.num_cores + lax.axis_index("core")
        base = worker * (windows * SC_GATHER_ROWS)

        @pl.loop(0, windows)
        def _(j):
            span = pl.ds(pl.multiple_of(base + j * SC_GATHER_ROWS, SC_GATHER_ROWS), SC_GATHER_ROWS)
            pltpu.sync_copy(idx_hbm.at[span], idx_vmem)
            pltpu.sync_copy(table_hbm.at[idx_vmem], rows_vmem)
            pltpu.sync_copy(rows_vmem, out_hbm.at[span])

    return gather(table, idx)


def _scatter_rows(table, dest, total_rows):
    n_lists, n_rows = dest.shape
    width = table.shape[1]
    info = plsc.get_sparse_core_info()
    workers = info.num_cores * info.num_subcores
    windows = n_rows // (workers * SC_GATHER_ROWS)
    assert windows * workers * SC_GATHER_ROWS == n_rows
    windows_per_list = n_rows // SC_GATHER_ROWS
    mesh = plsc.VectorSubcoreMesh(core_axis_name="core", subcore_axis_name="subcore")

    @functools.partial(pl.kernel, out_type=jax.ShapeDtypeStruct((total_rows, width), table.dtype), mesh=mesh,
                       scratch_types=[pltpu.VMEM((1, SC_GATHER_ROWS), jnp.int32),
                                      pltpu.VMEM((SC_GATHER_ROWS, width), table.dtype)],
                       name="moe_scatter")
    def scatter(table_hbm, dest_hbm, out_hbm, idx_vmem, rows_vmem):
        worker = lax.axis_index("subcore") * info.num_cores + lax.axis_index("core")

        @pl.loop(0, windows)
        def _(j):
            window = worker * windows + j
            span = pl.ds(pl.multiple_of(window * SC_GATHER_ROWS, SC_GATHER_ROWS), SC_GATHER_ROWS)
            pltpu.sync_copy(table_hbm.at[span], rows_vmem)
            for k in range(n_lists):
                pltpu.sync_copy(dest_hbm.at[pl.ds(k * windows_per_list + window, 1)], idx_vmem)
                pltpu.sync_copy(rows_vmem, out_hbm.at[idx_vmem.at[0]])

    return scatter(table, dest.reshape(n_lists * windows_per_list, SC_GATHER_ROWS))


def _dispatch_plan(ids, pos, counts, n_tokens, tg):
    E = N_EXPERTS
    padded = (counts + tg - 1) // tg * tg
    ends = jnp.cumsum(padded)
    off = ends - padded
    onehot = ids[:, :, None] == jnp.arange(E, dtype=jnp.int32)[None, None, :]
    dest = pos + jnp.sum(jnp.where(onehot, off[None, None, :], 0), axis=-1)
    total = n_tokens * TOP_K + E * tg
    tile_start = jnp.arange(total // tg, dtype=jnp.int32) * tg
    tile_expert = jnp.sum((ends[None, :] <= tile_start[:, None]).astype(jnp.int32), axis=1)
    return dest, total, jnp.minimum(tile_expert, E - 1)


def _moe(x, mod, T, lw, final_g, final):
    N = x.shape[0]
    tm = min(MOE_TOKENS, T)
    tps = T // tm
    D, F, E = D_MODEL, EXPERT_FF, N_EXPERTS
    tg = max(MOE_GROUP_ROWS_MIN, min(MOE_GROUP_ROWS_MAX, N * TOP_K // (E * 16)))
    half = D // 2
    x_spec, mod_spec, g_spec = _token_specs(tm, tps)
    list_spec = pl.BlockSpec((TOP_K, tm), lambda i: (0, i))

    tri = jnp.asarray(np.triu(np.ones((tm, tm), np.float32)), BF16)
    hp, ids, wts, pos, base, counts = pl.pallas_call(
        functools.partial(_moe_route_kernel, tm=tm), grid=(N // tm,),
        in_specs=[x_spec, mod_spec, g_spec, _full((E, D)), _full((E, V7X_LANES)), _full((D, 2 * F)), _full((F, D)),
                  _full((tm, tm))],
        out_specs=[pl.BlockSpec((tm, half), lambda i: (i, 0)), list_spec, list_spec, list_spec, x_spec,
                   _full((E, V7X_LANES))],
        out_shape=[jax.ShapeDtypeStruct((N, half), jnp.int32), jax.ShapeDtypeStruct((TOP_K, N), jnp.int32),
                   jax.ShapeDtypeStruct((TOP_K, N), F32), jax.ShapeDtypeStruct((TOP_K, N), jnp.int32),
                   jax.ShapeDtypeStruct((N, D), F32), jax.ShapeDtypeStruct((E, V7X_LANES), F32)],
        compiler_params=_params("arbitrary"), name="moe_route",
    )(x, mod, lw["norm2_g"], lw["router_wt"], lw["router_bias"], lw["sh_gu"], lw["sh_d"], tri)

    dest, total, tile_expert = _dispatch_plan(ids, pos, counts[:, 0].astype(jnp.int32), N, tg)
    xs = _scatter_rows(hp, dest, total)
    n_tiles = total // tg
    ys = pl.pallas_call(
        _moe_expert_kernel,
        grid_spec=pltpu.PrefetchScalarGridSpec(
            num_scalar_prefetch=1, grid=(n_tiles,),
            in_specs=[pl.BlockSpec((tg, half), lambda i, te: (i, 0)),
                      pl.BlockSpec((None, D, 2 * F), lambda i, te: (te[i], 0, 0)),
                      pl.BlockSpec((None, F, D), lambda i, te: (te[i], 0, 0))],
            out_specs=pl.BlockSpec((tg, half), lambda i, te: (i, 0))),
        out_shape=jax.ShapeDtypeStruct(xs.shape, jnp.int32),
        compiler_params=_params("parallel"), name="moe_experts",
    )(tile_expert, xs, lw["exp_gu"], lw["exp_d"])

    yk = _gather_rows(ys, dest.reshape(-1)).reshape(TOP_K, N, half)
    return pl.pallas_call(
        functools.partial(_moe_combine_kernel, final=final), grid=(N // tm,),
        in_specs=[x_spec, mod_spec, pl.BlockSpec((TOP_K, tm, half), lambda i: (0, i, 0)),
                  pl.BlockSpec((tm, TOP_K), lambda i: (i, 0)), g_spec],
        out_specs=x_spec, out_shape=jax.ShapeDtypeStruct((N, D), F32),
        compiler_params=_params("parallel"), name="moe_combine",
    )(base, mod, yk, wts.T, final_g)


def _swap_halves(w, heads):
    shp = w.shape
    w = w.reshape(shp[:-1] + (heads, 2, HEAD_DIM // 2))
    return jnp.flip(w, axis=-2).reshape(shp)


def _block_diag2(a, b):
    za = jnp.zeros((a.shape[0], b.shape[1]), a.dtype)
    zb = jnp.zeros((b.shape[0], a.shape[1]), a.dtype)
    return jnp.concatenate([jnp.concatenate([a, za], 1), jnp.concatenate([zb, b], 1)], 0)


def _layer_weights(P, l):
    w_in = P["w_in"][l]
    o1, o2, o3 = RET_COLS, RET_COLS + ATT_COLS, RET_COLS + ATT_COLS + RW_COLS
    w_ret, w_att, w_rw, w_gate = w_in[:, :o1], w_in[:, o1:o2], w_in[:, o2:o3], w_in[:, o3:]
    W = RET_WIDTH
    w_ret = jnp.concatenate([w_ret, _swap_halves(w_ret[:, 0:W], RET_HEADS),
                             _swap_halves(w_ret[:, W:2 * W], RET_HEADS)], axis=1)
    QW, KW = ATT_Q_WIDTH, ATT_KV_WIDTH
    w_att = jnp.concatenate([w_att, _swap_halves(w_att[:, 0:QW], ATT_Q_HEADS),
                             _swap_halves(w_att[:, QW:QW + KW], ATT_KV_HEADS)], axis=1)
    qg, kg = P["q_norm_g"][l], P["k_norm_g"][l]
    pad = lambda v: jnp.zeros((QW,), F32).at[:v.shape[0]].set(v)
    qk_gain = jnp.zeros((8, QW), F32)
    qk_gain = qk_gain.at[0].set(jnp.tile(qg, ATT_Q_HEADS))
    qk_gain = qk_gain.at[1].set(jnp.tile(_swap_halves(qg, 1), ATT_Q_HEADS))
    qk_gain = qk_gain.at[2].set(pad(jnp.tile(kg, ATT_KV_HEADS)))
    qk_gain = qk_gain.at[3].set(pad(jnp.tile(_swap_halves(kg, 1), ATT_KV_HEADS)))
    rw_vec = jnp.zeros((8, RWKV_WIDTH), F32)
    for j, name in enumerate(("rw_w0_f", "rw_w0_b", "rw_a0_f", "rw_a0_b", "rw_k_k", "rw_k_a")):
        rw_vec = rw_vec.at[j].set(P[name][l])
    rw_vec = rw_vec.at[6].set(P["rw_r_k"][l].reshape(RWKV_WIDTH))
    row = lambda v: v.reshape(1, -1)
    return {
        "norm1_g": row(P["norm1_g"][l]), "norm2_g": row(P["norm2_g"][l]),
        "w_ret": w_ret.astype(BF16), "w_att": w_att.astype(BF16), "w_rw": w_rw.astype(BF16),
        "w_gate": w_gate.astype(BF16), "qk_gain": qk_gain, "ret_gn": row(P["ret_gn"][l]),
        "shift_mu": P["shift_mu"][l], "rw_vec": rw_vec,
        "rw_w2": _block_diag2(P["rw_w2_f"][l], P["rw_w2_b"][l]).astype(BF16),
        "rw_a2": _block_diag2(P["rw_a2_f"][l], P["rw_a2_b"][l]).astype(BF16),
        "rw_g2": P["rw_g2"][l].astype(BF16), "rw_gn": row(P["rw_gn"][l]),
        "w_branch_a": P["w_branch_a"][l].astype(BF16), "w_branch_b": P["w_branch_b"][l].astype(BF16),
        "w_branch_c": P["w_branch_c"][l].astype(BF16), "w_out": P["w_out"][l].astype(BF16),
        "router_wt": P["router_w"][l].T,
        "router_bias": jnp.broadcast_to(P["router_bias"][l][:, None], (N_EXPERTS, V7X_LANES)),
        "sh_gu": jnp.concatenate([P["sh_w_gate"][l], P["sh_w_up"][l]], axis=1).astype(BF16),
        "sh_d": P["sh_w_down"][l].astype(BF16),
        "exp_gu": jnp.concatenate([P["exp_w_gate"][l], P["exp_w_up"][l]], axis=2).astype(BF16),
        "exp_d": P["exp_w_down"][l].astype(BF16),
    }


def _ones_block_diag(n):
    idx = np.arange(n) // HEAD_DIM
    return jnp.asarray(idx[:, None] == idx[None, :], BF16)


def _tables(T):
    rows = T // GRID_W
    row = jnp.repeat(jnp.arange(rows, dtype=F32), GRID_W)
    col = jnp.tile(jnp.arange(GRID_W, dtype=F32), rows)
    freqs = ROPE_THETA ** (-jnp.arange(ROPE_PAIRS_PER_AXIS, dtype=F32) / ROPE_PAIRS_PER_AXIS)
    ang = jnp.concatenate([row[:, None] * freqs, col[:, None] * freqs], axis=-1)
    cos, sin = jnp.cos(ang), jnp.sin(ang)
    return {
        "cos": jnp.tile(jnp.concatenate([cos, cos], axis=-1), (1, ATT_Q_HEADS)),
        "sin": jnp.tile(jnp.concatenate([-sin, sin], axis=-1), (1, ATT_Q_HEADS)),
        "ones128": _ones_block_diag(128), "ones256": _ones_block_diag(256), "ones512": _ones_block_diag(512),
    }


def _trunks(requests, P, layer_weights):
    groups = []
    for x, c in requests:
        B, T, D = x.shape
        groups.append({"B": B, "T": T, "x": x.reshape(B * T, D), "tabs": _tables(T),
                       "mods": _ada_mod(c, P["ada_w"], P["ada_b"])})
    final_g = P["final_g"].reshape(1, D_MODEL)
    depth = len(layer_weights)
    for l, lw in enumerate(layer_weights):
        mixed = []
        for g in groups:
            B, T, tabs, mod = g["B"], g["T"], g["tabs"], g["mods"][l]
            ret, q, k, v, gates, rwp = _input_projections(g["x"], mod, T, lw, tabs)
            ya = _retention(ret, B, T, lw["ret_gn"], tabs["ones256"])
            yb = _attention(q, k, v, B, T)
            mixed.append((mod, ya, yb, gates, rwp))
        scans = _rwkv_scan([(m[4], g["B"], g["T"]) for m, g in zip(mixed, groups)], groups[0]["tabs"]["ones256"])
        for g, (mod, ya, yb, gates, rwp), (yf, ybk) in zip(groups, mixed, scans):
            x = _merge(g["x"], mod, g["T"], ya, yb, yf, ybk, rwp, gates, lw, g["tabs"]["ones256"])
            g["x"] = _moe(x, mod, g["T"], lw, final_g, final=(l == depth - 1))
    return [g["x"].reshape(g["B"], g["T"], D_MODEL) for g in groups]


def kernel(x_prompt, x_sample, c_prompt, c_sample, norm1_g, norm2_g, final_g, ada_w, ada_b, w_in, ret_gn, q_norm_g, k_norm_g, shift_mu, rw_w0_f, rw_w2_f, rw_w0_b, rw_w2_b, rw_a0_f, rw_a2_f, rw_a0_b, rw_a2_b, rw_g2, rw_k_k, rw_k_a, rw_r_k, rw_gn, w_branch_a, w_branch_b, w_branch_c, w_out, router_w, router_bias, exp_w_gate, exp_w_up, exp_w_down, sh_w_gate, sh_w_up, sh_w_down):
    P = {
        "norm1_g": norm1_g, "norm2_g": norm2_g, "final_g": final_g, "ada_w": ada_w, "ada_b": ada_b,
        "w_in": w_in, "ret_gn": ret_gn, "q_norm_g": q_norm_g, "k_norm_g": k_norm_g, "shift_mu": shift_mu,
        "rw_w0_f": rw_w0_f, "rw_w2_f": rw_w2_f, "rw_w0_b": rw_w0_b, "rw_w2_b": rw_w2_b,
        "rw_a0_f": rw_a0_f, "rw_a2_f": rw_a2_f, "rw_a0_b": rw_a0_b, "rw_a2_b": rw_a2_b,
        "rw_g2": rw_g2, "rw_k_k": rw_k_k, "rw_k_a": rw_k_a, "rw_r_k": rw_r_k, "rw_gn": rw_gn,
        "w_branch_a": w_branch_a, "w_branch_b": w_branch_b, "w_branch_c": w_branch_c, "w_out": w_out,
        "router_w": router_w, "router_bias": router_bias,
        "exp_w_gate": exp_w_gate, "exp_w_up": exp_w_up, "exp_w_down": exp_w_down,
        "sh_w_gate": sh_w_gate, "sh_w_up": sh_w_up, "sh_w_down": sh_w_down,
    }
    layer_weights = [_layer_weights(P, l) for l in range(w_in.shape[0])]
    y_prompt, y_sample = _trunks([(x_prompt, c_prompt), (x_sample, c_sample)], P, layer_weights)
    return (y_prompt, y_sample)
```

```python
import functools

import numpy as np
import jax
import jax.numpy as jnp
from jax import lax
from jax.experimental import pallas as pl
from jax.experimental.pallas import tpu as pltpu
from jax.experimental.pallas import tpu_sc as plsc

F32 = jnp.float32
BF16 = jnp.bfloat16

D_MODEL = 1024
DEPTH = 2
GRID_W = 64
HEAD_DIM = 64
ROPE_THETA = 10000.0
ROPE_PAIRS_PER_AXIS = HEAD_DIM // 4

RET_HEADS = 4
RET_WIDTH = RET_HEADS * HEAD_DIM
RET_CHUNK = 128
RET_CHUNKS_PER_STEP = 4
RET_EPS = 1e-5

ATT_Q_HEADS = 8
ATT_KV_HEADS = 2
ATT_GROUP = ATT_Q_HEADS // ATT_KV_HEADS
ATT_Q_WIDTH = ATT_Q_HEADS * HEAD_DIM
ATT_KV_WIDTH = ATT_KV_HEADS * HEAD_DIM

RWKV_HEADS = 4
RWKV_WIDTH = RWKV_HEADS * HEAD_DIM
DECAY_LORA = 64
AAA_LORA = 64
GATE_LORA = 128
RWKV_GN_EPS = 64e-5
RW_CHUNK = 64
RW_FIELDS = 11
RW_BATCH_ROWS = 4

FLASH_Q_ROWS = 512
FLASH_KV_ROWS = 512
FLASH_SUB_ROWS = 1024

MOE_TOKENS = 512
MOE_GROUP_ROWS_MIN = 256
MOE_GROUP_ROWS_MAX = 512
SC_GATHER_ROWS = 128

N_EXPERTS = 64
TOP_K = 8
N_GROUPS = 8
TOPK_GROUPS = 4
EXPERT_FF = 256
ROUTED_SCALE = 2.5
NORM_EPS = 1e-6

RET_COLS = 4 * RET_WIDTH
ATT_COLS = ATT_Q_WIDTH + 2 * ATT_KV_WIDTH
RW_COLS = 3 * RWKV_WIDTH + 2 * DECAY_LORA + 2 * AAA_LORA + GATE_LORA
GATE_COLS = 3 * D_MODEL

V7X_LANES = 128
VMEM_LIMIT_BYTES = 48 * 1024 * 1024


def _params(*dims):
    return pltpu.CompilerParams(dimension_semantics=dims, vmem_limit_bytes=VMEM_LIMIT_BYTES)


def _sigmoid(x):
    return 1.0 / (1.0 + jnp.exp(-x))


def _silu(x):
    return x * _sigmoid(x)


def _dot(a, b):
    return jnp.dot(a, b, preferred_element_type=F32)


def _dot_nt(a, b):
    return lax.dot_general(a, b, (((1,), (1,)), ((), ())), preferred_element_type=F32)


def _split(a):
    hi = a.astype(BF16)
    lo = (a - hi.astype(F32)).astype(BF16)
    return hi, lo


def _dot_split_lhs(a, b_bf16):
    hi, lo = _split(a)
    return _dot(hi, b_bf16) + _dot(lo, b_bf16)


def _dot3(a, b):
    ah, al = _split(a)
    bh, bl = _split(b)
    return _dot(ah, bh) + _dot(ah, bl) + _dot(al, bh)


def _swap_rotary_halves(x):
    half = HEAD_DIM // 2
    lane = lax.broadcasted_iota(jnp.int32, x.shape, 1) & (HEAD_DIM - 1)
    return jnp.where(lane < half, pltpu.roll(x, x.shape[1] - half, 1), pltpu.roll(x, half, 1))


def _norm_mod(x, gain, shift, scale):
    ms = jnp.mean(x * x, axis=-1, keepdims=True)
    return x * lax.rsqrt(ms + NORM_EPS) * gain * (1.0 + scale) + shift


def _head_layer_norm(y, ones_bd, eps):
    mean = _dot_split_lhs(y, ones_bd) * (1.0 / HEAD_DIM)
    yc = y - mean
    var = _dot_split_lhs(yc * yc, ones_bd) * (1.0 / HEAD_DIM)
    return yc * lax.rsqrt(var + eps)


def _ada_kernel(c_ref, w_ref, b_ref, o_ref):
    c = c_ref[...]
    o_ref[...] = _dot3(_silu(c), w_ref[...]) + b_ref[...]


def _ada_mod(c, ada_w, ada_b):
    B = c.shape[0]
    Bp = max(8, B)
    cp = jnp.zeros((Bp, D_MODEL), F32).at[:B].set(c)
    L = ada_w.shape[0]
    tn = 1536
    out = pl.pallas_call(
        _ada_kernel,
        grid=(L, 6 * D_MODEL // tn),
        in_specs=[pl.BlockSpec((Bp, D_MODEL), lambda l, j: (0, 0)),
                  pl.BlockSpec((None, D_MODEL, tn), lambda l, j: (l, 0, j)),
                  pl.BlockSpec((None, 1, tn), lambda l, j: (l, 0, j))],
        out_specs=pl.BlockSpec((None, Bp, tn), lambda l, j: (l, 0, j)),
        out_shape=jax.ShapeDtypeStruct((L, Bp, 6 * D_MODEL), F32),
        compiler_params=_params("parallel", "parallel"),
        name="ada_mod",
    )(cp, ada_w, ada_b.reshape(L, 1, 6 * D_MODEL))
    return out[:, :B].reshape(L, B, 6, D_MODEL)


def _proj_ret_kernel(x_ref, mod_ref, g_ref, w_ref, cos_ref, sin_ref, o_ref):
    h = _norm_mod(x_ref[...], g_ref[...], mod_ref[0:1, :], mod_ref[1:2, :]).astype(BF16)
    p = _dot(h, w_ref[...])
    W = RET_WIDTH
    cos = cos_ref[:, 0:W]
    sin = sin_ref[:, 0:W]
    q, k = p[:, 0:W], p[:, W:2 * W]
    o_ref[:, 0:W] = (q * cos + _swap_rotary_halves(q) * sin).astype(BF16)
    o_ref[:, W:2 * W] = ((k * cos + _swap_rotary_halves(k) * sin) * (HEAD_DIM ** -0.5)).astype(BF16)
    o_ref[:, 2 * W:3 * W] = p[:, 2 * W:3 * W].astype(BF16)
    o_ref[:, 3 * W:4 * W] = _silu(p[:, 3 * W:4 * W]).astype(BF16)


def _proj_att_kernel(x_ref, mod_ref, g_ref, w_ref, cos_ref, sin_ref, qkg_ref, ones_ref, q_ref, k_ref, v_ref):
    h = _norm_mod(x_ref[...], g_ref[...], mod_ref[0:1, :], mod_ref[1:2, :]).astype(BF16)
    p = _dot(h, w_ref[...])
    QW, KW = ATT_Q_WIDTH, ATT_KV_WIDTH
    q = p[:, 0:QW]
    k = p[:, QW:QW + KW]
    v = p[:, QW + KW:QW + 2 * KW]
    ones = ones_ref[...]
    rq = lax.rsqrt(_dot_split_lhs(q * q, ones) * (1.0 / HEAD_DIM) + NORM_EPS)
    rk = lax.rsqrt(_dot_split_lhs(k * k, ones[0:KW, 0:KW]) * (1.0 / HEAD_DIM) + NORM_EPS)
    cos = cos_ref[...]
    sin = sin_ref[...]
    qg = q * qkg_ref[0:1, :]
    kg = k * qkg_ref[1:2, 0:KW]
    qo = (qg * cos + _swap_rotary_halves(qg) * sin) * (rq * (HEAD_DIM ** -0.5))
    ko = (kg * cos[:, 0:KW] + _swap_rotary_halves(kg) * sin[:, 0:KW]) * rk
    q_ref[...] = qo.astype(BF16)
    kb = ko.astype(BF16)
    vb = v.astype(BF16)
    for hk in range(ATT_KV_HEADS):
        k_ref[hk] = kb[:, hk * HEAD_DIM:(hk + 1) * HEAD_DIM]
        v_ref[hk, :, 0:HEAD_DIM] = vb[:, hk * HEAD_DIM:(hk + 1) * HEAD_DIM]
        v_ref[hk, :, HEAD_DIM:2 * HEAD_DIM] = jnp.ones((vb.shape[0], HEAD_DIM), BF16)


def _proj_gate_kernel(x_ref, mod_ref, g_ref, w_ref, o_ref):
    h = _norm_mod(x_ref[...], g_ref[...], mod_ref[0:1, :], mod_ref[1:2, :]).astype(BF16)
    o_ref[...] = _sigmoid(_dot(h, w_ref[...])).astype(BF16)


def _proj_rw_kernel(x_ref, xp_ref, xn_ref, mod_ref, g_ref, w_ref, mu_ref, vec_ref, w2_ref, a2_ref, g2_ref,
                    ones_ref, o_ref, p_scr, *, tm, tiles_per_seq):
    i = pl.program_id(0)
    gain, shift, scale = g_ref[...], mod_ref[0:1, :], mod_ref[1:2, :]
    w = w_ref[...]
    h = _norm_mod(x_ref[...], gain, shift, scale).astype(BF16)
    p = _dot(h, w)
    hp = _norm_mod(xp_ref[...], gain, shift, scale).astype(BF16)
    hn = _norm_mod(xn_ref[...], gain, shift, scale).astype(BF16)
    first = (i % tiles_per_seq) == 0
    last = (i % tiles_per_seq) == tiles_per_seq - 1
    prev_row = jnp.where(first, 0.0, _dot(hp, w)[7:8, :])
    next_row = jnp.where(last, 0.0, _dot(hn, w)[0:1, :])
    p_scr[8:8 + tm, :] = p
    p_scr[7:8, :] = prev_row
    p_scr[8 + tm:9 + tm, :] = next_row
    prev = p_scr[7:7 + tm, :]
    nxt = p_scr[9:9 + tm, :]
    x = p + mu_ref[0:1, :] * (prev - p) + mu_ref[1:2, :] * (nxt - p)

    W = RWKV_WIDTH
    r, k, v = x[:, 0:W], x[:, W:2 * W], x[:, 2 * W:3 * W]
    xw = x[:, 3 * W:3 * W + 2 * DECAY_LORA]
    xa = x[:, 3 * W + 2 * DECAY_LORA:3 * W + 2 * DECAY_LORA + 2 * AAA_LORA]
    xg = x[:, 3 * W + 2 * DECAY_LORA + 2 * AAA_LORA:]
    w0_f, w0_b = vec_ref[0:1, :], vec_ref[1:2, :]
    a0_f, a0_b = vec_ref[2:3, :], vec_ref[3:4, :]
    k_k, k_a, r_k = vec_ref[4:5, :], vec_ref[5:6, :], vec_ref[6:7, :]
    wl = _dot(jnp.tanh(xw).astype(BF16), w2_ref[...])
    al = _dot(xa.astype(BF16), a2_ref[...])
    dec_c = float(np.exp(-0.5))
    w_f = jnp.exp(-dec_c * _sigmoid(w0_f + wl[:, 0:W]))
    w_b = jnp.exp(-dec_c * _sigmoid(w0_b + wl[:, W:2 * W]))
    a_f = _sigmoid(a0_f + al[:, 0:W])
    a_b = _sigmoid(a0_b + al[:, W:2 * W])
    gate = _dot(_sigmoid(xg).astype(BF16), g2_ref[...])
    ones = ones_ref[...]
    kk = k * k_k
    kk = kk * lax.rsqrt(_dot_split_lhs(kk * kk, ones) + 1e-12)
    k_f = k * (1.0 + (a_f - 1.0) * k_a)
    k_b = k * (1.0 + (a_b - 1.0) * k_a)
    bonus = _dot_split_lhs(r * k_f * r_k, ones) * v
    for j, val in enumerate((r, kk, v, w_f, w_b, k_f, k_b, kk * a_f, kk * a_b, gate, bonus)):
        o_ref[:, j * W:(j + 1) * W] = val


def _token_specs(tm, tiles_per_seq):
    x_spec = pl.BlockSpec((tm, D_MODEL), lambda i: (i, 0))
    mod_spec = pl.BlockSpec((None, 6, D_MODEL), lambda i: (i // tiles_per_seq, 0, 0))
    g_spec = pl.BlockSpec((1, D_MODEL), lambda i: (0, 0))
    return x_spec, mod_spec, g_spec


def _full(shape):
    nd = len(shape)
    return pl.BlockSpec(shape, lambda *_: (0,) * nd)


def _input_projections(x, mod, T, lw, tabs):
    N = x.shape[0]
    tm = min(512, T)
    tps = T // tm
    grid = (N // tm,)
    x_spec, mod_spec, g_spec = _token_specs(tm, tps)
    tab_spec = pl.BlockSpec((tm, ATT_Q_WIDTH), lambda i: (i % tps, 0))
    cos, sin = tabs["cos"], tabs["sin"]

    ret = pl.pallas_call(
        _proj_ret_kernel, grid=grid,
        in_specs=[x_spec, mod_spec, g_spec, _full(lw["w_ret"].shape), tab_spec, tab_spec],
        out_specs=pl.BlockSpec((tm, RET_COLS), lambda i: (i, 0)),
        out_shape=jax.ShapeDtypeStruct((N, RET_COLS), BF16),
        compiler_params=_params("parallel"), name="proj_ret",
    )(x, mod, lw["norm1_g"], lw["w_ret"], cos, sin)

    q, k, v = pl.pallas_call(
        _proj_att_kernel, grid=grid,
        in_specs=[x_spec, mod_spec, g_spec, _full(lw["w_att"].shape), tab_spec, tab_spec,
                  _full(lw["qk_gain"].shape), _full(tabs["ones512"].shape)],
        out_specs=[pl.BlockSpec((tm, ATT_Q_WIDTH), lambda i: (i, 0)),
                   pl.BlockSpec((ATT_KV_HEADS, tm, HEAD_DIM), lambda i: (0, i, 0)),
                   pl.BlockSpec((ATT_KV_HEADS, tm, 2 * HEAD_DIM), lambda i: (0, i, 0))],
        out_shape=[jax.ShapeDtypeStruct((N, ATT_Q_WIDTH), BF16),
                   jax.ShapeDtypeStruct((ATT_KV_HEADS, N, HEAD_DIM), BF16),
                   jax.ShapeDtypeStruct((ATT_KV_HEADS, N, 2 * HEAD_DIM), BF16)],
        compiler_params=_params("parallel"), name="proj_att",
    )(x, mod, lw["norm1_g"], lw["w_att"], cos, sin, lw["qk_gain"], tabs["ones512"])

    gates = pl.pallas_call(
        _proj_gate_kernel, grid=grid,
        in_specs=[x_spec, mod_spec, g_spec, _full(lw["w_gate"].shape)],
        out_specs=pl.BlockSpec((tm, GATE_COLS), lambda i: (i, 0)),
        out_shape=jax.ShapeDtypeStruct((N, GATE_COLS), BF16),
        compiler_params=_params("parallel"), name="proj_gate",
    )(x, mod, lw["norm1_g"], lw["w_gate"])

    tm_rw = min(256, T)
    tps_rw = T // tm_rw
    x_spec, mod_spec, g_spec = _token_specs(tm_rw, tps_rw)
    rows8 = tm_rw // 8
    last8 = N // 8 - 1
    rwp = pl.pallas_call(
        functools.partial(_proj_rw_kernel, tm=tm_rw, tiles_per_seq=tps_rw), grid=(N // tm_rw,),
        in_specs=[x_spec,
                  pl.BlockSpec((8, D_MODEL), lambda i: (jnp.maximum(i * rows8 - 1, 0), 0)),
                  pl.BlockSpec((8, D_MODEL), lambda i: (jnp.minimum((i + 1) * rows8, last8), 0)),
                  mod_spec, g_spec, _full(lw["w_rw"].shape), _full(lw["shift_mu"].shape),
                  _full(lw["rw_vec"].shape), _full(lw["rw_w2"].shape), _full(lw["rw_a2"].shape),
                  _full(lw["rw_g2"].shape), _full(tabs["ones256"].shape)],
        out_specs=pl.BlockSpec((tm_rw, RW_FIELDS * RWKV_WIDTH), lambda i: (i, 0)),
        out_shape=jax.ShapeDtypeStruct((N, RW_FIELDS * RWKV_WIDTH), F32),
        scratch_shapes=[pltpu.VMEM((tm_rw + 16, RW_COLS), F32)],
        compiler_params=_params("parallel"), name="proj_rw",
    )(x, x, x, mod, lw["norm1_g"], lw["w_rw"], lw["shift_mu"], lw["rw_vec"], lw["rw_w2"], lw["rw_a2"],
      lw["rw_g2"], tabs["ones256"])
    return ret, q, k, v, gates, rwp


def _retention_tables(reverse):
    lg = np.log1p(-np.exp2(-5.0 - np.arange(RET_HEADS, dtype=np.float64)))
    if reverse:
        lg = lg[::-1]
    C = RET_CHUNK
    pos = np.arange(C, dtype=np.float64)
    diff = pos[:, None] - pos[None, :]
    if reverse:
        dec = np.where(diff < 0, np.exp(lg[:, None, None] * np.maximum(-diff, 0.0)[None]), 0.0)
        xi = np.exp(lg[:, None] * (C - pos)[None, :])
        zeta = np.exp(lg[:, None] * pos[None, :])
    else:
        dec = np.where(diff >= 0, np.exp(lg[:, None, None] * np.maximum(diff, 0.0)[None]), 0.0)
        xi = np.exp(lg[:, None] * (pos + 1.0)[None, :])
        zeta = np.exp(lg[:, None] * (C - 1.0 - pos)[None, :])
    chunk_decay = tuple(float(v) for v in np.exp(lg * C))
    widen = lambda a: jnp.asarray(np.repeat(a.T, HEAD_DIM, axis=1), F32)
    return jnp.asarray(dec, F32), widen(xi), widen(zeta), chunk_decay


def _retention_kernel(ret_ref, dec_ref, xi_ref, zeta_ref, *rest, chunk_decay, final, chunks):
    if final:
        yf_ref, gn_ref, ones_ref, o_ref, state, y_scr = rest
    else:
        o_ref, state, y_scr = rest

    @pl.when(pl.program_id(1) == 0)
    def _():
        state[...] = jnp.zeros_like(state)

    W, C = RET_WIDTH, RET_CHUNK
    for step in range(chunks):
        cb = chunks - 1 - step if final else step
        rows = slice(cb * C, (cb + 1) * C)
        q = ret_ref[rows, 0:W]
        k = ret_ref[rows, W:2 * W]
        v = ret_ref[rows, 2 * W:3 * W]
        qx = (q.astype(F32) * xi_ref[...]).astype(BF16)
        kzt = (k.astype(F32) * zeta_ref[...]).T
        for h in range(RET_HEADS):
            sl = slice(h * HEAD_DIM, (h + 1) * HEAD_DIM)
            s = _dot_nt(q[:, sl], k[:, sl]) * dec_ref[h]
            inner = _dot(s.astype(BF16), v[:, sl])
            r_prev = state[h]
            cross = _dot(qx[:, sl], r_prev.astype(BF16))
            kv = _dot(kzt[sl, :].astype(BF16), v[:, sl])
            state[h] = r_prev * chunk_decay[h] + kv
            y_scr[rows, sl] = inner + cross
    y = y_scr[...]
    if final:
        y = _head_layer_norm(y + yf_ref[...], ones_ref[...], RET_EPS) * gn_ref[...]
        o_ref[...] = (y * ret_ref[:, 3 * W:4 * W].astype(F32)).astype(BF16)
    else:
        o_ref[...] = y


def _retention(ret, B, T, ret_gn, ones256):
    N = ret.shape[0]
    chunks = min(RET_CHUNKS_PER_STEP, T // RET_CHUNK)
    R = chunks * RET_CHUNK
    nblk = T // R
    scratch = [pltpu.VMEM((RET_HEADS, HEAD_DIM, HEAD_DIM), F32), pltpu.VMEM((R, RET_WIDTH), F32)]

    dec, xi, zeta, cd = _retention_tables(False)
    fwd_row = lambda b, c: (b * nblk + c, 0)
    yf = pl.pallas_call(
        functools.partial(_retention_kernel, chunk_decay=cd, final=False, chunks=chunks), grid=(B, nblk),
        in_specs=[pl.BlockSpec((R, RET_COLS), fwd_row), _full(dec.shape), _full(xi.shape), _full(zeta.shape)],
        out_specs=pl.BlockSpec((R, RET_WIDTH), fwd_row),
        out_shape=jax.ShapeDtypeStruct((N, RET_WIDTH), F32),
        scratch_shapes=scratch, compiler_params=_params("parallel", "arbitrary"), name="retention_fwd",
    )(ret, dec, xi, zeta)

    dec, xi, zeta, cd = _retention_tables(True)
    bwd_row = lambda b, c: (b * nblk + nblk - 1 - c, 0)
    return pl.pallas_call(
        functools.partial(_retention_kernel, chunk_decay=cd, final=True, chunks=chunks), grid=(B, nblk),
        in_specs=[pl.BlockSpec((R, RET_COLS), bwd_row), _full(dec.shape), _full(xi.shape), _full(zeta.shape),
                  pl.BlockSpec((R, RET_WIDTH), bwd_row), _full(ret_gn.shape), _full(ones256.shape)],
        out_specs=pl.BlockSpec((R, RET_WIDTH), bwd_row),
        out_shape=jax.ShapeDtypeStruct((N, RET_WIDTH), BF16),
        scratch_shapes=scratch, compiler_params=_params("parallel", "arbitrary"), name="retention_bwd",
    )(ret, dec, xi, zeta, yf, ret_gn, ones256)


def _flash_kernel(q_ref, k_ref, v_ref, o_ref, q_scr, m_scr, acc_scr, *, tq, tk):
    j = pl.program_id(3)

    @pl.when(j == 0)
    def _():
        for g in range(ATT_GROUP):
            q_scr[g * tq:(g + 1) * tq, :] = q_ref[:, g * HEAD_DIM:(g + 1) * HEAD_DIM]
        m_scr[...] = jnp.full_like(m_scr, -jnp.inf)
        acc_scr[...] = jnp.zeros_like(acc_scr)

    k = k_ref[...]
    v = v_ref[...]
    sub = min(FLASH_SUB_ROWS, ATT_GROUP * tq)
    for r0 in range(0, ATT_GROUP * tq, sub):
        rows = slice(r0, r0 + sub)
        s = _dot_nt(q_scr[rows, :], k)
        m_prev = m_scr[rows, :]
        m_next = jnp.maximum(m_prev, jnp.max(s, axis=1, keepdims=True))
        p = jnp.exp(s - jnp.concatenate([m_next] * (tk // V7X_LANES), axis=1))
        alpha = jnp.exp(m_prev - m_next)
        m_scr[rows, :] = m_next
        acc_scr[rows, :] = alpha * acc_scr[rows, :] + _dot(p.astype(BF16), v)

    @pl.when(j == pl.num_programs(3) - 1)
    def _():
        o = acc_scr[:, 0:HEAD_DIM] / acc_scr[:, HEAD_DIM:2 * HEAD_DIM]
        for g in range(ATT_GROUP):
            o_ref[:, g * HEAD_DIM:(g + 1) * HEAD_DIM] = o[g * tq:(g + 1) * tq, :].astype(BF16)


def _attention(q, k, v, B, T):
    N = q.shape[0]
    tq = min(FLASH_Q_ROWS, T)
    tk = min(FLASH_KV_ROWS, T)
    nq, nk = T // tq, T // tk
    GW = ATT_GROUP * HEAD_DIM
    rows = ATT_GROUP * tq
    return pl.pallas_call(
        functools.partial(_flash_kernel, tq=tq, tk=tk), grid=(B, ATT_KV_HEADS, nq, nk),
        in_specs=[pl.BlockSpec((tq, GW), lambda b, h, i, j: (b * nq + i, h)),
                  pl.BlockSpec((None, tk, HEAD_DIM), lambda b, h, i, j: (h, b * nk + j, 0)),
                  pl.BlockSpec((None, tk, 2 * HEAD_DIM), lambda b, h, i, j: (h, b * nk + j, 0))],
        out_specs=pl.BlockSpec((tq, GW), lambda b, h, i, j: (b * nq + i, h)),
        out_shape=jax.ShapeDtypeStruct((N, ATT_Q_WIDTH), BF16),
        scratch_shapes=[pltpu.VMEM((rows, HEAD_DIM), BF16), pltpu.VMEM((rows, V7X_LANES), F32),
                        pltpu.VMEM((rows, 2 * HEAD_DIM), F32)],
        compiler_params=_params("parallel", "parallel", "parallel", "arbitrary"), name="flash_attention",
    )(q, k, v)


def _rwkv_scan_kernel(*refs, rows_per_set):
    C = RW_CHUNK
    H = C // 2
    W = RWKV_WIDTH
    SUB = 8
    n_sets = len(rows_per_set)
    ones_ref = refs[12 * n_sets]
    outs = refs[12 * n_sets + 1:12 * n_sets + 1 + 2 * n_sets]
    state, vk_scr = refs[-2:]
    dirs = [(refs[12 * s:12 * s + 6], refs[12 * s + 6:12 * s + 12]) for s in range(n_sets)]
    chains = [(s, n, d) for s in range(n_sets) for n in range(rows_per_set[s]) for d in range(2)]

    first_chunk = pl.program_id(1) == 0

    @pl.when(first_chunk)
    def _():
        for c, (s, n, d) in enumerate(chains):
            if s == 0:
                state[c] = jnp.zeros((HEAD_DIM, W), F32)

    @pl.when(jnp.logical_and(first_chunk, pl.program_id(0) == 0))
    def _():
        for c, (s, n, d) in enumerate(chains):
            if s > 0:
                state[c] = jnp.zeros((HEAD_DIM, W), F32)

    ones = ones_ref[...]
    same_head = (lax.broadcasted_iota(jnp.int32, (W, W), 0) // HEAD_DIM
                 == lax.broadcasted_iota(jnp.int32, (W, W), 1) // HEAD_DIM)
    step_iota = lax.broadcasted_iota(jnp.int32, (H, HEAD_DIM, W), 0)
    lane_iota = lax.broadcasted_iota(jnp.int32, (H, HEAD_DIM, W), 2) & (HEAD_DIM - 1)

    def head_sums(vals):
        stacked = jnp.concatenate([v.astype(BF16) for v in vals], axis=0)
        out = _dot(stacked, ones)
        return [out[c * HEAD_DIM:(c + 1) * HEAD_DIM, :] for c in range(len(vals))]

    own_head = (lax.broadcasted_iota(jnp.int32, (SUB, W), 0)
                == lax.broadcasted_iota(jnp.int32, (SUB, W), 1) // HEAD_DIM)
    carry = tuple(state[c] for c in range(len(chains)))
    for half in range(2):
        lo = (half * H, (1 - half) * H)
        for c, (s, n, d) in enumerate(chains):
            kc = dirs[s][d][3][n]
            kbd = jnp.where(same_head, jnp.concatenate([kc] * (W // C), axis=0), 0.0).astype(BF16)
            sel = jnp.where(step_iota + lo[d] == lane_iota, dirs[s][d][5][n][None], 0.0).reshape(H * HEAD_DIM, W)
            vk_scr[c] = _dot(sel.astype(BF16), kbd).reshape(H, HEAD_DIM, W)

        def group(g, states, lo=lo):
            states = list(states)
            bases = (lo[0] + g * SUB, lo[1] + (H // SUB - 1 - g) * SUB)
            rows = [[ref[n, pl.ds(pl.multiple_of(bases[d], SUB), SUB), :] for ref in dirs[s][d][:5]]
                    for (s, n, d) in chains]
            ys = [[None] * SUB for _ in chains]
            for j in range(SUB):
                jj = (j, SUB - 1 - j)
                row = lambda c, f: rows[c][f][jj[chains[c][2]]:jj[chains[c][2]] + 1, :]
                sa = head_sums([states[c] * row(c, 0) for c in range(len(chains))])
                new = [states[c] * row(c, 2) - sa[c] * row(c, 4)
                       + vk_scr[c, bases[chains[c][2]] + jj[chains[c][2]] - lo[chains[c][2]]]
                       for c in range(len(chains))]
                for c, (s, n, d) in enumerate(chains):
                    read = new[c] if d == 0 else states[c]
                    r_heads = jnp.where(own_head, row(c, 1), 0.0).astype(BF16)
                    ys[c][jj[d]] = _dot_nt(r_heads, read.astype(BF16))
                states = new
            for c, (s, n, d) in enumerate(chains):
                span = pl.ds(pl.multiple_of(bases[d], SUB), SUB)
                for h in range(RWKV_HEADS):
                    col = jnp.concatenate([ys[c][r][h:h + 1, :] for r in range(SUB)], axis=0)
                    outs[2 * s + d][n, span, h * HEAD_DIM:(h + 1) * HEAD_DIM] = col
            return tuple(states)

        carry = lax.fori_loop(0, H // SUB, group, carry)

    for c in range(len(chains)):
        state[c] = carry[c]


def _rwkv_scan_call(sets, ones256):
    W, C = RWKV_WIDTH, RW_CHUNK
    _, B0, T0, nb0 = sets[0]
    nc0 = T0 // C
    operands, in_specs, out_specs, out_shapes = [], [], [], []
    for s, (rwp, B, T, nb) in enumerate(sets):
        nc = T // C
        fields = rwp.reshape(B, T, RW_FIELDS * W)
        v = rwp[:, 2 * W:3 * W].reshape(B, nc, C, RWKV_HEADS, HEAD_DIM)
        vt = v.transpose(0, 1, 4, 3, 2).reshape(B, nc, HEAD_DIM, W)

        def block(b, c, reverse, s=s, nc=nc):
            bi, ci = (b, c) if s == 0 else (0, b * nc0 + c)
            return bi, (nc - 1 - ci if reverse else ci)

        field = lambda f, rev, nb=nb, block=block: pl.BlockSpec(
            (nb, C, W), lambda b, c: block(b, c, rev) + (f,))
        vt_spec = lambda rev, nb=nb, block=block: pl.BlockSpec(
            (nb, None, HEAD_DIM, W), lambda b, c: block(b, c, rev) + (0, 0))
        in_specs += [field(1, False), field(0, False), field(3, False), field(5, False), field(7, False),
                     vt_spec(False),
                     field(1, True), field(0, True), field(4, True), field(6, True), field(8, True),
                     vt_spec(True)]
        operands += [fields] * 5 + [vt] + [fields] * 5 + [vt]
        y_spec = lambda rev, nb=nb, block=block: pl.BlockSpec(
            (nb, C, W), lambda b, c: block(b, c, rev) + (0,))
        out_specs += [y_spec(False), y_spec(True)]
        out_shapes += [jax.ShapeDtypeStruct((B, T, W), F32)] * 2
    n_chains = 2 * sum(nb for _, _, _, nb in sets)
    outs = pl.pallas_call(
        functools.partial(_rwkv_scan_kernel, rows_per_set=tuple(nb for _, _, _, nb in sets)),
        grid=(B0 // nb0, nc0), in_specs=in_specs + [_full(ones256.shape)],
        out_specs=out_specs, out_shape=out_shapes,
        scratch_shapes=[pltpu.VMEM((n_chains, HEAD_DIM, W), F32),
                        pltpu.VMEM((n_chains, C // 2, HEAD_DIM, W), F32)],
        compiler_params=_params("arbitrary", "arbitrary"), name="rwkv_scan",
    )(*operands, ones256)
    return [(outs[2 * s].reshape(B * T, W), outs[2 * s + 1].reshape(B * T, W))
            for s, (_, B, T, _) in enumerate(sets)]


def _rwkv_scan(groups, ones256):
    rows = lambda B: RW_BATCH_ROWS if B % RW_BATCH_ROWS == 0 else 1
    if len(groups) == 2:
        (_, B0, T0), (_, B1, T1) = groups
        if B1 == 1 and (B0 // rows(B0)) * (T0 // RW_CHUNK) == T1 // RW_CHUNK:
            return _rwkv_scan_call([groups[0] + (rows(B0),), groups[1] + (1,)], ones256)
    return [_rwkv_scan_call([g + (rows(g[1]),)], ones256)[0] for g in groups]


def _merge_kernel(x_ref, mod_ref, ya_ref, yb_ref, yf_ref, ybk_ref, rg_ref, bonus_ref, gn_ref, gates_ref,
                  ones_ref, wa_ref, wb_ref, wc_ref, wo_ref, o_ref):
    y = _head_layer_norm(yf_ref[...] + ybk_ref[...], ones_ref[...], RWKV_GN_EPS) * gn_ref[...]
    yc = ((y + bonus_ref[...]) * rg_ref[...]).astype(BF16)
    D = D_MODEL
    merged = (gates_ref[:, 0:D].astype(F32) * _dot(ya_ref[...], wa_ref[...])
              + gates_ref[:, D:2 * D].astype(F32) * _dot(yb_ref[...], wb_ref[...])
              + gates_ref[:, 2 * D:3 * D].astype(F32) * _dot(yc, wc_ref[...]))
    o_ref[...] = x_ref[...] + mod_ref[2:3, :] * _dot(merged.astype(BF16), wo_ref[...])


def _merge(x, mod, T, ya, yb, yf, ybk, rwp, gates, lw, ones256):
    N = x.shape[0]
    tm = min(512, T)
    tps = T // tm
    x_spec, mod_spec, _ = _token_specs(tm, tps)
    row = lambda w: pl.BlockSpec((tm, w), lambda i: (i, 0))
    field = lambda f: pl.BlockSpec((tm, RWKV_WIDTH), lambda i: (i, f))
    return pl.pallas_call(
        _merge_kernel, grid=(N // tm,),
        in_specs=[x_spec, mod_spec, row(RET_WIDTH), row(ATT_Q_WIDTH), row(RWKV_WIDTH), row(RWKV_WIDTH),
                  field(9), field(10), _full(lw["rw_gn"].shape), row(GATE_COLS), _full(ones256.shape),
                  _full(lw["w_branch_a"].shape), _full(lw["w_branch_b"].shape), _full(lw["w_branch_c"].shape),
                  _full(lw["w_out"].shape)],
        out_specs=x_spec, out_shape=jax.ShapeDtypeStruct((N, D_MODEL), F32),
        compiler_params=_params("parallel"), name="merge_out",
    )(x, mod, ya, yb, yf, ybk, rwp, rwp, lw["rw_gn"], gates, ones256,
      lw["w_branch_a"], lw["w_branch_b"], lw["w_branch_c"], lw["w_out"])


def _route(h, hb, rwt_ref, rbias_ref, tm):
    h_lo = (h - hb.astype(F32)).astype(BF16)
    rw_hi, rw_lo = _split(rwt_ref[...])
    logits = _dot_nt(rw_hi, hb) + _dot_nt(rw_hi, h_lo) + _dot_nt(rw_lo, hb)
    scores = _sigmoid(logits)
    choice = scores + jnp.concatenate([rbias_ref[...]] * (tm // V7X_LANES), axis=1)
    per_group = N_EXPERTS // N_GROUPS
    sub = lax.broadcasted_iota(jnp.int32, (per_group, tm), 0)
    groups, gscore = [], []
    for g in range(N_GROUPS):
        cg = choice[g * per_group:(g + 1) * per_group, :]
        m1 = jnp.max(cg, axis=0, keepdims=True)
        first = jnp.min(jnp.where(cg == m1, sub, per_group), axis=0, keepdims=True)
        m2 = jnp.max(jnp.where(sub == first, -jnp.inf, cg), axis=0, keepdims=True)
        groups.append(cg)
        gscore.append(m1 + m2)
    masked = []
    for g in range(N_GROUPS):
        beaten = jnp.zeros((1, tm), F32)
        for o in range(N_GROUPS):
            if o == g:
                continue
            wins = (gscore[o] >= gscore[g]) if o < g else (gscore[o] > gscore[g])
            beaten = beaten + jnp.where(wins, 1.0, 0.0)
        keep = jnp.where(beaten < TOPK_GROUPS, 1.0, 0.0)
        masked.append(jnp.where(jnp.broadcast_to(keep, (per_group, tm)) > 0.5, groups[g], -jnp.inf))
    mc = jnp.concatenate(masked, axis=0)
    eidx = lax.broadcasted_iota(jnp.int32, (N_EXPERTS, tm), 0).astype(F32)
    sel = jnp.zeros((N_EXPERTS, tm), F32)
    for _ in range(TOP_K):
        top = jnp.max(mc, axis=0, keepdims=True)
        first = jnp.min(jnp.where(mc == top, eidx, float(N_EXPERTS)), axis=0, keepdims=True)
        pick = eidx == first
        sel = jnp.where(pick, 1.0, sel)
        mc = jnp.where(pick, -jnp.inf, mc)
    wts = jnp.where(sel > 0.5, scores, 0.0)
    return wts / jnp.sum(wts, axis=0, keepdims=True) * ROUTED_SCALE, sel


def _selected_lists(comb, sel, rank):
    eidx = lax.broadcasted_iota(jnp.int32, comb.shape, 0).astype(F32)
    prev = jnp.full((1, comb.shape[1]), -1.0, F32)
    ids, wts, pos = [], [], []
    for _ in range(TOP_K):
        cand = jnp.where(sel > 0.5, jnp.where(eidx > prev, eidx, float(N_EXPERTS)), float(N_EXPERTS))
        prev = jnp.min(cand, axis=0, keepdims=True)
        wts.append(jnp.sum(jnp.where(eidx == prev, comb, 0.0), axis=0, keepdims=True))
        pos.append(jnp.sum(jnp.where(eidx == prev, rank, 0.0), axis=0, keepdims=True))
        ids.append(jnp.minimum(prev, N_EXPERTS - 1.0))
    as_int = lambda rows: jnp.concatenate(rows, axis=0).astype(jnp.int32)
    return as_int(ids), jnp.concatenate(wts, axis=0), as_int(pos)


def _pack_halves(y):
    n = y.shape[1] // 2
    hi = pltpu.bitcast(y[:, :n].astype(BF16).astype(F32), jnp.int32)
    lo = pltpu.bitcast(y[:, n:].astype(BF16).astype(F32), jnp.int32)
    return hi | lax.shift_right_logical(lo, 16)


def _unpack_halves(w):
    hi = pltpu.bitcast(w & jnp.int32(-65536), F32)
    lo = pltpu.bitcast(lax.shift_left(w, 16), F32)
    return hi, lo


def _moe_route_kernel(x_ref, mod_ref, g_ref, rwt_ref, rbias_ref, shgu_ref, shd_ref, tri_ref,
                      hp_ref, ids_ref, wts_ref, pos_ref, base_ref, count_ref, *, tm):
    F = EXPERT_FF

    @pl.when(pl.program_id(0) == 0)
    def _():
        count_ref[...] = jnp.zeros_like(count_ref)

    x = x_ref[...]
    h = _norm_mod(x, g_ref[...], mod_ref[3:4, :], mod_ref[4:5, :])
    hb = h.astype(BF16)
    hp_ref[...] = _pack_halves(hb.astype(F32))
    comb, sel = _route(h, hb, rwt_ref, rbias_ref, tm)
    before = count_ref[...]
    inclusive = _dot(sel.astype(BF16), tri_ref[...])
    rank = inclusive - sel + jnp.concatenate([before] * (tm // V7X_LANES), axis=1)
    count_ref[...] = before + jnp.sum(sel, axis=1, keepdims=True)
    ids, wts, pos = _selected_lists(comb, sel, rank)
    ids_ref[...] = ids
    wts_ref[...] = wts
    pos_ref[...] = pos
    gu = _dot(hb, shgu_ref[...])
    shared = _dot((_silu(gu[:, 0:F]) * gu[:, F:2 * F]).astype(BF16), shd_ref[...])
    base_ref[...] = x + mod_ref[5:6, :] * shared


def _moe_expert_kernel(tile_expert_ref, xs_ref, wgu_ref, wd_ref, o_ref):
    del tile_expert_ref
    F = EXPERT_FF
    half = D_MODEL // 2
    hi, lo = _unpack_halves(xs_ref[...])
    gu = _dot(hi.astype(BF16), wgu_ref[0:half, :]) + _dot(lo.astype(BF16), wgu_ref[half:, :])
    act = (_silu(gu[:, 0:F]) * gu[:, F:2 * F]).astype(BF16)
    o_ref[...] = _pack_halves(_dot(act, wd_ref[...]))


def _moe_combine_kernel(base_ref, mod_ref, y_ref, w_ref, fin_ref, o_ref, *, final):
    half = D_MODEL // 2
    acc_hi = jnp.zeros((base_ref.shape[0], half), F32)
    acc_lo = jnp.zeros((base_ref.shape[0], half), F32)
    for k in range(TOP_K):
        hi, lo = _unpack_halves(y_ref[k])
        wk = w_ref[:, k:k + 1]
        acc_hi = acc_hi + wk * hi
        acc_lo = acc_lo + wk * lo
    out_hi = base_ref[:, 0:half] + mod_ref[5:6, 0:half] * acc_hi
    out_lo = base_ref[:, half:] + mod_ref[5:6, half:] * acc_lo
    if final:
        ms = (jnp.sum(out_hi * out_hi, axis=-1, keepdims=True)
              + jnp.sum(out_lo * out_lo, axis=-1, keepdims=True)) * (1.0 / D_MODEL)
        scale = lax.rsqrt(ms + NORM_EPS)
        out_hi = out_hi * scale * fin_ref[:, 0:half]
        out_lo = out_lo * scale * fin_ref[:, half:]
    o_ref[:, 0:half] = out_hi
    o_ref[:, half:] = out_lo


def _gather_rows(table, idx):
    rows, width = idx.shape[0], table.shape[1]
    info = plsc.get_sparse_core_info()
    workers = info.num_cores * info.num_subcores
    windows = rows // (workers * SC_GATHER_ROWS)
    assert windows * workers * SC_GATHER_ROWS == rows
    mesh = plsc.VectorSubcoreMesh(core_axis_name="core", subcore_axis_name="subcore")

    @functools.partial(pl.kernel, out_type=jax.ShapeDtypeStruct((rows, width), table.dtype), mesh=mesh,
                       scratch_types=[pltpu.VMEM((SC_GATHER_ROWS,), jnp.int32),
                                      pltpu.VMEM((SC_GATHER_ROWS, width), table.dtype)],
                       name="moe_gather")
    def gather(table_hbm, idx_hbm, out_hbm, idx_vmem, rows_vmem):
        worker = lax.axis_index("subcore") * info.num_cores + lax.axis_index("core")
        base = worker * (windows * SC_GATHER_ROWS)

        @pl.loop(0, windows)
        def _(j):
            span = pl.ds(pl.multiple_of(base + j * SC_GATHER_ROWS, SC_GATHER_ROWS), SC_GATHER_ROWS)
            pltpu.sync_copy(idx_hbm.at[span], idx_vmem)
            pltpu.sync_copy(table_hbm.at[idx_vmem], rows_vmem)
            pltpu.sync_copy(rows_vmem, out_hbm.at[span])

    return gather(table, idx)


def _scatter_rows(table, dest, total_rows):
    n_lists, n_rows = dest.shape
    width = table.shape[1]
    info = plsc.get_sparse_core_info()
    workers = info.num_cores * info.num_subcores
    windows = n_rows // (workers * SC_GATHER_ROWS)
    assert windows * workers * SC_GATHER_ROWS == n_rows
    windows_per_list = n_rows // SC_GATHER_ROWS
    mesh = plsc.VectorSubcoreMesh(core_axis_name="core", subcore_axis_name="subcore")

    @functools.partial(pl.kernel, out_type=jax.ShapeDtypeStruct((total_rows, width), table.dtype), mesh=mesh,
                       scratch_types=[pltpu.VMEM((1, SC_GATHER_ROWS), jnp.int32),
                                      pltpu.VMEM((SC_GATHER_ROWS, width), table.dtype)],
                       name="moe_scatter")
    def scatter(table_hbm, dest_hbm, out_hbm, idx_vmem, rows_vmem):
        worker = lax.axis_index("subcore") * info.num_cores + lax.axis_index("core")

        @pl.loop(0, windows)
        def _(j):
            window = worker * windows + j
            span = pl.ds(pl.multiple_of(window * SC_GATHER_ROWS, SC_GATHER_ROWS), SC_GATHER_ROWS)
            pltpu.sync_copy(table_hbm.at[span], rows_vmem)
            for k in range(n_lists):
                pltpu.sync_copy(dest_hbm.at[pl.ds(k * windows_per_list + window, 1)], idx_vmem)
                pltpu.sync_copy(rows_vmem, out_hbm.at[idx_vmem.at[0]])

    return scatter(table, dest.reshape(n_lists * windows_per_list, SC_GATHER_ROWS))


def _dispatch_plan(ids, pos, counts, n_tokens, tg):
    E = N_EXPERTS
    padded = (counts + tg - 1) // tg * tg
    ends = jnp.cumsum(padded)
    off = ends - padded
    onehot = ids[:, :, None] == jnp.arange(E, dtype=jnp.int32)[None, None, :]
    dest = pos + jnp.sum(jnp.where(onehot, off[None, None, :], 0), axis=-1)
    total = n_tokens * TOP_K + E * tg
    tile_start = jnp.arange(total // tg, dtype=jnp.int32) * tg
    tile_expert = jnp.sum((ends[None, :] <= tile_start[:, None]).astype(jnp.int32), axis=1)
    return dest, total, jnp.minimum(tile_expert, E - 1)


def _moe(x, mod, T, lw, final_g, final):
    N = x.shape[0]
    tm = min(MOE_TOKENS, T)
    tps = T // tm
    D, F, E = D_MODEL, EXPERT_FF, N_EXPERTS
    tg = max(MOE_GROUP_ROWS_MIN, min(MOE_GROUP_ROWS_MAX, N * TOP_K // (E * 16)))
    half = D // 2
    x_spec, mod_spec, g_spec = _token_specs(tm, tps)
    list_spec = pl.BlockSpec((TOP_K, tm), lambda i: (0, i))

    tri = jnp.asarray(np.triu(np.ones((tm, tm), np.float32)), BF16)
    hp, ids, wts, pos, base, counts = pl.pallas_call(
        functools.partial(_moe_route_kernel, tm=tm), grid=(N // tm,),
        in_specs=[x_spec, mod_spec, g_spec, _full((E, D)), _full((E, V7X_LANES)), _full((D, 2 * F)), _full((F, D)),
                  _full((tm, tm))],
        out_specs=[pl.BlockSpec((tm, half), lambda i: (i, 0)), list_spec, list_spec, list_spec, x_spec,
                   _full((E, V7X_LANES))],
        out_shape=[jax.ShapeDtypeStruct((N, half), jnp.int32), jax.ShapeDtypeStruct((TOP_K, N), jnp.int32),
                   jax.ShapeDtypeStruct((TOP_K, N), F32), jax.ShapeDtypeStruct((TOP_K, N), jnp.int32),
                   jax.ShapeDtypeStruct((N, D), F32), jax.ShapeDtypeStruct((E, V7X_LANES), F32)],
        compiler_params=_params("arbitrary"), name="moe_route",
    )(x, mod, lw["norm2_g"], lw["router_wt"], lw["router_bias"], lw["sh_gu"], lw["sh_d"], tri)

    dest, total, tile_expert = _dispatch_plan(ids, pos, counts[:, 0].astype(jnp.int32), N, tg)
    xs = _scatter_rows(hp, dest, total)
    n_tiles = total // tg
    ys = pl.pallas_call(
        _moe_expert_kernel,
        grid_spec=pltpu.PrefetchScalarGridSpec(
            num_scalar_prefetch=1, grid=(n_tiles,),
            in_specs=[pl.BlockSpec((tg, half), lambda i, te: (i, 0)),
                      pl.BlockSpec((None, D, 2 * F), lambda i, te: (te[i], 0, 0)),
                      pl.BlockSpec((None, F, D), lambda i, te: (te[i], 0, 0))],
            out_specs=pl.BlockSpec((tg, half), lambda i, te: (i, 0))),
        out_shape=jax.ShapeDtypeStruct(xs.shape, jnp.int32),
        compiler_params=_params("parallel"), name="moe_experts",
    )(tile_expert, xs, lw["exp_gu"], lw["exp_d"])

    yk = _gather_rows(ys, dest.reshape(-1)).reshape(TOP_K, N, half)
    return pl.pallas_call(
        functools.partial(_moe_combine_kernel, final=final), grid=(N // tm,),
        in_specs=[x_spec, mod_spec, pl.BlockSpec((TOP_K, tm, half), lambda i: (0, i, 0)),
                  pl.BlockSpec((tm, TOP_K), lambda i: (i, 0)), g_spec],
        out_specs=x_spec, out_shape=jax.ShapeDtypeStruct((N, D), F32),
        compiler_params=_params("parallel"), name="moe_combine",
    )(base, mod, yk, wts.T, final_g)


def _block_diag2(a, b):
    za = jnp.zeros((a.shape[0], b.shape[1]), a.dtype)
    zb = jnp.zeros((b.shape[0], a.shape[1]), a.dtype)
    return jnp.concatenate([jnp.concatenate([a, za], 1), jnp.concatenate([zb, b], 1)], 0)


def _layer_weights(P, l):
    w_in = P["w_in"][l]
    o1, o2, o3 = RET_COLS, RET_COLS + ATT_COLS, RET_COLS + ATT_COLS + RW_COLS
    w_ret, w_att, w_rw, w_gate = w_in[:, :o1], w_in[:, o1:o2], w_in[:, o2:o3], w_in[:, o3:]
    qk_gain = jnp.zeros((8, ATT_Q_WIDTH), F32)
    qk_gain = qk_gain.at[0].set(jnp.tile(P["q_norm_g"][l], ATT_Q_HEADS))
    qk_gain = qk_gain.at[1, :ATT_KV_WIDTH].set(jnp.tile(P["k_norm_g"][l], ATT_KV_HEADS))
    rw_vec = jnp.zeros((8, RWKV_WIDTH), F32)
    for j, name in enumerate(("rw_w0_f", "rw_w0_b", "rw_a0_f", "rw_a0_b", "rw_k_k", "rw_k_a")):
        rw_vec = rw_vec.at[j].set(P[name][l])
    rw_vec = rw_vec.at[6].set(P["rw_r_k"][l].reshape(RWKV_WIDTH))
    row = lambda v: v.reshape(1, -1)
    return {
        "norm1_g": row(P["norm1_g"][l]), "norm2_g": row(P["norm2_g"][l]),
        "w_ret": w_ret.astype(BF16), "w_att": w_att.astype(BF16), "w_rw": w_rw.astype(BF16),
        "w_gate": w_gate.astype(BF16), "qk_gain": qk_gain, "ret_gn": row(P["ret_gn"][l]),
        "shift_mu": P["shift_mu"][l], "rw_vec": rw_vec,
        "rw_w2": _block_diag2(P["rw_w2_f"][l], P["rw_w2_b"][l]).astype(BF16),
        "rw_a2": _block_diag2(P["rw_a2_f"][l], P["rw_a2_b"][l]).astype(BF16),
        "rw_g2": P["rw_g2"][l].astype(BF16), "rw_gn": row(P["rw_gn"][l]),
        "w_branch_a": P["w_branch_a"][l].astype(BF16), "w_branch_b": P["w_branch_b"][l].astype(BF16),
        "w_branch_c": P["w_branch_c"][l].astype(BF16), "w_out": P["w_out"][l].astype(BF16),
        "router_wt": P["router_w"][l].T,
        "router_bias": jnp.broadcast_to(P["router_bias"][l][:, None], (N_EXPERTS, V7X_LANES)),
        "sh_gu": jnp.concatenate([P["sh_w_gate"][l], P["sh_w_up"][l]], axis=1).astype(BF16),
        "sh_d": P["sh_w_down"][l].astype(BF16),
        "exp_gu": jnp.concatenate([P["exp_w_gate"][l], P["exp_w_up"][l]], axis=2).astype(BF16),
        "exp_d": P["exp_w_down"][l].astype(BF16),
    }


def _ones_block_diag(n):
    idx = np.arange(n) // HEAD_DIM
    return jnp.asarray(idx[:, None] == idx[None, :], BF16)


def _tables(T):
    rows = T // GRID_W
    row = jnp.repeat(jnp.arange(rows, dtype=F32), GRID_W)
    col = jnp.tile(jnp.arange(GRID_W, dtype=F32), rows)
    freqs = ROPE_THETA ** (-jnp.arange(ROPE_PAIRS_PER_AXIS, dtype=F32) / ROPE_PAIRS_PER_AXIS)
    ang = jnp.concatenate([row[:, None] * freqs, col[:, None] * freqs], axis=-1)
    cos, sin = jnp.cos(ang), jnp.sin(ang)
    return {
        "cos": jnp.tile(jnp.concatenate([cos, cos], axis=-1), (1, ATT_Q_HEADS)),
        "sin": jnp.tile(jnp.concatenate([-sin, sin], axis=-1), (1, ATT_Q_HEADS)),
        "ones128": _ones_block_diag(128), "ones256": _ones_block_diag(256), "ones512": _ones_block_diag(512),
    }


def _trunks(requests, P, layer_weights):
    groups = []
    for x, c in requests:
        B, T, D = x.shape
        groups.append({"B": B, "T": T, "x": x.reshape(B * T, D), "tabs": _tables(T),
                       "mods": _ada_mod(c, P["ada_w"], P["ada_b"])})
    final_g = P["final_g"].reshape(1, D_MODEL)
    depth = len(layer_weights)
    for l, lw in enumerate(layer_weights):
        mixed = []
        for g in groups:
            B, T, tabs, mod = g["B"], g["T"], g["tabs"], g["mods"][l]
            ret, q, k, v, gates, rwp = _input_projections(g["x"], mod, T, lw, tabs)
            ya = _retention(ret, B, T, lw["ret_gn"], tabs["ones256"])
            yb = _attention(q, k, v, B, T)
            mixed.append((mod, ya, yb, gates, rwp))
        scans = _rwkv_scan([(m[4], g["B"], g["T"]) for m, g in zip(mixed, groups)], groups[0]["tabs"]["ones256"])
        for g, (mod, ya, yb, gates, rwp), (yf, ybk) in zip(groups, mixed, scans):
            x = _merge(g["x"], mod, g["T"], ya, yb, yf, ybk, rwp, gates, lw, g["tabs"]["ones256"])
            g["x"] = _moe(x, mod, g["T"], lw, final_g, final=(l == depth - 1))
    return [g["x"].reshape(g["B"], g["T"], D_MODEL) for g in groups]


def kernel(x_prompt, x_sample, c_prompt, c_sample, norm1_g, norm2_g, final_g, ada_w, ada_b, w_in, ret_gn, q_norm_g, k_norm_g, shift_mu, rw_w0_f, rw_w2_f, rw_w0_b, rw_w2_b, rw_a0_f, rw_a2_f, rw_a0_b, rw_a2_b, rw_g2, rw_k_k, rw_k_a, rw_r_k, rw_gn, w_branch_a, w_branch_b, w_branch_c, w_out, router_w, router_bias, exp_w_gate, exp_w_up, exp_w_down, sh_w_gate, sh_w_up, sh_w_down):
    P = {
        "norm1_g": norm1_g, "norm2_g": norm2_g, "final_g": final_g, "ada_w": ada_w, "ada_b": ada_b,
        "w_in": w_in, "ret_gn": ret_gn, "q_norm_g": q_norm_g, "k_norm_g": k_norm_g, "shift_mu": shift_mu,
        "rw_w0_f": rw_w0_f, "rw_w2_f": rw_w2_f, "rw_w0_b": rw_w0_b, "rw_w2_b": rw_w2_b,
        "rw_a0_f": rw_a0_f, "rw_a2_f": rw_a2_f, "rw_a0_b": rw_a0_b, "rw_a2_b": rw_a2_b,
        "rw_g2": rw_g2, "rw_k_k": rw_k_k, "rw_k_a": rw_k_a, "rw_r_k": rw_r_k, "rw_gn": rw_gn,
        "w_branch_a": w_branch_a, "w_branch_b": w_branch_b, "w_branch_c": w_branch_c, "w_out": w_out,
        "router_w": router_w, "router_bias": router_bias,
        "exp_w_gate": exp_w_gate, "exp_w_up": exp_w_up, "exp_w_down": exp_w_down,
        "sh_w_gate": sh_w_gate, "sh_w_up": sh_w_up, "sh_w_down": sh_w_down,
    }
    layer_weights = [_layer_weights(P, l) for l in range(w_in.shape[0])]
    y_prompt, y_sample = _trunks([(x_prompt, c_prompt), (x_sample, c_sample)], P, layer_weights)
    return (y_prompt, y_sample)
```

```python
import functools

import numpy as np
import jax
import jax.numpy as jnp
from jax import lax
from jax.experimental import pallas as pl
from jax.experimental.pallas import tpu as pltpu
from jax.experimental.pallas import tpu_sc as plsc

F32 = jnp.float32
BF16 = jnp.bfloat16

D_MODEL = 1024
DEPTH = 2
GRID_W = 64
HEAD_DIM = 64
ROPE_THETA = 10000.0
ROPE_PAIRS_PER_AXIS = HEAD_DIM // 4

RET_HEADS = 4
RET_WIDTH = RET_HEADS * HEAD_DIM
RET_CHUNK = 128
RET_CHUNKS_PER_STEP = 4
RET_EPS = 1e-5

ATT_Q_HEADS = 8
ATT_KV_HEADS = 2
ATT_GROUP = ATT_Q_HEADS // ATT_KV_HEADS
ATT_Q_WIDTH = ATT_Q_HEADS * HEAD_DIM
ATT_KV_WIDTH = ATT_KV_HEADS * HEAD_DIM

RWKV_HEADS = 4
RWKV_WIDTH = RWKV_HEADS * HEAD_DIM
DECAY_LORA = 64
AAA_LORA = 64
GATE_LORA = 128
RWKV_GN_EPS = 64e-5
RW_CHUNK = 64
RW_FIELDS = 11
RW_BATCH_ROWS = 4

FLASH_Q_ROWS = 1024
FLASH_KV_ROWS = 512
FLASH_SUB_ROWS = 1024

MOE_TOKENS = 512
MOE_GROUP_ROWS_MIN = 256
MOE_GROUP_ROWS_MAX = 1024
SC_GATHER_ROWS = 128

N_EXPERTS = 64
TOP_K = 8
N_GROUPS = 8
TOPK_GROUPS = 4
EXPERT_FF = 256
ROUTED_SCALE = 2.5
NORM_EPS = 1e-6

RET_COLS = 4 * RET_WIDTH
ATT_COLS = ATT_Q_WIDTH + 2 * ATT_KV_WIDTH
RW_COLS = 3 * RWKV_WIDTH + 2 * DECAY_LORA + 2 * AAA_LORA + GATE_LORA
GATE_COLS = 3 * D_MODEL

V7X_LANES = 128
VMEM_LIMIT_BYTES = 48 * 1024 * 1024


def _params(*dims):
    return pltpu.CompilerParams(dimension_semantics=dims, vmem_limit_bytes=VMEM_LIMIT_BYTES)


def _sigmoid(x):
    return 1.0 / (1.0 + jnp.exp(-x))


def _silu(x):
    return x * _sigmoid(x)


def _dot(a, b):
    return jnp.dot(a, b, preferred_element_type=F32)


def _dot_nt(a, b):
    return lax.dot_general(a, b, (((1,), (1,)), ((), ())), preferred_element_type=F32)


def _split(a):
    hi = a.astype(BF16)
    lo = (a - hi.astype(F32)).astype(BF16)
    return hi, lo


def _dot_split_lhs(a, b_bf16):
    hi, lo = _split(a)
    return _dot(hi, b_bf16) + _dot(lo, b_bf16)


def _dot3(a, b):
    ah, al = _split(a)
    bh, bl = _split(b)
    return _dot(ah, bh) + _dot(ah, bl) + _dot(al, bh)


def _swap_rotary_halves(x):
    half = HEAD_DIM // 2
    lane = lax.broadcasted_iota(jnp.int32, x.shape, 1) & (HEAD_DIM - 1)
    return jnp.where(lane < half, pltpu.roll(x, x.shape[1] - half, 1), pltpu.roll(x, half, 1))


def _norm_mod(x, gain, shift, scale):
    ms = jnp.mean(x * x, axis=-1, keepdims=True)
    return x * lax.rsqrt(ms + NORM_EPS) * gain * (1.0 + scale) + shift


def _head_layer_norm(y, ones_bd, eps):
    mean = _dot_split_lhs(y, ones_bd) * (1.0 / HEAD_DIM)
    yc = y - mean
    var = _dot_split_lhs(yc * yc, ones_bd) * (1.0 / HEAD_DIM)
    return yc * lax.rsqrt(var + eps)


def _ada_kernel(c_ref, w_ref, b_ref, o_ref):
    c = c_ref[...]
    o_ref[...] = _dot3(_silu(c), w_ref[...]) + b_ref[...]


def _ada_mod(c, ada_w, ada_b):
    B = c.shape[0]
    Bp = max(8, B)
    cp = jnp.zeros((Bp, D_MODEL), F32).at[:B].set(c)
    L = ada_w.shape[0]
    tn = 1536
    out = pl.pallas_call(
        _ada_kernel,
        grid=(L, 6 * D_MODEL // tn),
        in_specs=[pl.BlockSpec((Bp, D_MODEL), lambda l, j: (0, 0)),
                  pl.BlockSpec((None, D_MODEL, tn), lambda l, j: (l, 0, j)),
                  pl.BlockSpec((None, 1, tn), lambda l, j: (l, 0, j))],
        out_specs=pl.BlockSpec((None, Bp, tn), lambda l, j: (l, 0, j)),
        out_shape=jax.ShapeDtypeStruct((L, Bp, 6 * D_MODEL), F32),
        compiler_params=_params("parallel", "parallel"),
        name="ada_mod",
    )(cp, ada_w, ada_b.reshape(L, 1, 6 * D_MODEL))
    return out[:, :B].reshape(L, B, 6, D_MODEL)


def _proj_ret_kernel(x_ref, mod_ref, g_ref, w_ref, cos_ref, sin_ref, o_ref):
    h = _norm_mod(x_ref[...], g_ref[...], mod_ref[0:1, :], mod_ref[1:2, :]).astype(BF16)
    p = _dot(h, w_ref[...])
    W = RET_WIDTH
    cos = cos_ref[:, 0:W]
    sin = sin_ref[:, 0:W]
    q, k = p[:, 0:W], p[:, W:2 * W]
    o_ref[:, 0:W] = (q * cos + _swap_rotary_halves(q) * sin).astype(BF16)
    o_ref[:, W:2 * W] = ((k * cos + _swap_rotary_halves(k) * sin) * (HEAD_DIM ** -0.5)).astype(BF16)
    o_ref[:, 2 * W:3 * W] = p[:, 2 * W:3 * W].astype(BF16)
    o_ref[:, 3 * W:4 * W] = _silu(p[:, 3 * W:4 * W]).astype(BF16)


def _proj_att_kernel(x_ref, mod_ref, g_ref, w_ref, cos_ref, sin_ref, qkg_ref, ones_ref, q_ref, k_ref, v_ref):
    h = _norm_mod(x_ref[...], g_ref[...], mod_ref[0:1, :], mod_ref[1:2, :]).astype(BF16)
    p = _dot(h, w_ref[...])
    QW, KW = ATT_Q_WIDTH, ATT_KV_WIDTH
    q = p[:, 0:QW]
    k = p[:, QW:QW + KW]
    v = p[:, QW + KW:QW + 2 * KW]
    ones = ones_ref[...]
    rq = lax.rsqrt(_dot_split_lhs(q * q, ones) * (1.0 / HEAD_DIM) + NORM_EPS)
    rk = lax.rsqrt(_dot_split_lhs(k * k, ones[0:KW, 0:KW]) * (1.0 / HEAD_DIM) + NORM_EPS)
    cos = cos_ref[...]
    sin = sin_ref[...]
    qg = q * qkg_ref[0:1, :]
    kg = k * qkg_ref[1:2, 0:KW]
    qo = (qg * cos + _swap_rotary_halves(qg) * sin) * (rq * (HEAD_DIM ** -0.5))
    ko = (kg * cos[:, 0:KW] + _swap_rotary_halves(kg) * sin[:, 0:KW]) * rk
    q_ref[...] = qo.astype(BF16)
    kb = ko.astype(BF16)
    vb = v.astype(BF16)
    for hk in range(ATT_KV_HEADS):
        k_ref[hk] = kb[:, hk * HEAD_DIM:(hk + 1) * HEAD_DIM]
        v_ref[hk, :, 0:HEAD_DIM] = vb[:, hk * HEAD_DIM:(hk + 1) * HEAD_DIM]
        v_ref[hk, :, HEAD_DIM:2 * HEAD_DIM] = jnp.ones((vb.shape[0], HEAD_DIM), BF16)


def _proj_gate_kernel(x_ref, mod_ref, g_ref, w_ref, o_ref):
    h = _norm_mod(x_ref[...], g_ref[...], mod_ref[0:1, :], mod_ref[1:2, :]).astype(BF16)
    o_ref[...] = _sigmoid(_dot(h, w_ref[...])).astype(BF16)


def _proj_rw_kernel(x_ref, xp_ref, xn_ref, mod_ref, g_ref, w_ref, mu_ref, vec_ref, w2_ref, a2_ref, g2_ref,
                    ones_ref, o_ref, p_scr, *, tm, tiles_per_seq):
    i = pl.program_id(0)
    gain, shift, scale = g_ref[...], mod_ref[0:1, :], mod_ref[1:2, :]
    w = w_ref[...]
    h = _norm_mod(x_ref[...], gain, shift, scale).astype(BF16)
    p = _dot(h, w)
    hp = _norm_mod(xp_ref[...], gain, shift, scale).astype(BF16)
    hn = _norm_mod(xn_ref[...], gain, shift, scale).astype(BF16)
    first = (i % tiles_per_seq) == 0
    last = (i % tiles_per_seq) == tiles_per_seq - 1
    prev_row = jnp.where(first, 0.0, _dot(hp, w)[7:8, :])
    next_row = jnp.where(last, 0.0, _dot(hn, w)[0:1, :])
    p_scr[8:8 + tm, :] = p
    p_scr[7:8, :] = prev_row
    p_scr[8 + tm:9 + tm, :] = next_row
    prev = p_scr[7:7 + tm, :]
    nxt = p_scr[9:9 + tm, :]
    x = p + mu_ref[0:1, :] * (prev - p) + mu_ref[1:2, :] * (nxt - p)

    W = RWKV_WIDTH
    r, k, v = x[:, 0:W], x[:, W:2 * W], x[:, 2 * W:3 * W]
    xw = x[:, 3 * W:3 * W + 2 * DECAY_LORA]
    xa = x[:, 3 * W + 2 * DECAY_LORA:3 * W + 2 * DECAY_LORA + 2 * AAA_LORA]
    xg = x[:, 3 * W + 2 * DECAY_LORA + 2 * AAA_LORA:]
    w0_f, w0_b = vec_ref[0:1, :], vec_ref[1:2, :]
    a0_f, a0_b = vec_ref[2:3, :], vec_ref[3:4, :]
    k_k, k_a, r_k = vec_ref[4:5, :], vec_ref[5:6, :], vec_ref[6:7, :]
    wl = _dot(jnp.tanh(xw).astype(BF16), w2_ref[...])
    al = _dot(xa.astype(BF16), a2_ref[...])
    dec_c = float(np.exp(-0.5))
    w_f = jnp.exp(-dec_c * _sigmoid(w0_f + wl[:, 0:W]))
    w_b = jnp.exp(-dec_c * _sigmoid(w0_b + wl[:, W:2 * W]))
    a_f = _sigmoid(a0_f + al[:, 0:W])
    a_b = _sigmoid(a0_b + al[:, W:2 * W])
    gate = _dot(_sigmoid(xg).astype(BF16), g2_ref[...])
    ones = ones_ref[...]
    kk = k * k_k
    kk = kk * lax.rsqrt(_dot_split_lhs(kk * kk, ones) + 1e-12)
    k_f = k * (1.0 + (a_f - 1.0) * k_a)
    k_b = k * (1.0 + (a_b - 1.0) * k_a)
    bonus = _dot_split_lhs(r * k_f * r_k, ones) * v
    for j, val in enumerate((r, kk, v, w_f, w_b, k_f, k_b, kk * a_f, kk * a_b, gate, bonus)):
        o_ref[:, j * W:(j + 1) * W] = val


def _token_specs(tm, tiles_per_seq):
    x_spec = pl.BlockSpec((tm, D_MODEL), lambda i: (i, 0))
    mod_spec = pl.BlockSpec((None, 6, D_MODEL), lambda i: (i // tiles_per_seq, 0, 0))
    g_spec = pl.BlockSpec((1, D_MODEL), lambda i: (0, 0))
    return x_spec, mod_spec, g_spec


def _full(shape):
    nd = len(shape)
    return pl.BlockSpec(shape, lambda *_: (0,) * nd)


def _input_projections(x, mod, T, lw, tabs):
    N = x.shape[0]
    tm = min(512, T)
    tps = T // tm
    grid = (N // tm,)
    x_spec, mod_spec, g_spec = _token_specs(tm, tps)
    tab_spec = pl.BlockSpec((tm, ATT_Q_WIDTH), lambda i: (i % tps, 0))
    cos, sin = tabs["cos"], tabs["sin"]

    ret = pl.pallas_call(
        _proj_ret_kernel, grid=grid,
        in_specs=[x_spec, mod_spec, g_spec, _full(lw["w_ret"].shape), tab_spec, tab_spec],
        out_specs=pl.BlockSpec((tm, RET_COLS), lambda i: (i, 0)),
        out_shape=jax.ShapeDtypeStruct((N, RET_COLS), BF16),
        compiler_params=_params("parallel"), name="proj_ret",
    )(x, mod, lw["norm1_g"], lw["w_ret"], cos, sin)

    q, k, v = pl.pallas_call(
        _proj_att_kernel, grid=grid,
        in_specs=[x_spec, mod_spec, g_spec, _full(lw["w_att"].shape), tab_spec, tab_spec,
                  _full(lw["qk_gain"].shape), _full(tabs["ones512"].shape)],
        out_specs=[pl.BlockSpec((tm, ATT_Q_WIDTH), lambda i: (i, 0)),
                   pl.BlockSpec((ATT_KV_HEADS, tm, HEAD_DIM), lambda i: (0, i, 0)),
                   pl.BlockSpec((ATT_KV_HEADS, tm, 2 * HEAD_DIM), lambda i: (0, i, 0))],
        out_shape=[jax.ShapeDtypeStruct((N, ATT_Q_WIDTH), BF16),
                   jax.ShapeDtypeStruct((ATT_KV_HEADS, N, HEAD_DIM), BF16),
                   jax.ShapeDtypeStruct((ATT_KV_HEADS, N, 2 * HEAD_DIM), BF16)],
        compiler_params=_params("parallel"), name="proj_att",
    )(x, mod, lw["norm1_g"], lw["w_att"], cos, sin, lw["qk_gain"], tabs["ones512"])

    gates = pl.pallas_call(
        _proj_gate_kernel, grid=grid,
        in_specs=[x_spec, mod_spec, g_spec, _full(lw["w_gate"].shape)],
        out_specs=pl.BlockSpec((tm, GATE_COLS), lambda i: (i, 0)),
        out_shape=jax.ShapeDtypeStruct((N, GATE_COLS), BF16),
        compiler_params=_params("parallel"), name="proj_gate",
    )(x, mod, lw["norm1_g"], lw["w_gate"])

    tm_rw = min(256, T)
    tps_rw = T // tm_rw
    x_spec, mod_spec, g_spec = _token_specs(tm_rw, tps_rw)
    rows8 = tm_rw // 8
    last8 = N // 8 - 1
    rwp = pl.pallas_call(
        functools.partial(_proj_rw_kernel, tm=tm_rw, tiles_per_seq=tps_rw), grid=(N // tm_rw,),
        in_specs=[x_spec,
                  pl.BlockSpec((8, D_MODEL), lambda i: (jnp.maximum(i * rows8 - 1, 0), 0)),
                  pl.BlockSpec((8, D_MODEL), lambda i: (jnp.minimum((i + 1) * rows8, last8), 0)),
                  mod_spec, g_spec, _full(lw["w_rw"].shape), _full(lw["shift_mu"].shape),
                  _full(lw["rw_vec"].shape), _full(lw["rw_w2"].shape), _full(lw["rw_a2"].shape),
                  _full(lw["rw_g2"].shape), _full(tabs["ones256"].shape)],
        out_specs=pl.BlockSpec((tm_rw, RW_FIELDS * RWKV_WIDTH), lambda i: (i, 0)),
        out_shape=jax.ShapeDtypeStruct((N, RW_FIELDS * RWKV_WIDTH), F32),
        scratch_shapes=[pltpu.VMEM((tm_rw + 16, RW_COLS), F32)],
        compiler_params=_params("parallel"), name="proj_rw",
    )(x, x, x, mod, lw["norm1_g"], lw["w_rw"], lw["shift_mu"], lw["rw_vec"], lw["rw_w2"], lw["rw_a2"],
      lw["rw_g2"], tabs["ones256"])
    return ret, q, k, v, gates, rwp


def _retention_tables(reverse):
    lg = np.log1p(-np.exp2(-5.0 - np.arange(RET_HEADS, dtype=np.float64)))
    if reverse:
        lg = lg[::-1]
    C = RET_CHUNK
    pos = np.arange(C, dtype=np.float64)
    diff = pos[:, None] - pos[None, :]
    if reverse:
        dec = np.where(diff < 0, np.exp(lg[:, None, None] * np.maximum(-diff, 0.0)[None]), 0.0)
        xi = np.exp(lg[:, None] * (C - pos)[None, :])
        zeta = np.exp(lg[:, None] * pos[None, :])
    else:
        dec = np.where(diff >= 0, np.exp(lg[:, None, None] * np.maximum(diff, 0.0)[None]), 0.0)
        xi = np.exp(lg[:, None] * (pos + 1.0)[None, :])
        zeta = np.exp(lg[:, None] * (C - 1.0 - pos)[None, :])
    chunk_decay = tuple(float(v) for v in np.exp(lg * C))
    widen = lambda a: jnp.asarray(np.repeat(a.T, HEAD_DIM, axis=1), F32)
    return jnp.asarray(dec, F32), widen(xi), widen(zeta), chunk_decay


def _retention_kernel(ret_ref, dec_ref, xi_ref, zeta_ref, *rest, chunk_decay, final, chunks):
    if final:
        yf_ref, gn_ref, ones_ref, o_ref, state, y_scr = rest
    else:
        o_ref, state, y_scr = rest

    @pl.when(pl.program_id(1) == 0)
    def _():
        state[...] = jnp.zeros_like(state)

    W, C = RET_WIDTH, RET_CHUNK
    for step in range(chunks):
        cb = chunks - 1 - step if final else step
        rows = slice(cb * C, (cb + 1) * C)
        q = ret_ref[rows, 0:W]
        k = ret_ref[rows, W:2 * W]
        v = ret_ref[rows, 2 * W:3 * W]
        qx = (q.astype(F32) * xi_ref[...]).astype(BF16)
        kzt = (k.astype(F32) * zeta_ref[...]).T
        for h in range(RET_HEADS):
            sl = slice(h * HEAD_DIM, (h + 1) * HEAD_DIM)
            s = _dot_nt(q[:, sl], k[:, sl]) * dec_ref[h]
            inner = _dot(s.astype(BF16), v[:, sl])
            r_prev = state[h]
            cross = _dot(qx[:, sl], r_prev.astype(BF16))
            kv = _dot(kzt[sl, :].astype(BF16), v[:, sl])
            state[h] = r_prev * chunk_decay[h] + kv
            y_scr[rows, sl] = inner + cross
    y = y_scr[...]
    if final:
        y = _head_layer_norm(y + yf_ref[...], ones_ref[...], RET_EPS) * gn_ref[...]
        o_ref[...] = (y * ret_ref[:, 3 * W:4 * W].astype(F32)).astype(BF16)
    else:
        o_ref[...] = y


def _retention(ret, B, T, ret_gn, ones256):
    N = ret.shape[0]
    chunks = min(RET_CHUNKS_PER_STEP, T // RET_CHUNK)
    R = chunks * RET_CHUNK
    nblk = T // R
    scratch = [pltpu.VMEM((RET_HEADS, HEAD_DIM, HEAD_DIM), F32), pltpu.VMEM((R, RET_WIDTH), F32)]

    dec, xi, zeta, cd = _retention_tables(False)
    fwd_row = lambda b, c: (b * nblk + c, 0)
    yf = pl.pallas_call(
        functools.partial(_retention_kernel, chunk_decay=cd, final=False, chunks=chunks), grid=(B, nblk),
        in_specs=[pl.BlockSpec((R, RET_COLS), fwd_row), _full(dec.shape), _full(xi.shape), _full(zeta.shape)],
        out_specs=pl.BlockSpec((R, RET_WIDTH), fwd_row),
        out_shape=jax.ShapeDtypeStruct((N, RET_WIDTH), F32),
        scratch_shapes=scratch, compiler_params=_params("parallel", "arbitrary"), name="retention_fwd",
    )(ret, dec, xi, zeta)

    dec, xi, zeta, cd = _retention_tables(True)
    bwd_row = lambda b, c: (b * nblk + nblk - 1 - c, 0)
    return pl.pallas_call(
        functools.partial(_retention_kernel, chunk_decay=cd, final=True, chunks=chunks), grid=(B, nblk),
        in_specs=[pl.BlockSpec((R, RET_COLS), bwd_row), _full(dec.shape), _full(xi.shape), _full(zeta.shape),
                  pl.BlockSpec((R, RET_WIDTH), bwd_row), _full(ret_gn.shape), _full(ones256.shape)],
        out_specs=pl.BlockSpec((R, RET_WIDTH), bwd_row),
        out_shape=jax.ShapeDtypeStruct((N, RET_WIDTH), BF16),
        scratch_shapes=scratch, compiler_params=_params("parallel", "arbitrary"), name="retention_bwd",
    )(ret, dec, xi, zeta, yf, ret_gn, ones256)


def _flash_kernel(q_ref, k_ref, v_ref, o_ref, q_scr, m_scr, acc_scr, *, tq, tk):
    j = pl.program_id(3)

    @pl.when(j == 0)
    def _():
        for g in range(ATT_GROUP):
            q_scr[g * tq:(g + 1) * tq, :] = q_ref[:, g * HEAD_DIM:(g + 1) * HEAD_DIM]
        m_scr[...] = jnp.full_like(m_scr, -jnp.inf)
        acc_scr[...] = jnp.zeros_like(acc_scr)

    k = k_ref[...]
    v = v_ref[...]
    sub = min(FLASH_SUB_ROWS, ATT_GROUP * tq)
    for r0 in range(0, ATT_GROUP * tq, sub):
        rows = slice(r0, r0 + sub)
        s = _dot_nt(q_scr[rows, :], k)
        m_prev = m_scr[rows, :]
        m_next = jnp.maximum(m_prev, jnp.max(s, axis=1, keepdims=True))
        p = jnp.exp(s - jnp.concatenate([m_next] * (tk // V7X_LANES), axis=1))
        alpha = jnp.exp(m_prev - m_next)
        m_scr[rows, :] = m_next
        acc_scr[rows, :] = alpha * acc_scr[rows, :] + _dot(p.astype(BF16), v)

    @pl.when(j == pl.num_programs(3) - 1)
    def _():
        o = acc_scr[:, 0:HEAD_DIM] / acc_scr[:, HEAD_DIM:2 * HEAD_DIM]
        for g in range(ATT_GROUP):
            o_ref[:, g * HEAD_DIM:(g + 1) * HEAD_DIM] = o[g * tq:(g + 1) * tq, :].astype(BF16)


def _attention(q, k, v, B, T):
    N = q.shape[0]
    tq = min(FLASH_Q_ROWS, T)
    tk = min(FLASH_KV_ROWS, T)
    nq, nk = T // tq, T // tk
    GW = ATT_GROUP * HEAD_DIM
    rows = ATT_GROUP * tq
    return pl.pallas_call(
        functools.partial(_flash_kernel, tq=tq, tk=tk), grid=(B, ATT_KV_HEADS, nq, nk),
        in_specs=[pl.BlockSpec((tq, GW), lambda b, h, i, j: (b * nq + i, h)),
                  pl.BlockSpec((None, tk, HEAD_DIM), lambda b, h, i, j: (h, b * nk + j, 0)),
                  pl.BlockSpec((None, tk, 2 * HEAD_DIM), lambda b, h, i, j: (h, b * nk + j, 0))],
        out_specs=pl.BlockSpec((tq, GW), lambda b, h, i, j: (b * nq + i, h)),
        out_shape=jax.ShapeDtypeStruct((N, ATT_Q_WIDTH), BF16),
        scratch_shapes=[pltpu.VMEM((rows, HEAD_DIM), BF16), pltpu.VMEM((rows, V7X_LANES), F32),
                        pltpu.VMEM((rows, 2 * HEAD_DIM), F32)],
        compiler_params=_params("parallel", "parallel", "parallel", "arbitrary"), name="flash_attention",
    )(q, k, v)


def _rwkv_scan_kernel(*refs, rows_per_set):
    C = RW_CHUNK
    H = C // 2
    W = RWKV_WIDTH
    SUB = 8
    n_sets = len(rows_per_set)
    ones_ref = refs[12 * n_sets]
    outs = refs[12 * n_sets + 1:12 * n_sets + 1 + 2 * n_sets]
    state, vk_scr = refs[-2:]
    dirs = [(refs[12 * s:12 * s + 6], refs[12 * s + 6:12 * s + 12]) for s in range(n_sets)]
    chains = [(s, n, d) for s in range(n_sets) for n in range(rows_per_set[s]) for d in range(2)]

    first_chunk = pl.program_id(1) == 0

    @pl.when(first_chunk)
    def _():
        for c, (s, n, d) in enumerate(chains):
            if s == 0:
                state[c] = jnp.zeros((HEAD_DIM, W), F32)

    @pl.when(jnp.logical_and(first_chunk, pl.program_id(0) == 0))
    def _():
        for c, (s, n, d) in enumerate(chains):
            if s > 0:
                state[c] = jnp.zeros((HEAD_DIM, W), F32)

    ones = ones_ref[...]
    same_head = (lax.broadcasted_iota(jnp.int32, (W, W), 0) // HEAD_DIM
                 == lax.broadcasted_iota(jnp.int32, (W, W), 1) // HEAD_DIM)
    step_iota = lax.broadcasted_iota(jnp.int32, (H, HEAD_DIM, W), 0)
    lane_iota = lax.broadcasted_iota(jnp.int32, (H, HEAD_DIM, W), 2) & (HEAD_DIM - 1)

    def head_sums(vals):
        stacked = jnp.concatenate([v.astype(BF16) for v in vals], axis=0)
        out = _dot(stacked, ones)
        return [out[c * HEAD_DIM:(c + 1) * HEAD_DIM, :] for c in range(len(vals))]

    own_head = (lax.broadcasted_iota(jnp.int32, (SUB, W), 0)
                == lax.broadcasted_iota(jnp.int32, (SUB, W), 1) // HEAD_DIM)
    carry = tuple(state[c] for c in range(len(chains)))
    for half in range(2):
        lo = (half * H, (1 - half) * H)
        for c, (s, n, d) in enumerate(chains):
            kc = dirs[s][d][3][n]
            kbd = jnp.where(same_head, jnp.concatenate([kc] * (W // C), axis=0), 0.0).astype(BF16)
            sel = jnp.where(step_iota + lo[d] == lane_iota, dirs[s][d][5][n][None], 0.0).reshape(H * HEAD_DIM, W)
            vk_scr[c] = _dot(sel.astype(BF16), kbd).reshape(H, HEAD_DIM, W)

        def group(g, states, lo=lo):
            states = list(states)
            bases = (lo[0] + g * SUB, lo[1] + (H // SUB - 1 - g) * SUB)
            rows = [[ref[n, pl.ds(pl.multiple_of(bases[d], SUB), SUB), :] for ref in dirs[s][d][:5]]
                    for (s, n, d) in chains]
            ys = [[None] * SUB for _ in chains]
            for j in range(SUB):
                jj = (j, SUB - 1 - j)
                row = lambda c, f: rows[c][f][jj[chains[c][2]]:jj[chains[c][2]] + 1, :]
                sa = head_sums([states[c] * row(c, 0) for c in range(len(chains))])
                new = [states[c] * row(c, 2) - sa[c] * row(c, 4)
                       + vk_scr[c, bases[chains[c][2]] + jj[chains[c][2]] - lo[chains[c][2]]]
                       for c in range(len(chains))]
                for c, (s, n, d) in enumerate(chains):
                    read = new[c] if d == 0 else states[c]
                    r_heads = jnp.where(own_head, row(c, 1), 0.0).astype(BF16)
                    ys[c][jj[d]] = _dot_nt(r_heads, read.astype(BF16))
                states = new
            for c, (s, n, d) in enumerate(chains):
                span = pl.ds(pl.multiple_of(bases[d], SUB), SUB)
                for h in range(RWKV_HEADS):
                    col = jnp.concatenate([ys[c][r][h:h + 1, :] for r in range(SUB)], axis=0)
                    outs[2 * s + d][n, span, h * HEAD_DIM:(h + 1) * HEAD_DIM] = col
            return tuple(states)

        carry = lax.fori_loop(0, H // SUB, group, carry)

    for c in range(len(chains)):
        state[c] = carry[c]


def _rwkv_scan_call(sets, ones256):
    W, C = RWKV_WIDTH, RW_CHUNK
    _, B0, T0, nb0 = sets[0]
    nc0 = T0 // C
    operands, in_specs, out_specs, out_shapes = [], [], [], []
    for s, (rwp, B, T, nb) in enumerate(sets):
        nc = T // C
        fields = rwp.reshape(B, T, RW_FIELDS * W)
        v = rwp[:, 2 * W:3 * W].reshape(B, nc, C, RWKV_HEADS, HEAD_DIM)
        vt = v.transpose(0, 1, 4, 3, 2).reshape(B, nc, HEAD_DIM, W)

        def block(b, c, reverse, s=s, nc=nc):
            bi, ci = (b, c) if s == 0 else (0, b * nc0 + c)
            return bi, (nc - 1 - ci if reverse else ci)

        field = lambda f, rev, nb=nb, block=block: pl.BlockSpec(
            (nb, C, W), lambda b, c: block(b, c, rev) + (f,))
        vt_spec = lambda rev, nb=nb, block=block: pl.BlockSpec(
            (nb, None, HEAD_DIM, W), lambda b, c: block(b, c, rev) + (0, 0))
        in_specs += [field(1, False), field(0, False), field(3, False), field(5, False), field(7, False),
                     vt_spec(False),
                     field(1, True), field(0, True), field(4, True), field(6, True), field(8, True),
                     vt_spec(True)]
        operands += [fields] * 5 + [vt] + [fields] * 5 + [vt]
        y_spec = lambda rev, nb=nb, block=block: pl.BlockSpec(
            (nb, C, W), lambda b, c: block(b, c, rev) + (0,))
        out_specs += [y_spec(False), y_spec(True)]
        out_shapes += [jax.ShapeDtypeStruct((B, T, W), F32)] * 2
    n_chains = 2 * sum(nb for _, _, _, nb in sets)
    outs = pl.pallas_call(
        functools.partial(_rwkv_scan_kernel, rows_per_set=tuple(nb for _, _, _, nb in sets)),
        grid=(B0 // nb0, nc0), in_specs=in_specs + [_full(ones256.shape)],
        out_specs=out_specs, out_shape=out_shapes,
        scratch_shapes=[pltpu.VMEM((n_chains, HEAD_DIM, W), F32),
                        pltpu.VMEM((n_chains, C // 2, HEAD_DIM, W), F32)],
        compiler_params=_params("arbitrary", "arbitrary"), name="rwkv_scan",
    )(*operands, ones256)
    return [(outs[2 * s].reshape(B * T, W), outs[2 * s + 1].reshape(B * T, W))
            for s, (_, B, T, _) in enumerate(sets)]


def _rwkv_scan(groups, ones256):
    rows = lambda B: RW_BATCH_ROWS if B % RW_BATCH_ROWS == 0 else 1
    if len(groups) == 2:
        (_, B0, T0), (_, B1, T1) = groups
        if B1 == 1 and (B0 // rows(B0)) * (T0 // RW_CHUNK) == T1 // RW_CHUNK:
            return _rwkv_scan_call([groups[0] + (rows(B0),), groups[1] + (1,)], ones256)
    return [_rwkv_scan_call([g + (rows(g[1]),)], ones256)[0] for g in groups]


def _merge_kernel(x_ref, mod_ref, ya_ref, yb_ref, yf_ref, ybk_ref, rg_ref, bonus_ref, gn_ref, gates_ref,
                  ones_ref, wa_ref, wb_ref, wc_ref, wo_ref, o_ref):
    y = _head_layer_norm(yf_ref[...] + ybk_ref[...], ones_ref[...], RWKV_GN_EPS) * gn_ref[...]
    yc = ((y + bonus_ref[...]) * rg_ref[...]).astype(BF16)
    D = D_MODEL
    merged = (gates_ref[:, 0:D].astype(F32) * _dot(ya_ref[...], wa_ref[...])
              + gates_ref[:, D:2 * D].astype(F32) * _dot(yb_ref[...], wb_ref[...])
              + gates_ref[:, 2 * D:3 * D].astype(F32) * _dot(yc, wc_ref[...]))
    o_ref[...] = x_ref[...] + mod_ref[2:3, :] * _dot(merged.astype(BF16), wo_ref[...])


def _merge(x, mod, T, ya, yb, yf, ybk, rwp, gates, lw, ones256):
    N = x.shape[0]
    tm = min(512, T)
    tps = T // tm
    x_spec, mod_spec, _ = _token_specs(tm, tps)
    row = lambda w: pl.BlockSpec((tm, w), lambda i: (i, 0))
    field = lambda f: pl.BlockSpec((tm, RWKV_WIDTH), lambda i: (i, f))
    return pl.pallas_call(
        _merge_kernel, grid=(N // tm,),
        in_specs=[x_spec, mod_spec, row(RET_WIDTH), row(ATT_Q_WIDTH), row(RWKV_WIDTH), row(RWKV_WIDTH),
                  field(9), field(10), _full(lw["rw_gn"].shape), row(GATE_COLS), _full(ones256.shape),
                  _full(lw["w_branch_a"].shape), _full(lw["w_branch_b"].shape), _full(lw["w_branch_c"].shape),
                  _full(lw["w_out"].shape)],
        out_specs=x_spec, out_shape=jax.ShapeDtypeStruct((N, D_MODEL), F32),
        compiler_params=_params("parallel"), name="merge_out",
    )(x, mod, ya, yb, yf, ybk, rwp, rwp, lw["rw_gn"], gates, ones256,
      lw["w_branch_a"], lw["w_branch_b"], lw["w_branch_c"], lw["w_out"])


def _route(h, hb, rwt_ref, rbias_ref, tm):
    h_lo = (h - hb.astype(F32)).astype(BF16)
    rw_hi, rw_lo = _split(rwt_ref[...])
    logits = _dot_nt(rw_hi, hb) + _dot_nt(rw_hi, h_lo) + _dot_nt(rw_lo, hb)
    scores = _sigmoid(logits)
    choice = scores + jnp.concatenate([rbias_ref[...]] * (tm // V7X_LANES), axis=1)
    per_group = N_EXPERTS // N_GROUPS
    sub = lax.broadcasted_iota(jnp.int32, (per_group, tm), 0)
    groups, gscore = [], []
    for g in range(N_GROUPS):
        cg = choice[g * per_group:(g + 1) * per_group, :]
        m1 = jnp.max(cg, axis=0, keepdims=True)
        first = jnp.min(jnp.where(cg == m1, sub, per_group), axis=0, keepdims=True)
        m2 = jnp.max(jnp.where(sub == first, -jnp.inf, cg), axis=0, keepdims=True)
        groups.append(cg)
        gscore.append(m1 + m2)
    masked = []
    for g in range(N_GROUPS):
        beaten = jnp.zeros((1, tm), F32)
        for o in range(N_GROUPS):
            if o == g:
                continue
            wins = (gscore[o] >= gscore[g]) if o < g else (gscore[o] > gscore[g])
            beaten = beaten + jnp.where(wins, 1.0, 0.0)
        keep = jnp.where(beaten < TOPK_GROUPS, 1.0, 0.0)
        masked.append(jnp.where(jnp.broadcast_to(keep, (per_group, tm)) > 0.5, groups[g], -jnp.inf))
    mc = jnp.concatenate(masked, axis=0)
    eidx = lax.broadcasted_iota(jnp.int32, (N_EXPERTS, tm), 0).astype(F32)
    sel = jnp.zeros((N_EXPERTS, tm), F32)
    for _ in range(TOP_K):
        top = jnp.max(mc, axis=0, keepdims=True)
        first = jnp.min(jnp.where(mc == top, eidx, float(N_EXPERTS)), axis=0, keepdims=True)
        pick = eidx == first
        sel = jnp.where(pick, 1.0, sel)
        mc = jnp.where(pick, -jnp.inf, mc)
    wts = jnp.where(sel > 0.5, scores, 0.0)
    return wts / jnp.sum(wts, axis=0, keepdims=True) * ROUTED_SCALE, sel


def _selected_lists(comb, sel, rank):
    eidx = lax.broadcasted_iota(jnp.int32, comb.shape, 0).astype(F32)
    prev = jnp.full((1, comb.shape[1]), -1.0, F32)
    ids, wts, pos = [], [], []
    for _ in range(TOP_K):
        cand = jnp.where(sel > 0.5, jnp.where(eidx > prev, eidx, float(N_EXPERTS)), float(N_EXPERTS))
        prev = jnp.min(cand, axis=0, keepdims=True)
        wts.append(jnp.sum(jnp.where(eidx == prev, comb, 0.0), axis=0, keepdims=True))
        pos.append(jnp.sum(jnp.where(eidx == prev, rank, 0.0), axis=0, keepdims=True))
        ids.append(jnp.minimum(prev, N_EXPERTS - 1.0))
    as_int = lambda rows: jnp.concatenate(rows, axis=0).astype(jnp.int32)
    return as_int(ids), jnp.concatenate(wts, axis=0), as_int(pos)


def _pack_halves(y):
    n = y.shape[1] // 2
    hi = pltpu.bitcast(y[:, :n].astype(BF16).astype(F32), jnp.int32)
    lo = pltpu.bitcast(y[:, n:].astype(BF16).astype(F32), jnp.int32)
    return hi | lax.shift_right_logical(lo, 16)


def _unpack_halves(w):
    hi = pltpu.bitcast(w & jnp.int32(-65536), F32)
    lo = pltpu.bitcast(lax.shift_left(w, 16), F32)
    return hi, lo


def _moe_route_kernel(x_ref, mod_ref, g_ref, rwt_ref, rbias_ref, shgu_ref, shd_ref, tri_ref,
                      hp_ref, ids_ref, wts_ref, pos_ref, base_ref, count_ref, *, tm):
    F = EXPERT_FF

    @pl.when(pl.program_id(0) == 0)
    def _():
        count_ref[...] = jnp.zeros_like(count_ref)

    x = x_ref[...]
    h = _norm_mod(x, g_ref[...], mod_ref[3:4, :], mod_ref[4:5, :])
    hb = h.astype(BF16)
    hp_ref[...] = _pack_halves(hb.astype(F32))
    comb, sel = _route(h, hb, rwt_ref, rbias_ref, tm)
    before = count_ref[...]
    inclusive = _dot(sel.astype(BF16), tri_ref[...])
    rank = inclusive - sel + jnp.concatenate([before] * (tm // V7X_LANES), axis=1)
    count_ref[...] = before + jnp.sum(sel, axis=1, keepdims=True)
    ids, wts, pos = _selected_lists(comb, sel, rank)
    ids_ref[...] = ids
    wts_ref[...] = wts
    pos_ref[...] = pos
    gu = _dot(hb, shgu_ref[...])
    shared = _dot((_silu(gu[:, 0:F]) * gu[:, F:2 * F]).astype(BF16), shd_ref[...])
    base_ref[...] = x + mod_ref[5:6, :] * shared


def _moe_expert_kernel(tile_expert_ref, xs_ref, wgu_ref, wd_ref, o_ref):
    del tile_expert_ref
    F = EXPERT_FF
    half = D_MODEL // 2
    hi, lo = _unpack_halves(xs_ref[...])
    gu = _dot(hi.astype(BF16), wgu_ref[0:half, :]) + _dot(lo.astype(BF16), wgu_ref[half:, :])
    act = (_silu(gu[:, 0:F]) * gu[:, F:2 * F]).astype(BF16)
    o_ref[...] = _pack_halves(_dot(act, wd_ref[...]))


def _moe_combine_kernel(base_ref, mod_ref, y_ref, w_ref, fin_ref, o_ref, *, final):
    half = D_MODEL // 2
    acc_hi = jnp.zeros((base_ref.shape[0], half), F32)
    acc_lo = jnp.zeros((base_ref.shape[0], half), F32)
    for k in range(TOP_K):
        hi, lo = _unpack_halves(y_ref[k])
        wk = w_ref[:, k:k + 1]
        acc_hi = acc_hi + wk * hi
        acc_lo = acc_lo + wk * lo
    out_hi = base_ref[:, 0:half] + mod_ref[5:6, 0:half] * acc_hi
    out_lo = base_ref[:, half:] + mod_ref[5:6, half:] * acc_lo
    if final:
        ms = (jnp.sum(out_hi * out_hi, axis=-1, keepdims=True)
              + jnp.sum(out_lo * out_lo, axis=-1, keepdims=True)) * (1.0 / D_MODEL)
        scale = lax.rsqrt(ms + NORM_EPS)
        out_hi = out_hi * scale * fin_ref[:, 0:half]
        out_lo = out_lo * scale * fin_ref[:, half:]
    o_ref[:, 0:half] = out_hi
    o_ref[:, half:] = out_lo


def _gather_rows(table, idx):
    rows, width = idx.shape[0], table.shape[1]
    info = plsc.get_sparse_core_info()
    workers = info.num_cores * info.num_subcores
    windows = rows // (workers * SC_GATHER_ROWS)
    assert windows * workers * SC_GATHER_ROWS == rows
    mesh = plsc.VectorSubcoreMesh(core_axis_name="core", subcore_axis_name="subcore")

    @functools.partial(pl.kernel, out_type=jax.ShapeDtypeStruct((rows, width), table.dtype), mesh=mesh,
                       scratch_types=[pltpu.VMEM((SC_GATHER_ROWS,), jnp.int32),
                                      pltpu.VMEM((SC_GATHER_ROWS, width), table.dtype)],
                       name="moe_gather")
    def gather(table_hbm, idx_hbm, out_hbm, idx_vmem, rows_vmem):
        worker = lax.axis_index("subcore") * info.num_cores + lax.axis_index("core")
        base = worker * (windows * SC_GATHER_ROWS)

        @pl.loop(0, windows)
        def _(j):
            span = pl.ds(pl.multiple_of(base + j * SC_GATHER_ROWS, SC_GATHER_ROWS), SC_GATHER_ROWS)
            pltpu.sync_copy(idx_hbm.at[span], idx_vmem)
            pltpu.sync_copy(table_hbm.at[idx_vmem], rows_vmem)
            pltpu.sync_copy(rows_vmem, out_hbm.at[span])

    return gather(table, idx)


def _scatter_rows(table, dest, total_rows):
    n_lists, n_rows = dest.shape
    width = table.shape[1]
    info = plsc.get_sparse_core_info()
    workers = info.num_cores * info.num_subcores
    windows = n_rows // (workers * SC_GATHER_ROWS)
    assert windows * workers * SC_GATHER_ROWS == n_rows
    windows_per_list = n_rows // SC_GATHER_ROWS
    mesh = plsc.VectorSubcoreMesh(core_axis_name="core", subcore_axis_name="subcore")

    @functools.partial(pl.kernel, out_type=jax.ShapeDtypeStruct((total_rows, width), table.dtype), mesh=mesh,
                       scratch_types=[pltpu.VMEM((1, SC_GATHER_ROWS), jnp.int32),
                                      pltpu.VMEM((SC_GATHER_ROWS, width), table.dtype)],
                       name="moe_scatter")
    def scatter(table_hbm, dest_hbm, out_hbm, idx_vmem, rows_vmem):
        worker = lax.axis_index("subcore") * info.num_cores + lax.axis_index("core")

        @pl.loop(0, windows)
        def _(j):
            window = worker * windows + j
            span = pl.ds(pl.multiple_of(window * SC_GATHER_ROWS, SC_GATHER_ROWS), SC_GATHER_ROWS)
            pltpu.sync_copy(table_hbm.at[span], rows_vmem)
            for k in range(n_lists):
                pltpu.sync_copy(dest_hbm.at[pl.ds(k * windows_per_list + window, 1)], idx_vmem)
                pltpu.sync_copy(rows_vmem, out_hbm.at[idx_vmem.at[0]])

    return scatter(table, dest.reshape(n_lists * windows_per_list, SC_GATHER_ROWS))


def _dispatch_plan(ids, pos, counts, n_tokens, tg):
    E = N_EXPERTS
    padded = (counts + tg - 1) // tg * tg
    ends = jnp.cumsum(padded)
    off = ends - padded
    onehot = ids[:, :, None] == jnp.arange(E, dtype=jnp.int32)[None, None, :]
    dest = pos + jnp.sum(jnp.where(onehot, off[None, None, :], 0), axis=-1)
    total = n_tokens * TOP_K + E * tg
    tile_start = jnp.arange(total // tg, dtype=jnp.int32) * tg
    tile_expert = jnp.sum((ends[None, :] <= tile_start[:, None]).astype(jnp.int32), axis=1)
    return dest, total, jnp.minimum(tile_expert, E - 1)


def _moe(x, mod, T, lw, final_g, final):
    N = x.shape[0]
    tm = min(MOE_TOKENS, T)
    tps = T // tm
    D, F, E = D_MODEL, EXPERT_FF, N_EXPERTS
    tg = max(MOE_GROUP_ROWS_MIN, min(MOE_GROUP_ROWS_MAX, N * TOP_K // (E * 8)))
    half = D // 2
    x_spec, mod_spec, g_spec = _token_specs(tm, tps)
    list_spec = pl.BlockSpec((TOP_K, tm), lambda i: (0, i))

    tri = jnp.asarray(np.triu(np.ones((tm, tm), np.float32)), BF16)
    hp, ids, wts, pos, base, counts = pl.pallas_call(
        functools.partial(_moe_route_kernel, tm=tm), grid=(N // tm,),
        in_specs=[x_spec, mod_spec, g_spec, _full((E, D)), _full((E, V7X_LANES)), _full((D, 2 * F)), _full((F, D)),
                  _full((tm, tm))],
        out_specs=[pl.BlockSpec((tm, half), lambda i: (i, 0)), list_spec, list_spec, list_spec, x_spec,
                   _full((E, V7X_LANES))],
        out_shape=[jax.ShapeDtypeStruct((N, half), jnp.int32), jax.ShapeDtypeStruct((TOP_K, N), jnp.int32),
                   jax.ShapeDtypeStruct((TOP_K, N), F32), jax.ShapeDtypeStruct((TOP_K, N), jnp.int32),
                   jax.ShapeDtypeStruct((N, D), F32), jax.ShapeDtypeStruct((E, V7X_LANES), F32)],
        compiler_params=_params("arbitrary"), name="moe_route",
    )(x, mod, lw["norm2_g"], lw["router_wt"], lw["router_bias"], lw["sh_gu"], lw["sh_d"], tri)

    dest, total, tile_expert = _dispatch_plan(ids, pos, counts[:, 0].astype(jnp.int32), N, tg)
    xs = _scatter_rows(hp, dest, total)
    n_tiles = total // tg
    ys = pl.pallas_call(
        _moe_expert_kernel,
        grid_spec=pltpu.PrefetchScalarGridSpec(
            num_scalar_prefetch=1, grid=(n_tiles,),
            in_specs=[pl.BlockSpec((tg, half), lambda i, te: (i, 0)),
                      pl.BlockSpec((None, D, 2 * F), lambda i, te: (te[i], 0, 0)),
                      pl.BlockSpec((None, F, D), lambda i, te: (te[i], 0, 0))],
            out_specs=pl.BlockSpec((tg, half), lambda i, te: (i, 0))),
        out_shape=jax.ShapeDtypeStruct(xs.shape, jnp.int32),
        compiler_params=_params("parallel"), name="moe_experts",
    )(tile_expert, xs, lw["exp_gu"], lw["exp_d"])

    yk = _gather_rows(ys, dest.reshape(-1)).reshape(TOP_K, N, half)
    return pl.pallas_call(
        functools.partial(_moe_combine_kernel, final=final), grid=(N // tm,),
        in_specs=[x_spec, mod_spec, pl.BlockSpec((TOP_K, tm, half), lambda i: (0, i, 0)),
                  pl.BlockSpec((tm, TOP_K), lambda i: (i, 0)), g_spec],
        out_specs=x_spec, out_shape=jax.ShapeDtypeStruct((N, D), F32),
        compiler_params=_params("parallel"), name="moe_combine",
    )(base, mod, yk, wts.T, final_g)


def _block_diag2(a, b):
    za = jnp.zeros((a.shape[0], b.shape[1]), a.dtype)
    zb = jnp.zeros((b.shape[0], a.shape[1]), a.dtype)
    return jnp.concatenate([jnp.concatenate([a, za], 1), jnp.concatenate([zb, b], 1)], 0)


def _layer_weights(P, l):
    w_in = P["w_in"][l]
    o1, o2, o3 = RET_COLS, RET_COLS + ATT_COLS, RET_COLS + ATT_COLS + RW_COLS
    w_ret, w_att, w_rw, w_gate = w_in[:, :o1], w_in[:, o1:o2], w_in[:, o2:o3], w_in[:, o3:]
    qk_gain = jnp.zeros((8, ATT_Q_WIDTH), F32)
    qk_gain = qk_gain.at[0].set(jnp.tile(P["q_norm_g"][l], ATT_Q_HEADS))
    qk_gain = qk_gain.at[1, :ATT_KV_WIDTH].set(jnp.tile(P["k_norm_g"][l], ATT_KV_HEADS))
    rw_vec = jnp.zeros((8, RWKV_WIDTH), F32)
    for j, name in enumerate(("rw_w0_f", "rw_w0_b", "rw_a0_f", "rw_a0_b", "rw_k_k", "rw_k_a")):
        rw_vec = rw_vec.at[j].set(P[name][l])
    rw_vec = rw_vec.at[6].set(P["rw_r_k"][l].reshape(RWKV_WIDTH))
    row = lambda v: v.reshape(1, -1)
    return {
        "norm1_g": row(P["norm1_g"][l]), "norm2_g": row(P["norm2_g"][l]),
        "w_ret": w_ret.astype(BF16), "w_att": w_att.astype(BF16), "w_rw": w_rw.astype(BF16),
        "w_gate": w_gate.astype(BF16), "qk_gain": qk_gain, "ret_gn": row(P["ret_gn"][l]),
        "shift_mu": P["shift_mu"][l], "rw_vec": rw_vec,
        "rw_w2": _block_diag2(P["rw_w2_f"][l], P["rw_w2_b"][l]).astype(BF16),
        "rw_a2": _block_diag2(P["rw_a2_f"][l], P["rw_a2_b"][l]).astype(BF16),
        "rw_g2": P["rw_g2"][l].astype(BF16), "rw_gn": row(P["rw_gn"][l]),
        "w_branch_a": P["w_branch_a"][l].astype(BF16), "w_branch_b": P["w_branch_b"][l].astype(BF16),
        "w_branch_c": P["w_branch_c"][l].astype(BF16), "w_out": P["w_out"][l].astype(BF16),
        "router_wt": P["router_w"][l].T,
        "router_bias": jnp.broadcast_to(P["router_bias"][l][:, None], (N_EXPERTS, V7X_LANES)),
        "sh_gu": jnp.concatenate([P["sh_w_gate"][l], P["sh_w_up"][l]], axis=1).astype(BF16),
        "sh_d": P["sh_w_down"][l].astype(BF16),
        "exp_gu": jnp.concatenate([P["exp_w_gate"][l], P["exp_w_up"][l]], axis=2).astype(BF16),
        "exp_d": P["exp_w_down"][l].astype(BF16),
    }


def _ones_block_diag(n):
    idx = np.arange(n) // HEAD_DIM
    return jnp.asarray(idx[:, None] == idx[None, :], BF16)


def _tables(T):
    rows = T // GRID_W
    row = jnp.repeat(jnp.arange(rows, dtype=F32), GRID_W)
    col = jnp.tile(jnp.arange(GRID_W, dtype=F32), rows)
    freqs = ROPE_THETA ** (-jnp.arange(ROPE_PAIRS_PER_AXIS, dtype=F32) / ROPE_PAIRS_PER_AXIS)
    ang = jnp.concatenate([row[:, None] * freqs, col[:, None] * freqs], axis=-1)
    cos, sin = jnp.cos(ang), jnp.sin(ang)
    return {
        "cos": jnp.tile(jnp.concatenate([cos, cos], axis=-1), (1, ATT_Q_HEADS)),
        "sin": jnp.tile(jnp.concatenate([-sin, sin], axis=-1), (1, ATT_Q_HEADS)),
        "ones128": _ones_block_diag(128), "ones256": _ones_block_diag(256), "ones512": _ones_block_diag(512),
    }


def _trunks(requests, P, layer_weights):
    groups = []
    for x, c in requests:
        B, T, D = x.shape
        groups.append({"B": B, "T": T, "x": x.reshape(B * T, D), "tabs": _tables(T),
                       "mods": _ada_mod(c, P["ada_w"], P["ada_b"])})
    final_g = P["final_g"].reshape(1, D_MODEL)
    depth = len(layer_weights)
    for l, lw in enumerate(layer_weights):
        mixed = []
        for g in groups:
            B, T, tabs, mod = g["B"], g["T"], g["tabs"], g["mods"][l]
            ret, q, k, v, gates, rwp = _input_projections(g["x"], mod, T, lw, tabs)
            ya = _retention(ret, B, T, lw["ret_gn"], tabs["ones256"])
            yb = _attention(q, k, v, B, T)
            mixed.append((mod, ya, yb, gates, rwp))
        scans = _rwkv_scan([(m[4], g["B"], g["T"]) for m, g in zip(mixed, groups)], groups[0]["tabs"]["ones256"])
        for g, (mod, ya, yb, gates, rwp), (yf, ybk) in zip(groups, mixed, scans):
            x = _merge(g["x"], mod, g["T"], ya, yb, yf, ybk, rwp, gates, lw, g["tabs"]["ones256"])
            g["x"] = _moe(x, mod, g["T"], lw, final_g, final=(l == depth - 1))
    return [g["x"].reshape(g["B"], g["T"], D_MODEL) for g in groups]


def kernel(x_prompt, x_sample, c_prompt, c_sample, norm1_g, norm2_g, final_g, ada_w, ada_b, w_in, ret_gn, q_norm_g, k_norm_g, shift_mu, rw_w0_f, rw_w2_f, rw_w0_b, rw_w2_b, rw_a0_f, rw_a2_f, rw_a0_b, rw_a2_b, rw_g2, rw_k_k, rw_k_a, rw_r_k, rw_gn, w_branch_a, w_branch_b, w_branch_c, w_out, router_w, router_bias, exp_w_gate, exp_w_up, exp_w_down, sh_w_gate, sh_w_up, sh_w_down):
    P = {
        "norm1_g": norm1_g, "norm2_g": norm2_g, "final_g": final_g, "ada_w": ada_w, "ada_b": ada_b,
        "w_in": w_in, "ret_gn": ret_gn, "q_norm_g": q_norm_g, "k_norm_g": k_norm_g, "shift_mu": shift_mu,
        "rw_w0_f": rw_w0_f, "rw_w2_f": rw_w2_f, "rw_w0_b": rw_w0_b, "rw_w2_b": rw_w2_b,
        "rw_a0_f": rw_a0_f, "rw_a2_f": rw_a2_f, "rw_a0_b": rw_a0_b, "rw_a2_b": rw_a2_b,
        "rw_g2": rw_g2, "rw_k_k": rw_k_k, "rw_k_a": rw_k_a, "rw_r_k": rw_r_k, "rw_gn": rw_gn,
        "w_branch_a": w_branch_a, "w_branch_b": w_branch_b, "w_branch_c": w_branch_c, "w_out": w_out,
        "router_w": router_w, "router_bias": router_bias,
        "exp_w_gate": exp_w_gate, "exp_w_up": exp_w_up, "exp_w_down": exp_w_down,
        "sh_w_gate": sh_w_gate, "sh_w_up": sh_w_up, "sh_w_down": sh_w_down,
    }
    layer_weights = [_layer_weights(P, l) for l in range(w_in.shape[0])]
    y_prompt, y_sample = _trunks([(x_prompt, c_prompt), (x_sample, c_sample)], P, layer_weights)
    return (y_prompt, y_sample)
```

```python
import functools

import numpy as np
import jax
import jax.numpy as jnp
from jax import lax
from jax.experimental import pallas as pl
from jax.experimental.pallas import tpu as pltpu
from jax.experimental.pallas import tpu_sc as plsc

F32 = jnp.float32
BF16 = jnp.bfloat16

D_MODEL = 1024
DEPTH = 2
GRID_W = 64
HEAD_DIM = 64
ROPE_THETA = 10000.0
ROPE_PAIRS_PER_AXIS = HEAD_DIM // 4

RET_HEADS = 4
RET_WIDTH = RET_HEADS * HEAD_DIM
RET_CHUNK = 128
RET_CHUNKS_PER_STEP = 4
RET_EPS = 1e-5

ATT_Q_HEADS = 8
ATT_KV_HEADS = 2
ATT_GROUP = ATT_Q_HEADS // ATT_KV_HEADS
ATT_Q_WIDTH = ATT_Q_HEADS * HEAD_DIM
ATT_KV_WIDTH = ATT_KV_HEADS * HEAD_DIM

RWKV_HEADS = 4
RWKV_WIDTH = RWKV_HEADS * HEAD_DIM
DECAY_LORA = 64
AAA_LORA = 64
GATE_LORA = 128
RWKV_GN_EPS = 64e-5
RW_CHUNK = 64
RW_FIELDS = 11
RW_BATCH_ROWS = 4

FLASH_Q_ROWS = 1024
FLASH_KV_ROWS = 512
FLASH_SUB_ROWS = 1024

MOE_TOKENS = 512
MOE_GROUP_ROWS_MIN = 256
MOE_GROUP_ROWS_MAX = 1024
SC_GATHER_ROWS = 128

N_EXPERTS = 64
TOP_K = 8
N_GROUPS = 8
TOPK_GROUPS = 4
EXPERT_FF = 256
ROUTED_SCALE = 2.5
NORM_EPS = 1e-6

RET_COLS = 4 * RET_WIDTH
ATT_COLS = ATT_Q_WIDTH + 2 * ATT_KV_WIDTH
RW_COLS = 3 * RWKV_WIDTH + 2 * DECAY_LORA + 2 * AAA_LORA + GATE_LORA
GATE_COLS = 3 * D_MODEL

V7X_LANES = 128
VMEM_LIMIT_BYTES = 48 * 1024 * 1024


def _params(*dims):
    return pltpu.CompilerParams(dimension_semantics=dims, vmem_limit_bytes=VMEM_LIMIT_BYTES)


def _sigmoid(x):
    return 1.0 / (1.0 + jnp.exp(-x))


def _silu(x):
    return x * _sigmoid(x)


def _dot(a, b):
    return jnp.dot(a, b, preferred_element_type=F32)


def _dot_nt(a, b):
    return lax.dot_general(a, b, (((1,), (1,)), ((), ())), preferred_element_type=F32)


def _split(a):
    hi = a.astype(BF16)
    lo = (a - hi.astype(F32)).astype(BF16)
    return hi, lo


def _dot_split_lhs(a, b_bf16):
    hi, lo = _split(a)
    return _dot(hi, b_bf16) + _dot(lo, b_bf16)


def _dot3(a, b):
    ah, al = _split(a)
    bh, bl = _split(b)
    return _dot(ah, bh) + _dot(ah, bl) + _dot(al, bh)


def _swap_rotary_halves(x):
    half = HEAD_DIM // 2
    lane = lax.broadcasted_iota(jnp.int32, x.shape, 1) & (HEAD_DIM - 1)
    return jnp.where(lane < half, pltpu.roll(x, x.shape[1] - half, 1), pltpu.roll(x, half, 1))


def _norm_mod(x, gain, shift, scale):
    ms = jnp.mean(x * x, axis=-1, keepdims=True)
    return x * lax.rsqrt(ms + NORM_EPS) * gain * (1.0 + scale) + shift


def _head_layer_norm(y, ones_bd, eps):
    mean = _dot_split_lhs(y, ones_bd) * (1.0 / HEAD_DIM)
    yc = y - mean
    var = _dot_split_lhs(yc * yc, ones_bd) * (1.0 / HEAD_DIM)
    return yc * lax.rsqrt(var + eps)


def _ada_kernel(c_ref, w_ref, b_ref, o_ref):
    c = c_ref[...]
    o_ref[...] = _dot3(_silu(c), w_ref[...]) + b_ref[...]


def _ada_mod(c, ada_w, ada_b):
    B = c.shape[0]
    Bp = max(8, B)
    cp = jnp.zeros((Bp, D_MODEL), F32).at[:B].set(c)
    L = ada_w.shape[0]
    tn = 1536
    out = pl.pallas_call(
        _ada_kernel,
        grid=(L, 6 * D_MODEL // tn),
        in_specs=[pl.BlockSpec((Bp, D_MODEL), lambda l, j: (0, 0)),
                  pl.BlockSpec((None, D_MODEL, tn), lambda l, j: (l, 0, j)),
                  pl.BlockSpec((None, 1, tn), lambda l, j: (l, 0, j))],
        out_specs=pl.BlockSpec((None, Bp, tn), lambda l, j: (l, 0, j)),
        out_shape=jax.ShapeDtypeStruct((L, Bp, 6 * D_MODEL), F32),
        compiler_params=_params("parallel", "parallel"),
        name="ada_mod",
    )(cp, ada_w, ada_b.reshape(L, 1, 6 * D_MODEL))
    return out[:, :B].reshape(L, B, 6, D_MODEL)


def _proj_ret_kernel(x_ref, mod_ref, g_ref, w_ref, cos_ref, sin_ref, o_ref):
    h = _norm_mod(x_ref[...], g_ref[...], mod_ref[0:1, :], mod_ref[1:2, :]).astype(BF16)
    p = _dot(h, w_ref[...])
    W = RET_WIDTH
    cos = cos_ref[:, 0:W]
    sin = sin_ref[:, 0:W]
    q, k = p[:, 0:W], p[:, W:2 * W]
    o_ref[:, 0:W] = (q * cos + _swap_rotary_halves(q) * sin).astype(BF16)
    o_ref[:, W:2 * W] = ((k * cos + _swap_rotary_halves(k) * sin) * (HEAD_DIM ** -0.5)).astype(BF16)
    o_ref[:, 2 * W:3 * W] = p[:, 2 * W:3 * W].astype(BF16)
    o_ref[:, 3 * W:4 * W] = _silu(p[:, 3 * W:4 * W]).astype(BF16)


def _proj_att_kernel(x_ref, mod_ref, g_ref, w_ref, cos_ref, sin_ref, qkg_ref, ones_ref, q_ref, k_ref, v_ref):
    h = _norm_mod(x_ref[...], g_ref[...], mod_ref[0:1, :], mod_ref[1:2, :]).astype(BF16)
    p = _dot(h, w_ref[...])
    QW, KW = ATT_Q_WIDTH, ATT_KV_WIDTH
    q = p[:, 0:QW]
    k = p[:, QW:QW + KW]
    v = p[:, QW + KW:QW + 2 * KW]
    ones = ones_ref[...]
    rq = lax.rsqrt(_dot_split_lhs(q * q, ones) * (1.0 / HEAD_DIM) + NORM_EPS)
    rk = lax.rsqrt(_dot_split_lhs(k * k, ones[0:KW, 0:KW]) * (1.0 / HEAD_DIM) + NORM_EPS)
    cos = cos_ref[...]
    sin = sin_ref[...]
    qg = q * qkg_ref[0:1, :]
    kg = k * qkg_ref[1:2, 0:KW]
    qo = (qg * cos + _swap_rotary_halves(qg) * sin) * (rq * (HEAD_DIM ** -0.5))
    ko = (kg * cos[:, 0:KW] + _swap_rotary_halves(kg) * sin[:, 0:KW]) * rk
    q_ref[...] = qo.astype(BF16)
    kb = ko.astype(BF16)
    vb = v.astype(BF16)
    for hk in range(ATT_KV_HEADS):
        k_ref[hk] = kb[:, hk * HEAD_DIM:(hk + 1) * HEAD_DIM]
        v_ref[hk, :, 0:HEAD_DIM] = vb[:, hk * HEAD_DIM:(hk + 1) * HEAD_DIM]
        v_ref[hk, :, HEAD_DIM:2 * HEAD_DIM] = jnp.ones((vb.shape[0], HEAD_DIM), BF16)


def _proj_gate_kernel(x_ref, mod_ref, g_ref, w_ref, o_ref):
    h = _norm_mod(x_ref[...], g_ref[...], mod_ref[0:1, :], mod_ref[1:2, :]).astype(BF16)
    o_ref[...] = _sigmoid(_dot(h, w_ref[...])).astype(BF16)


def _proj_rw_kernel(x_ref, xp_ref, xn_ref, mod_ref, g_ref, w_ref, mu_ref, vec_ref, w2_ref, a2_ref, g2_ref,
                    ones_ref, o_ref, p_scr, *, tm, tiles_per_seq):
    i = pl.program_id(0)
    gain, shift, scale = g_ref[...], mod_ref[0:1, :], mod_ref[1:2, :]
    w = w_ref[...]
    h = _norm_mod(x_ref[...], gain, shift, scale).astype(BF16)
    p = _dot(h, w)
    hp = _norm_mod(xp_ref[...], gain, shift, scale).astype(BF16)
    hn = _norm_mod(xn_ref[...], gain, shift, scale).astype(BF16)
    first = (i % tiles_per_seq) == 0
    last = (i % tiles_per_seq) == tiles_per_seq - 1
    prev_row = jnp.where(first, 0.0, _dot(hp, w)[7:8, :])
    next_row = jnp.where(last, 0.0, _dot(hn, w)[0:1, :])
    p_scr[8:8 + tm, :] = p
    p_scr[7:8, :] = prev_row
    p_scr[8 + tm:9 + tm, :] = next_row
    prev = p_scr[7:7 + tm, :]
    nxt = p_scr[9:9 + tm, :]
    x = p + mu_ref[0:1, :] * (prev - p) + mu_ref[1:2, :] * (nxt - p)

    W = RWKV_WIDTH
    r, k, v = x[:, 0:W], x[:, W:2 * W], x[:, 2 * W:3 * W]
    xw = x[:, 3 * W:3 * W + 2 * DECAY_LORA]
    xa = x[:, 3 * W + 2 * DECAY_LORA:3 * W + 2 * DECAY_LORA + 2 * AAA_LORA]
    xg = x[:, 3 * W + 2 * DECAY_LORA + 2 * AAA_LORA:]
    w0_f, w0_b = vec_ref[0:1, :], vec_ref[1:2, :]
    a0_f, a0_b = vec_ref[2:3, :], vec_ref[3:4, :]
    k_k, k_a, r_k = vec_ref[4:5, :], vec_ref[5:6, :], vec_ref[6:7, :]
    wl = _dot(jnp.tanh(xw).astype(BF16), w2_ref[...])
    al = _dot(xa.astype(BF16), a2_ref[...])
    dec_c = float(np.exp(-0.5))
    w_f = jnp.exp(-dec_c * _sigmoid(w0_f + wl[:, 0:W]))
    w_b = jnp.exp(-dec_c * _sigmoid(w0_b + wl[:, W:2 * W]))
    a_f = _sigmoid(a0_f + al[:, 0:W])
    a_b = _sigmoid(a0_b + al[:, W:2 * W])
    gate = _dot(_sigmoid(xg).astype(BF16), g2_ref[...])
    ones = ones_ref[...]
    kk = k * k_k
    kk = kk * lax.rsqrt(_dot_split_lhs(kk * kk, ones) + 1e-12)
    k_f = k * (1.0 + (a_f - 1.0) * k_a)
    k_b = k * (1.0 + (a_b - 1.0) * k_a)
    bonus = _dot_split_lhs(r * k_f * r_k, ones) * v
    for j, val in enumerate((r, kk, v, w_f, w_b, k_f, k_b, kk * a_f, kk * a_b, gate, bonus)):
        o_ref[:, j * W:(j + 1) * W] = val


def _token_specs(tm, tiles_per_seq):
    x_spec = pl.BlockSpec((tm, D_MODEL), lambda i: (i, 0))
    mod_spec = pl.BlockSpec((None, 6, D_MODEL), lambda i: (i // tiles_per_seq, 0, 0))
    g_spec = pl.BlockSpec((1, D_MODEL), lambda i: (0, 0))
    return x_spec, mod_spec, g_spec


def _full(shape):
    nd = len(shape)
    return pl.BlockSpec(shape, lambda *_: (0,) * nd)


def _input_projections(x, mod, T, lw, tabs):
    N = x.shape[0]
    tm = min(512, T)
    tps = T // tm
    grid = (N // tm,)
    x_spec, mod_spec, g_spec = _token_specs(tm, tps)
    tab_spec = pl.BlockSpec((tm, ATT_Q_WIDTH), lambda i: (i % tps, 0))
    cos, sin = tabs["cos"], tabs["sin"]

    ret = pl.pallas_call(
        _proj_ret_kernel, grid=grid,
        in_specs=[x_spec, mod_spec, g_spec, _full(lw["w_ret"].shape), tab_spec, tab_spec],
        out_specs=pl.BlockSpec((tm, RET_COLS), lambda i: (i, 0)),
        out_shape=jax.ShapeDtypeStruct((N, RET_COLS), BF16),
        compiler_params=_params("parallel"), name="proj_ret",
    )(x, mod, lw["norm1_g"], lw["w_ret"], cos, sin)

    q, k, v = pl.pallas_call(
        _proj_att_kernel, grid=grid,
        in_specs=[x_spec, mod_spec, g_spec, _full(lw["w_att"].shape), tab_spec, tab_spec,
                  _full(lw["qk_gain"].shape), _full(tabs["ones512"].shape)],
        out_specs=[pl.BlockSpec((tm, ATT_Q_WIDTH), lambda i: (i, 0)),
                   pl.BlockSpec((ATT_KV_HEADS, tm, HEAD_DIM), lambda i: (0, i, 0)),
                   pl.BlockSpec((ATT_KV_HEADS, tm, 2 * HEAD_DIM), lambda i: (0, i, 0))],
        out_shape=[jax.ShapeDtypeStruct((N, ATT_Q_WIDTH), BF16),
                   jax.ShapeDtypeStruct((ATT_KV_HEADS, N, HEAD_DIM), BF16),
                   jax.ShapeDtypeStruct((ATT_KV_HEADS, N, 2 * HEAD_DIM), BF16)],
        compiler_params=_params("parallel"), name="proj_att",
    )(x, mod, lw["norm1_g"], lw["w_att"], cos, sin, lw["qk_gain"], tabs["ones512"])

    gates = pl.pallas_call(
        _proj_gate_kernel, grid=grid,
        in_specs=[x_spec, mod_spec, g_spec, _full(lw["w_gate"].shape)],
        out_specs=pl.BlockSpec((tm, GATE_COLS), lambda i: (i, 0)),
        out_shape=jax.ShapeDtypeStruct((N, GATE_COLS), BF16),
        compiler_params=_params("parallel"), name="proj_gate",
    )(x, mod, lw["norm1_g"], lw["w_gate"])

    tm_rw = min(256, T)
    tps_rw = T // tm_rw
    x_spec, mod_spec, g_spec = _token_specs(tm_rw, tps_rw)
    rows8 = tm_rw // 8
    last8 = N // 8 - 1
    rwp = pl.pallas_call(
        functools.partial(_proj_rw_kernel, tm=tm_rw, tiles_per_seq=tps_rw), grid=(N // tm_rw,),
        in_specs=[x_spec,
                  pl.BlockSpec((8, D_MODEL), lambda i: (jnp.maximum(i * rows8 - 1, 0), 0)),
                  pl.BlockSpec((8, D_MODEL), lambda i: (jnp.minimum((i + 1) * rows8, last8), 0)),
                  mod_spec, g_spec, _full(lw["w_rw"].shape), _full(lw["shift_mu"].shape),
                  _full(lw["rw_vec"].shape), _full(lw["rw_w2"].shape), _full(lw["rw_a2"].shape),
                  _full(lw["rw_g2"].shape), _full(tabs["ones256"].shape)],
        out_specs=pl.BlockSpec((tm_rw, RW_FIELDS * RWKV_WIDTH), lambda i: (i, 0)),
        out_shape=jax.ShapeDtypeStruct((N, RW_FIELDS * RWKV_WIDTH), F32),
        scratch_shapes=[pltpu.VMEM((tm_rw + 16, RW_COLS), F32)],
        compiler_params=_params("parallel"), name="proj_rw",
    )(x, x, x, mod, lw["norm1_g"], lw["w_rw"], lw["shift_mu"], lw["rw_vec"], lw["rw_w2"], lw["rw_a2"],
      lw["rw_g2"], tabs["ones256"])
    return ret, q, k, v, gates, rwp


def _retention_tables(reverse):
    lg = np.log1p(-np.exp2(-5.0 - np.arange(RET_HEADS, dtype=np.float64)))
    if reverse:
        lg = lg[::-1]
    C = RET_CHUNK
    pos = np.arange(C, dtype=np.float64)
    diff = pos[:, None] - pos[None, :]
    if reverse:
        dec = np.where(diff < 0, np.exp(lg[:, None, None] * np.maximum(-diff, 0.0)[None]), 0.0)
        xi = np.exp(lg[:, None] * (C - pos)[None, :])
        zeta = np.exp(lg[:, None] * pos[None, :])
    else:
        dec = np.where(diff >= 0, np.exp(lg[:, None, None] * np.maximum(diff, 0.0)[None]), 0.0)
        xi = np.exp(lg[:, None] * (pos + 1.0)[None, :])
        zeta = np.exp(lg[:, None] * (C - 1.0 - pos)[None, :])
    chunk_decay = tuple(float(v) for v in np.exp(lg * C))
    widen = lambda a: jnp.asarray(np.repeat(a.T, HEAD_DIM, axis=1), F32)
    return jnp.asarray(dec, F32), widen(xi), widen(zeta), chunk_decay


def _retention_kernel(ret_ref, dec_ref, xi_ref, zeta_ref, *rest, chunk_decay, final, chunks):
    if final:
        yf_ref, gn_ref, ones_ref, o_ref, state, y_scr = rest
    else:
        o_ref, state, y_scr = rest

    @pl.when(pl.program_id(1) == 0)
    def _():
        state[...] = jnp.zeros_like(state)

    W, C = RET_WIDTH, RET_CHUNK
    for step in range(chunks):
        cb = chunks - 1 - step if final else step
        rows = slice(cb * C, (cb + 1) * C)
        q = ret_ref[rows, 0:W]
        k = ret_ref[rows, W:2 * W]
        v = ret_ref[rows, 2 * W:3 * W]
        qx = (q.astype(F32) * xi_ref[...]).astype(BF16)
        kzt = (k.astype(F32) * zeta_ref[...]).T
        for h in range(RET_HEADS):
            sl = slice(h * HEAD_DIM, (h + 1) * HEAD_DIM)
            s = _dot_nt(q[:, sl], k[:, sl]) * dec_ref[h]
            inner = _dot(s.astype(BF16), v[:, sl])
            r_prev = state[h]
            cross = _dot(qx[:, sl], r_prev.astype(BF16))
            kv = _dot(kzt[sl, :].astype(BF16), v[:, sl])
            state[h] = r_prev * chunk_decay[h] + kv
            y_scr[rows, sl] = inner + cross
    y = y_scr[...]
    if final:
        y = _head_layer_norm(y + yf_ref[...], ones_ref[...], RET_EPS) * gn_ref[...]
        o_ref[...] = (y * ret_ref[:, 3 * W:4 * W].astype(F32)).astype(BF16)
    else:
        o_ref[...] = y


def _retention(ret, B, T, ret_gn, ones256):
    N = ret.shape[0]
    chunks = min(RET_CHUNKS_PER_STEP, T // RET_CHUNK)
    R = chunks * RET_CHUNK
    nblk = T // R
    scratch = [pltpu.VMEM((RET_HEADS, HEAD_DIM, HEAD_DIM), F32), pltpu.VMEM((R, RET_WIDTH), F32)]

    dec, xi, zeta, cd = _retention_tables(False)
    fwd_row = lambda b, c: (b * nblk + c, 0)
    yf = pl.pallas_call(
        functools.partial(_retention_kernel, chunk_decay=cd, final=False, chunks=chunks), grid=(B, nblk),
        in_specs=[pl.BlockSpec((R, RET_COLS), fwd_row), _full(dec.shape), _full(xi.shape), _full(zeta.shape)],
        out_specs=pl.BlockSpec((R, RET_WIDTH), fwd_row),
        out_shape=jax.ShapeDtypeStruct((N, RET_WIDTH), F32),
        scratch_shapes=scratch, compiler_params=_params("parallel", "arbitrary"), name="retention_fwd",
    )(ret, dec, xi, zeta)

    dec, xi, zeta, cd = _retention_tables(True)
    bwd_row = lambda b, c: (b * nblk + nblk - 1 - c, 0)
    return pl.pallas_call(
        functools.partial(_retention_kernel, chunk_decay=cd, final=True, chunks=chunks), grid=(B, nblk),
        in_specs=[pl.BlockSpec((R, RET_COLS), bwd_row), _full(dec.shape), _full(xi.shape), _full(zeta.shape),
                  pl.BlockSpec((R, RET_WIDTH), bwd_row), _full(ret_gn.shape), _full(ones256.shape)],
        out_specs=pl.BlockSpec((R, RET_WIDTH), bwd_row),
        out_shape=jax.ShapeDtypeStruct((N, RET_WIDTH), BF16),
        scratch_shapes=scratch, compiler_params=_params("parallel", "arbitrary"), name="retention_bwd",
    )(ret, dec, xi, zeta, yf, ret_gn, ones256)


def _flash_kernel(q_ref, k_ref, v_ref, o_ref, q_scr, m_scr, acc_scr, *, tq, tk):
    j = pl.program_id(3)

    @pl.when(j == 0)
    def _():
        for g in range(ATT_GROUP):
            q_scr[g * tq:(g + 1) * tq, :] = q_ref[:, g * HEAD_DIM:(g + 1) * HEAD_DIM]
        m_scr[...] = jnp.full_like(m_scr, -jnp.inf)
        acc_scr[...] = jnp.zeros_like(acc_scr)

    k = k_ref[...]
    v = v_ref[...]
    sub = min(FLASH_SUB_ROWS, ATT_GROUP * tq)
    for r0 in range(0, ATT_GROUP * tq, sub):
        rows = slice(r0, r0 + sub)
        s = _dot_nt(q_scr[rows, :], k)
        m_prev = m_scr[rows, :]
        m_next = jnp.maximum(m_prev, jnp.max(s, axis=1, keepdims=True))
        p = jnp.exp(s - jnp.concatenate([m_next] * (tk // V7X_LANES), axis=1))
        alpha = jnp.exp(m_prev - m_next)
        m_scr[rows, :] = m_next
        acc_scr[rows, :] = alpha * acc_scr[rows, :] + _dot(p.astype(BF16), v)

    @pl.when(j == pl.num_programs(3) - 1)
    def _():
        o = acc_scr[:, 0:HEAD_DIM] / acc_scr[:, HEAD_DIM:2 * HEAD_DIM]
        for g in range(ATT_GROUP):
            o_ref[:, g * HEAD_DIM:(g + 1) * HEAD_DIM] = o[g * tq:(g + 1) * tq, :].astype(BF16)


def _attention(q, k, v, B, T):
    N = q.shape[0]
    tq = min(FLASH_Q_ROWS, T)
    tk = min(FLASH_KV_ROWS, T)
    nq, nk = T // tq, T // tk
    GW = ATT_GROUP * HEAD_DIM
    rows = ATT_GROUP * tq
    return pl.pallas_call(
        functools.partial(_flash_kernel, tq=tq, tk=tk), grid=(B, ATT_KV_HEADS, nq, nk),
        in_specs=[pl.BlockSpec((tq, GW), lambda b, h, i, j: (b * nq + i, h)),
                  pl.BlockSpec((None, tk, HEAD_DIM), lambda b, h, i, j: (h, b * nk + j, 0)),
                  pl.BlockSpec((None, tk, 2 * HEAD_DIM), lambda b, h, i, j: (h, b * nk + j, 0))],
        out_specs=pl.BlockSpec((tq, GW), lambda b, h, i, j: (b * nq + i, h)),
        out_shape=jax.ShapeDtypeStruct((N, ATT_Q_WIDTH), BF16),
        scratch_shapes=[pltpu.VMEM((rows, HEAD_DIM), BF16), pltpu.VMEM((rows, V7X_LANES), F32),
                        pltpu.VMEM((rows, 2 * HEAD_DIM), F32)],
        compiler_params=_params("parallel", "parallel", "parallel", "arbitrary"), name="flash_attention",
    )(q, k, v)


def _rwkv_scan_kernel(*refs, rows_per_set):
    C = RW_CHUNK
    H = C // 2
    W = RWKV_WIDTH
    SUB = 8
    n_sets = len(rows_per_set)
    ones_ref = refs[12 * n_sets]
    outs = refs[12 * n_sets + 1:12 * n_sets + 1 + 2 * n_sets]
    state, vk_scr = refs[-2:]
    dirs = [(refs[12 * s:12 * s + 6], refs[12 * s + 6:12 * s + 12]) for s in range(n_sets)]
    chains = [(s, n, d) for s in range(n_sets) for n in range(rows_per_set[s]) for d in range(2)]

    first_chunk = pl.program_id(1) == 0

    @pl.when(first_chunk)
    def _():
        for c, (s, n, d) in enumerate(chains):
            if s == 0:
                state[c] = jnp.zeros((HEAD_DIM, W), F32)

    @pl.when(jnp.logical_and(first_chunk, pl.program_id(0) == 0))
    def _():
        for c, (s, n, d) in enumerate(chains):
            if s > 0:
                state[c] = jnp.zeros((HEAD_DIM, W), F32)

    ones = ones_ref[...]
    same_head = (lax.broadcasted_iota(jnp.int32, (W, W), 0) // HEAD_DIM
                 == lax.broadcasted_iota(jnp.int32, (W, W), 1) // HEAD_DIM)
    step_iota = lax.broadcasted_iota(jnp.int32, (H, HEAD_DIM, W), 0)
    lane_iota = lax.broadcasted_iota(jnp.int32, (H, HEAD_DIM, W), 2) & (HEAD_DIM - 1)

    def head_sums(vals):
        stacked = jnp.concatenate([v.astype(BF16) for v in vals], axis=0)
        out = _dot(stacked, ones)
        return [out[c * HEAD_DIM:(c + 1) * HEAD_DIM, :] for c in range(len(vals))]

    own_head = (lax.broadcasted_iota(jnp.int32, (SUB, W), 0)
                == lax.broadcasted_iota(jnp.int32, (SUB, W), 1) // HEAD_DIM)
    carry = tuple(state[c] for c in range(len(chains)))
    for half in range(2):
        lo = (half * H, (1 - half) * H)
        for c, (s, n, d) in enumerate(chains):
            kc = dirs[s][d][3][n]
            kbd = jnp.where(same_head, jnp.concatenate([kc] * (W // C), axis=0), 0.0).astype(BF16)
            sel = jnp.where(step_iota + lo[d] == lane_iota, dirs[s][d][5][n][None], 0.0).reshape(H * HEAD_DIM, W)
            vk_scr[c] = _dot(sel.astype(BF16), kbd).reshape(H, HEAD_DIM, W)

        def group(g, states, lo=lo):
            states = list(states)
            bases = (lo[0] + g * SUB, lo[1] + (H // SUB - 1 - g) * SUB)
            rows = [[ref[n, pl.ds(pl.multiple_of(bases[d], SUB), SUB), :] for ref in dirs[s][d][:5]]
                    for (s, n, d) in chains]
            ys = [[None] * SUB for _ in chains]
            for j in range(SUB):
                jj = (j, SUB - 1 - j)
                row = lambda c, f: rows[c][f][jj[chains[c][2]]:jj[chains[c][2]] + 1, :]
                sa = head_sums([states[c] * row(c, 0) for c in range(len(chains))])
                new = [states[c] * row(c, 2) - sa[c] * row(c, 4)
                       + vk_scr[c, bases[chains[c][2]] + jj[chains[c][2]] - lo[chains[c][2]]]
                       for c in range(len(chains))]
                for c, (s, n, d) in enumerate(chains):
                    read = new[c] if d == 0 else states[c]
                    r_heads = jnp.where(own_head, row(c, 1), 0.0).astype(BF16)
                    ys[c][jj[d]] = _dot_nt(r_heads, read.astype(BF16))
                states = new
            for c, (s, n, d) in enumerate(chains):
                span = pl.ds(pl.multiple_of(bases[d], SUB), SUB)
                for h in range(RWKV_HEADS):
                    col = jnp.concatenate([ys[c][r][h:h + 1, :] for r in range(SUB)], axis=0)
                    outs[2 * s + d][n, span, h * HEAD_DIM:(h + 1) * HEAD_DIM] = col
            return tuple(states)

        carry = lax.fori_loop(0, H // SUB, group, carry)

    for c in range(len(chains)):
        state[c] = carry[c]


def _rwkv_scan_call(sets, ones256):
    W, C = RWKV_WIDTH, RW_CHUNK
    _, B0, T0, nb0 = sets[0]
    nc0 = T0 // C
    operands, in_specs, out_specs, out_shapes = [], [], [], []
    for s, (rwp, B, T, nb) in enumerate(sets):
        nc = T // C
        fields = rwp.reshape(B, T, RW_FIELDS * W)
        v = rwp[:, 2 * W:3 * W].reshape(B, nc, C, RWKV_HEADS, HEAD_DIM)
        vt = v.transpose(0, 1, 4, 3, 2).reshape(B, nc, HEAD_DIM, W)

        def block(b, c, reverse, s=s, nc=nc):
            bi, ci = (b, c) if s == 0 else (0, b * nc0 + c)
            return bi, (nc - 1 - ci if reverse else ci)

        field = lambda f, rev, nb=nb, block=block: pl.BlockSpec(
            (nb, C, W), lambda b, c: block(b, c, rev) + (f,))
        vt_spec = lambda rev, nb=nb, block=block: pl.BlockSpec(
            (nb, None, HEAD_DIM, W), lambda b, c: block(b, c, rev) + (0, 0))
        in_specs += [field(1, False), field(0, False), field(3, False), field(5, False), field(7, False),
                     vt_spec(False),
                     field(1, True), field(0, True), field(4, True), field(6, True), field(8, True),
                     vt_spec(True)]
        operands += [fields] * 5 + [vt] + [fields] * 5 + [vt]
        y_spec = lambda rev, nb=nb, block=block: pl.BlockSpec(
            (nb, C, W), lambda b, c: block(b, c, rev) + (0,))
        out_specs += [y_spec(False), y_spec(True)]
        out_shapes += [jax.ShapeDtypeStruct((B, T, W), F32)] * 2
    n_chains = 2 * sum(nb for _, _, _, nb in sets)
    outs = pl.pallas_call(
        functools.partial(_rwkv_scan_kernel, rows_per_set=tuple(nb for _, _, _, nb in sets)),
        grid=(B0 // nb0, nc0), in_specs=in_specs + [_full(ones256.shape)],
        out_specs=out_specs, out_shape=out_shapes,
        scratch_shapes=[pltpu.VMEM((n_chains, HEAD_DIM, W), F32),
                        pltpu.VMEM((n_chains, C // 2, HEAD_DIM, W), F32)],
        compiler_params=_params("arbitrary", "arbitrary"), name="rwkv_scan",
    )(*operands, ones256)
    return [(outs[2 * s].reshape(B * T, W), outs[2 * s + 1].reshape(B * T, W))
            for s, (_, B, T, _) in enumerate(sets)]


def _rwkv_scan(groups, ones256):
    rows = lambda B: RW_BATCH_ROWS if B % RW_BATCH_ROWS == 0 else 1
    if len(groups) == 2:
        (_, B0, T0), (_, B1, T1) = groups
        if B1 == 1 and (B0 // rows(B0)) * (T0 // RW_CHUNK) == T1 // RW_CHUNK:
            return _rwkv_scan_call([groups[0] + (rows(B0),), groups[1] + (1,)], ones256)
    return [_rwkv_scan_call([g + (rows(g[1]),)], ones256)[0] for g in groups]


def _merge_kernel(x_ref, mod_ref, ya_ref, yb_ref, yf_ref, ybk_ref, rg_ref, bonus_ref, gn_ref, gates_ref,
                  ones_ref, wa_ref, wb_ref, wc_ref, wo_ref, o_ref):
    y = _head_layer_norm(yf_ref[...] + ybk_ref[...], ones_ref[...], RWKV_GN_EPS) * gn_ref[...]
    yc = ((y + bonus_ref[...]) * rg_ref[...]).astype(BF16)
    D = D_MODEL
    merged = (gates_ref[:, 0:D].astype(F32) * _dot(ya_ref[...], wa_ref[...])
              + gates_ref[:, D:2 * D].astype(F32) * _dot(yb_ref[...], wb_ref[...])
              + gates_ref[:, 2 * D:3 * D].astype(F32) * _dot(yc, wc_ref[...]))
    o_ref[...] = x_ref[...] + mod_ref[2:3, :] * _dot(merged.astype(BF16), wo_ref[...])


def _merge(x, mod, T, ya, yb, yf, ybk, rwp, gates, lw, ones256):
    N = x.shape[0]
    tm = min(512, T)
    tps = T // tm
    x_spec, mod_spec, _ = _token_specs(tm, tps)
    row = lambda w: pl.BlockSpec((tm, w), lambda i: (i, 0))
    field = lambda f: pl.BlockSpec((tm, RWKV_WIDTH), lambda i: (i, f))
    return pl.pallas_call(
        _merge_kernel, grid=(N // tm,),
        in_specs=[x_spec, mod_spec, row(RET_WIDTH), row(ATT_Q_WIDTH), row(RWKV_WIDTH), row(RWKV_WIDTH),
                  field(9), field(10), _full(lw["rw_gn"].shape), row(GATE_COLS), _full(ones256.shape),
                  _full(lw["w_branch_a"].shape), _full(lw["w_branch_b"].shape), _full(lw["w_branch_c"].shape),
                  _full(lw["w_out"].shape)],
        out_specs=x_spec, out_shape=jax.ShapeDtypeStruct((N, D_MODEL), F32),
        compiler_params=_params("parallel"), name="merge_out",
    )(x, mod, ya, yb, yf, ybk, rwp, rwp, lw["rw_gn"], gates, ones256,
      lw["w_branch_a"], lw["w_branch_b"], lw["w_branch_c"], lw["w_out"])


def _route(h, hb, rwt_ref, rbias_ref, tm):
    h_lo = (h - hb.astype(F32)).astype(BF16)
    rw_hi, rw_lo = _split(rwt_ref[...])
    logits = _dot_nt(rw_hi, hb) + _dot_nt(rw_hi, h_lo) + _dot_nt(rw_lo, hb)
    scores = _sigmoid(logits)
    choice = scores + jnp.concatenate([rbias_ref[...]] * (tm // V7X_LANES), axis=1)
    per_group = N_EXPERTS // N_GROUPS
    sub = lax.broadcasted_iota(jnp.int32, (per_group, tm), 0)
    groups, gscore = [], []
    for g in range(N_GROUPS):
        cg = choice[g * per_group:(g + 1) * per_group, :]
        m1 = jnp.max(cg, axis=0, keepdims=True)
        first = jnp.min(jnp.where(cg == m1, sub, per_group), axis=0, keepdims=True)
        m2 = jnp.max(jnp.where(sub == first, -jnp.inf, cg), axis=0, keepdims=True)
        groups.append(cg)
        gscore.append(m1 + m2)
    masked = []
    for g in range(N_GROUPS):
        beaten = jnp.zeros((1, tm), F32)
        for o in range(N_GROUPS):
            if o == g:
                continue
            wins = (gscore[o] >= gscore[g]) if o < g else (gscore[o] > gscore[g])
            beaten = beaten + jnp.where(wins, 1.0, 0.0)
        keep = jnp.where(beaten < TOPK_GROUPS, 1.0, 0.0)
        masked.append(jnp.where(jnp.broadcast_to(keep, (per_group, tm)) > 0.5, groups[g], -jnp.inf))
    mc = jnp.concatenate(masked, axis=0)
    eidx = lax.broadcasted_iota(jnp.int32, (N_EXPERTS, tm), 0).astype(F32)
    sel = jnp.zeros((N_EXPERTS, tm), F32)
    for _ in range(TOP_K):
        top = jnp.max(mc, axis=0, keepdims=True)
        first = jnp.min(jnp.where(mc == top, eidx, float(N_EXPERTS)), axis=0, keepdims=True)
        pick = eidx == first
        sel = jnp.where(pick, 1.0, sel)
        mc = jnp.where(pick, -jnp.inf, mc)
    wts = jnp.where(sel > 0.5, scores, 0.0)
    return wts / jnp.sum(wts, axis=0, keepdims=True) * ROUTED_SCALE, sel


def _selected_lists(comb, sel, rank):
    eidx = lax.broadcasted_iota(jnp.int32, comb.shape, 0).astype(F32)
    prev = jnp.full((1, comb.shape[1]), -1.0, F32)
    ids, wts, pos = [], [], []
    for _ in range(TOP_K):
        cand = jnp.where(sel > 0.5, jnp.where(eidx > prev, eidx, float(N_EXPERTS)), float(N_EXPERTS))
        prev = jnp.min(cand, axis=0, keepdims=True)
        wts.append(jnp.sum(jnp.where(eidx == prev, comb, 0.0), axis=0, keepdims=True))
        pos.append(jnp.sum(jnp.where(eidx == prev, rank, 0.0), axis=0, keepdims=True))
        ids.append(jnp.minimum(prev, N_EXPERTS - 1.0))
    as_int = lambda rows: jnp.concatenate(rows, axis=0).astype(jnp.int32)
    return as_int(ids), jnp.concatenate(wts, axis=0), as_int(pos)


def _pack_halves(y):
    n = y.shape[1] // 2
    hi = pltpu.bitcast(y[:, :n].astype(BF16).astype(F32), jnp.int32)
    lo = pltpu.bitcast(y[:, n:].astype(BF16).astype(F32), jnp.int32)
    return hi | lax.shift_right_logical(lo, 16)


def _unpack_halves(w):
    hi = pltpu.bitcast(w & jnp.int32(-65536), F32)
    lo = pltpu.bitcast(lax.shift_left(w, 16), F32)
    return hi, lo


def _moe_route_kernel(x_ref, mod_ref, g_ref, rwt_ref, rbias_ref, shgu_ref, shd_ref, tri_ref,
                      hp_ref, ids_ref, wts_ref, pos_ref, base_ref, count_ref, *, tm):
    F = EXPERT_FF

    @pl.when(pl.program_id(0) == 0)
    def _():
        count_ref[...] = jnp.zeros_like(count_ref)

    x = x_ref[...]
    h = _norm_mod(x, g_ref[...], mod_ref[3:4, :], mod_ref[4:5, :])
    hb = h.astype(BF16)
    hp_ref[...] = _pack_halves(hb.astype(F32))
    comb, sel = _route(h, hb, rwt_ref, rbias_ref, tm)
    before = count_ref[...]
    inclusive = _dot(sel.astype(BF16), tri_ref[...])
    rank = inclusive - sel + jnp.concatenate([before] * (tm // V7X_LANES), axis=1)
    count_ref[...] = before + jnp.sum(sel, axis=1, keepdims=True)
    ids, wts, pos = _selected_lists(comb, sel, rank)
    ids_ref[...] = ids
    wts_ref[...] = wts
    pos_ref[...] = pos
    gu = _dot(hb, shgu_ref[...])
    shared = _dot((_silu(gu[:, 0:F]) * gu[:, F:2 * F]).astype(BF16), shd_ref[...])
    base_ref[...] = x + mod_ref[5:6, :] * shared


def _moe_expert_kernel(tile_expert_ref, xs_ref, wgu_ref, wd_ref, o_ref):
    del tile_expert_ref
    F = EXPERT_FF
    half = D_MODEL // 2
    hi, lo = _unpack_halves(xs_ref[...])
    gu = _dot(hi.astype(BF16), wgu_ref[0:half, :]) + _dot(lo.astype(BF16), wgu_ref[half:, :])
    act = (_silu(gu[:, 0:F]) * gu[:, F:2 * F]).astype(BF16)
    o_ref[...] = _pack_halves(_dot(act, wd_ref[...]))


def _moe_combine_kernel(base_ref, mod_ref, y_ref, w_ref, fin_ref, o_ref, *, final):
    half = D_MODEL // 2
    acc_hi = jnp.zeros((base_ref.shape[0], half), F32)
    acc_lo = jnp.zeros((base_ref.shape[0], half), F32)
    for k in range(TOP_K):
        hi, lo = _unpack_halves(y_ref[k])
        wk = w_ref[:, k:k + 1]
        acc_hi = acc_hi + wk * hi
        acc_lo = acc_lo + wk * lo
    out_hi = base_ref[:, 0:half] + mod_ref[5:6, 0:half] * acc_hi
    out_lo = base_ref[:, half:] + mod_ref[5:6, half:] * acc_lo
    if final:
        ms = (jnp.sum(out_hi * out_hi, axis=-1, keepdims=True)
              + jnp.sum(out_lo * out_lo, axis=-1, keepdims=True)) * (1.0 / D_MODEL)
        scale = lax.rsqrt(ms + NORM_EPS)
        out_hi = out_hi * scale * fin_ref[:, 0:half]
        out_lo = out_lo * scale * fin_ref[:, half:]
    o_ref[:, 0:half] = out_hi
    o_ref[:, half:] = out_lo


def _gather_rows(table, idx):
    rows, width = idx.shape[0], table.shape[1]
    info = plsc.get_sparse_core_info()
    workers = info.num_cores * info.num_subcores
    windows = rows // (workers * SC_GATHER_ROWS)
    assert windows * workers * SC_GATHER_ROWS == rows
    mesh = plsc.VectorSubcoreMesh(core_axis_name="core", subcore_axis_name="subcore")

    @functools.partial(pl.kernel, out_type=jax.ShapeDtypeStruct((rows, width), table.dtype), mesh=mesh,
                       scratch_types=[pltpu.VMEM((SC_GATHER_ROWS,), jnp.int32),
                                      pltpu.VMEM((SC_GATHER_ROWS, width), table.dtype)],
                       name="moe_gather")
    def gather(table_hbm, idx_hbm, out_hbm, idx_vmem, rows_vmem):
        worker = lax.axis_index("subcore") * info.num_cores + lax.axis_index("core")
        base = worker * (windows * SC_GATHER_ROWS)

        @pl.loop(0, windows)
        def _(j):
            span = pl.ds(pl.multiple_of(base + j * SC_GATHER_ROWS, SC_GATHER_ROWS), SC_GATHER_ROWS)
            pltpu.sync_copy(idx_hbm.at[span], idx_vmem)
            pltpu.sync_copy(table_hbm.at[idx_vmem], rows_vmem)
            pltpu.sync_copy(rows_vmem, out_hbm.at[span])

    return gather(table, idx)


def _scatter_rows(table, dest, total_rows):
    n_lists, n_rows = dest.shape
    width = table.shape[1]
    info = plsc.get_sparse_core_info()
    workers = info.num_cores * info.num_subcores
    windows = n_rows // (workers * SC_GATHER_ROWS)
    assert windows * workers * SC_GATHER_ROWS == n_rows
    windows_per_list = n_rows // SC_GATHER_ROWS
    mesh = plsc.VectorSubcoreMesh(core_axis_name="core", subcore_axis_name="subcore")

    @functools.partial(pl.kernel, out_type=jax.ShapeDtypeStruct((total_rows, width), table.dtype), mesh=mesh,
                       scratch_types=[pltpu.VMEM((1, SC_GATHER_ROWS), jnp.int32),
                                      pltpu.VMEM((SC_GATHER_ROWS, width), table.dtype)],
                       name="moe_scatter")
    def scatter(table_hbm, dest_hbm, out_hbm, idx_vmem, rows_vmem):
        worker = lax.axis_index("subcore") * info.num_cores + lax.axis_index("core")

        @pl.loop(0, windows)
        def _(j):
            window = worker * windows + j
            span = pl.ds(pl.multiple_of(window * SC_GATHER_ROWS, SC_GATHER_ROWS), SC_GATHER_ROWS)
            pltpu.sync_copy(table_hbm.at[span], rows_vmem)
            for k in range(n_lists):
                pltpu.sync_copy(dest_hbm.at[pl.ds(k * windows_per_list + window, 1)], idx_vmem)
                pltpu.sync_copy(rows_vmem, out_hbm.at[idx_vmem.at[0]])

    return scatter(table, dest.reshape(n_lists * windows_per_list, SC_GATHER_ROWS))


def _dispatch_plan(ids, pos, counts, tg):
    E = N_EXPERTS
    n_tokens = sum(i.shape[1] for i in ids)
    padded = (sum(counts) + tg - 1) // tg * tg
    ends = jnp.cumsum(padded)
    start = ends - padded
    experts = jnp.arange(E, dtype=jnp.int32)[None, None, :]
    dests = []
    for i, p, c in zip(ids, pos, counts):
        dests.append(p + jnp.sum(jnp.where(i[:, :, None] == experts, start[None, None, :], 0), axis=-1))
        start = start + c
    total = n_tokens * TOP_K + E * tg
    tile_start = jnp.arange(total // tg, dtype=jnp.int32) * tg
    tile_expert = jnp.sum((ends[None, :] <= tile_start[:, None]).astype(jnp.int32), axis=1)
    return jnp.concatenate(dests, axis=1), total, jnp.minimum(tile_expert, E - 1)


def _moe(groups, lw, final_g, final):
    D, F, E = D_MODEL, EXPERT_FF, N_EXPERTS
    half = D // 2
    routed = []
    for x, mod, T in groups:
        N = x.shape[0]
        tm = min(MOE_TOKENS, T)
        x_spec, mod_spec, g_spec = _token_specs(tm, T // tm)
        list_spec = pl.BlockSpec((TOP_K, tm), lambda i: (0, i))
        tri = jnp.asarray(np.triu(np.ones((tm, tm), np.float32)), BF16)
        routed.append(pl.pallas_call(
            functools.partial(_moe_route_kernel, tm=tm), grid=(N // tm,),
            in_specs=[x_spec, mod_spec, g_spec, _full((E, D)), _full((E, V7X_LANES)), _full((D, 2 * F)),
                      _full((F, D)), _full((tm, tm))],
            out_specs=[pl.BlockSpec((tm, half), lambda i: (i, 0)), list_spec, list_spec, list_spec, x_spec,
                       _full((E, V7X_LANES))],
            out_shape=[jax.ShapeDtypeStruct((N, half), jnp.int32), jax.ShapeDtypeStruct((TOP_K, N), jnp.int32),
                       jax.ShapeDtypeStruct((TOP_K, N), F32), jax.ShapeDtypeStruct((TOP_K, N), jnp.int32),
                       jax.ShapeDtypeStruct((N, D), F32), jax.ShapeDtypeStruct((E, V7X_LANES), F32)],
            compiler_params=_params("arbitrary"), name="moe_route",
        )(x, mod, lw["norm2_g"], lw["router_wt"], lw["router_bias"], lw["sh_gu"], lw["sh_d"], tri))

    n_all = sum(x.shape[0] for x, _, _ in groups)
    tg = max(MOE_GROUP_ROWS_MIN, min(MOE_GROUP_ROWS_MAX, n_all * TOP_K // (E * 8)))
    dest, total, tile_expert = _dispatch_plan([r[1] for r in routed], [r[3] for r in routed],
                                              [r[5][:, 0].astype(jnp.int32) for r in routed], tg)
    xs = _scatter_rows(jnp.concatenate([r[0] for r in routed], axis=0), dest, total)
    ys = pl.pallas_call(
        _moe_expert_kernel,
        grid_spec=pltpu.PrefetchScalarGridSpec(
            num_scalar_prefetch=1, grid=(total // tg,),
            in_specs=[pl.BlockSpec((tg, half), lambda i, te: (i, 0)),
                      pl.BlockSpec((None, D, 2 * F), lambda i, te: (te[i], 0, 0)),
                      pl.BlockSpec((None, F, D), lambda i, te: (te[i], 0, 0))],
            out_specs=pl.BlockSpec((tg, half), lambda i, te: (i, 0))),
        out_shape=jax.ShapeDtypeStruct(xs.shape, jnp.int32),
        compiler_params=_params("parallel"), name="moe_experts",
    )(tile_expert, xs, lw["exp_gu"], lw["exp_d"])

    yk = _gather_rows(ys, dest.reshape(-1)).reshape(TOP_K, n_all, half)
    outs, first_row = [], 0
    for (x, mod, T), (_, _, wts, _, base, _) in zip(groups, routed):
        N = x.shape[0]
        tm = min(MOE_TOKENS, T)
        x_spec, mod_spec, g_spec = _token_specs(tm, T // tm)
        assert first_row % tm == 0
        first_block = first_row // tm
        outs.append(pl.pallas_call(
            functools.partial(_moe_combine_kernel, final=final), grid=(N // tm,),
            in_specs=[x_spec, mod_spec,
                      pl.BlockSpec((TOP_K, tm, half), lambda i, first_block=first_block: (0, first_block + i, 0)),
                      pl.BlockSpec((tm, TOP_K), lambda i: (i, 0)), g_spec],
            out_specs=x_spec, out_shape=jax.ShapeDtypeStruct((N, D), F32),
            compiler_params=_params("parallel"), name="moe_combine",
        )(base, mod, yk, wts.T, final_g))
        first_row += N
    return outs


def _block_diag2(a, b):
    za = jnp.zeros((a.shape[0], b.shape[1]), a.dtype)
    zb = jnp.zeros((b.shape[0], a.shape[1]), a.dtype)
    return jnp.concatenate([jnp.concatenate([a, za], 1), jnp.concatenate([zb, b], 1)], 0)


def _layer_weights(P, l):
    w_in = P["w_in"][l]
    o1, o2, o3 = RET_COLS, RET_COLS + ATT_COLS, RET_COLS + ATT_COLS + RW_COLS
    w_ret, w_att, w_rw, w_gate = w_in[:, :o1], w_in[:, o1:o2], w_in[:, o2:o3], w_in[:, o3:]
    qk_gain = jnp.zeros((8, ATT_Q_WIDTH), F32)
    qk_gain = qk_gain.at[0].set(jnp.tile(P["q_norm_g"][l], ATT_Q_HEADS))
    qk_gain = qk_gain.at[1, :ATT_KV_WIDTH].set(jnp.tile(P["k_norm_g"][l], ATT_KV_HEADS))
    rw_vec = jnp.zeros((8, RWKV_WIDTH), F32)
    for j, name in enumerate(("rw_w0_f", "rw_w0_b", "rw_a0_f", "rw_a0_b", "rw_k_k", "rw_k_a")):
        rw_vec = rw_vec.at[j].set(P[name][l])
    rw_vec = rw_vec.at[6].set(P["rw_r_k"][l].reshape(RWKV_WIDTH))
    row = lambda v: v.reshape(1, -1)
    return {
        "norm1_g": row(P["norm1_g"][l]), "norm2_g": row(P["norm2_g"][l]),
        "w_ret": w_ret.astype(BF16), "w_att": w_att.astype(BF16), "w_rw": w_rw.astype(BF16),
        "w_gate": w_gate.astype(BF16), "qk_gain": qk_gain, "ret_gn": row(P["ret_gn"][l]),
        "shift_mu": P["shift_mu"][l], "rw_vec": rw_vec,
        "rw_w2": _block_diag2(P["rw_w2_f"][l], P["rw_w2_b"][l]).astype(BF16),
        "rw_a2": _block_diag2(P["rw_a2_f"][l], P["rw_a2_b"][l]).astype(BF16),
        "rw_g2": P["rw_g2"][l].astype(BF16), "rw_gn": row(P["rw_gn"][l]),
        "w_branch_a": P["w_branch_a"][l].astype(BF16), "w_branch_b": P["w_branch_b"][l].astype(BF16),
        "w_branch_c": P["w_branch_c"][l].astype(BF16), "w_out": P["w_out"][l].astype(BF16),
        "router_wt": P["router_w"][l].T,
        "router_bias": jnp.broadcast_to(P["router_bias"][l][:, None], (N_EXPERTS, V7X_LANES)),
        "sh_gu": jnp.concatenate([P["sh_w_gate"][l], P["sh_w_up"][l]], axis=1).astype(BF16),
        "sh_d": P["sh_w_down"][l].astype(BF16),
        "exp_gu": jnp.concatenate([P["exp_w_gate"][l], P["exp_w_up"][l]], axis=2).astype(BF16),
        "exp_d": P["exp_w_down"][l].astype(BF16),
    }


def _ones_block_diag(n):
    idx = np.arange(n) // HEAD_DIM
    return jnp.asarray(idx[:, None] == idx[None, :], BF16)


def _tables(T):
    rows = T // GRID_W
    row = jnp.repeat(jnp.arange(rows, dtype=F32), GRID_W)
    col = jnp.tile(jnp.arange(GRID_W, dtype=F32), rows)
    freqs = ROPE_THETA ** (-jnp.arange(ROPE_PAIRS_PER_AXIS, dtype=F32) / ROPE_PAIRS_PER_AXIS)
    ang = jnp.concatenate([row[:, None] * freqs, col[:, None] * freqs], axis=-1)
    cos, sin = jnp.cos(ang), jnp.sin(ang)
    return {
        "cos": jnp.tile(jnp.concatenate([cos, cos], axis=-1), (1, ATT_Q_HEADS)),
        "sin": jnp.tile(jnp.concatenate([-sin, sin], axis=-1), (1, ATT_Q_HEADS)),
        "ones128": _ones_block_diag(128), "ones256": _ones_block_diag(256), "ones512": _ones_block_diag(512),
    }


def _trunks(requests, P, layer_weights):
    groups = []
    for x, c in requests:
        B, T, D = x.shape
        groups.append({"B": B, "T": T, "x": x.reshape(B * T, D), "tabs": _tables(T),
                       "mods": _ada_mod(c, P["ada_w"], P["ada_b"])})
    final_g = P["final_g"].reshape(1, D_MODEL)
    depth = len(layer_weights)
    for l, lw in enumerate(layer_weights):
        mixed = []
        for g in groups:
            B, T, tabs, mod = g["B"], g["T"], g["tabs"], g["mods"][l]
            ret, q, k, v, gates, rwp = _input_projections(g["x"], mod, T, lw, tabs)
            ya = _retention(ret, B, T, lw["ret_gn"], tabs["ones256"])
            yb = _attention(q, k, v, B, T)
            mixed.append((mod, ya, yb, gates, rwp))
        scans = _rwkv_scan([(m[4], g["B"], g["T"]) for m, g in zip(mixed, groups)], groups[0]["tabs"]["ones256"])
        merged = [(_merge(g["x"], mod, g["T"], ya, yb, yf, ybk, rwp, gates, lw, g["tabs"]["ones256"]), mod, g["T"])
                  for g, (mod, ya, yb, gates, rwp), (yf, ybk) in zip(groups, mixed, scans)]
        for g, x in zip(groups, _moe(merged, lw, final_g, final=(l == depth - 1))):
            g["x"] = x
    return [g["x"].reshape(g["B"], g["T"], D_MODEL) for g in groups]


def kernel(x_prompt, x_sample, c_prompt, c_sample, norm1_g, norm2_g, final_g, ada_w, ada_b, w_in, ret_gn, q_norm_g, k_norm_g, shift_mu, rw_w0_f, rw_w2_f, rw_w0_b, rw_w2_b, rw_a0_f, rw_a2_f, rw_a0_b, rw_a2_b, rw_g2, rw_k_k, rw_k_a, rw_r_k, rw_gn, w_branch_a, w_branch_b, w_branch_c, w_out, router_w, router_bias, exp_w_gate, exp_w_up, exp_w_down, sh_w_gate, sh_w_up, sh_w_down):
    P = {
        "norm1_g": norm1_g, "norm2_g": norm2_g, "final_g": final_g, "ada_w": ada_w, "ada_b": ada_b,
        "w_in": w_in, "ret_gn": ret_gn, "q_norm_g": q_norm_g, "k_norm_g": k_norm_g, "shift_mu": shift_mu,
        "rw_w0_f": rw_w0_f, "rw_w2_f": rw_w2_f, "rw_w0_b": rw_w0_b, "rw_w2_b": rw_w2_b,
        "rw_a0_f": rw_a0_f, "rw_a2_f": rw_a2_f, "rw_a0_b": rw_a0_b, "rw_a2_b": rw_a2_b,
        "rw_g2": rw_g2, "rw_k_k": rw_k_k, "rw_k_a": rw_k_a, "rw_r_k": rw_r_k, "rw_gn": rw_gn,
        "w_branch_a": w_branch_a, "w_branch_b": w_branch_b, "w_branch_c": w_branch_c, "w_out": w_out,
        "router_w": router_w, "router_bias": router_bias,
        "exp_w_gate": exp_w_gate, "exp_w_up": exp_w_up, "exp_w_down": exp_w_down,
        "sh_w_gate": sh_w_gate, "sh_w_up": sh_w_up, "sh_w_down": sh_w_down,
    }
    layer_weights = [_layer_weights(P, l) for l in range(w_in.shape[0])]
    y_prompt, y_sample = _trunks([(x_prompt, c_prompt), (x_sample, c_sample)], P, layer_weights)
    return (y_prompt, y_sample)
```

```python
import functools

import numpy as np
import jax
import jax.numpy as jnp
from jax import lax
from jax.experimental import pallas as pl
from jax.experimental.pallas import tpu as pltpu
from jax.experimental.pallas import tpu_sc as plsc

F32 = jnp.float32
BF16 = jnp.bfloat16

D_MODEL = 1024
DEPTH = 2
GRID_W = 64
HEAD_DIM = 64
ROPE_THETA = 10000.0
ROPE_PAIRS_PER_AXIS = HEAD_DIM // 4

RET_HEADS = 4
RET_WIDTH = RET_HEADS * HEAD_DIM
RET_CHUNK = 128
RET_CHUNKS_PER_STEP = 8
RET_EPS = 1e-5

ATT_Q_HEADS = 8
ATT_KV_HEADS = 2
ATT_GROUP = ATT_Q_HEADS // ATT_KV_HEADS
ATT_Q_WIDTH = ATT_Q_HEADS * HEAD_DIM
ATT_KV_WIDTH = ATT_KV_HEADS * HEAD_DIM

RWKV_HEADS = 4
RWKV_WIDTH = RWKV_HEADS * HEAD_DIM
DECAY_LORA = 64
AAA_LORA = 64
GATE_LORA = 128
RWKV_GN_EPS = 64e-5
RW_CHUNK = 64
RW_FIELDS = 11
RW_BATCH_ROWS = 4

FLASH_Q_ROWS = 1024
FLASH_KV_ROWS = 1024
FLASH_SUB_ROWS = 1024

MOE_TOKENS = 512
MOE_GROUP_ROWS_MIN = 256
MOE_GROUP_ROWS_MAX = 1024
SC_GATHER_ROWS = 128

N_EXPERTS = 64
TOP_K = 8
N_GROUPS = 8
TOPK_GROUPS = 4
EXPERT_FF = 256
ROUTED_SCALE = 2.5
NORM_EPS = 1e-6

RET_COLS = 4 * RET_WIDTH
ATT_COLS = ATT_Q_WIDTH + 2 * ATT_KV_WIDTH
RW_COLS = 3 * RWKV_WIDTH + 2 * DECAY_LORA + 2 * AAA_LORA + GATE_LORA
GATE_COLS = 3 * D_MODEL

V7X_LANES = 128
VMEM_LIMIT_BYTES = 48 * 1024 * 1024


def _params(*dims):
    return pltpu.CompilerParams(dimension_semantics=dims, vmem_limit_bytes=VMEM_LIMIT_BYTES)


def _sigmoid(x):
    return 1.0 / (1.0 + jnp.exp(-x))


def _silu(x):
    return x * _sigmoid(x)


def _dot(a, b):
    return jnp.dot(a, b, preferred_element_type=F32)


def _dot_nt(a, b):
    return lax.dot_general(a, b, (((1,), (1,)), ((), ())), preferred_element_type=F32)


def _split(a):
    hi = a.astype(BF16)
    lo = (a - hi.astype(F32)).astype(BF16)
    return hi, lo


def _dot_split_lhs(a, b_bf16):
    hi, lo = _split(a)
    return _dot(hi, b_bf16) + _dot(lo, b_bf16)


def _dot3(a, b):
    ah, al = _split(a)
    bh, bl = _split(b)
    return _dot(ah, bh) + _dot(ah, bl) + _dot(al, bh)


def _swap_rotary_halves(x):
    half = HEAD_DIM // 2
    lane = lax.broadcasted_iota(jnp.int32, x.shape, 1) & (HEAD_DIM - 1)
    return jnp.where(lane < half, pltpu.roll(x, x.shape[1] - half, 1), pltpu.roll(x, half, 1))


def _norm_mod(x, gain, shift, scale):
    ms = jnp.mean(x * x, axis=-1, keepdims=True)
    return x * lax.rsqrt(ms + NORM_EPS) * gain * (1.0 + scale) + shift


def _head_layer_norm(y, ones_bd, eps):
    mean = _dot_split_lhs(y, ones_bd) * (1.0 / HEAD_DIM)
    yc = y - mean
    var = _dot_split_lhs(yc * yc, ones_bd) * (1.0 / HEAD_DIM)
    return yc * lax.rsqrt(var + eps)


def _ada_kernel(c_ref, w_ref, b_ref, o_ref):
    c = c_ref[...]
    o_ref[...] = _dot3(_silu(c), w_ref[...]) + b_ref[...]


def _ada_mod(c, ada_w, ada_b):
    B = c.shape[0]
    Bp = max(8, B)
    cp = jnp.zeros((Bp, D_MODEL), F32).at[:B].set(c)
    L = ada_w.shape[0]
    tn = 1536
    out = pl.pallas_call(
        _ada_kernel,
        grid=(L, 6 * D_MODEL // tn),
        in_specs=[pl.BlockSpec((Bp, D_MODEL), lambda l, j: (0, 0)),
                  pl.BlockSpec((None, D_MODEL, tn), lambda l, j: (l, 0, j)),
                  pl.BlockSpec((None, 1, tn), lambda l, j: (l, 0, j))],
        out_specs=pl.BlockSpec((None, Bp, tn), lambda l, j: (l, 0, j)),
        out_shape=jax.ShapeDtypeStruct((L, Bp, 6 * D_MODEL), F32),
        compiler_params=_params("parallel", "parallel"),
        name="ada_mod",
    )(cp, ada_w, ada_b.reshape(L, 1, 6 * D_MODEL))
    return out[:, :B].reshape(L, B, 6, D_MODEL)


def _proj_ret_kernel(x_ref, mod_ref, g_ref, w_ref, cos_ref, sin_ref, o_ref):
    h = _norm_mod(x_ref[...], g_ref[...], mod_ref[0:1, :], mod_ref[1:2, :]).astype(BF16)
    p = _dot(h, w_ref[...])
    W = RET_WIDTH
    cos = cos_ref[:, 0:W]
    sin = sin_ref[:, 0:W]
    q, k = p[:, 0:W], p[:, W:2 * W]
    o_ref[:, 0:W] = (q * cos + _swap_rotary_halves(q) * sin).astype(BF16)
    o_ref[:, W:2 * W] = ((k * cos + _swap_rotary_halves(k) * sin) * (HEAD_DIM ** -0.5)).astype(BF16)
    o_ref[:, 2 * W:3 * W] = p[:, 2 * W:3 * W].astype(BF16)
    o_ref[:, 3 * W:4 * W] = _silu(p[:, 3 * W:4 * W]).astype(BF16)


def _proj_att_kernel(x_ref, mod_ref, g_ref, w_ref, cos_ref, sin_ref, qkg_ref, ones_ref, q_ref, k_ref, v_ref):
    h = _norm_mod(x_ref[...], g_ref[...], mod_ref[0:1, :], mod_ref[1:2, :]).astype(BF16)
    p = _dot(h, w_ref[...])
    QW, KW = ATT_Q_WIDTH, ATT_KV_WIDTH
    q = p[:, 0:QW]
    k = p[:, QW:QW + KW]
    v = p[:, QW + KW:QW + 2 * KW]
    ones = ones_ref[...]
    rq = lax.rsqrt(_dot_split_lhs(q * q, ones) * (1.0 / HEAD_DIM) + NORM_EPS)
    rk = lax.rsqrt(_dot_split_lhs(k * k, ones[0:KW, 0:KW]) * (1.0 / HEAD_DIM) + NORM_EPS)
    cos = cos_ref[...]
    sin = sin_ref[...]
    qg = q * qkg_ref[0:1, :]
    kg = k * qkg_ref[1:2, 0:KW]
    qo = (qg * cos + _swap_rotary_halves(qg) * sin) * (rq * (HEAD_DIM ** -0.5))
    ko = (kg * cos[:, 0:KW] + _swap_rotary_halves(kg) * sin[:, 0:KW]) * rk
    q_ref[...] = qo.astype(BF16)
    kb = ko.astype(BF16)
    vb = v.astype(BF16)
    for hk in range(ATT_KV_HEADS):
        k_ref[hk] = kb[:, hk * HEAD_DIM:(hk + 1) * HEAD_DIM]
        v_ref[hk, :, 0:HEAD_DIM] = vb[:, hk * HEAD_DIM:(hk + 1) * HEAD_DIM]
        v_ref[hk, :, HEAD_DIM:2 * HEAD_DIM] = jnp.ones((vb.shape[0], HEAD_DIM), BF16)


def _proj_gate_kernel(x_ref, mod_ref, g_ref, w_ref, o_ref):
    h = _norm_mod(x_ref[...], g_ref[...], mod_ref[0:1, :], mod_ref[1:2, :]).astype(BF16)
    o_ref[...] = _sigmoid(_dot(h, w_ref[...])).astype(BF16)


def _proj_rw_kernel(x_ref, xp_ref, xn_ref, mod_ref, g_ref, w_ref, mu_ref, vec_ref, w2_ref, a2_ref, g2_ref,
                    ones_ref, o_ref, p_scr, *, tm, tiles_per_seq):
    i = pl.program_id(0)
    gain, shift, scale = g_ref[...], mod_ref[0:1, :], mod_ref[1:2, :]
    w = w_ref[...]
    h = _norm_mod(x_ref[...], gain, shift, scale).astype(BF16)
    p = _dot(h, w)
    hp = _norm_mod(xp_ref[...], gain, shift, scale).astype(BF16)
    hn = _norm_mod(xn_ref[...], gain, shift, scale).astype(BF16)
    first = (i % tiles_per_seq) == 0
    last = (i % tiles_per_seq) == tiles_per_seq - 1
    prev_row = jnp.where(first, 0.0, _dot(hp, w)[7:8, :])
    next_row = jnp.where(last, 0.0, _dot(hn, w)[0:1, :])
    p_scr[8:8 + tm, :] = p
    p_scr[7:8, :] = prev_row
    p_scr[8 + tm:9 + tm, :] = next_row
    prev = p_scr[7:7 + tm, :]
    nxt = p_scr[9:9 + tm, :]
    x = p + mu_ref[0:1, :] * (prev - p) + mu_ref[1:2, :] * (nxt - p)

    W = RWKV_WIDTH
    r, k, v = x[:, 0:W], x[:, W:2 * W], x[:, 2 * W:3 * W]
    xw = x[:, 3 * W:3 * W + 2 * DECAY_LORA]
    xa = x[:, 3 * W + 2 * DECAY_LORA:3 * W + 2 * DECAY_LORA + 2 * AAA_LORA]
    xg = x[:, 3 * W + 2 * DECAY_LORA + 2 * AAA_LORA:]
    w0_f, w0_b = vec_ref[0:1, :], vec_ref[1:2, :]
    a0_f, a0_b = vec_ref[2:3, :], vec_ref[3:4, :]
    k_k, k_a, r_k = vec_ref[4:5, :], vec_ref[5:6, :], vec_ref[6:7, :]
    wl = _dot(jnp.tanh(xw).astype(BF16), w2_ref[...])
    al = _dot(xa.astype(BF16), a2_ref[...])
    dec_c = float(np.exp(-0.5))
    w_f = jnp.exp(-dec_c * _sigmoid(w0_f + wl[:, 0:W]))
    w_b = jnp.exp(-dec_c * _sigmoid(w0_b + wl[:, W:2 * W]))
    a_f = _sigmoid(a0_f + al[:, 0:W])
    a_b = _sigmoid(a0_b + al[:, W:2 * W])
    gate = _dot(_sigmoid(xg).astype(BF16), g2_ref[...])
    ones = ones_ref[...]
    kk = k * k_k
    kk = kk * lax.rsqrt(_dot_split_lhs(kk * kk, ones) + 1e-12)
    k_f = k * (1.0 + (a_f - 1.0) * k_a)
    k_b = k * (1.0 + (a_b - 1.0) * k_a)
    bonus = _dot_split_lhs(r * k_f * r_k, ones) * v
    for j, val in enumerate((r, kk, v, w_f, w_b, k_f, k_b, kk * a_f, kk * a_b, gate, bonus)):
        o_ref[:, j * W:(j + 1) * W] = val


def _token_specs(tm, tiles_per_seq):
    x_spec = pl.BlockSpec((tm, D_MODEL), lambda i: (i, 0))
    mod_spec = pl.BlockSpec((None, 6, D_MODEL), lambda i: (i // tiles_per_seq, 0, 0))
    g_spec = pl.BlockSpec((1, D_MODEL), lambda i: (0, 0))
    return x_spec, mod_spec, g_spec


def _full(shape):
    nd = len(shape)
    return pl.BlockSpec(shape, lambda *_: (0,) * nd)


def _input_projections(x, mod, T, lw, tabs):
    N = x.shape[0]
    tm = min(512, T)
    tps = T // tm
    grid = (N // tm,)
    x_spec, mod_spec, g_spec = _token_specs(tm, tps)
    tab_spec = pl.BlockSpec((tm, ATT_Q_WIDTH), lambda i: (i % tps, 0))
    cos, sin = tabs["cos"], tabs["sin"]

    ret = pl.pallas_call(
        _proj_ret_kernel, grid=grid,
        in_specs=[x_spec, mod_spec, g_spec, _full(lw["w_ret"].shape), tab_spec, tab_spec],
        out_specs=pl.BlockSpec((tm, RET_COLS), lambda i: (i, 0)),
        out_shape=jax.ShapeDtypeStruct((N, RET_COLS), BF16),
        compiler_params=_params("parallel"), name="proj_ret",
    )(x, mod, lw["norm1_g"], lw["w_ret"], cos, sin)

    q, k, v = pl.pallas_call(
        _proj_att_kernel, grid=grid,
        in_specs=[x_spec, mod_spec, g_spec, _full(lw["w_att"].shape), tab_spec, tab_spec,
                  _full(lw["qk_gain"].shape), _full(tabs["ones512"].shape)],
        out_specs=[pl.BlockSpec((tm, ATT_Q_WIDTH), lambda i: (i, 0)),
                   pl.BlockSpec((ATT_KV_HEADS, tm, HEAD_DIM), lambda i: (0, i, 0)),
                   pl.BlockSpec((ATT_KV_HEADS, tm, 2 * HEAD_DIM), lambda i: (0, i, 0))],
        out_shape=[jax.ShapeDtypeStruct((N, ATT_Q_WIDTH), BF16),
                   jax.ShapeDtypeStruct((ATT_KV_HEADS, N, HEAD_DIM), BF16),
                   jax.ShapeDtypeStruct((ATT_KV_HEADS, N, 2 * HEAD_DIM), BF16)],
        compiler_params=_params("parallel"), name="proj_att",
    )(x, mod, lw["norm1_g"], lw["w_att"], cos, sin, lw["qk_gain"], tabs["ones512"])

    gates = pl.pallas_call(
        _proj_gate_kernel, grid=grid,
        in_specs=[x_spec, mod_spec, g_spec, _full(lw["w_gate"].shape)],
        out_specs=pl.BlockSpec((tm, GATE_COLS), lambda i: (i, 0)),
        out_shape=jax.ShapeDtypeStruct((N, GATE_COLS), BF16),
        compiler_params=_params("parallel"), name="proj_gate",
    )(x, mod, lw["norm1_g"], lw["w_gate"])

    tm_rw = min(256, T)
    tps_rw = T // tm_rw
    x_spec, mod_spec, g_spec = _token_specs(tm_rw, tps_rw)
    rows8 = tm_rw // 8
    last8 = N // 8 - 1
    rwp = pl.pallas_call(
        functools.partial(_proj_rw_kernel, tm=tm_rw, tiles_per_seq=tps_rw), grid=(N // tm_rw,),
        in_specs=[x_spec,
                  pl.BlockSpec((8, D_MODEL), lambda i: (jnp.maximum(i * rows8 - 1, 0), 0)),
                  pl.BlockSpec((8, D_MODEL), lambda i: (jnp.minimum((i + 1) * rows8, last8), 0)),
                  mod_spec, g_spec, _full(lw["w_rw"].shape), _full(lw["shift_mu"].shape),
                  _full(lw["rw_vec"].shape), _full(lw["rw_w2"].shape), _full(lw["rw_a2"].shape),
                  _full(lw["rw_g2"].shape), _full(tabs["ones256"].shape)],
        out_specs=pl.BlockSpec((tm_rw, RW_FIELDS * RWKV_WIDTH), lambda i: (i, 0)),
        out_shape=jax.ShapeDtypeStruct((N, RW_FIELDS * RWKV_WIDTH), F32),
        scratch_shapes=[pltpu.VMEM((tm_rw + 16, RW_COLS), F32)],
        compiler_params=_params("parallel"), name="proj_rw",
    )(x, x, x, mod, lw["norm1_g"], lw["w_rw"], lw["shift_mu"], lw["rw_vec"], lw["rw_w2"], lw["rw_a2"],
      lw["rw_g2"], tabs["ones256"])
    return ret, q, k, v, gates, rwp


def _retention_tables(reverse):
    lg = np.log1p(-np.exp2(-5.0 - np.arange(RET_HEADS, dtype=np.float64)))
    if reverse:
        lg = lg[::-1]
    C = RET_CHUNK
    pos = np.arange(C, dtype=np.float64)
    diff = pos[:, None] - pos[None, :]
    if reverse:
        dec = np.where(diff < 0, np.exp(lg[:, None, None] * np.maximum(-diff, 0.0)[None]), 0.0)
        xi = np.exp(lg[:, None] * (C - pos)[None, :])
        zeta = np.exp(lg[:, None] * pos[None, :])
    else:
        dec = np.where(diff >= 0, np.exp(lg[:, None, None] * np.maximum(diff, 0.0)[None]), 0.0)
        xi = np.exp(lg[:, None] * (pos + 1.0)[None, :])
        zeta = np.exp(lg[:, None] * (C - 1.0 - pos)[None, :])
    chunk_decay = tuple(float(v) for v in np.exp(lg * C))
    widen = lambda a: jnp.asarray(np.repeat(a.T, HEAD_DIM, axis=1), F32)
    return jnp.asarray(dec, F32), widen(xi), widen(zeta), chunk_decay


def _retention_kernel(ret_ref, dec_ref, xi_ref, zeta_ref, *rest, chunk_decay, final, chunks):
    if final:
        yf_ref, gn_ref, ones_ref, o_ref, state, y_scr = rest
    else:
        o_ref, state, y_scr = rest

    @pl.when(pl.program_id(1) == 0)
    def _():
        state[...] = jnp.zeros_like(state)

    W, C = RET_WIDTH, RET_CHUNK
    for step in range(chunks):
        cb = chunks - 1 - step if final else step
        rows = slice(cb * C, (cb + 1) * C)
        q = ret_ref[rows, 0:W]
        k = ret_ref[rows, W:2 * W]
        v = ret_ref[rows, 2 * W:3 * W]
        qx = (q.astype(F32) * xi_ref[...]).astype(BF16)
        kzt = (k.astype(F32) * zeta_ref[...]).T
        for h in range(RET_HEADS):
            sl = slice(h * HEAD_DIM, (h + 1) * HEAD_DIM)
            s = _dot_nt(q[:, sl], k[:, sl]) * dec_ref[h]
            inner = _dot(s.astype(BF16), v[:, sl])
            r_prev = state[h]
            cross = _dot(qx[:, sl], r_prev.astype(BF16))
            kv = _dot(kzt[sl, :].astype(BF16), v[:, sl])
            state[h] = r_prev * chunk_decay[h] + kv
            y_scr[rows, sl] = inner + cross
    y = y_scr[...]
    if final:
        y = _head_layer_norm(y + yf_ref[...], ones_ref[...], RET_EPS) * gn_ref[...]
        o_ref[...] = (y * ret_ref[:, 3 * W:4 * W].astype(F32)).astype(BF16)
    else:
        o_ref[...] = y


def _retention(ret, B, T, ret_gn, ones256):
    N = ret.shape[0]
    chunks = min(RET_CHUNKS_PER_STEP, T // RET_CHUNK)
    R = chunks * RET_CHUNK
    nblk = T // R
    scratch = [pltpu.VMEM((RET_HEADS, HEAD_DIM, HEAD_DIM), F32), pltpu.VMEM((R, RET_WIDTH), F32)]

    dec, xi, zeta, cd = _retention_tables(False)
    fwd_row = lambda b, c: (b * nblk + c, 0)
    yf = pl.pallas_call(
        functools.partial(_retention_kernel, chunk_decay=cd, final=False, chunks=chunks), grid=(B, nblk),
        in_specs=[pl.BlockSpec((R, RET_COLS), fwd_row), _full(dec.shape), _full(xi.shape), _full(zeta.shape)],
        out_specs=pl.BlockSpec((R, RET_WIDTH), fwd_row),
        out_shape=jax.ShapeDtypeStruct((N, RET_WIDTH), F32),
        scratch_shapes=scratch, compiler_params=_params("parallel", "arbitrary"), name="retention_fwd",
    )(ret, dec, xi, zeta)

    dec, xi, zeta, cd = _retention_tables(True)
    bwd_row = lambda b, c: (b * nblk + nblk - 1 - c, 0)
    return pl.pallas_call(
        functools.partial(_retention_kernel, chunk_decay=cd, final=True, chunks=chunks), grid=(B, nblk),
        in_specs=[pl.BlockSpec((R, RET_COLS), bwd_row), _full(dec.shape), _full(xi.shape), _full(zeta.shape),
                  pl.BlockSpec((R, RET_WIDTH), bwd_row), _full(ret_gn.shape), _full(ones256.shape)],
        out_specs=pl.BlockSpec((R, RET_WIDTH), bwd_row),
        out_shape=jax.ShapeDtypeStruct((N, RET_WIDTH), BF16),
        scratch_shapes=scratch, compiler_params=_params("parallel", "arbitrary"), name="retention_bwd",
    )(ret, dec, xi, zeta, yf, ret_gn, ones256)


def _flash_kernel(q_ref, k_ref, v_ref, o_ref, q_scr, m_scr, acc_scr, *, tq, tk):
    j = pl.program_id(3)

    @pl.when(j == 0)
    def _():
        for g in range(ATT_GROUP):
            q_scr[g * tq:(g + 1) * tq, :] = q_ref[:, g * HEAD_DIM:(g + 1) * HEAD_DIM]
        m_scr[...] = jnp.full_like(m_scr, -jnp.inf)
        acc_scr[...] = jnp.zeros_like(acc_scr)

    k = k_ref[...]
    v = v_ref[...]
    sub = min(FLASH_SUB_ROWS, ATT_GROUP * tq)
    for r0 in range(0, ATT_GROUP * tq, sub):
        rows = slice(r0, r0 + sub)
        s = _dot_nt(q_scr[rows, :], k)
        m_prev = m_scr[rows, :]
        m_next = jnp.maximum(m_prev, jnp.max(s, axis=1, keepdims=True))
        p = jnp.exp(s - jnp.concatenate([m_next] * (tk // V7X_LANES), axis=1))
        alpha = jnp.exp(m_prev - m_next)
        m_scr[rows, :] = m_next
        acc_scr[rows, :] = alpha * acc_scr[rows, :] + _dot(p.astype(BF16), v)

    @pl.when(j == pl.num_programs(3) - 1)
    def _():
        o = acc_scr[:, 0:HEAD_DIM] / acc_scr[:, HEAD_DIM:2 * HEAD_DIM]
        for g in range(ATT_GROUP):
            o_ref[:, g * HEAD_DIM:(g + 1) * HEAD_DIM] = o[g * tq:(g + 1) * tq, :].astype(BF16)


def _attention(q, k, v, B, T):
    N = q.shape[0]
    tq = min(FLASH_Q_ROWS, T)
    tk = min(FLASH_KV_ROWS, T)
    nq, nk = T // tq, T // tk
    GW = ATT_GROUP * HEAD_DIM
    rows = ATT_GROUP * tq
    return pl.pallas_call(
        functools.partial(_flash_kernel, tq=tq, tk=tk), grid=(B, ATT_KV_HEADS, nq, nk),
        in_specs=[pl.BlockSpec((tq, GW), lambda b, h, i, j: (b * nq + i, h)),
                  pl.BlockSpec((None, tk, HEAD_DIM), lambda b, h, i, j: (h, b * nk + j, 0)),
                  pl.BlockSpec((None, tk, 2 * HEAD_DIM), lambda b, h, i, j: (h, b * nk + j, 0))],
        out_specs=pl.BlockSpec((tq, GW), lambda b, h, i, j: (b * nq + i, h)),
        out_shape=jax.ShapeDtypeStruct((N, ATT_Q_WIDTH), BF16),
        scratch_shapes=[pltpu.VMEM((rows, HEAD_DIM), BF16), pltpu.VMEM((rows, V7X_LANES), F32),
                        pltpu.VMEM((rows, 2 * HEAD_DIM), F32)],
        compiler_params=_params("parallel", "parallel", "parallel", "arbitrary"), name="flash_attention",
    )(q, k, v)


def _rwkv_scan_kernel(*refs, rows_per_set):
    C = RW_CHUNK
    H = C // 2
    W = RWKV_WIDTH
    SUB = 8
    n_sets = len(rows_per_set)
    ones_ref = refs[12 * n_sets]
    outs = refs[12 * n_sets + 1:12 * n_sets + 1 + 2 * n_sets]
    state, vk_scr = refs[-2:]
    dirs = [(refs[12 * s:12 * s + 6], refs[12 * s + 6:12 * s + 12]) for s in range(n_sets)]
    chains = [(s, n, d) for s in range(n_sets) for n in range(rows_per_set[s]) for d in range(2)]

    first_chunk = pl.program_id(1) == 0

    @pl.when(first_chunk)
    def _():
        for c, (s, n, d) in enumerate(chains):
            if s == 0:
                state[c] = jnp.zeros((HEAD_DIM, W), F32)

    @pl.when(jnp.logical_and(first_chunk, pl.program_id(0) == 0))
    def _():
        for c, (s, n, d) in enumerate(chains):
            if s > 0:
                state[c] = jnp.zeros((HEAD_DIM, W), F32)

    ones = ones_ref[...]
    same_head = (lax.broadcasted_iota(jnp.int32, (W, W), 0) // HEAD_DIM
                 == lax.broadcasted_iota(jnp.int32, (W, W), 1) // HEAD_DIM)
    step_iota = lax.broadcasted_iota(jnp.int32, (H, HEAD_DIM, W), 0)
    lane_iota = lax.broadcasted_iota(jnp.int32, (H, HEAD_DIM, W), 2) & (HEAD_DIM - 1)

    def head_sums(vals):
        stacked = jnp.concatenate([v.astype(BF16) for v in vals], axis=0)
        out = _dot(stacked, ones)
        return [out[c * HEAD_DIM:(c + 1) * HEAD_DIM, :] for c in range(len(vals))]

    own_head = (lax.broadcasted_iota(jnp.int32, (SUB, W), 0)
                == lax.broadcasted_iota(jnp.int32, (SUB, W), 1) // HEAD_DIM)
    carry = tuple(state[c] for c in range(len(chains)))
    for half in range(2):
        lo = (half * H, (1 - half) * H)
        for c, (s, n, d) in enumerate(chains):
            kc = dirs[s][d][3][n]
            kbd = jnp.where(same_head, jnp.concatenate([kc] * (W // C), axis=0), 0.0).astype(BF16)
            sel = jnp.where(step_iota + lo[d] == lane_iota, dirs[s][d][5][n][None], 0.0).reshape(H * HEAD_DIM, W)
            vk_scr[c] = _dot(sel.astype(BF16), kbd).reshape(H, HEAD_DIM, W)

        def group(g, states, lo=lo):
            states = list(states)
            bases = (lo[0] + g * SUB, lo[1] + (H // SUB - 1 - g) * SUB)
            rows = [[ref[n, pl.ds(pl.multiple_of(bases[d], SUB), SUB), :] for ref in dirs[s][d][:5]]
                    for (s, n, d) in chains]
            ys = [[None] * SUB for _ in chains]
            for j in range(SUB):
                jj = (j, SUB - 1 - j)
                row = lambda c, f: rows[c][f][jj[chains[c][2]]:jj[chains[c][2]] + 1, :]
                sa = head_sums([states[c] * row(c, 0) for c in range(len(chains))])
                new = [states[c] * row(c, 2) - sa[c] * row(c, 4)
                       + vk_scr[c, bases[chains[c][2]] + jj[chains[c][2]] - lo[chains[c][2]]]
                       for c in range(len(chains))]
                for c, (s, n, d) in enumerate(chains):
                    read = new[c] if d == 0 else states[c]
                    r_heads = jnp.where(own_head, row(c, 1), 0.0).astype(BF16)
                    ys[c][jj[d]] = _dot_nt(r_heads, read.astype(BF16))
                states = new
            for c, (s, n, d) in enumerate(chains):
                span = pl.ds(pl.multiple_of(bases[d], SUB), SUB)
                for h in range(RWKV_HEADS):
                    col = jnp.concatenate([ys[c][r][h:h + 1, :] for r in range(SUB)], axis=0)
                    outs[2 * s + d][n, span, h * HEAD_DIM:(h + 1) * HEAD_DIM] = col
            return tuple(states)

        carry = lax.fori_loop(0, H // SUB, group, carry)

    for c in range(len(chains)):
        state[c] = carry[c]


def _rwkv_scan_call(sets, ones256):
    W, C = RWKV_WIDTH, RW_CHUNK
    _, B0, T0, nb0 = sets[0]
    nc0 = T0 // C
    operands, in_specs, out_specs, out_shapes = [], [], [], []
    for s, (rwp, B, T, nb) in enumerate(sets):
        nc = T // C
        fields = rwp.reshape(B, T, RW_FIELDS * W)
        v = rwp[:, 2 * W:3 * W].reshape(B, nc, C, RWKV_HEADS, HEAD_DIM)
        vt = v.transpose(0, 1, 4, 3, 2).reshape(B, nc, HEAD_DIM, W)

        def block(b, c, reverse, s=s, nc=nc):
            bi, ci = (b, c) if s == 0 else (0, b * nc0 + c)
            return bi, (nc - 1 - ci if reverse else ci)

        field = lambda f, rev, nb=nb, block=block: pl.BlockSpec(
            (nb, C, W), lambda b, c: block(b, c, rev) + (f,))
        vt_spec = lambda rev, nb=nb, block=block: pl.BlockSpec(
            (nb, None, HEAD_DIM, W), lambda b, c: block(b, c, rev) + (0, 0))
        in_specs += [field(1, False), field(0, False), field(3, False), field(5, False), field(7, False),
                     vt_spec(False),
                     field(1, True), field(0, True), field(4, True), field(6, True), field(8, True),
                     vt_spec(True)]
        operands += [fields] * 5 + [vt] + [fields] * 5 + [vt]
        y_spec = lambda rev, nb=nb, block=block: pl.BlockSpec(
            (nb, C, W), lambda b, c: block(b, c, rev) + (0,))
        out_specs += [y_spec(False), y_spec(True)]
        out_shapes += [jax.ShapeDtypeStruct((B, T, W), F32)] * 2
    n_chains = 2 * sum(nb for _, _, _, nb in sets)
    outs = pl.pallas_call(
        functools.partial(_rwkv_scan_kernel, rows_per_set=tuple(nb for _, _, _, nb in sets)),
        grid=(B0 // nb0, nc0), in_specs=in_specs + [_full(ones256.shape)],
        out_specs=out_specs, out_shape=out_shapes,
        scratch_shapes=[pltpu.VMEM((n_chains, HEAD_DIM, W), F32),
                        pltpu.VMEM((n_chains, C // 2, HEAD_DIM, W), F32)],
        compiler_params=_params("arbitrary", "arbitrary"), name="rwkv_scan",
    )(*operands, ones256)
    return [(outs[2 * s].reshape(B * T, W), outs[2 * s + 1].reshape(B * T, W))
            for s, (_, B, T, _) in enumerate(sets)]


def _rwkv_scan(groups, ones256):
    rows = lambda B: RW_BATCH_ROWS if B % RW_BATCH_ROWS == 0 else 1
    if len(groups) == 2:
        (_, B0, T0), (_, B1, T1) = groups
        if B1 == 1 and (B0 // rows(B0)) * (T0 // RW_CHUNK) == T1 // RW_CHUNK:
            return _rwkv_scan_call([groups[0] + (rows(B0),), groups[1] + (1,)], ones256)
    return [_rwkv_scan_call([g + (rows(g[1]),)], ones256)[0] for g in groups]


def _merge_kernel(x_ref, mod_ref, ya_ref, yb_ref, yf_ref, ybk_ref, rg_ref, bonus_ref, gn_ref, gates_ref,
                  ones_ref, wa_ref, wb_ref, wc_ref, wo_ref, o_ref):
    y = _head_layer_norm(yf_ref[...] + ybk_ref[...], ones_ref[...], RWKV_GN_EPS) * gn_ref[...]
    yc = ((y + bonus_ref[...]) * rg_ref[...]).astype(BF16)
    D = D_MODEL
    merged = (gates_ref[:, 0:D].astype(F32) * _dot(ya_ref[...], wa_ref[...])
              + gates_ref[:, D:2 * D].astype(F32) * _dot(yb_ref[...], wb_ref[...])
              + gates_ref[:, 2 * D:3 * D].astype(F32) * _dot(yc, wc_ref[...]))
    o_ref[...] = x_ref[...] + mod_ref[2:3, :] * _dot(merged.astype(BF16), wo_ref[...])


def _merge(x, mod, T, ya, yb, yf, ybk, rwp, gates, lw, ones256):
    N = x.shape[0]
    tm = min(512, T)
    tps = T // tm
    x_spec, mod_spec, _ = _token_specs(tm, tps)
    row = lambda w: pl.BlockSpec((tm, w), lambda i: (i, 0))
    field = lambda f: pl.BlockSpec((tm, RWKV_WIDTH), lambda i: (i, f))
    return pl.pallas_call(
        _merge_kernel, grid=(N // tm,),
        in_specs=[x_spec, mod_spec, row(RET_WIDTH), row(ATT_Q_WIDTH), row(RWKV_WIDTH), row(RWKV_WIDTH),
                  field(9), field(10), _full(lw["rw_gn"].shape), row(GATE_COLS), _full(ones256.shape),
                  _full(lw["w_branch_a"].shape), _full(lw["w_branch_b"].shape), _full(lw["w_branch_c"].shape),
                  _full(lw["w_out"].shape)],
        out_specs=x_spec, out_shape=jax.ShapeDtypeStruct((N, D_MODEL), F32),
        compiler_params=_params("parallel"), name="merge_out",
    )(x, mod, ya, yb, yf, ybk, rwp, rwp, lw["rw_gn"], gates, ones256,
      lw["w_branch_a"], lw["w_branch_b"], lw["w_branch_c"], lw["w_out"])


def _route(h, hb, rwt_ref, rbias_ref, tm):
    h_lo = (h - hb.astype(F32)).astype(BF16)
    rw_hi, rw_lo = _split(rwt_ref[...])
    logits = _dot_nt(rw_hi, hb) + _dot_nt(rw_hi, h_lo) + _dot_nt(rw_lo, hb)
    scores = _sigmoid(logits)
    choice = scores + jnp.concatenate([rbias_ref[...]] * (tm // V7X_LANES), axis=1)
    per_group = N_EXPERTS // N_GROUPS
    sub = lax.broadcasted_iota(jnp.int32, (per_group, tm), 0)
    groups, gscore = [], []
    for g in range(N_GROUPS):
        cg = choice[g * per_group:(g + 1) * per_group, :]
        m1 = jnp.max(cg, axis=0, keepdims=True)
        first = jnp.min(jnp.where(cg == m1, sub, per_group), axis=0, keepdims=True)
        m2 = jnp.max(jnp.where(sub == first, -jnp.inf, cg), axis=0, keepdims=True)
        groups.append(cg)
        gscore.append(m1 + m2)
    masked = []
    for g in range(N_GROUPS):
        beaten = jnp.zeros((1, tm), F32)
        for o in range(N_GROUPS):
            if o == g:
                continue
            wins = (gscore[o] >= gscore[g]) if o < g else (gscore[o] > gscore[g])
            beaten = beaten + jnp.where(wins, 1.0, 0.0)
        keep = jnp.where(beaten < TOPK_GROUPS, 1.0, 0.0)
        masked.append(jnp.where(jnp.broadcast_to(keep, (per_group, tm)) > 0.5, groups[g], -jnp.inf))
    mc = jnp.concatenate(masked, axis=0)
    eidx = lax.broadcasted_iota(jnp.int32, (N_EXPERTS, tm), 0).astype(F32)
    sel = jnp.zeros((N_EXPERTS, tm), F32)
    for _ in range(TOP_K):
        top = jnp.max(mc, axis=0, keepdims=True)
        first = jnp.min(jnp.where(mc == top, eidx, float(N_EXPERTS)), axis=0, keepdims=True)
        pick = eidx == first
        sel = jnp.where(pick, 1.0, sel)
        mc = jnp.where(pick, -jnp.inf, mc)
    wts = jnp.where(sel > 0.5, scores, 0.0)
    return wts / jnp.sum(wts, axis=0, keepdims=True) * ROUTED_SCALE, sel


def _selected_lists(comb, sel, rank):
    eidx = lax.broadcasted_iota(jnp.int32, comb.shape, 0).astype(F32)
    prev = jnp.full((1, comb.shape[1]), -1.0, F32)
    ids, wts, pos = [], [], []
    for _ in range(TOP_K):
        cand = jnp.where(sel > 0.5, jnp.where(eidx > prev, eidx, float(N_EXPERTS)), float(N_EXPERTS))
        prev = jnp.min(cand, axis=0, keepdims=True)
        wts.append(jnp.sum(jnp.where(eidx == prev, comb, 0.0), axis=0, keepdims=True))
        pos.append(jnp.sum(jnp.where(eidx == prev, rank, 0.0), axis=0, keepdims=True))
        ids.append(jnp.minimum(prev, N_EXPERTS - 1.0))
    as_int = lambda rows: jnp.concatenate(rows, axis=0).astype(jnp.int32)
    return as_int(ids), jnp.concatenate(wts, axis=0), as_int(pos)


def _pack_halves(y):
    n = y.shape[1] // 2
    hi = pltpu.bitcast(y[:, :n].astype(BF16).astype(F32), jnp.int32)
    lo = pltpu.bitcast(y[:, n:].astype(BF16).astype(F32), jnp.int32)
    return hi | lax.shift_right_logical(lo, 16)


def _unpack_halves(w):
    hi = pltpu.bitcast(w & jnp.int32(-65536), F32)
    lo = pltpu.bitcast(lax.shift_left(w, 16), F32)
    return hi, lo


def _moe_route_kernel(x_ref, mod_ref, g_ref, rwt_ref, rbias_ref, shgu_ref, shd_ref, tri_ref,
                      hp_ref, ids_ref, wts_ref, pos_ref, base_ref, count_ref, *, tm):
    F = EXPERT_FF

    @pl.when(pl.program_id(0) == 0)
    def _():
        count_ref[...] = jnp.zeros_like(count_ref)

    x = x_ref[...]
    h = _norm_mod(x, g_ref[...], mod_ref[3:4, :], mod_ref[4:5, :])
    hb = h.astype(BF16)
    hp_ref[...] = _pack_halves(hb.astype(F32))
    comb, sel = _route(h, hb, rwt_ref, rbias_ref, tm)
    before = count_ref[...]
    inclusive = _dot(sel.astype(BF16), tri_ref[...])
    rank = inclusive - sel + jnp.concatenate([before] * (tm // V7X_LANES), axis=1)
    count_ref[...] = before + jnp.sum(sel, axis=1, keepdims=True)
    ids, wts, pos = _selected_lists(comb, sel, rank)
    ids_ref[...] = ids
    wts_ref[...] = wts
    pos_ref[...] = pos
    gu = _dot(hb, shgu_ref[...])
    shared = _dot((_silu(gu[:, 0:F]) * gu[:, F:2 * F]).astype(BF16), shd_ref[...])
    base_ref[...] = x + mod_ref[5:6, :] * shared


def _moe_expert_kernel(tile_expert_ref, xs_ref, wgu_ref, wd_ref, o_ref):
    del tile_expert_ref
    F = EXPERT_FF
    half = D_MODEL // 2
    hi, lo = _unpack_halves(xs_ref[...])
    gu = _dot(hi.astype(BF16), wgu_ref[0:half, :]) + _dot(lo.astype(BF16), wgu_ref[half:, :])
    act = (_silu(gu[:, 0:F]) * gu[:, F:2 * F]).astype(BF16)
    o_ref[...] = _pack_halves(_dot(act, wd_ref[...]))


def _moe_combine_kernel(base_ref, mod_ref, y_ref, w_ref, fin_ref, o_ref, *, final):
    half = D_MODEL // 2
    acc_hi = jnp.zeros((base_ref.shape[0], half), F32)
    acc_lo = jnp.zeros((base_ref.shape[0], half), F32)
    for k in range(TOP_K):
        hi, lo = _unpack_halves(y_ref[k])
        wk = w_ref[:, k:k + 1]
        acc_hi = acc_hi + wk * hi
        acc_lo = acc_lo + wk * lo
    out_hi = base_ref[:, 0:half] + mod_ref[5:6, 0:half] * acc_hi
    out_lo = base_ref[:, half:] + mod_ref[5:6, half:] * acc_lo
    if final:
        ms = (jnp.sum(out_hi * out_hi, axis=-1, keepdims=True)
              + jnp.sum(out_lo * out_lo, axis=-1, keepdims=True)) * (1.0 / D_MODEL)
        scale = lax.rsqrt(ms + NORM_EPS)
        out_hi = out_hi * scale * fin_ref[:, 0:half]
        out_lo = out_lo * scale * fin_ref[:, half:]
    o_ref[:, 0:half] = out_hi
    o_ref[:, half:] = out_lo


def _gather_rows(table, idx):
    rows, width = idx.shape[0], table.shape[1]
    info = plsc.get_sparse_core_info()
    workers = info.num_cores * info.num_subcores
    windows = rows // (workers * SC_GATHER_ROWS)
    assert windows * workers * SC_GATHER_ROWS == rows
    mesh = plsc.VectorSubcoreMesh(core_axis_name="core", subcore_axis_name="subcore")

    @functools.partial(pl.kernel, out_type=jax.ShapeDtypeStruct((rows, width), table.dtype), mesh=mesh,
                       scratch_types=[pltpu.VMEM((SC_GATHER_ROWS,), jnp.int32),
                                      pltpu.VMEM((SC_GATHER_ROWS, width), table.dtype)],
                       name="moe_gather")
    def gather(table_hbm, idx_hbm, out_hbm, idx_vmem, rows_vmem):
        worker = lax.axis_index("subcore") * info.num_cores + lax.axis_index("core")
        base = worker * (windows * SC_GATHER_ROWS)

        @pl.loop(0, windows)
        def _(j):
            span = pl.ds(pl.multiple_of(base + j * SC_GATHER_ROWS, SC_GATHER_ROWS), SC_GATHER_ROWS)
            pltpu.sync_copy(idx_hbm.at[span], idx_vmem)
            pltpu.sync_copy(table_hbm.at[idx_vmem], rows_vmem)
            pltpu.sync_copy(rows_vmem, out_hbm.at[span])

    return gather(table, idx)


def _scatter_rows(table, dest, total_rows):
    n_lists, n_rows = dest.shape
    width = table.shape[1]
    info = plsc.get_sparse_core_info()
    workers = info.num_cores * info.num_subcores
    windows = n_rows // (workers * SC_GATHER_ROWS)
    assert windows * workers * SC_GATHER_ROWS == n_rows
    windows_per_list = n_rows // SC_GATHER_ROWS
    mesh = plsc.VectorSubcoreMesh(core_axis_name="core", subcore_axis_name="subcore")

    @functools.partial(pl.kernel, out_type=jax.ShapeDtypeStruct((total_rows, width), table.dtype), mesh=mesh,
                       scratch_types=[pltpu.VMEM((1, SC_GATHER_ROWS), jnp.int32),
                                      pltpu.VMEM((SC_GATHER_ROWS, width), table.dtype)],
                       name="moe_scatter")
    def scatter(table_hbm, dest_hbm, out_hbm, idx_vmem, rows_vmem):
        worker = lax.axis_index("subcore") * info.num_cores + lax.axis_index("core")

        @pl.loop(0, windows)
        def _(j):
            window = worker * windows + j
            span = pl.ds(pl.multiple_of(window * SC_GATHER_ROWS, SC_GATHER_ROWS), SC_GATHER_ROWS)
            pltpu.sync_copy(table_hbm.at[span], rows_vmem)
            for k in range(n_lists):
                pltpu.sync_copy(dest_hbm.at[pl.ds(k * windows_per_list + window, 1)], idx_vmem)
                pltpu.sync_copy(rows_vmem, out_hbm.at[idx_vmem.at[0]])

    return scatter(table, dest.reshape(n_lists * windows_per_list, SC_GATHER_ROWS))


def _dispatch_plan(ids, pos, counts, n_tokens, tg):
    E = N_EXPERTS
    padded = (counts + tg - 1) // tg * tg
    ends = jnp.cumsum(padded)
    off = ends - padded
    onehot = ids[:, :, None] == jnp.arange(E, dtype=jnp.int32)[None, None, :]
    dest = pos + jnp.sum(jnp.where(onehot, off[None, None, :], 0), axis=-1)
    total = n_tokens * TOP_K + E * tg
    tile_start = jnp.arange(total // tg, dtype=jnp.int32) * tg
    tile_expert = jnp.sum((ends[None, :] <= tile_start[:, None]).astype(jnp.int32), axis=1)
    return dest, total, jnp.minimum(tile_expert, E - 1)


def _moe(x, mod, T, lw, final_g, final):
    N = x.shape[0]
    tm = min(MOE_TOKENS, T)
    tps = T // tm
    D, F, E = D_MODEL, EXPERT_FF, N_EXPERTS
    tg = max(MOE_GROUP_ROWS_MIN, min(MOE_GROUP_ROWS_MAX, N * TOP_K // (E * 8)))
    half = D // 2
    x_spec, mod_spec, g_spec = _token_specs(tm, tps)
    list_spec = pl.BlockSpec((TOP_K, tm), lambda i: (0, i))

    tri = jnp.asarray(np.triu(np.ones((tm, tm), np.float32)), BF16)
    hp, ids, wts, pos, base, counts = pl.pallas_call(
        functools.partial(_moe_route_kernel, tm=tm), grid=(N // tm,),
        in_specs=[x_spec, mod_spec, g_spec, _full((E, D)), _full((E, V7X_LANES)), _full((D, 2 * F)), _full((F, D)),
                  _full((tm, tm))],
        out_specs=[pl.BlockSpec((tm, half), lambda i: (i, 0)), list_spec, list_spec, list_spec, x_spec,
                   _full((E, V7X_LANES))],
        out_shape=[jax.ShapeDtypeStruct((N, half), jnp.int32), jax.ShapeDtypeStruct((TOP_K, N), jnp.int32),
                   jax.ShapeDtypeStruct((TOP_K, N), F32), jax.ShapeDtypeStruct((TOP_K, N), jnp.int32),
                   jax.ShapeDtypeStruct((N, D), F32), jax.ShapeDtypeStruct((E, V7X_LANES), F32)],
        compiler_params=_params("arbitrary"), name="moe_route",
    )(x, mod, lw["norm2_g"], lw["router_wt"], lw["router_bias"], lw["sh_gu"], lw["sh_d"], tri)

    dest, total, tile_expert = _dispatch_plan(ids, pos, counts[:, 0].astype(jnp.int32), N, tg)
    xs = _scatter_rows(hp, dest, total)
    n_tiles = total // tg
    ys = pl.pallas_call(
        _moe_expert_kernel,
        grid_spec=pltpu.PrefetchScalarGridSpec(
            num_scalar_prefetch=1, grid=(n_tiles,),
            in_specs=[pl.BlockSpec((tg, half), lambda i, te: (i, 0)),
                      pl.BlockSpec((None, D, 2 * F), lambda i, te: (te[i], 0, 0)),
                      pl.BlockSpec((None, F, D), lambda i, te: (te[i], 0, 0))],
            out_specs=pl.BlockSpec((tg, half), lambda i, te: (i, 0))),
        out_shape=jax.ShapeDtypeStruct(xs.shape, jnp.int32),
        compiler_params=_params("parallel"), name="moe_experts",
    )(tile_expert, xs, lw["exp_gu"], lw["exp_d"])

    yk = _gather_rows(ys, dest.reshape(-1)).reshape(TOP_K, N, half)
    return pl.pallas_call(
        functools.partial(_moe_combine_kernel, final=final), grid=(N // tm,),
        in_specs=[x_spec, mod_spec, pl.BlockSpec((TOP_K, tm, half), lambda i: (0, i, 0)),
                  pl.BlockSpec((tm, TOP_K), lambda i: (i, 0)), g_spec],
        out_specs=x_spec, out_shape=jax.ShapeDtypeStruct((N, D), F32),
        compiler_params=_params("parallel"), name="moe_combine",
    )(base, mod, yk, wts.T, final_g)


def _block_diag2(a, b):
    za = jnp.zeros((a.shape[0], b.shape[1]), a.dtype)
    zb = jnp.zeros((b.shape[0], a.shape[1]), a.dtype)
    return jnp.concatenate([jnp.concatenate([a, za], 1), jnp.concatenate([zb, b], 1)], 0)


def _layer_weights(P, l):
    w_in = P["w_in"][l]
    o1, o2, o3 = RET_COLS, RET_COLS + ATT_COLS, RET_COLS + ATT_COLS + RW_COLS
    w_ret, w_att, w_rw, w_gate = w_in[:, :o1], w_in[:, o1:o2], w_in[:, o2:o3], w_in[:, o3:]
    qk_gain = jnp.zeros((8, ATT_Q_WIDTH), F32)
    qk_gain = qk_gain.at[0].set(jnp.tile(P["q_norm_g"][l], ATT_Q_HEADS))
    qk_gain = qk_gain.at[1, :ATT_KV_WIDTH].set(jnp.tile(P["k_norm_g"][l], ATT_KV_HEADS))
    rw_vec = jnp.zeros((8, RWKV_WIDTH), F32)
    for j, name in enumerate(("rw_w0_f", "rw_w0_b", "rw_a0_f", "rw_a0_b", "rw_k_k", "rw_k_a")):
        rw_vec = rw_vec.at[j].set(P[name][l])
    rw_vec = rw_vec.at[6].set(P["rw_r_k"][l].reshape(RWKV_WIDTH))
    row = lambda v: v.reshape(1, -1)
    return {
        "norm1_g": row(P["norm1_g"][l]), "norm2_g": row(P["norm2_g"][l]),
        "w_ret": w_ret.astype(BF16), "w_att": w_att.astype(BF16), "w_rw": w_rw.astype(BF16),
        "w_gate": w_gate.astype(BF16), "qk_gain": qk_gain, "ret_gn": row(P["ret_gn"][l]),
        "shift_mu": P["shift_mu"][l], "rw_vec": rw_vec,
        "rw_w2": _block_diag2(P["rw_w2_f"][l], P["rw_w2_b"][l]).astype(BF16),
        "rw_a2": _block_diag2(P["rw_a2_f"][l], P["rw_a2_b"][l]).astype(BF16),
        "rw_g2": P["rw_g2"][l].astype(BF16), "rw_gn": row(P["rw_gn"][l]),
        "w_branch_a": P["w_branch_a"][l].astype(BF16), "w_branch_b": P["w_branch_b"][l].astype(BF16),
        "w_branch_c": P["w_branch_c"][l].astype(BF16), "w_out": P["w_out"][l].astype(BF16),
        "router_wt": P["router_w"][l].T,
        "router_bias": jnp.broadcast_to(P["router_bias"][l][:, None], (N_EXPERTS, V7X_LANES)),
        "sh_gu": jnp.concatenate([P["sh_w_gate"][l], P["sh_w_up"][l]], axis=1).astype(BF16),
        "sh_d": P["sh_w_down"][l].astype(BF16),
        "exp_gu": jnp.concatenate([P["exp_w_gate"][l], P["exp_w_up"][l]], axis=2).astype(BF16),
        "exp_d": P["exp_w_down"][l].astype(BF16),
    }


def _ones_block_diag(n):
    idx = np.arange(n) // HEAD_DIM
    return jnp.asarray(idx[:, None] == idx[None, :], BF16)


def _tables(T):
    rows = T // GRID_W
    row = jnp.repeat(jnp.arange(rows, dtype=F32), GRID_W)
    col = jnp.tile(jnp.arange(GRID_W, dtype=F32), rows)
    freqs = ROPE_THETA ** (-jnp.arange(ROPE_PAIRS_PER_AXIS, dtype=F32) / ROPE_PAIRS_PER_AXIS)
    ang = jnp.concatenate([row[:, None] * freqs, col[:, None] * freqs], axis=-1)
    cos, sin = jnp.cos(ang), jnp.sin(ang)
    return {
        "cos": jnp.tile(jnp.concatenate([cos, cos], axis=-1), (1, ATT_Q_HEADS)),
        "sin": jnp.tile(jnp.concatenate([-sin, sin], axis=-1), (1, ATT_Q_HEADS)),
        "ones128": _ones_block_diag(128), "ones256": _ones_block_diag(256), "ones512": _ones_block_diag(512),
    }


def _trunks(requests, P, layer_weights):
    groups = []
    for x, c in requests:
        B, T, D = x.shape
        groups.append({"B": B, "T": T, "x": x.reshape(B * T, D), "tabs": _tables(T),
                       "mods": _ada_mod(c, P["ada_w"], P["ada_b"])})
    final_g = P["final_g"].reshape(1, D_MODEL)
    depth = len(layer_weights)
    for l, lw in enumerate(layer_weights):
        mixed = []
        for g in groups:
            B, T, tabs, mod = g["B"], g["T"], g["tabs"], g["mods"][l]
            ret, q, k, v, gates, rwp = _input_projections(g["x"], mod, T, lw, tabs)
            ya = _retention(ret, B, T, lw["ret_gn"], tabs["ones256"])
            yb = _attention(q, k, v, B, T)
            mixed.append((mod, ya, yb, gates, rwp))
        scans = _rwkv_scan([(m[4], g["B"], g["T"]) for m, g in zip(mixed, groups)], groups[0]["tabs"]["ones256"])
        for g, (mod, ya, yb, gates, rwp), (yf, ybk) in zip(groups, mixed, scans):
            x = _merge(g["x"], mod, g["T"], ya, yb, yf, ybk, rwp, gates, lw, g["tabs"]["ones256"])
            g["x"] = _moe(x, mod, g["T"], lw, final_g, final=(l == depth - 1))
    return [g["x"].reshape(g["B"], g["T"], D_MODEL) for g in groups]


def kernel(x_prompt, x_sample, c_prompt, c_sample, norm1_g, norm2_g, final_g, ada_w, ada_b, w_in, ret_gn, q_norm_g, k_norm_g, shift_mu, rw_w0_f, rw_w2_f, rw_w0_b, rw_w2_b, rw_a0_f, rw_a2_f, rw_a0_b, rw_a2_b, rw_g2, rw_k_k, rw_k_a, rw_r_k, rw_gn, w_branch_a, w_branch_b, w_branch_c, w_out, router_w, router_bias, exp_w_gate, exp_w_up, exp_w_down, sh_w_gate, sh_w_up, sh_w_down):
    P = {
        "norm1_g": norm1_g, "norm2_g": norm2_g, "final_g": final_g, "ada_w": ada_w, "ada_b": ada_b,
        "w_in": w_in, "ret_gn": ret_gn, "q_norm_g": q_norm_g, "k_norm_g": k_norm_g, "shift_mu": shift_mu,
        "rw_w0_f": rw_w0_f, "rw_w2_f": rw_w2_f, "rw_w0_b": rw_w0_b, "rw_w2_b": rw_w2_b,
        "rw_a0_f": rw_a0_f, "rw_a2_f": rw_a2_f, "rw_a0_b": rw_a0_b, "rw_a2_b": rw_a2_b,
        "rw_g2": rw_g2, "rw_k_k": rw_k_k, "rw_k_a": rw_k_a, "rw_r_k": rw_r_k, "rw_gn": rw_gn,
        "w_branch_a": w_branch_a, "w_branch_b": w_branch_b, "w_branch_c": w_branch_c, "w_out": w_out,
        "router_w": router_w, "router_bias": router_bias,
        "exp_w_gate": exp_w_gate, "exp_w_up": exp_w_up, "exp_w_down": exp_w_down,
        "sh_w_gate": sh_w_gate, "sh_w_up": sh_w_up, "sh_w_down": sh_w_down,
    }
    layer_weights = [_layer_weights(P, l) for l in range(w_in.shape[0])]
    y_prompt, y_sample = _trunks([(x_prompt, c_prompt), (x_sample, c_sample)], P, layer_weights)
    return (y_prompt, y_sample)
```

```python
import functools

import numpy as np
import jax
import jax.numpy as jnp
from jax import lax
from jax.experimental import pallas as pl
from jax.experimental.pallas import tpu as pltpu
from jax.experimental.pallas import tpu_sc as plsc

F32 = jnp.float32
BF16 = jnp.bfloat16

D_MODEL = 1024
DEPTH = 2
GRID_W = 64
HEAD_DIM = 64
ROPE_THETA = 10000.0
ROPE_PAIRS_PER_AXIS = HEAD_DIM // 4

RET_HEADS = 4
RET_WIDTH = RET_HEADS * HEAD_DIM
RET_CHUNK = 128
RET_CHUNKS_PER_STEP = 8
RET_EPS = 1e-5

ATT_Q_HEADS = 8
ATT_KV_HEADS = 2
ATT_GROUP = ATT_Q_HEADS // ATT_KV_HEADS
ATT_Q_WIDTH = ATT_Q_HEADS * HEAD_DIM
ATT_KV_WIDTH = ATT_KV_HEADS * HEAD_DIM

RWKV_HEADS = 4
RWKV_WIDTH = RWKV_HEADS * HEAD_DIM
DECAY_LORA = 64
AAA_LORA = 64
GATE_LORA = 128
RWKV_GN_EPS = 64e-5
RW_CHUNK = 64
RW_FIELDS = 11
RW_BATCH_ROWS = 4

FLASH_Q_ROWS = 1024
FLASH_KV_ROWS = 2048
FLASH_SUB_ROWS = 512

MOE_TOKENS = 512
MOE_GROUP_ROWS_MIN = 256
MOE_GROUP_ROWS_MAX = 1024
SC_GATHER_ROWS = 128

N_EXPERTS = 64
TOP_K = 8
N_GROUPS = 8
TOPK_GROUPS = 4
EXPERT_FF = 256
ROUTED_SCALE = 2.5
NORM_EPS = 1e-6

RET_COLS = 4 * RET_WIDTH
ATT_COLS = ATT_Q_WIDTH + 2 * ATT_KV_WIDTH
RW_COLS = 3 * RWKV_WIDTH + 2 * DECAY_LORA + 2 * AAA_LORA + GATE_LORA
GATE_COLS = 3 * D_MODEL

V7X_LANES = 128
VMEM_LIMIT_BYTES = 48 * 1024 * 1024


def _params(*dims):
    return pltpu.CompilerParams(dimension_semantics=dims, vmem_limit_bytes=VMEM_LIMIT_BYTES)


def _sigmoid(x):
    return 1.0 / (1.0 + jnp.exp(-x))


def _silu(x):
    return x * _sigmoid(x)


def _dot(a, b):
    return jnp.dot(a, b, preferred_element_type=F32)


def _dot_nt(a, b):
    return lax.dot_general(a, b, (((1,), (1,)), ((), ())), preferred_element_type=F32)


def _split(a):
    hi = a.astype(BF16)
    lo = (a - hi.astype(F32)).astype(BF16)
    return hi, lo


def _dot_split_lhs(a, b_bf16):
    hi, lo = _split(a)
    return _dot(hi, b_bf16) + _dot(lo, b_bf16)


def _dot3(a, b):
    ah, al = _split(a)
    bh, bl = _split(b)
    return _dot(ah, bh) + _dot(ah, bl) + _dot(al, bh)


def _swap_rotary_halves(x):
    half = HEAD_DIM // 2
    lane = lax.broadcasted_iota(jnp.int32, x.shape, 1) & (HEAD_DIM - 1)
    return jnp.where(lane < half, pltpu.roll(x, x.shape[1] - half, 1), pltpu.roll(x, half, 1))


def _norm_mod(x, gain, shift, scale):
    ms = jnp.mean(x * x, axis=-1, keepdims=True)
    return x * lax.rsqrt(ms + NORM_EPS) * gain * (1.0 + scale) + shift


def _head_layer_norm(y, ones_bd, eps):
    mean = _dot_split_lhs(y, ones_bd) * (1.0 / HEAD_DIM)
    yc = y - mean
    var = _dot_split_lhs(yc * yc, ones_bd) * (1.0 / HEAD_DIM)
    return yc * lax.rsqrt(var + eps)


def _ada_kernel(c_ref, w_ref, b_ref, o_ref):
    c = c_ref[...]
    o_ref[...] = _dot3(_silu(c), w_ref[...]) + b_ref[...]


def _ada_mod(c, ada_w, ada_b):
    B = c.shape[0]
    Bp = max(8, B)
    cp = jnp.zeros((Bp, D_MODEL), F32).at[:B].set(c)
    L = ada_w.shape[0]
    tn = 1536
    out = pl.pallas_call(
        _ada_kernel,
        grid=(L, 6 * D_MODEL // tn),
        in_specs=[pl.BlockSpec((Bp, D_MODEL), lambda l, j: (0, 0)),
                  pl.BlockSpec((None, D_MODEL, tn), lambda l, j: (l, 0, j)),
                  pl.BlockSpec((None, 1, tn), lambda l, j: (l, 0, j))],
        out_specs=pl.BlockSpec((None, Bp, tn), lambda l, j: (l, 0, j)),
        out_shape=jax.ShapeDtypeStruct((L, Bp, 6 * D_MODEL), F32),
        compiler_params=_params("parallel", "parallel"),
        name="ada_mod",
    )(cp, ada_w, ada_b.reshape(L, 1, 6 * D_MODEL))
    return out[:, :B].reshape(L, B, 6, D_MODEL)


def _proj_ret_kernel(x_ref, mod_ref, g_ref, w_ref, cos_ref, sin_ref, o_ref):
    h = _norm_mod(x_ref[...], g_ref[...], mod_ref[0:1, :], mod_ref[1:2, :]).astype(BF16)
    p = _dot(h, w_ref[...])
    W = RET_WIDTH
    cos = cos_ref[:, 0:W]
    sin = sin_ref[:, 0:W]
    q, k = p[:, 0:W], p[:, W:2 * W]
    o_ref[:, 0:W] = (q * cos + _swap_rotary_halves(q) * sin).astype(BF16)
    o_ref[:, W:2 * W] = ((k * cos + _swap_rotary_halves(k) * sin) * (HEAD_DIM ** -0.5)).astype(BF16)
    o_ref[:, 2 * W:3 * W] = p[:, 2 * W:3 * W].astype(BF16)
    o_ref[:, 3 * W:4 * W] = _silu(p[:, 3 * W:4 * W]).astype(BF16)


def _proj_att_kernel(x_ref, mod_ref, g_ref, w_ref, cos_ref, sin_ref, qkg_ref, ones_ref, q_ref, k_ref, v_ref):
    h = _norm_mod(x_ref[...], g_ref[...], mod_ref[0:1, :], mod_ref[1:2, :]).astype(BF16)
    p = _dot(h, w_ref[...])
    QW, KW = ATT_Q_WIDTH, ATT_KV_WIDTH
    q = p[:, 0:QW]
    k = p[:, QW:QW + KW]
    v = p[:, QW + KW:QW + 2 * KW]
    ones = ones_ref[...]
    rq = lax.rsqrt(_dot_split_lhs(q * q, ones) * (1.0 / HEAD_DIM) + NORM_EPS)
    rk = lax.rsqrt(_dot_split_lhs(k * k, ones[0:KW, 0:KW]) * (1.0 / HEAD_DIM) + NORM_EPS)
    cos = cos_ref[...]
    sin = sin_ref[...]
    qg = q * qkg_ref[0:1, :]
    kg = k * qkg_ref[1:2, 0:KW]
    qo = (qg * cos + _swap_rotary_halves(qg) * sin) * (rq * (HEAD_DIM ** -0.5))
    ko = (kg * cos[:, 0:KW] + _swap_rotary_halves(kg) * sin[:, 0:KW]) * rk
    q_ref[...] = qo.astype(BF16)
    kb = ko.astype(BF16)
    vb = v.astype(BF16)
    for hk in range(ATT_KV_HEADS):
        k_ref[hk] = kb[:, hk * HEAD_DIM:(hk + 1) * HEAD_DIM]
        v_ref[hk, :, 0:HEAD_DIM] = vb[:, hk * HEAD_DIM:(hk + 1) * HEAD_DIM]
        v_ref[hk, :, HEAD_DIM:2 * HEAD_DIM] = jnp.ones((vb.shape[0], HEAD_DIM), BF16)


def _proj_gate_kernel(x_ref, mod_ref, g_ref, w_ref, o_ref):
    h = _norm_mod(x_ref[...], g_ref[...], mod_ref[0:1, :], mod_ref[1:2, :]).astype(BF16)
    o_ref[...] = _sigmoid(_dot(h, w_ref[...])).astype(BF16)


def _proj_rw_kernel(x_ref, xp_ref, xn_ref, mod_ref, g_ref, w_ref, mu_ref, vec_ref, w2_ref, a2_ref, g2_ref,
                    ones_ref, o_ref, p_scr, *, tm, tiles_per_seq):
    i = pl.program_id(0)
    gain, shift, scale = g_ref[...], mod_ref[0:1, :], mod_ref[1:2, :]
    w = w_ref[...]
    h = _norm_mod(x_ref[...], gain, shift, scale).astype(BF16)
    p = _dot(h, w)
    hp = _norm_mod(xp_ref[...], gain, shift, scale).astype(BF16)
    hn = _norm_mod(xn_ref[...], gain, shift, scale).astype(BF16)
    first = (i % tiles_per_seq) == 0
    last = (i % tiles_per_seq) == tiles_per_seq - 1
    prev_row = jnp.where(first, 0.0, _dot(hp, w)[7:8, :])
    next_row = jnp.where(last, 0.0, _dot(hn, w)[0:1, :])
    p_scr[8:8 + tm, :] = p
    p_scr[7:8, :] = prev_row
    p_scr[8 + tm:9 + tm, :] = next_row
    prev = p_scr[7:7 + tm, :]
    nxt = p_scr[9:9 + tm, :]
    x = p + mu_ref[0:1, :] * (prev - p) + mu_ref[1:2, :] * (nxt - p)

    W = RWKV_WIDTH
    r, k, v = x[:, 0:W], x[:, W:2 * W], x[:, 2 * W:3 * W]
    xw = x[:, 3 * W:3 * W + 2 * DECAY_LORA]
    xa = x[:, 3 * W + 2 * DECAY_LORA:3 * W + 2 * DECAY_LORA + 2 * AAA_LORA]
    xg = x[:, 3 * W + 2 * DECAY_LORA + 2 * AAA_LORA:]
    w0_f, w0_b = vec_ref[0:1, :], vec_ref[1:2, :]
    a0_f, a0_b = vec_ref[2:3, :], vec_ref[3:4, :]
    k_k, k_a, r_k = vec_ref[4:5, :], vec_ref[5:6, :], vec_ref[6:7, :]
    wl = _dot(jnp.tanh(xw).astype(BF16), w2_ref[...])
    al = _dot(xa.astype(BF16), a2_ref[...])
    dec_c = float(np.exp(-0.5))
    w_f = jnp.exp(-dec_c * _sigmoid(w0_f + wl[:, 0:W]))
    w_b = jnp.exp(-dec_c * _sigmoid(w0_b + wl[:, W:2 * W]))
    a_f = _sigmoid(a0_f + al[:, 0:W])
    a_b = _sigmoid(a0_b + al[:, W:2 * W])
    gate = _dot(_sigmoid(xg).astype(BF16), g2_ref[...])
    ones = ones_ref[...]
    kk = k * k_k
    kk = kk * lax.rsqrt(_dot_split_lhs(kk * kk, ones) + 1e-12)
    k_f = k * (1.0 + (a_f - 1.0) * k_a)
    k_b = k * (1.0 + (a_b - 1.0) * k_a)
    bonus = _dot_split_lhs(r * k_f * r_k, ones) * v
    for j, val in enumerate((r, kk, v, w_f, w_b, k_f, k_b, kk * a_f, kk * a_b, gate, bonus)):
        o_ref[:, j * W:(j + 1) * W] = val


def _token_specs(tm, tiles_per_seq):
    x_spec = pl.BlockSpec((tm, D_MODEL), lambda i: (i, 0))
    mod_spec = pl.BlockSpec((None, 6, D_MODEL), lambda i: (i // tiles_per_seq, 0, 0))
    g_spec = pl.BlockSpec((1, D_MODEL), lambda i: (0, 0))
    return x_spec, mod_spec, g_spec


def _full(shape):
    nd = len(shape)
    return pl.BlockSpec(shape, lambda *_: (0,) * nd)


def _input_projections(x, mod, T, lw, tabs):
    N = x.shape[0]
    tm = min(512, T)
    tps = T // tm
    grid = (N // tm,)
    x_spec, mod_spec, g_spec = _token_specs(tm, tps)
    tab_spec = pl.BlockSpec((tm, ATT_Q_WIDTH), lambda i: (i % tps, 0))
    cos, sin = tabs["cos"], tabs["sin"]

    ret = pl.pallas_call(
        _proj_ret_kernel, grid=grid,
        in_specs=[x_spec, mod_spec, g_spec, _full(lw["w_ret"].shape), tab_spec, tab_spec],
        out_specs=pl.BlockSpec((tm, RET_COLS), lambda i: (i, 0)),
        out_shape=jax.ShapeDtypeStruct((N, RET_COLS), BF16),
        compiler_params=_params("parallel"), name="proj_ret",
    )(x, mod, lw["norm1_g"], lw["w_ret"], cos, sin)

    q, k, v = pl.pallas_call(
        _proj_att_kernel, grid=grid,
        in_specs=[x_spec, mod_spec, g_spec, _full(lw["w_att"].shape), tab_spec, tab_spec,
                  _full(lw["qk_gain"].shape), _full(tabs["ones512"].shape)],
        out_specs=[pl.BlockSpec((tm, ATT_Q_WIDTH), lambda i: (i, 0)),
                   pl.BlockSpec((ATT_KV_HEADS, tm, HEAD_DIM), lambda i: (0, i, 0)),
                   pl.BlockSpec((ATT_KV_HEADS, tm, 2 * HEAD_DIM), lambda i: (0, i, 0))],
        out_shape=[jax.ShapeDtypeStruct((N, ATT_Q_WIDTH), BF16),
                   jax.ShapeDtypeStruct((ATT_KV_HEADS, N, HEAD_DIM), BF16),
                   jax.ShapeDtypeStruct((ATT_KV_HEADS, N, 2 * HEAD_DIM), BF16)],
        compiler_params=_params("parallel"), name="proj_att",
    )(x, mod, lw["norm1_g"], lw["w_att"], cos, sin, lw["qk_gain"], tabs["ones512"])

    gates = pl.pallas_call(
        _proj_gate_kernel, grid=grid,
        in_specs=[x_spec, mod_spec, g_spec, _full(lw["w_gate"].shape)],
        out_specs=pl.BlockSpec((tm, GATE_COLS), lambda i: (i, 0)),
        out_shape=jax.ShapeDtypeStruct((N, GATE_COLS), BF16),
        compiler_params=_params("parallel"), name="proj_gate",
    )(x, mod, lw["norm1_g"], lw["w_gate"])

    tm_rw = min(256, T)
    tps_rw = T // tm_rw
    x_spec, mod_spec, g_spec = _token_specs(tm_rw, tps_rw)
    rows8 = tm_rw // 8
    last8 = N // 8 - 1
    rwp = pl.pallas_call(
        functools.partial(_proj_rw_kernel, tm=tm_rw, tiles_per_seq=tps_rw), grid=(N // tm_rw,),
        in_specs=[x_spec,
                  pl.BlockSpec((8, D_MODEL), lambda i: (jnp.maximum(i * rows8 - 1, 0), 0)),
                  pl.BlockSpec((8, D_MODEL), lambda i: (jnp.minimum((i + 1) * rows8, last8), 0)),
                  mod_spec, g_spec, _full(lw["w_rw"].shape), _full(lw["shift_mu"].shape),
                  _full(lw["rw_vec"].shape), _full(lw["rw_w2"].shape), _full(lw["rw_a2"].shape),
                  _full(lw["rw_g2"].shape), _full(tabs["ones256"].shape)],
        out_specs=pl.BlockSpec((tm_rw, RW_FIELDS * RWKV_WIDTH), lambda i: (i, 0)),
        out_shape=jax.ShapeDtypeStruct((N, RW_FIELDS * RWKV_WIDTH), F32),
        scratch_shapes=[pltpu.VMEM((tm_rw + 16, RW_COLS), F32)],
        compiler_params=_params("parallel"), name="proj_rw",
    )(x, x, x, mod, lw["norm1_g"], lw["w_rw"], lw["shift_mu"], lw["rw_vec"], lw["rw_w2"], lw["rw_a2"],
      lw["rw_g2"], tabs["ones256"])
    return ret, q, k, v, gates, rwp


def _retention_tables(reverse):
    lg = np.log1p(-np.exp2(-5.0 - np.arange(RET_HEADS, dtype=np.float64)))
    if reverse:
        lg = lg[::-1]
    C = RET_CHUNK
    pos = np.arange(C, dtype=np.float64)
    diff = pos[:, None] - pos[None, :]
    if reverse:
        dec = np.where(diff < 0, np.exp(lg[:, None, None] * np.maximum(-diff, 0.0)[None]), 0.0)
        xi = np.exp(lg[:, None] * (C - pos)[None, :])
        zeta = np.exp(lg[:, None] * pos[None, :])
    else:
        dec = np.where(diff >= 0, np.exp(lg[:, None, None] * np.maximum(diff, 0.0)[None]), 0.0)
        xi = np.exp(lg[:, None] * (pos + 1.0)[None, :])
        zeta = np.exp(lg[:, None] * (C - 1.0 - pos)[None, :])
    chunk_decay = tuple(float(v) for v in np.exp(lg * C))
    widen = lambda a: jnp.asarray(np.repeat(a.T, HEAD_DIM, axis=1), F32)
    return jnp.asarray(dec, F32), widen(xi), widen(zeta), chunk_decay


def _retention_kernel(ret_ref, dec_ref, xi_ref, zeta_ref, *rest, chunk_decay, final, chunks):
    if final:
        yf_ref, gn_ref, ones_ref, o_ref, state, y_scr = rest
    else:
        o_ref, state, y_scr = rest

    @pl.when(pl.program_id(1) == 0)
    def _():
        state[...] = jnp.zeros_like(state)

    W, C = RET_WIDTH, RET_CHUNK
    for step in range(chunks):
        cb = chunks - 1 - step if final else step
        rows = slice(cb * C, (cb + 1) * C)
        q = ret_ref[rows, 0:W]
        k = ret_ref[rows, W:2 * W]
        v = ret_ref[rows, 2 * W:3 * W]
        qx = (q.astype(F32) * xi_ref[...]).astype(BF16)
        kzt = (k.astype(F32) * zeta_ref[...]).T
        for h in range(RET_HEADS):
            sl = slice(h * HEAD_DIM, (h + 1) * HEAD_DIM)
            s = _dot_nt(q[:, sl], k[:, sl]) * dec_ref[h]
            inner = _dot(s.astype(BF16), v[:, sl])
            r_prev = state[h]
            cross = _dot(qx[:, sl], r_prev.astype(BF16))
            kv = _dot(kzt[sl, :].astype(BF16), v[:, sl])
            state[h] = r_prev * chunk_decay[h] + kv
            y_scr[rows, sl] = inner + cross
    y = y_scr[...]
    if final:
        y = _head_layer_norm(y + yf_ref[...], ones_ref[...], RET_EPS) * gn_ref[...]
        o_ref[...] = (y * ret_ref[:, 3 * W:4 * W].astype(F32)).astype(BF16)
    else:
        o_ref[...] = y


def _retention(ret, B, T, ret_gn, ones256):
    N = ret.shape[0]
    chunks = min(RET_CHUNKS_PER_STEP, T // RET_CHUNK)
    R = chunks * RET_CHUNK
    nblk = T // R
    scratch = [pltpu.VMEM((RET_HEADS, HEAD_DIM, HEAD_DIM), F32), pltpu.VMEM((R, RET_WIDTH), F32)]

    dec, xi, zeta, cd = _retention_tables(False)
    fwd_row = lambda b, c: (b * nblk + c, 0)
    yf = pl.pallas_call(
        functools.partial(_retention_kernel, chunk_decay=cd, final=False, chunks=chunks), grid=(B, nblk),
        in_specs=[pl.BlockSpec((R, RET_COLS), fwd_row), _full(dec.shape), _full(xi.shape), _full(zeta.shape)],
        out_specs=pl.BlockSpec((R, RET_WIDTH), fwd_row),
        out_shape=jax.ShapeDtypeStruct((N, RET_WIDTH), F32),
        scratch_shapes=scratch, compiler_params=_params("parallel", "arbitrary"), name="retention_fwd",
    )(ret, dec, xi, zeta)

    dec, xi, zeta, cd = _retention_tables(True)
    bwd_row = lambda b, c: (b * nblk + nblk - 1 - c, 0)
    return pl.pallas_call(
        functools.partial(_retention_kernel, chunk_decay=cd, final=True, chunks=chunks), grid=(B, nblk),
        in_specs=[pl.BlockSpec((R, RET_COLS), bwd_row), _full(dec.shape), _full(xi.shape), _full(zeta.shape),
                  pl.BlockSpec((R, RET_WIDTH), bwd_row), _full(ret_gn.shape), _full(ones256.shape)],
        out_specs=pl.BlockSpec((R, RET_WIDTH), bwd_row),
        out_shape=jax.ShapeDtypeStruct((N, RET_WIDTH), BF16),
        scratch_shapes=scratch, compiler_params=_params("parallel", "arbitrary"), name="retention_bwd",
    )(ret, dec, xi, zeta, yf, ret_gn, ones256)


def _flash_kernel(q_ref, k_ref, v_ref, o_ref, q_scr, m_scr, acc_scr, *, tq, tk):
    j = pl.program_id(3)

    @pl.when(j == 0)
    def _():
        for g in range(ATT_GROUP):
            q_scr[g * tq:(g + 1) * tq, :] = q_ref[:, g * HEAD_DIM:(g + 1) * HEAD_DIM]
        m_scr[...] = jnp.full_like(m_scr, -jnp.inf)
        acc_scr[...] = jnp.zeros_like(acc_scr)

    k = k_ref[...]
    v = v_ref[...]
    sub = min(FLASH_SUB_ROWS, ATT_GROUP * tq)
    for r0 in range(0, ATT_GROUP * tq, sub):
        rows = slice(r0, r0 + sub)
        s = _dot_nt(q_scr[rows, :], k)
        m_prev = m_scr[rows, :]
        m_next = jnp.maximum(m_prev, jnp.max(s, axis=1, keepdims=True))
        p = jnp.exp(s - jnp.concatenate([m_next] * (tk // V7X_LANES), axis=1))
        alpha = jnp.exp(m_prev - m_next)
        m_scr[rows, :] = m_next
        acc_scr[rows, :] = alpha * acc_scr[rows, :] + _dot(p.astype(BF16), v)

    @pl.when(j == pl.num_programs(3) - 1)
    def _():
        o = acc_scr[:, 0:HEAD_DIM] / acc_scr[:, HEAD_DIM:2 * HEAD_DIM]
        for g in range(ATT_GROUP):
            o_ref[:, g * HEAD_DIM:(g + 1) * HEAD_DIM] = o[g * tq:(g + 1) * tq, :].astype(BF16)


def _attention(q, k, v, B, T):
    N = q.shape[0]
    tq = min(FLASH_Q_ROWS, T)
    tk = min(FLASH_KV_ROWS, T)
    nq, nk = T // tq, T // tk
    GW = ATT_GROUP * HEAD_DIM
    rows = ATT_GROUP * tq
    return pl.pallas_call(
        functools.partial(_flash_kernel, tq=tq, tk=tk), grid=(B, ATT_KV_HEADS, nq, nk),
        in_specs=[pl.BlockSpec((tq, GW), lambda b, h, i, j: (b * nq + i, h)),
                  pl.BlockSpec((None, tk, HEAD_DIM), lambda b, h, i, j: (h, b * nk + j, 0)),
                  pl.BlockSpec((None, tk, 2 * HEAD_DIM), lambda b, h, i, j: (h, b * nk + j, 0))],
        out_specs=pl.BlockSpec((tq, GW), lambda b, h, i, j: (b * nq + i, h)),
        out_shape=jax.ShapeDtypeStruct((N, ATT_Q_WIDTH), BF16),
        scratch_shapes=[pltpu.VMEM((rows, HEAD_DIM), BF16), pltpu.VMEM((rows, V7X_LANES), F32),
                        pltpu.VMEM((rows, 2 * HEAD_DIM), F32)],
        compiler_params=_params("parallel", "parallel", "parallel", "arbitrary"), name="flash_attention",
    )(q, k, v)


def _rwkv_scan_kernel(*refs, rows_per_set):
    C = RW_CHUNK
    H = C // 2
    W = RWKV_WIDTH
    SUB = 8
    n_sets = len(rows_per_set)
    ones_ref = refs[12 * n_sets]
    outs = refs[12 * n_sets + 1:12 * n_sets + 1 + 2 * n_sets]
    state, vk_scr = refs[-2:]
    dirs = [(refs[12 * s:12 * s + 6], refs[12 * s + 6:12 * s + 12]) for s in range(n_sets)]
    chains = [(s, n, d) for s in range(n_sets) for n in range(rows_per_set[s]) for d in range(2)]

    first_chunk = pl.program_id(1) == 0

    @pl.when(first_chunk)
    def _():
        for c, (s, n, d) in enumerate(chains):
            if s == 0:
                state[c] = jnp.zeros((HEAD_DIM, W), F32)

    @pl.when(jnp.logical_and(first_chunk, pl.program_id(0) == 0))
    def _():
        for c, (s, n, d) in enumerate(chains):
            if s > 0:
                state[c] = jnp.zeros((HEAD_DIM, W), F32)

    ones = ones_ref[...]
    same_head = (lax.broadcasted_iota(jnp.int32, (W, W), 0) // HEAD_DIM
                 == lax.broadcasted_iota(jnp.int32, (W, W), 1) // HEAD_DIM)
    step_iota = lax.broadcasted_iota(jnp.int32, (H, HEAD_DIM, W), 0)
    lane_iota = lax.broadcasted_iota(jnp.int32, (H, HEAD_DIM, W), 2) & (HEAD_DIM - 1)

    def head_sums(vals):
        stacked = jnp.concatenate([v.astype(BF16) for v in vals], axis=0)
        out = _dot(stacked, ones)
        return [out[c * HEAD_DIM:(c + 1) * HEAD_DIM, :] for c in range(len(vals))]

    own_head = (lax.broadcasted_iota(jnp.int32, (SUB, W), 0)
                == lax.broadcasted_iota(jnp.int32, (SUB, W), 1) // HEAD_DIM)
    carry = tuple(state[c] for c in range(len(chains)))
    for half in range(2):
        lo = (half * H, (1 - half) * H)
        for c, (s, n, d) in enumerate(chains):
            kc = dirs[s][d][3][n]
            kbd = jnp.where(same_head, jnp.concatenate([kc] * (W // C), axis=0), 0.0).astype(BF16)
            sel = jnp.where(step_iota + lo[d] == lane_iota, dirs[s][d][5][n][None], 0.0).reshape(H * HEAD_DIM, W)
            vk_scr[c] = _dot(sel.astype(BF16), kbd).reshape(H, HEAD_DIM, W)

        def group(g, states, lo=lo):
            states = list(states)
            bases = (lo[0] + g * SUB, lo[1] + (H // SUB - 1 - g) * SUB)
            rows = [[ref[n, pl.ds(pl.multiple_of(bases[d], SUB), SUB), :] for ref in dirs[s][d][:5]]
                    for (s, n, d) in chains]
            ys = [[None] * SUB for _ in chains]
            for j in range(SUB):
                jj = (j, SUB - 1 - j)
                row = lambda c, f: rows[c][f][jj[chains[c][2]]:jj[chains[c][2]] + 1, :]
                sa = head_sums([states[c] * row(c, 0) for c in range(len(chains))])
                new = [states[c] * row(c, 2) - sa[c] * row(c, 4)
                       + vk_scr[c, bases[chains[c][2]] + jj[chains[c][2]] - lo[chains[c][2]]]
                       for c in range(len(chains))]
                for c, (s, n, d) in enumerate(chains):
                    read = new[c] if d == 0 else states[c]
                    r_heads = jnp.where(own_head, row(c, 1), 0.0).astype(BF16)
                    ys[c][jj[d]] = _dot_nt(r_heads, read.astype(BF16))
                states = new
            for c, (s, n, d) in enumerate(chains):
                span = pl.ds(pl.multiple_of(bases[d], SUB), SUB)
                for h in range(RWKV_HEADS):
                    col = jnp.concatenate([ys[c][r][h:h + 1, :] for r in range(SUB)], axis=0)
                    outs[2 * s + d][n, span, h * HEAD_DIM:(h + 1) * HEAD_DIM] = col
            return tuple(states)

        carry = lax.fori_loop(0, H // SUB, group, carry)

    for c in range(len(chains)):
        state[c] = carry[c]


def _rwkv_scan_call(sets, ones256):
    W, C = RWKV_WIDTH, RW_CHUNK
    _, B0, T0, nb0 = sets[0]
    nc0 = T0 // C
    operands, in_specs, out_specs, out_shapes = [], [], [], []
    for s, (rwp, B, T, nb) in enumerate(sets):
        nc = T // C
        fields = rwp.reshape(B, T, RW_FIELDS * W)
        v = rwp[:, 2 * W:3 * W].reshape(B, nc, C, RWKV_HEADS, HEAD_DIM)
        vt = v.transpose(0, 1, 4, 3, 2).reshape(B, nc, HEAD_DIM, W)

        def block(b, c, reverse, s=s, nc=nc):
            bi, ci = (b, c) if s == 0 else (0, b * nc0 + c)
            return bi, (nc - 1 - ci if reverse else ci)

        field = lambda f, rev, nb=nb, block=block: pl.BlockSpec(
            (nb, C, W), lambda b, c: block(b, c, rev) + (f,))
        vt_spec = lambda rev, nb=nb, block=block: pl.BlockSpec(
            (nb, None, HEAD_DIM, W), lambda b, c: block(b, c, rev) + (0, 0))
        in_specs += [field(1, False), field(0, False), field(3, False), field(5, False), field(7, False),
                     vt_spec(False),
                     field(1, True), field(0, True), field(4, True), field(6, True), field(8, True),
                     vt_spec(True)]
        operands += [fields] * 5 + [vt] + [fields] * 5 + [vt]
        y_spec = lambda rev, nb=nb, block=block: pl.BlockSpec(
            (nb, C, W), lambda b, c: block(b, c, rev) + (0,))
        out_specs += [y_spec(False), y_spec(True)]
        out_shapes += [jax.ShapeDtypeStruct((B, T, W), F32)] * 2
    n_chains = 2 * sum(nb for _, _, _, nb in sets)
    outs = pl.pallas_call(
        functools.partial(_rwkv_scan_kernel, rows_per_set=tuple(nb for _, _, _, nb in sets)),
        grid=(B0 // nb0, nc0), in_specs=in_specs + [_full(ones256.shape)],
        out_specs=out_specs, out_shape=out_shapes,
        scratch_shapes=[pltpu.VMEM((n_chains, HEAD_DIM, W), F32),
                        pltpu.VMEM((n_chains, C // 2, HEAD_DIM, W), F32)],
        compiler_params=_params("arbitrary", "arbitrary"), name="rwkv_scan",
    )(*operands, ones256)
    return [(outs[2 * s].reshape(B * T, W), outs[2 * s + 1].reshape(B * T, W))
            for s, (_, B, T, _) in enumerate(sets)]


def _rwkv_scan(groups, ones256):
    rows = lambda B: RW_BATCH_ROWS if B % RW_BATCH_ROWS == 0 else 1
    if len(groups) == 2:
        (_, B0, T0), (_, B1, T1) = groups
        if B1 == 1 and (B0 // rows(B0)) * (T0 // RW_CHUNK) == T1 // RW_CHUNK:
            return _rwkv_scan_call([groups[0] + (rows(B0),), groups[1] + (1,)], ones256)
    return [_rwkv_scan_call([g + (rows(g[1]),)], ones256)[0] for g in groups]


def _merge_kernel(x_ref, mod_ref, ya_ref, yb_ref, yf_ref, ybk_ref, rg_ref, bonus_ref, gn_ref, gates_ref,
                  ones_ref, wa_ref, wb_ref, wc_ref, wo_ref, o_ref):
    y = _head_layer_norm(yf_ref[...] + ybk_ref[...], ones_ref[...], RWKV_GN_EPS) * gn_ref[...]
    yc = ((y + bonus_ref[...]) * rg_ref[...]).astype(BF16)
    D = D_MODEL
    merged = (gates_ref[:, 0:D].astype(F32) * _dot(ya_ref[...], wa_ref[...])
              + gates_ref[:, D:2 * D].astype(F32) * _dot(yb_ref[...], wb_ref[...])
              + gates_ref[:, 2 * D:3 * D].astype(F32) * _dot(yc, wc_ref[...]))
    o_ref[...] = x_ref[...] + mod_ref[2:3, :] * _dot(merged.astype(BF16), wo_ref[...])


def _merge(x, mod, T, ya, yb, yf, ybk, rwp, gates, lw, ones256):
    N = x.shape[0]
    tm = min(512, T)
    tps = T // tm
    x_spec, mod_spec, _ = _token_specs(tm, tps)
    row = lambda w: pl.BlockSpec((tm, w), lambda i: (i, 0))
    field = lambda f: pl.BlockSpec((tm, RWKV_WIDTH), lambda i: (i, f))
    return pl.pallas_call(
        _merge_kernel, grid=(N // tm,),
        in_specs=[x_spec, mod_spec, row(RET_WIDTH), row(ATT_Q_WIDTH), row(RWKV_WIDTH), row(RWKV_WIDTH),
                  field(9), field(10), _full(lw["rw_gn"].shape), row(GATE_COLS), _full(ones256.shape),
                  _full(lw["w_branch_a"].shape), _full(lw["w_branch_b"].shape), _full(lw["w_branch_c"].shape),
                  _full(lw["w_out"].shape)],
        out_specs=x_spec, out_shape=jax.ShapeDtypeStruct((N, D_MODEL), F32),
        compiler_params=_params("parallel"), name="merge_out",
    )(x, mod, ya, yb, yf, ybk, rwp, rwp, lw["rw_gn"], gates, ones256,
      lw["w_branch_a"], lw["w_branch_b"], lw["w_branch_c"], lw["w_out"])


def _route(h, hb, rwt_ref, rbias_ref, tm):
    h_lo = (h - hb.astype(F32)).astype(BF16)
    rw_hi, rw_lo = _split(rwt_ref[...])
    logits = _dot_nt(rw_hi, hb) + _dot_nt(rw_hi, h_lo) + _dot_nt(rw_lo, hb)
    scores = _sigmoid(logits)
    choice = scores + jnp.concatenate([rbias_ref[...]] * (tm // V7X_LANES), axis=1)
    per_group = N_EXPERTS // N_GROUPS
    sub = lax.broadcasted_iota(jnp.int32, (per_group, tm), 0)
    groups, gscore = [], []
    for g in range(N_GROUPS):
        cg = choice[g * per_group:(g + 1) * per_group, :]
        m1 = jnp.max(cg, axis=0, keepdims=True)
        first = jnp.min(jnp.where(cg == m1, sub, per_group), axis=0, keepdims=True)
        m2 = jnp.max(jnp.where(sub == first, -jnp.inf, cg), axis=0, keepdims=True)
        groups.append(cg)
        gscore.append(m1 + m2)
    masked = []
    for g in range(N_GROUPS):
        beaten = jnp.zeros((1, tm), F32)
        for o in range(N_GROUPS):
            if o == g:
                continue
            wins = (gscore[o] >= gscore[g]) if o < g else (gscore[o] > gscore[g])
            beaten = beaten + jnp.where(wins, 1.0, 0.0)
        keep = jnp.where(beaten < TOPK_GROUPS, 1.0, 0.0)
        masked.append(jnp.where(jnp.broadcast_to(keep, (per_group, tm)) > 0.5, groups[g], -jnp.inf))
    mc = jnp.concatenate(masked, axis=0)
    eidx = lax.broadcasted_iota(jnp.int32, (N_EXPERTS, tm), 0).astype(F32)
    sel = jnp.zeros((N_EXPERTS, tm), F32)
    for _ in range(TOP_K):
        top = jnp.max(mc, axis=0, keepdims=True)
        first = jnp.min(jnp.where(mc == top, eidx, float(N_EXPERTS)), axis=0, keepdims=True)
        pick = eidx == first
        sel = jnp.where(pick, 1.0, sel)
        mc = jnp.where(pick, -jnp.inf, mc)
    wts = jnp.where(sel > 0.5, scores, 0.0)
    return wts / jnp.sum(wts, axis=0, keepdims=True) * ROUTED_SCALE, sel


def _selected_lists(comb, sel, rank):
    eidx = lax.broadcasted_iota(jnp.int32, comb.shape, 0).astype(F32)
    prev = jnp.full((1, comb.shape[1]), -1.0, F32)
    ids, wts, pos = [], [], []
    for _ in range(TOP_K):
        cand = jnp.where(sel > 0.5, jnp.where(eidx > prev, eidx, float(N_EXPERTS)), float(N_EXPERTS))
        prev = jnp.min(cand, axis=0, keepdims=True)
        wts.append(jnp.sum(jnp.where(eidx == prev, comb, 0.0), axis=0, keepdims=True))
        pos.append(jnp.sum(jnp.where(eidx == prev, rank, 0.0), axis=0, keepdims=True))
        ids.append(jnp.minimum(prev, N_EXPERTS - 1.0))
    as_int = lambda rows: jnp.concatenate(rows, axis=0).astype(jnp.int32)
    return as_int(ids), jnp.concatenate(wts, axis=0), as_int(pos)


def _pack_halves(y):
    n = y.shape[1] // 2
    hi = pltpu.bitcast(y[:, :n].astype(BF16).astype(F32), jnp.int32)
    lo = pltpu.bitcast(y[:, n:].astype(BF16).astype(F32), jnp.int32)
    return hi | lax.shift_right_logical(lo, 16)


def _unpack_halves(w):
    hi = pltpu.bitcast(w & jnp.int32(-65536), F32)
    lo = pltpu.bitcast(lax.shift_left(w, 16), F32)
    return hi, lo


def _moe_route_kernel(x_ref, mod_ref, g_ref, rwt_ref, rbias_ref, shgu_ref, shd_ref, tri_ref,
                      hp_ref, ids_ref, wts_ref, pos_ref, base_ref, count_ref, *, tm):
    F = EXPERT_FF

    @pl.when(pl.program_id(0) == 0)
    def _():
        count_ref[...] = jnp.zeros_like(count_ref)

    x = x_ref[...]
    h = _norm_mod(x, g_ref[...], mod_ref[3:4, :], mod_ref[4:5, :])
    hb = h.astype(BF16)
    hp_ref[...] = _pack_halves(hb.astype(F32))
    comb, sel = _route(h, hb, rwt_ref, rbias_ref, tm)
    before = count_ref[...]
    inclusive = _dot(sel.astype(BF16), tri_ref[...])
    rank = inclusive - sel + jnp.concatenate([before] * (tm // V7X_LANES), axis=1)
    count_ref[...] = before + jnp.sum(sel, axis=1, keepdims=True)
    ids, wts, pos = _selected_lists(comb, sel, rank)
    ids_ref[...] = ids
    wts_ref[...] = wts
    pos_ref[...] = pos
    gu = _dot(hb, shgu_ref[...])
    shared = _dot((_silu(gu[:, 0:F]) * gu[:, F:2 * F]).astype(BF16), shd_ref[...])
    base_ref[...] = x + mod_ref[5:6, :] * shared


def _moe_expert_kernel(tile_expert_ref, xs_ref, wgu_ref, wd_ref, o_ref):
    del tile_expert_ref
    F = EXPERT_FF
    half = D_MODEL // 2
    hi, lo = _unpack_halves(xs_ref[...])
    gu = _dot(hi.astype(BF16), wgu_ref[0:half, :]) + _dot(lo.astype(BF16), wgu_ref[half:, :])
    act = (_silu(gu[:, 0:F]) * gu[:, F:2 * F]).astype(BF16)
    o_ref[...] = _pack_halves(_dot(act, wd_ref[...]))


def _moe_combine_kernel(base_ref, mod_ref, y_ref, w_ref, fin_ref, o_ref, *, final):
    half = D_MODEL // 2
    acc_hi = jnp.zeros((base_ref.shape[0], half), F32)
    acc_lo = jnp.zeros((base_ref.shape[0], half), F32)
    for k in range(TOP_K):
        hi, lo = _unpack_halves(y_ref[k])
        wk = w_ref[:, k:k + 1]
        acc_hi = acc_hi + wk * hi
        acc_lo = acc_lo + wk * lo
    out_hi = base_ref[:, 0:half] + mod_ref[5:6, 0:half] * acc_hi
    out_lo = base_ref[:, half:] + mod_ref[5:6, half:] * acc_lo
    if final:
        ms = (jnp.sum(out_hi * out_hi, axis=-1, keepdims=True)
              + jnp.sum(out_lo * out_lo, axis=-1, keepdims=True)) * (1.0 / D_MODEL)
        scale = lax.rsqrt(ms + NORM_EPS)
        out_hi = out_hi * scale * fin_ref[:, 0:half]
        out_lo = out_lo * scale * fin_ref[:, half:]
    o_ref[:, 0:half] = out_hi
    o_ref[:, half:] = out_lo


def _gather_rows(table, idx):
    rows, width = idx.shape[0], table.shape[1]
    info = plsc.get_sparse_core_info()
    workers = info.num_cores * info.num_subcores
    windows = rows // (workers * SC_GATHER_ROWS)
    assert windows * workers * SC_GATHER_ROWS == rows
    mesh = plsc.VectorSubcoreMesh(core_axis_name="core", subcore_axis_name="subcore")

    @functools.partial(pl.kernel, out_type=jax.ShapeDtypeStruct((rows, width), table.dtype), mesh=mesh,
                       scratch_types=[pltpu.VMEM((SC_GATHER_ROWS,), jnp.int32),
                                      pltpu.VMEM((SC_GATHER_ROWS, width), table.dtype)],
                       name="moe_gather")
    def gather(table_hbm, idx_hbm, out_hbm, idx_vmem, rows_vmem):
        worker = lax.axis_index("subcore") * info.num_cores + lax.axis_index("core")
        base = worker * (windows * SC_GATHER_ROWS)

        @pl.loop(0, windows)
        def _(j):
            span = pl.ds(pl.multiple_of(base + j * SC_GATHER_ROWS, SC_GATHER_ROWS), SC_GATHER_ROWS)
            pltpu.sync_copy(idx_hbm.at[span], idx_vmem)
            pltpu.sync_copy(table_hbm.at[idx_vmem], rows_vmem)
            pltpu.sync_copy(rows_vmem, out_hbm.at[span])

    return gather(table, idx)


def _scatter_rows(table, dest, total_rows):
    n_lists, n_rows = dest.shape
    width = table.shape[1]
    info = plsc.get_sparse_core_info()
    workers = info.num_cores * info.num_subcores
    windows = n_rows // (workers * SC_GATHER_ROWS)
    assert windows * workers * SC_GATHER_ROWS == n_rows
    windows_per_list = n_rows // SC_GATHER_ROWS
    mesh = plsc.VectorSubcoreMesh(core_axis_name="core", subcore_axis_name="subcore")

    @functools.partial(pl.kernel, out_type=jax.ShapeDtypeStruct((total_rows, width), table.dtype), mesh=mesh,
                       scratch_types=[pltpu.VMEM((1, SC_GATHER_ROWS), jnp.int32),
                                      pltpu.VMEM((SC_GATHER_ROWS, width), table.dtype)],
                       name="moe_scatter")
    def scatter(table_hbm, dest_hbm, out_hbm, idx_vmem, rows_vmem):
        worker = lax.axis_index("subcore") * info.num_cores + lax.axis_index("core")

        @pl.loop(0, windows)
        def _(j):
            window = worker * windows + j
            span = pl.ds(pl.multiple_of(window * SC_GATHER_ROWS, SC_GATHER_ROWS), SC_GATHER_ROWS)
            pltpu.sync_copy(table_hbm.at[span], rows_vmem)
            for k in range(n_lists):
                pltpu.sync_copy(dest_hbm.at[pl.ds(k * windows_per_list + window, 1)], idx_vmem)
                pltpu.sync_copy(rows_vmem, out_hbm.at[idx_vmem.at[0]])

    return scatter(table, dest.reshape(n_lists * windows_per_list, SC_GATHER_ROWS))


def _dispatch_plan(ids, pos, counts, n_tokens, tg):
    E = N_EXPERTS
    padded = (counts + tg - 1) // tg * tg
    ends = jnp.cumsum(padded)
    off = ends - padded
    onehot = ids[:, :, None] == jnp.arange(E, dtype=jnp.int32)[None, None, :]
    dest = pos + jnp.sum(jnp.where(onehot, off[None, None, :], 0), axis=-1)
    total = n_tokens * TOP_K + E * tg
    tile_start = jnp.arange(total // tg, dtype=jnp.int32) * tg
    tile_expert = jnp.sum((ends[None, :] <= tile_start[:, None]).astype(jnp.int32), axis=1)
    return dest, total, jnp.minimum(tile_expert, E - 1)


def _moe(x, mod, T, lw, final_g, final):
    N = x.shape[0]
    tm = min(MOE_TOKENS, T)
    tps = T // tm
    D, F, E = D_MODEL, EXPERT_FF, N_EXPERTS
    tg = max(MOE_GROUP_ROWS_MIN, min(MOE_GROUP_ROWS_MAX, N * TOP_K // (E * 8)))
    half = D // 2
    x_spec, mod_spec, g_spec = _token_specs(tm, tps)
    list_spec = pl.BlockSpec((TOP_K, tm), lambda i: (0, i))

    tri = jnp.asarray(np.triu(np.ones((tm, tm), np.float32)), BF16)
    hp, ids, wts, pos, base, counts = pl.pallas_call(
        functools.partial(_moe_route_kernel, tm=tm), grid=(N // tm,),
        in_specs=[x_spec, mod_spec, g_spec, _full((E, D)), _full((E, V7X_LANES)), _full((D, 2 * F)), _full((F, D)),
                  _full((tm, tm))],
        out_specs=[pl.BlockSpec((tm, half), lambda i: (i, 0)), list_spec, list_spec, list_spec, x_spec,
                   _full((E, V7X_LANES))],
        out_shape=[jax.ShapeDtypeStruct((N, half), jnp.int32), jax.ShapeDtypeStruct((TOP_K, N), jnp.int32),
                   jax.ShapeDtypeStruct((TOP_K, N), F32), jax.ShapeDtypeStruct((TOP_K, N), jnp.int32),
                   jax.ShapeDtypeStruct((N, D), F32), jax.ShapeDtypeStruct((E, V7X_LANES), F32)],
        compiler_params=_params("arbitrary"), name="moe_route",
    )(x, mod, lw["norm2_g"], lw["router_wt"], lw["router_bias"], lw["sh_gu"], lw["sh_d"], tri)

    dest, total, tile_expert = _dispatch_plan(ids, pos, counts[:, 0].astype(jnp.int32), N, tg)
    xs = _scatter_rows(hp, dest, total)
    n_tiles = total // tg
    ys = pl.pallas_call(
        _moe_expert_kernel,
        grid_spec=pltpu.PrefetchScalarGridSpec(
            num_scalar_prefetch=1, grid=(n_tiles,),
            in_specs=[pl.BlockSpec((tg, half), lambda i, te: (i, 0)),
                      pl.BlockSpec((None, D, 2 * F), lambda i, te: (te[i], 0, 0)),
                      pl.BlockSpec((None, F, D), lambda i, te: (te[i], 0, 0))],
            out_specs=pl.BlockSpec((tg, half), lambda i, te: (i, 0))),
        out_shape=jax.ShapeDtypeStruct(xs.shape, jnp.int32),
        compiler_params=_params("parallel"), name="moe_experts",
    )(tile_expert, xs, lw["exp_gu"], lw["exp_d"])

    yk = _gather_rows(ys, dest.reshape(-1)).reshape(TOP_K, N, half)
    return pl.pallas_call(
        functools.partial(_moe_combine_kernel, final=final), grid=(N // tm,),
        in_specs=[x_spec, mod_spec, pl.BlockSpec((TOP_K, tm, half), lambda i: (0, i, 0)),
                  pl.BlockSpec((tm, TOP_K), lambda i: (i, 0)), g_spec],
        out_specs=x_spec, out_shape=jax.ShapeDtypeStruct((N, D), F32),
        compiler_params=_params("parallel"), name="moe_combine",
    )(base, mod, yk, wts.T, final_g)


def _block_diag2(a, b):
    za = jnp.zeros((a.shape[0], b.shape[1]), a.dtype)
    zb = jnp.zeros((b.shape[0], a.shape[1]), a.dtype)
    return jnp.concatenate([jnp.concatenate([a, za], 1), jnp.concatenate([zb, b], 1)], 0)


def _layer_weights(P, l):
    w_in = P["w_in"][l]
    o1, o2, o3 = RET_COLS, RET_COLS + ATT_COLS, RET_COLS + ATT_COLS + RW_COLS
    w_ret, w_att, w_rw, w_gate = w_in[:, :o1], w_in[:, o1:o2], w_in[:, o2:o3], w_in[:, o3:]
    qk_gain = jnp.zeros((8, ATT_Q_WIDTH), F32)
    qk_gain = qk_gain.at[0].set(jnp.tile(P["q_norm_g"][l], ATT_Q_HEADS))
    qk_gain = qk_gain.at[1, :ATT_KV_WIDTH].set(jnp.tile(P["k_norm_g"][l], ATT_KV_HEADS))
    rw_vec = jnp.zeros((8, RWKV_WIDTH), F32)
    for j, name in enumerate(("rw_w0_f", "rw_w0_b", "rw_a0_f", "rw_a0_b", "rw_k_k", "rw_k_a")):
        rw_vec = rw_vec.at[j].set(P[name][l])
    rw_vec = rw_vec.at[6].set(P["rw_r_k"][l].reshape(RWKV_WIDTH))
    row = lambda v: v.reshape(1, -1)
    return {
        "norm1_g": row(P["norm1_g"][l]), "norm2_g": row(P["norm2_g"][l]),
        "w_ret": w_ret.astype(BF16), "w_att": w_att.astype(BF16), "w_rw": w_rw.astype(BF16),
        "w_gate": w_gate.astype(BF16), "qk_gain": qk_gain, "ret_gn": row(P["ret_gn"][l]),
        "shift_mu": P["shift_mu"][l], "rw_vec": rw_vec,
        "rw_w2": _block_diag2(P["rw_w2_f"][l], P["rw_w2_b"][l]).astype(BF16),
        "rw_a2": _block_diag2(P["rw_a2_f"][l], P["rw_a2_b"][l]).astype(BF16),
        "rw_g2": P["rw_g2"][l].astype(BF16), "rw_gn": row(P["rw_gn"][l]),
        "w_branch_a": P["w_branch_a"][l].astype(BF16), "w_branch_b": P["w_branch_b"][l].astype(BF16),
        "w_branch_c": P["w_branch_c"][l].astype(BF16), "w_out": P["w_out"][l].astype(BF16),
        "router_wt": P["router_w"][l].T,
        "router_bias": jnp.broadcast_to(P["router_bias"][l][:, None], (N_EXPERTS, V7X_LANES)),
        "sh_gu": jnp.concatenate([P["sh_w_gate"][l], P["sh_w_up"][l]], axis=1).astype(BF16),
        "sh_d": P["sh_w_down"][l].astype(BF16),
        "exp_gu": jnp.concatenate([P["exp_w_gate"][l], P["exp_w_up"][l]], axis=2).astype(BF16),
        "exp_d": P["exp_w_down"][l].astype(BF16),
    }


def _ones_block_diag(n):
    idx = np.arange(n) // HEAD_DIM
    return jnp.asarray(idx[:, None] == idx[None, :], BF16)


def _tables(T):
    rows = T // GRID_W
    row = jnp.repeat(jnp.arange(rows, dtype=F32), GRID_W)
    col = jnp.tile(jnp.arange(GRID_W, dtype=F32), rows)
    freqs = ROPE_THETA ** (-jnp.arange(ROPE_PAIRS_PER_AXIS, dtype=F32) / ROPE_PAIRS_PER_AXIS)
    ang = jnp.concatenate([row[:, None] * freqs, col[:, None] * freqs], axis=-1)
    cos, sin = jnp.cos(ang), jnp.sin(ang)
    return {
        "cos": jnp.tile(jnp.concatenate([cos, cos], axis=-1), (1, ATT_Q_HEADS)),
        "sin": jnp.tile(jnp.concatenate([-sin, sin], axis=-1), (1, ATT_Q_HEADS)),
        "ones128": _ones_block_diag(128), "ones256": _ones_block_diag(256), "ones512": _ones_block_diag(512),
    }


def _trunks(requests, P, layer_weights):
    groups = []
    for x, c in requests:
        B, T, D = x.shape
        groups.append({"B": B, "T": T, "x": x.reshape(B * T, D), "tabs": _tables(T),
                       "mods": _ada_mod(c, P["ada_w"], P["ada_b"])})
    final_g = P["final_g"].reshape(1, D_MODEL)
    depth = len(layer_weights)
    for l, lw in enumerate(layer_weights):
        mixed = []
        for g in groups:
            B, T, tabs, mod = g["B"], g["T"], g["tabs"], g["mods"][l]
            ret, q, k, v, gates, rwp = _input_projections(g["x"], mod, T, lw, tabs)
            ya = _retention(ret, B, T, lw["ret_gn"], tabs["ones256"])
            yb = _attention(q, k, v, B, T)
            mixed.append((mod, ya, yb, gates, rwp))
        scans = _rwkv_scan([(m[4], g["B"], g["T"]) for m, g in zip(mixed, groups)], groups[0]["tabs"]["ones256"])
        for g, (mod, ya, yb, gates, rwp), (yf, ybk) in zip(groups, mixed, scans):
            x = _merge(g["x"], mod, g["T"], ya, yb, yf, ybk, rwp, gates, lw, g["tabs"]["ones256"])
            g["x"] = _moe(x, mod, g["T"], lw, final_g, final=(l == depth - 1))
    return [g["x"].reshape(g["B"], g["T"], D_MODEL) for g in groups]


def kernel(x_prompt, x_sample, c_prompt, c_sample, norm1_g, norm2_g, final_g, ada_w, ada_b, w_in, ret_gn, q_norm_g, k_norm_g, shift_mu, rw_w0_f, rw_w2_f, rw_w0_b, rw_w2_b, rw_a0_f, rw_a2_f, rw_a0_b, rw_a2_b, rw_g2, rw_k_k, rw_k_a, rw_r_k, rw_gn, w_branch_a, w_branch_b, w_branch_c, w_out, router_w, router_bias, exp_w_gate, exp_w_up, exp_w_down, sh_w_gate, sh_w_up, sh_w_down):
    P = {
        "norm1_g": norm1_g, "norm2_g": norm2_g, "final_g": final_g, "ada_w": ada_w, "ada_b": ada_b,
        "w_in": w_in, "ret_gn": ret_gn, "q_norm_g": q_norm_g, "k_norm_g": k_norm_g, "shift_mu": shift_mu,
        "rw_w0_f": rw_w0_f, "rw_w2_f": rw_w2_f, "rw_w0_b": rw_w0_b, "rw_w2_b": rw_w2_b,
        "rw_a0_f": rw_a0_f, "rw_a2_f": rw_a2_f, "rw_a0_b": rw_a0_b, "rw_a2_b": rw_a2_b,
        "rw_g2": rw_g2, "rw_k_k": rw_k_k, "rw_k_a": rw_k_a, "rw_r_k": rw_r_k, "rw_gn": rw_gn,
        "w_branch_a": w_branch_a, "w_branch_b": w_branch_b, "w_branch_c": w_branch_c, "w_out": w_out,
        "router_w": router_w, "router_bias": router_bias,
        "exp_w_gate": exp_w_gate, "exp_w_up": exp_w_up, "exp_w_down": exp_w_down,
        "sh_w_gate": sh_w_gate, "sh_w_up": sh_w_up, "sh_w_down": sh_w_down,
    }
    layer_weights = [_layer_weights(P, l) for l in range(w_in.shape[0])]
    y_prompt, y_sample = _trunks([(x_prompt, c_prompt), (x_sample, c_sample)], P, layer_weights)
    return (y_prompt, y_sample)
```

```python
import functools

import numpy as np
import jax
import jax.numpy as jnp
from jax import lax
from jax.experimental import pallas as pl
from jax.experimental.pallas import tpu as pltpu
from jax.experimental.pallas import tpu_sc as plsc

F32 = jnp.float32
BF16 = jnp.bfloat16

D_MODEL = 1024
DEPTH = 2
GRID_W = 64
HEAD_DIM = 64
ROPE_THETA = 10000.0
ROPE_PAIRS_PER_AXIS = HEAD_DIM // 4

RET_HEADS = 4
RET_WIDTH = RET_HEADS * HEAD_DIM
RET_CHUNK = 128
RET_CHUNKS_PER_STEP = 8
RET_EPS = 1e-5

ATT_Q_HEADS = 8
ATT_KV_HEADS = 2
ATT_GROUP = ATT_Q_HEADS // ATT_KV_HEADS
ATT_Q_WIDTH = ATT_Q_HEADS * HEAD_DIM
ATT_KV_WIDTH = ATT_KV_HEADS * HEAD_DIM

RWKV_HEADS = 4
RWKV_WIDTH = RWKV_HEADS * HEAD_DIM
DECAY_LORA = 64
AAA_LORA = 64
GATE_LORA = 128
RWKV_GN_EPS = 64e-5
RW_CHUNK = 64
RW_FIELDS = 11
RW_BATCH_ROWS = 4

FLASH_Q_ROWS = 1024
FLASH_KV_ROWS = 2048
FLASH_SUB_ROWS = 512

MOE_TOKENS = 512
MOE_GROUP_ROWS_MIN = 256
MOE_GROUP_ROWS_MAX = 1024
SC_GATHER_ROWS = 128

N_EXPERTS = 64
TOP_K = 8
N_GROUPS = 8
TOPK_GROUPS = 4
EXPERT_FF = 256
ROUTED_SCALE = 2.5
NORM_EPS = 1e-6

RET_COLS = 4 * RET_WIDTH
ATT_COLS = ATT_Q_WIDTH + 2 * ATT_KV_WIDTH
RW_COLS = 3 * RWKV_WIDTH + 2 * DECAY_LORA + 2 * AAA_LORA + GATE_LORA
GATE_COLS = 3 * D_MODEL

V7X_LANES = 128
VMEM_LIMIT_BYTES = 48 * 1024 * 1024


def _params(*dims):
    return pltpu.CompilerParams(dimension_semantics=dims, vmem_limit_bytes=VMEM_LIMIT_BYTES)


def _sigmoid(x):
    return 1.0 / (1.0 + jnp.exp(-x))


def _silu(x):
    return x * _sigmoid(x)


def _dot(a, b):
    return jnp.dot(a, b, preferred_element_type=F32)


def _dot_nt(a, b):
    return lax.dot_general(a, b, (((1,), (1,)), ((), ())), preferred_element_type=F32)


def _split(a):
    hi = a.astype(BF16)
    lo = (a - hi.astype(F32)).astype(BF16)
    return hi, lo


def _dot_split_lhs(a, b_bf16):
    hi, lo = _split(a)
    return _dot(hi, b_bf16) + _dot(lo, b_bf16)


def _dot3(a, b):
    ah, al = _split(a)
    bh, bl = _split(b)
    return _dot(ah, bh) + _dot(ah, bl) + _dot(al, bh)


def _swap_rotary_halves(x):
    half = HEAD_DIM // 2
    lane = lax.broadcasted_iota(jnp.int32, x.shape, 1) & (HEAD_DIM - 1)
    return jnp.where(lane < half, pltpu.roll(x, x.shape[1] - half, 1), pltpu.roll(x, half, 1))


def _norm_mod(x, gain, shift, scale):
    ms = jnp.mean(x * x, axis=-1, keepdims=True)
    return x * lax.rsqrt(ms + NORM_EPS) * gain * (1.0 + scale) + shift


def _head_layer_norm(y, ones_bd, eps):
    mean = _dot_split_lhs(y, ones_bd) * (1.0 / HEAD_DIM)
    yc = y - mean
    var = _dot_split_lhs(yc * yc, ones_bd) * (1.0 / HEAD_DIM)
    return yc * lax.rsqrt(var + eps)


def _ada_kernel(c_ref, w_ref, b_ref, o_ref):
    c = c_ref[...]
    o_ref[...] = _dot3(_silu(c), w_ref[...]) + b_ref[...]


def _ada_mod(c, ada_w, ada_b):
    B = c.shape[0]
    Bp = max(8, B)
    cp = jnp.zeros((Bp, D_MODEL), F32).at[:B].set(c)
    L = ada_w.shape[0]
    tn = 1536
    out = pl.pallas_call(
        _ada_kernel,
        grid=(L, 6 * D_MODEL // tn),
        in_specs=[pl.BlockSpec((Bp, D_MODEL), lambda l, j: (0, 0)),
                  pl.BlockSpec((None, D_MODEL, tn), lambda l, j: (l, 0, j)),
                  pl.BlockSpec((None, 1, tn), lambda l, j: (l, 0, j))],
        out_specs=pl.BlockSpec((None, Bp, tn), lambda l, j: (l, 0, j)),
        out_shape=jax.ShapeDtypeStruct((L, Bp, 6 * D_MODEL), F32),
        compiler_params=_params("parallel", "parallel"),
        name="ada_mod",
    )(cp, ada_w, ada_b.reshape(L, 1, 6 * D_MODEL))
    return out[:, :B].reshape(L, B, 6, D_MODEL)


def _proj_ret_kernel(x_ref, mod_ref, g_ref, w_ref, cos_ref, sin_ref, o_ref):
    h = _norm_mod(x_ref[...], g_ref[...], mod_ref[0:1, :], mod_ref[1:2, :]).astype(BF16)
    p = _dot(h, w_ref[...])
    W = RET_WIDTH
    cos = cos_ref[:, 0:W]
    sin = sin_ref[:, 0:W]
    q, k = p[:, 0:W], p[:, W:2 * W]
    o_ref[:, 0:W] = (q * cos + _swap_rotary_halves(q) * sin).astype(BF16)
    o_ref[:, W:2 * W] = ((k * cos + _swap_rotary_halves(k) * sin) * (HEAD_DIM ** -0.5)).astype(BF16)
    o_ref[:, 2 * W:3 * W] = p[:, 2 * W:3 * W].astype(BF16)
    o_ref[:, 3 * W:4 * W] = _silu(p[:, 3 * W:4 * W]).astype(BF16)


def _proj_att_kernel(x_ref, mod_ref, g_ref, w_ref, cos_ref, sin_ref, qkg_ref, ones_ref, q_ref, k_ref, v_ref):
    h = _norm_mod(x_ref[...], g_ref[...], mod_ref[0:1, :], mod_ref[1:2, :]).astype(BF16)
    p = _dot(h, w_ref[...])
    QW, KW = ATT_Q_WIDTH, ATT_KV_WIDTH
    q = p[:, 0:QW]
    k = p[:, QW:QW + KW]
    v = p[:, QW + KW:QW + 2 * KW]
    ones = ones_ref[...]
    rq = lax.rsqrt(_dot_split_lhs(q * q, ones) * (1.0 / HEAD_DIM) + NORM_EPS)
    rk = lax.rsqrt(_dot_split_lhs(k * k, ones[0:KW, 0:KW]) * (1.0 / HEAD_DIM) + NORM_EPS)
    cos = cos_ref[...]
    sin = sin_ref[...]
    qg = q * qkg_ref[0:1, :]
    kg = k * qkg_ref[1:2, 0:KW]
    qo = (qg * cos + _swap_rotary_halves(qg) * sin) * (rq * (HEAD_DIM ** -0.5))
    ko = (kg * cos[:, 0:KW] + _swap_rotary_halves(kg) * sin[:, 0:KW]) * rk
    q_ref[...] = qo.astype(BF16)
    kb = ko.astype(BF16)
    vb = v.astype(BF16)
    for hk in range(ATT_KV_HEADS):
        k_ref[hk] = kb[:, hk * HEAD_DIM:(hk + 1) * HEAD_DIM]
        v_ref[hk, :, 0:HEAD_DIM] = vb[:, hk * HEAD_DIM:(hk + 1) * HEAD_DIM]
        v_ref[hk, :, HEAD_DIM:2 * HEAD_DIM] = jnp.ones((vb.shape[0], HEAD_DIM), BF16)


def _proj_gate_kernel(x_ref, mod_ref, g_ref, w_ref, o_ref):
    h = _norm_mod(x_ref[...], g_ref[...], mod_ref[0:1, :], mod_ref[1:2, :]).astype(BF16)
    o_ref[...] = _sigmoid(_dot(h, w_ref[...])).astype(BF16)


def _proj_rw_kernel(x_ref, xp_ref, xn_ref, mod_ref, g_ref, w_ref, mu_ref, vec_ref, w2_ref, a2_ref, g2_ref,
                    ones_ref, o_ref, p_scr, *, tm, tiles_per_seq):
    i = pl.program_id(0)
    gain, shift, scale = g_ref[...], mod_ref[0:1, :], mod_ref[1:2, :]
    w = w_ref[...]
    h = _norm_mod(x_ref[...], gain, shift, scale).astype(BF16)
    p = _dot(h, w)
    hp = _norm_mod(xp_ref[...], gain, shift, scale).astype(BF16)
    hn = _norm_mod(xn_ref[...], gain, shift, scale).astype(BF16)
    first = (i % tiles_per_seq) == 0
    last = (i % tiles_per_seq) == tiles_per_seq - 1
    prev_row = jnp.where(first, 0.0, _dot(hp, w)[7:8, :])
    next_row = jnp.where(last, 0.0, _dot(hn, w)[0:1, :])
    p_scr[8:8 + tm, :] = p
    p_scr[7:8, :] = prev_row
    p_scr[8 + tm:9 + tm, :] = next_row
    prev = p_scr[7:7 + tm, :]
    nxt = p_scr[9:9 + tm, :]
    x = p + mu_ref[0:1, :] * (prev - p) + mu_ref[1:2, :] * (nxt - p)

    W = RWKV_WIDTH
    r, k, v = x[:, 0:W], x[:, W:2 * W], x[:, 2 * W:3 * W]
    xw = x[:, 3 * W:3 * W + 2 * DECAY_LORA]
    xa = x[:, 3 * W + 2 * DECAY_LORA:3 * W + 2 * DECAY_LORA + 2 * AAA_LORA]
    xg = x[:, 3 * W + 2 * DECAY_LORA + 2 * AAA_LORA:]
    w0_f, w0_b = vec_ref[0:1, :], vec_ref[1:2, :]
    a0_f, a0_b = vec_ref[2:3, :], vec_ref[3:4, :]
    k_k, k_a, r_k = vec_ref[4:5, :], vec_ref[5:6, :], vec_ref[6:7, :]
    wl = _dot(jnp.tanh(xw).astype(BF16), w2_ref[...])
    al = _dot(xa.astype(BF16), a2_ref[...])
    dec_c = float(np.exp(-0.5))
    w_f = jnp.exp(-dec_c * _sigmoid(w0_f + wl[:, 0:W]))
    w_b = jnp.exp(-dec_c * _sigmoid(w0_b + wl[:, W:2 * W]))
    a_f = _sigmoid(a0_f + al[:, 0:W])
    a_b = _sigmoid(a0_b + al[:, W:2 * W])
    gate = _dot(_sigmoid(xg).astype(BF16), g2_ref[...])
    ones = ones_ref[...]
    kk = k * k_k
    kk = kk * lax.rsqrt(_dot_split_lhs(kk * kk, ones) + 1e-12)
    k_f = k * (1.0 + (a_f - 1.0) * k_a)
    k_b = k * (1.0 + (a_b - 1.0) * k_a)
    bonus = _dot_split_lhs(r * k_f * r_k, ones) * v
    for j, val in enumerate((r, kk, v, w_f, w_b, k_f, k_b, kk * a_f, kk * a_b, gate, bonus)):
        o_ref[:, j * W:(j + 1) * W] = val


def _token_specs(tm, tiles_per_seq):
    x_spec = pl.BlockSpec((tm, D_MODEL), lambda i: (i, 0))
    mod_spec = pl.BlockSpec((None, 6, D_MODEL), lambda i: (i // tiles_per_seq, 0, 0))
    g_spec = pl.BlockSpec((1, D_MODEL), lambda i: (0, 0))
    return x_spec, mod_spec, g_spec


def _full(shape):
    nd = len(shape)
    return pl.BlockSpec(shape, lambda *_: (0,) * nd)


def _input_projections(x, mod, T, lw, tabs):
    N = x.shape[0]
    tm = min(512, T)
    tps = T // tm
    grid = (N // tm,)
    x_spec, mod_spec, g_spec = _token_specs(tm, tps)
    tab_spec = pl.BlockSpec((tm, ATT_Q_WIDTH), lambda i: (i % tps, 0))
    cos, sin = tabs["cos"], tabs["sin"]

    ret = pl.pallas_call(
        _proj_ret_kernel, grid=grid,
        in_specs=[x_spec, mod_spec, g_spec, _full(lw["w_ret"].shape), tab_spec, tab_spec],
        out_specs=pl.BlockSpec((tm, RET_COLS), lambda i: (i, 0)),
        out_shape=jax.ShapeDtypeStruct((N, RET_COLS), BF16),
        compiler_params=_params("parallel"), name="proj_ret",
    )(x, mod, lw["norm1_g"], lw["w_ret"], cos, sin)

    q, k, v = pl.pallas_call(
        _proj_att_kernel, grid=grid,
        in_specs=[x_spec, mod_spec, g_spec, _full(lw["w_att"].shape), tab_spec, tab_spec,
                  _full(lw["qk_gain"].shape), _full(tabs["ones512"].shape)],
        out_specs=[pl.BlockSpec((tm, ATT_Q_WIDTH), lambda i: (i, 0)),
                   pl.BlockSpec((ATT_KV_HEADS, tm, HEAD_DIM), lambda i: (0, i, 0)),
                   pl.BlockSpec((ATT_KV_HEADS, tm, 2 * HEAD_DIM), lambda i: (0, i, 0))],
        out_shape=[jax.ShapeDtypeStruct((N, ATT_Q_WIDTH), BF16),
                   jax.ShapeDtypeStruct((ATT_KV_HEADS, N, HEAD_DIM), BF16),
                   jax.ShapeDtypeStruct((ATT_KV_HEADS, N, 2 * HEAD_DIM), BF16)],
        compiler_params=_params("parallel"), name="proj_att",
    )(x, mod, lw["norm1_g"], lw["w_att"], cos, sin, lw["qk_gain"], tabs["ones512"])

    gates = pl.pallas_call(
        _proj_gate_kernel, grid=grid,
        in_specs=[x_spec, mod_spec, g_spec, _full(lw["w_gate"].shape)],
        out_specs=pl.BlockSpec((tm, GATE_COLS), lambda i: (i, 0)),
        out_shape=jax.ShapeDtypeStruct((N, GATE_COLS), BF16),
        compiler_params=_params("parallel"), name="proj_gate",
    )(x, mod, lw["norm1_g"], lw["w_gate"])

    tm_rw = min(256, T)
    tps_rw = T // tm_rw
    x_spec, mod_spec, g_spec = _token_specs(tm_rw, tps_rw)
    rows8 = tm_rw // 8
    last8 = N // 8 - 1
    rwp = pl.pallas_call(
        functools.partial(_proj_rw_kernel, tm=tm_rw, tiles_per_seq=tps_rw), grid=(N // tm_rw,),
        in_specs=[x_spec,
                  pl.BlockSpec((8, D_MODEL), lambda i: (jnp.maximum(i * rows8 - 1, 0), 0)),
                  pl.BlockSpec((8, D_MODEL), lambda i: (jnp.minimum((i + 1) * rows8, last8), 0)),
                  mod_spec, g_spec, _full(lw["w_rw"].shape), _full(lw["shift_mu"].shape),
                  _full(lw["rw_vec"].shape), _full(lw["rw_w2"].shape), _full(lw["rw_a2"].shape),
                  _full(lw["rw_g2"].shape), _full(tabs["ones256"].shape)],
        out_specs=pl.BlockSpec((tm_rw, RW_FIELDS * RWKV_WIDTH), lambda i: (i, 0)),
        out_shape=jax.ShapeDtypeStruct((N, RW_FIELDS * RWKV_WIDTH), F32),
        scratch_shapes=[pltpu.VMEM((tm_rw + 16, RW_COLS), F32)],
        compiler_params=_params("parallel"), name="proj_rw",
    )(x, x, x, mod, lw["norm1_g"], lw["w_rw"], lw["shift_mu"], lw["rw_vec"], lw["rw_w2"], lw["rw_a2"],
      lw["rw_g2"], tabs["ones256"])
    return ret, q, k, v, gates, rwp


def _retention_tables(reverse):
    lg = np.log1p(-np.exp2(-5.0 - np.arange(RET_HEADS, dtype=np.float64)))
    if reverse:
        lg = lg[::-1]
    C = RET_CHUNK
    pos = np.arange(C, dtype=np.float64)
    diff = pos[:, None] - pos[None, :]
    if reverse:
        dec = np.where(diff < 0, np.exp(lg[:, None, None] * np.maximum(-diff, 0.0)[None]), 0.0)
        xi = np.exp(lg[:, None] * (C - pos)[None, :])
        zeta = np.exp(lg[:, None] * pos[None, :])
    else:
        dec = np.where(diff >= 0, np.exp(lg[:, None, None] * np.maximum(diff, 0.0)[None]), 0.0)
        xi = np.exp(lg[:, None] * (pos + 1.0)[None, :])
        zeta = np.exp(lg[:, None] * (C - 1.0 - pos)[None, :])
    chunk_decay = tuple(float(v) for v in np.exp(lg * C))
    widen = lambda a: jnp.asarray(np.repeat(a.T, HEAD_DIM, axis=1), F32)
    return jnp.asarray(dec, F32), widen(xi), widen(zeta), chunk_decay


def _retention_kernel(ret_ref, dec_ref, xi_ref, zeta_ref, *rest, chunk_decay, final, chunks):
    if final:
        yf_ref, gn_ref, ones_ref, o_ref, state, y_scr = rest
    else:
        o_ref, state, y_scr = rest

    @pl.when(pl.program_id(1) == 0)
    def _():
        state[...] = jnp.zeros_like(state)

    W, C = RET_WIDTH, RET_CHUNK
    for step in range(chunks):
        cb = chunks - 1 - step if final else step
        rows = slice(cb * C, (cb + 1) * C)
        q = ret_ref[rows, 0:W]
        k = ret_ref[rows, W:2 * W]
        v = ret_ref[rows, 2 * W:3 * W]
        qx = (q.astype(F32) * xi_ref[...]).astype(BF16)
        kzt = (k.astype(F32) * zeta_ref[...]).T
        for h in range(RET_HEADS):
            sl = slice(h * HEAD_DIM, (h + 1) * HEAD_DIM)
            s = _dot_nt(q[:, sl], k[:, sl]) * dec_ref[h]
            inner = _dot(s.astype(BF16), v[:, sl])
            r_prev = state[h]
            cross = _dot(qx[:, sl], r_prev.astype(BF16))
            kv = _dot(kzt[sl, :].astype(BF16), v[:, sl])
            state[h] = r_prev * chunk_decay[h] + kv
            y_scr[rows, sl] = inner + cross
    y = y_scr[...]
    if final:
        y = _head_layer_norm(y + yf_ref[...], ones_ref[...], RET_EPS) * gn_ref[...]
        o_ref[...] = (y * ret_ref[:, 3 * W:4 * W].astype(F32)).astype(BF16)
    else:
        o_ref[...] = y


def _retention(ret, B, T, ret_gn, ones256):
    N = ret.shape[0]
    chunks = min(RET_CHUNKS_PER_STEP, T // RET_CHUNK)
    R = chunks * RET_CHUNK
    nblk = T // R
    scratch = [pltpu.VMEM((RET_HEADS, HEAD_DIM, HEAD_DIM), F32), pltpu.VMEM((R, RET_WIDTH), F32)]

    dec, xi, zeta, cd = _retention_tables(False)
    fwd_row = lambda b, c: (b * nblk + c, 0)
    yf = pl.pallas_call(
        functools.partial(_retention_kernel, chunk_decay=cd, final=False, chunks=chunks), grid=(B, nblk),
        in_specs=[pl.BlockSpec((R, RET_COLS), fwd_row), _full(dec.shape), _full(xi.shape), _full(zeta.shape)],
        out_specs=pl.BlockSpec((R, RET_WIDTH), fwd_row),
        out_shape=jax.ShapeDtypeStruct((N, RET_WIDTH), F32),
        scratch_shapes=scratch, compiler_params=_params("parallel", "arbitrary"), name="retention_fwd",
    )(ret, dec, xi, zeta)

    dec, xi, zeta, cd = _retention_tables(True)
    bwd_row = lambda b, c: (b * nblk + nblk - 1 - c, 0)
    return pl.pallas_call(
        functools.partial(_retention_kernel, chunk_decay=cd, final=True, chunks=chunks), grid=(B, nblk),
        in_specs=[pl.BlockSpec((R, RET_COLS), bwd_row), _full(dec.shape), _full(xi.shape), _full(zeta.shape),
                  pl.BlockSpec((R, RET_WIDTH), bwd_row), _full(ret_gn.shape), _full(ones256.shape)],
        out_specs=pl.BlockSpec((R, RET_WIDTH), bwd_row),
        out_shape=jax.ShapeDtypeStruct((N, RET_WIDTH), BF16),
        scratch_shapes=scratch, compiler_params=_params("parallel", "arbitrary"), name="retention_bwd",
    )(ret, dec, xi, zeta, yf, ret_gn, ones256)


def _flash_kernel(q_ref, k_ref, v_ref, o_ref, q_scr, m_scr, acc_scr, *, tq, tk):
    j = pl.program_id(3)

    @pl.when(j == 0)
    def _():
        for g in range(ATT_GROUP):
            q_scr[g * tq:(g + 1) * tq, :] = q_ref[:, g * HEAD_DIM:(g + 1) * HEAD_DIM]
        m_scr[...] = jnp.full_like(m_scr, -jnp.inf)
        acc_scr[...] = jnp.zeros_like(acc_scr)

    k = k_ref[...]
    v = v_ref[...]
    sub = min(FLASH_SUB_ROWS, ATT_GROUP * tq)
    for r0 in range(0, ATT_GROUP * tq, sub):
        rows = slice(r0, r0 + sub)
        s = _dot_nt(q_scr[rows, :], k)
        m_prev = m_scr[rows, :]
        m_next = jnp.maximum(m_prev, jnp.max(s, axis=1, keepdims=True))
        p = jnp.exp(s - jnp.concatenate([m_next] * (tk // V7X_LANES), axis=1))
        alpha = jnp.exp(m_prev - m_next)
        m_scr[rows, :] = m_next
        acc_scr[rows, :] = alpha * acc_scr[rows, :] + _dot(p.astype(BF16), v)

    @pl.when(j == pl.num_programs(3) - 1)
    def _():
        o = acc_scr[:, 0:HEAD_DIM] / acc_scr[:, HEAD_DIM:2 * HEAD_DIM]
        for g in range(ATT_GROUP):
            o_ref[:, g * HEAD_DIM:(g + 1) * HEAD_DIM] = o[g * tq:(g + 1) * tq, :].astype(BF16)


def _attention(q, k, v, B, T):
    N = q.shape[0]
    tq = min(FLASH_Q_ROWS, T)
    tk = min(FLASH_KV_ROWS, T)
    nq, nk = T // tq, T // tk
    GW = ATT_GROUP * HEAD_DIM
    rows = ATT_GROUP * tq
    return pl.pallas_call(
        functools.partial(_flash_kernel, tq=tq, tk=tk), grid=(B, ATT_KV_HEADS, nq, nk),
        in_specs=[pl.BlockSpec((tq, GW), lambda b, h, i, j: (b * nq + i, h)),
                  pl.BlockSpec((None, tk, HEAD_DIM), lambda b, h, i, j: (h, b * nk + j, 0)),
                  pl.BlockSpec((None, tk, 2 * HEAD_DIM), lambda b, h, i, j: (h, b * nk + j, 0))],
        out_specs=pl.BlockSpec((tq, GW), lambda b, h, i, j: (b * nq + i, h)),
        out_shape=jax.ShapeDtypeStruct((N, ATT_Q_WIDTH), BF16),
        scratch_shapes=[pltpu.VMEM((rows, HEAD_DIM), BF16), pltpu.VMEM((rows, V7X_LANES), F32),
                        pltpu.VMEM((rows, 2 * HEAD_DIM), F32)],
        compiler_params=_params("parallel", "parallel", "parallel", "arbitrary"), name="flash_attention",
    )(q, k, v)


def _rwkv_scan_kernel(*refs, rows_per_set):
    C = RW_CHUNK
    H = C // 2
    W = RWKV_WIDTH
    SUB = 8
    n_sets = len(rows_per_set)
    ones_ref = refs[12 * n_sets]
    outs = refs[12 * n_sets + 1:12 * n_sets + 1 + 2 * n_sets]
    state, vk_scr = refs[-2:]
    dirs = [(refs[12 * s:12 * s + 6], refs[12 * s + 6:12 * s + 12]) for s in range(n_sets)]
    chains = [(s, n, d) for s in range(n_sets) for n in range(rows_per_set[s]) for d in range(2)]

    first_chunk = pl.program_id(1) == 0

    @pl.when(first_chunk)
    def _():
        for c, (s, n, d) in enumerate(chains):
            if s == 0:
                state[c] = jnp.zeros((HEAD_DIM, W), F32)

    @pl.when(jnp.logical_and(first_chunk, pl.program_id(0) == 0))
    def _():
        for c, (s, n, d) in enumerate(chains):
            if s > 0:
                state[c] = jnp.zeros((HEAD_DIM, W), F32)

    ones = ones_ref[...]
    same_head = (lax.broadcasted_iota(jnp.int32, (W, W), 0) // HEAD_DIM
                 == lax.broadcasted_iota(jnp.int32, (W, W), 1) // HEAD_DIM)
    step_iota = lax.broadcasted_iota(jnp.int32, (H, HEAD_DIM, W), 0)
    lane_iota = lax.broadcasted_iota(jnp.int32, (H, HEAD_DIM, W), 2) & (HEAD_DIM - 1)

    def head_sums(vals):
        stacked = jnp.concatenate([v.astype(BF16) for v in vals], axis=0)
        out = _dot(stacked, ones)
        return [out[c * HEAD_DIM:(c + 1) * HEAD_DIM, :] for c in range(len(vals))]

    own_head = (lax.broadcasted_iota(jnp.int32, (SUB, W), 0)
                == lax.broadcasted_iota(jnp.int32, (SUB, W), 1) // HEAD_DIM)
    carry = tuple(state[c] for c in range(len(chains)))
    for half in range(2):
        lo = (half * H, (1 - half) * H)
        for c, (s, n, d) in enumerate(chains):
            kc = dirs[s][d][3][n]
            kbd = jnp.where(same_head, jnp.concatenate([kc] * (W // C), axis=0), 0.0).astype(BF16)
            sel = jnp.where(step_iota + lo[d] == lane_iota, dirs[s][d][5][n][None], 0.0).reshape(H * HEAD_DIM, W)
            vk_scr[c] = _dot(sel.astype(BF16), kbd).reshape(H, HEAD_DIM, W)

        def group(g, states, lo=lo):
            states = list(states)
            bases = (lo[0] + g * SUB, lo[1] + (H // SUB - 1 - g) * SUB)
            rows = [[ref[n, pl.ds(pl.multiple_of(bases[d], SUB), SUB), :] for ref in dirs[s][d][:5]]
                    for (s, n, d) in chains]
            ys = [[None] * SUB for _ in chains]
            for j in range(SUB):
                jj = (j, SUB - 1 - j)
                row = lambda c, f: rows[c][f][jj[chains[c][2]]:jj[chains[c][2]] + 1, :]
                sa = head_sums([states[c] * row(c, 0) for c in range(len(chains))])
                new = [states[c] * row(c, 2) - sa[c] * row(c, 4)
                       + vk_scr[c, bases[chains[c][2]] + jj[chains[c][2]] - lo[chains[c][2]]]
                       for c in range(len(chains))]
                for c, (s, n, d) in enumerate(chains):
                    read = new[c] if d == 0 else states[c]
                    r_heads = jnp.where(own_head, row(c, 1), 0.0).astype(BF16)
                    ys[c][jj[d]] = _dot_nt(r_heads, read.astype(BF16))
                states = new
            for c, (s, n, d) in enumerate(chains):
                span = pl.ds(pl.multiple_of(bases[d], SUB), SUB)
                for h in range(RWKV_HEADS):
                    col = jnp.concatenate([ys[c][r][h:h + 1, :] for r in range(SUB)], axis=0)
                    outs[2 * s + d][n, span, h * HEAD_DIM:(h + 1) * HEAD_DIM] = col
            return tuple(states)

        carry = lax.fori_loop(0, H // SUB, group, carry)

    for c in range(len(chains)):
        state[c] = carry[c]


def _rwkv_scan_call(sets, ones256):
    W, C = RWKV_WIDTH, RW_CHUNK
    _, B0, T0, nb0 = sets[0]
    nc0 = T0 // C
    operands, in_specs, out_specs, out_shapes = [], [], [], []
    for s, (rwp, B, T, nb) in enumerate(sets):
        nc = T // C
        fields = rwp.reshape(B, T, RW_FIELDS * W)
        v = rwp[:, 2 * W:3 * W].reshape(B, nc, C, RWKV_HEADS, HEAD_DIM)
        vt = v.transpose(0, 1, 4, 3, 2).reshape(B, nc, HEAD_DIM, W)

        def block(b, c, reverse, s=s, nc=nc):
            bi, ci = (b, c) if s == 0 else (0, b * nc0 + c)
            return bi, (nc - 1 - ci if reverse else ci)

        field = lambda f, rev, nb=nb, block=block: pl.BlockSpec(
            (nb, C, W), lambda b, c: block(b, c, rev) + (f,))
        vt_spec = lambda rev, nb=nb, block=block: pl.BlockSpec(
            (nb, None, HEAD_DIM, W), lambda b, c: block(b, c, rev) + (0, 0))
        in_specs += [field(1, False), field(0, False), field(3, False), field(5, False), field(7, False),
                     vt_spec(False),
                     field(1, True), field(0, True), field(4, True), field(6, True), field(8, True),
                     vt_spec(True)]
        operands += [fields] * 5 + [vt] + [fields] * 5 + [vt]
        y_spec = lambda rev, nb=nb, block=block: pl.BlockSpec(
            (nb, C, W), lambda b, c: block(b, c, rev) + (0,))
        out_specs += [y_spec(False), y_spec(True)]
        out_shapes += [jax.ShapeDtypeStruct((B, T, W), F32)] * 2
    n_chains = 2 * sum(nb for _, _, _, nb in sets)
    outs = pl.pallas_call(
        functools.partial(_rwkv_scan_kernel, rows_per_set=tuple(nb for _, _, _, nb in sets)),
        grid=(B0 // nb0, nc0), in_specs=in_specs + [_full(ones256.shape)],
        out_specs=out_specs, out_shape=out_shapes,
        scratch_shapes=[pltpu.VMEM((n_chains, HEAD_DIM, W), F32),
                        pltpu.VMEM((n_chains, C // 2, HEAD_DIM, W), F32)],
        compiler_params=_params("arbitrary", "arbitrary"), name="rwkv_scan",
    )(*operands, ones256)
    return [(outs[2 * s].reshape(B * T, W), outs[2 * s + 1].reshape(B * T, W))
            for s, (_, B, T, _) in enumerate(sets)]


def _rwkv_scan(groups, ones256):
    rows = lambda B: RW_BATCH_ROWS if B % RW_BATCH_ROWS == 0 else 1
    if len(groups) == 2:
        (_, B0, T0), (_, B1, T1) = groups
        if B1 == 1 and (B0 // rows(B0)) * (T0 // RW_CHUNK) == T1 // RW_CHUNK:
            return _rwkv_scan_call([groups[0] + (rows(B0),), groups[1] + (1,)], ones256)
    return [_rwkv_scan_call([g + (rows(g[1]),)], ones256)[0] for g in groups]


def _merge_kernel(x_ref, mod_ref, ya_ref, yb_ref, yf_ref, ybk_ref, rg_ref, bonus_ref, gn_ref, gates_ref,
                  ones_ref, wa_ref, wb_ref, wc_ref, wo_ref, o_ref):
    y = _head_layer_norm(yf_ref[...] + ybk_ref[...], ones_ref[...], RWKV_GN_EPS) * gn_ref[...]
    yc = ((y + bonus_ref[...]) * rg_ref[...]).astype(BF16)
    D = D_MODEL
    merged = (gates_ref[:, 0:D].astype(F32) * _dot(ya_ref[...], wa_ref[...])
              + gates_ref[:, D:2 * D].astype(F32) * _dot(yb_ref[...], wb_ref[...])
              + gates_ref[:, 2 * D:3 * D].astype(F32) * _dot(yc, wc_ref[...]))
    o_ref[...] = x_ref[...] + mod_ref[2:3, :] * _dot(merged.astype(BF16), wo_ref[...])


def _merge(x, mod, T, ya, yb, yf, ybk, rwp, gates, lw, ones256):
    N = x.shape[0]
    tm = min(512, T)
    tps = T // tm
    x_spec, mod_spec, _ = _token_specs(tm, tps)
    row = lambda w: pl.BlockSpec((tm, w), lambda i: (i, 0))
    field = lambda f: pl.BlockSpec((tm, RWKV_WIDTH), lambda i: (i, f))
    return pl.pallas_call(
        _merge_kernel, grid=(N // tm,),
        in_specs=[x_spec, mod_spec, row(RET_WIDTH), row(ATT_Q_WIDTH), row(RWKV_WIDTH), row(RWKV_WIDTH),
                  field(9), field(10), _full(lw["rw_gn"].shape), row(GATE_COLS), _full(ones256.shape),
                  _full(lw["w_branch_a"].shape), _full(lw["w_branch_b"].shape), _full(lw["w_branch_c"].shape),
                  _full(lw["w_out"].shape)],
        out_specs=x_spec, out_shape=jax.ShapeDtypeStruct((N, D_MODEL), F32),
        compiler_params=_params("parallel"), name="merge_out",
    )(x, mod, ya, yb, yf, ybk, rwp, rwp, lw["rw_gn"], gates, ones256,
      lw["w_branch_a"], lw["w_branch_b"], lw["w_branch_c"], lw["w_out"])


def _route(h, hb, rwt_ref, rbias_ref, tm):
    h_lo = (h - hb.astype(F32)).astype(BF16)
    rw_hi, rw_lo = _split(rwt_ref[...])
    logits = _dot_nt(rw_hi, hb) + _dot_nt(rw_hi, h_lo) + _dot_nt(rw_lo, hb)
    scores = _sigmoid(logits)
    choice = scores + jnp.concatenate([rbias_ref[...]] * (tm // V7X_LANES), axis=1)
    per_group = N_EXPERTS // N_GROUPS
    sub = lax.broadcasted_iota(jnp.int32, (per_group, tm), 0)
    groups, gscore = [], []
    for g in range(N_GROUPS):
        cg = choice[g * per_group:(g + 1) * per_group, :]
        m1 = jnp.max(cg, axis=0, keepdims=True)
        first = jnp.min(jnp.where(cg == m1, sub, per_group), axis=0, keepdims=True)
        m2 = jnp.max(jnp.where(sub == first, -jnp.inf, cg), axis=0, keepdims=True)
        groups.append(cg)
        gscore.append(m1 + m2)
    masked = []
    for g in range(N_GROUPS):
        beaten = jnp.zeros((1, tm), F32)
        for o in range(N_GROUPS):
            if o == g:
                continue
            wins = (gscore[o] >= gscore[g]) if o < g else (gscore[o] > gscore[g])
            beaten = beaten + jnp.where(wins, 1.0, 0.0)
        keep = jnp.where(beaten < TOPK_GROUPS, 1.0, 0.0)
        masked.append(jnp.where(jnp.broadcast_to(keep, (per_group, tm)) > 0.5, groups[g], -jnp.inf))
    mc = jnp.concatenate(masked, axis=0)
    eidx = lax.broadcasted_iota(jnp.int32, (N_EXPERTS, tm), 0).astype(F32)
    sel = jnp.zeros((N_EXPERTS, tm), F32)
    for _ in range(TOP_K):
        top = jnp.max(mc, axis=0, keepdims=True)
        first = jnp.min(jnp.where(mc == top, eidx, float(N_EXPERTS)), axis=0, keepdims=True)
        pick = eidx == first
        sel = jnp.where(pick, 1.0, sel)
        mc = jnp.where(pick, -jnp.inf, mc)
    wts = jnp.where(sel > 0.5, scores, 0.0)
    return wts / jnp.sum(wts, axis=0, keepdims=True) * ROUTED_SCALE, sel


def _selected_lists(comb, sel, rank):
    eidx = lax.broadcasted_iota(jnp.int32, comb.shape, 0).astype(F32)
    prev = jnp.full((1, comb.shape[1]), -1.0, F32)
    ids, wts, pos = [], [], []
    for _ in range(TOP_K):
        cand = jnp.where(sel > 0.5, jnp.where(eidx > prev, eidx, float(N_EXPERTS)), float(N_EXPERTS))
        prev = jnp.min(cand, axis=0, keepdims=True)
        wts.append(jnp.sum(jnp.where(eidx == prev, comb, 0.0), axis=0, keepdims=True))
        pos.append(jnp.sum(jnp.where(eidx == prev, rank, 0.0), axis=0, keepdims=True))
        ids.append(jnp.minimum(prev, N_EXPERTS - 1.0))
    as_int = lambda rows: jnp.concatenate(rows, axis=0).astype(jnp.int32)
    return as_int(ids), jnp.concatenate(wts, axis=0), as_int(pos)


def _pack_halves(y):
    n = y.shape[1] // 2
    hi = pltpu.bitcast(y[:, :n].astype(BF16).astype(F32), jnp.int32)
    lo = pltpu.bitcast(y[:, n:].astype(BF16).astype(F32), jnp.int32)
    return hi | lax.shift_right_logical(lo, 16)


def _unpack_halves(w):
    hi = pltpu.bitcast(w & jnp.int32(-65536), F32)
    lo = pltpu.bitcast(lax.shift_left(w, 16), F32)
    return hi, lo


def _moe_route_kernel(x_ref, mod_ref, g_ref, rwt_ref, rbias_ref, shgu_ref, shd_ref, tri_ref,
                      hp_ref, ids_ref, wts_ref, pos_ref, base_ref, count_ref, *, tm):
    F = EXPERT_FF

    @pl.when(pl.program_id(0) == 0)
    def _():
        count_ref[...] = jnp.zeros_like(count_ref)

    x = x_ref[...]
    h = _norm_mod(x, g_ref[...], mod_ref[3:4, :], mod_ref[4:5, :])
    hb = h.astype(BF16)
    hp_ref[...] = _pack_halves(hb.astype(F32))
    comb, sel = _route(h, hb, rwt_ref, rbias_ref, tm)
    before = count_ref[...]
    inclusive = _dot(sel.astype(BF16), tri_ref[...])
    rank = inclusive - sel + jnp.concatenate([before] * (tm // V7X_LANES), axis=1)
    count_ref[...] = before + jnp.sum(sel, axis=1, keepdims=True)
    ids, wts, pos = _selected_lists(comb, sel, rank)
    ids_ref[...] = ids
    wts_ref[...] = wts
    pos_ref[...] = pos
    gu = _dot(hb, shgu_ref[...])
    shared = _dot((_silu(gu[:, 0:F]) * gu[:, F:2 * F]).astype(BF16), shd_ref[...])
    base_ref[...] = x + mod_ref[5:6, :] * shared


def _moe_expert_kernel(tile_expert_ref, xs_ref, wgu_ref, wd_ref, o_ref):
    del tile_expert_ref
    F = EXPERT_FF
    half = D_MODEL // 2
    hi, lo = _unpack_halves(xs_ref[...])
    gu = _dot(hi.astype(BF16), wgu_ref[0:half, :]) + _dot(lo.astype(BF16), wgu_ref[half:, :])
    act = (_silu(gu[:, 0:F]) * gu[:, F:2 * F]).astype(BF16)
    o_ref[...] = _pack_halves(_dot(act, wd_ref[...]))


def _moe_combine_kernel(base_ref, mod_ref, y_ref, w_ref, fin_ref, o_ref, *, final):
    half = D_MODEL // 2
    acc_hi = jnp.zeros((base_ref.shape[0], half), F32)
    acc_lo = jnp.zeros((base_ref.shape[0], half), F32)
    for k in range(TOP_K):
        hi, lo = _unpack_halves(y_ref[k])
        wk = w_ref[:, k:k + 1]
        acc_hi = acc_hi + wk * hi
        acc_lo = acc_lo + wk * lo
    out_hi = base_ref[:, 0:half] + mod_ref[5:6, 0:half] * acc_hi
    out_lo = base_ref[:, half:] + mod_ref[5:6, half:] * acc_lo
    if final:
        ms = (jnp.sum(out_hi * out_hi, axis=-1, keepdims=True)
              + jnp.sum(out_lo * out_lo, axis=-1, keepdims=True)) * (1.0 / D_MODEL)
        scale = lax.rsqrt(ms + NORM_EPS)
        out_hi = out_hi * scale * fin_ref[:, 0:half]
        out_lo = out_lo * scale * fin_ref[:, half:]
    o_ref[:, 0:half] = out_hi
    o_ref[:, half:] = out_lo


def _gather_rows(table, idx):
    rows, width = idx.shape[0], table.shape[1]
    info = plsc.get_sparse_core_info()
    workers = info.num_cores * info.num_subcores
    windows = rows // (workers * SC_GATHER_ROWS)
    assert windows * workers * SC_GATHER_ROWS == rows
    mesh = plsc.VectorSubcoreMesh(core_axis_name="core", subcore_axis_name="subcore")

    @functools.partial(pl.kernel, out_type=jax.ShapeDtypeStruct((rows, width), table.dtype), mesh=mesh,
                       scratch_types=[pltpu.VMEM((SC_GATHER_ROWS,), jnp.int32),
                                      pltpu.VMEM((SC_GATHER_ROWS, width), table.dtype)],
                       name="moe_gather")
    def gather(table_hbm, idx_hbm, out_hbm, idx_vmem, rows_vmem):
        worker = lax.axis_index("subcore") * info.num_cores + lax.axis_index("core")
        base = worker * (windows * SC_GATHER_ROWS)

        @pl.loop(0, windows)
        def _(j):
            span = pl.ds(pl.multiple_of(base + j * SC_GATHER_ROWS, SC_GATHER_ROWS), SC_GATHER_ROWS)
            pltpu.sync_copy(idx_hbm.at[span], idx_vmem)
            pltpu.sync_copy(table_hbm.at[idx_vmem], rows_vmem)
            pltpu.sync_copy(rows_vmem, out_hbm.at[span])

    return gather(table, idx)


def _scatter_rows(table, dest, total_rows):
    n_lists, n_rows = dest.shape
    width = table.shape[1]
    info = plsc.get_sparse_core_info()
    workers = info.num_cores * info.num_subcores
    windows = n_rows // (workers * SC_GATHER_ROWS)
    assert windows * workers * SC_GATHER_ROWS == n_rows
    windows_per_list = n_rows // SC_GATHER_ROWS
    mesh = plsc.VectorSubcoreMesh(core_axis_name="core", subcore_axis_name="subcore")

    @functools.partial(pl.kernel, out_type=jax.ShapeDtypeStruct((total_rows, width), table.dtype), mesh=mesh,
                       scratch_types=[pltpu.VMEM((1, SC_GATHER_ROWS), jnp.int32),
                                      pltpu.VMEM((SC_GATHER_ROWS, width), table.dtype)],
                       name="moe_scatter")
    def scatter(table_hbm, dest_hbm, out_hbm, idx_vmem, rows_vmem):
        worker = lax.axis_index("subcore") * info.num_cores + lax.axis_index("core")

        @pl.loop(0, windows)
        def _(j):
            window = worker * windows + j
            span = pl.ds(pl.multiple_of(window * SC_GATHER_ROWS, SC_GATHER_ROWS), SC_GATHER_ROWS)
            pltpu.sync_copy(table_hbm.at[span], rows_vmem)
            for k in range(n_lists):
                pltpu.sync_copy(dest_hbm.at[pl.ds(k * windows_per_list + window, 1)], idx_vmem)
                pltpu.sync_copy(rows_vmem, out_hbm.at[idx_vmem.at[0]])

    return scatter(table, dest.reshape(n_lists * windows_per_list, SC_GATHER_ROWS))


def _dispatch_plan(ids, pos, counts, n_tokens, tg):
    E = N_EXPERTS
    padded = (counts + tg - 1) // tg * tg
    ends = jnp.cumsum(padded)
    off = ends - padded
    onehot = ids[:, :, None] == jnp.arange(E, dtype=jnp.int32)[None, None, :]
    dest = pos + jnp.sum(jnp.where(onehot, off[None, None, :], 0), axis=-1)
    total = n_tokens * TOP_K + E * tg
    tile_start = jnp.arange(total // tg, dtype=jnp.int32) * tg
    tile_expert = jnp.sum((ends[None, :] <= tile_start[:, None]).astype(jnp.int32), axis=1)
    return dest, total, jnp.minimum(tile_expert, E - 1)


def _moe(x, mod, T, lw, final_g, final):
    N = x.shape[0]
    tm = min(MOE_TOKENS, T)
    tps = T // tm
    D, F, E = D_MODEL, EXPERT_FF, N_EXPERTS
    tg = max(MOE_GROUP_ROWS_MIN, min(MOE_GROUP_ROWS_MAX, N * TOP_K // (E * 4)))
    half = D // 2
    x_spec, mod_spec, g_spec = _token_specs(tm, tps)
    list_spec = pl.BlockSpec((TOP_K, tm), lambda i: (0, i))

    tri = jnp.asarray(np.triu(np.ones((tm, tm), np.float32)), BF16)
    hp, ids, wts, pos, base, counts = pl.pallas_call(
        functools.partial(_moe_route_kernel, tm=tm), grid=(N // tm,),
        in_specs=[x_spec, mod_spec, g_spec, _full((E, D)), _full((E, V7X_LANES)), _full((D, 2 * F)), _full((F, D)),
                  _full((tm, tm))],
        out_specs=[pl.BlockSpec((tm, half), lambda i: (i, 0)), list_spec, list_spec, list_spec, x_spec,
                   _full((E, V7X_LANES))],
        out_shape=[jax.ShapeDtypeStruct((N, half), jnp.int32), jax.ShapeDtypeStruct((TOP_K, N), jnp.int32),
                   jax.ShapeDtypeStruct((TOP_K, N), F32), jax.ShapeDtypeStruct((TOP_K, N), jnp.int32),
                   jax.ShapeDtypeStruct((N, D), F32), jax.ShapeDtypeStruct((E, V7X_LANES), F32)],
        compiler_params=_params("arbitrary"), name="moe_route",
    )(x, mod, lw["norm2_g"], lw["router_wt"], lw["router_bias"], lw["sh_gu"], lw["sh_d"], tri)

    dest, total, tile_expert = _dispatch_plan(ids, pos, counts[:, 0].astype(jnp.int32), N, tg)
    xs = _scatter_rows(hp, dest, total)
    n_tiles = total // tg
    ys = pl.pallas_call(
        _moe_expert_kernel,
        grid_spec=pltpu.PrefetchScalarGridSpec(
            num_scalar_prefetch=1, grid=(n_tiles,),
            in_specs=[pl.BlockSpec((tg, half), lambda i, te: (i, 0)),
                      pl.BlockSpec((None, D, 2 * F), lambda i, te: (te[i], 0, 0)),
                      pl.BlockSpec((None, F, D), lambda i, te: (te[i], 0, 0))],
            out_specs=pl.BlockSpec((tg, half), lambda i, te: (i, 0))),
        out_shape=jax.ShapeDtypeStruct(xs.shape, jnp.int32),
        compiler_params=_params("parallel"), name="moe_experts",
    )(tile_expert, xs, lw["exp_gu"], lw["exp_d"])

    yk = _gather_rows(ys, dest.reshape(-1)).reshape(TOP_K, N, half)
    return pl.pallas_call(
        functools.partial(_moe_combine_kernel, final=final), grid=(N // tm,),
        in_specs=[x_spec, mod_spec, pl.BlockSpec((TOP_K, tm, half), lambda i: (0, i, 0)),
                  pl.BlockSpec((tm, TOP_K), lambda i: (i, 0)), g_spec],
        out_specs=x_spec, out_shape=jax.ShapeDtypeStruct((N, D), F32),
        compiler_params=_params("parallel"), name="moe_combine",
    )(base, mod, yk, wts.T, final_g)


def _block_diag2(a, b):
    za = jnp.zeros((a.shape[0], b.shape[1]), a.dtype)
    zb = jnp.zeros((b.shape[0], a.shape[1]), a.dtype)
    return jnp.concatenate([jnp.concatenate([a, za], 1), jnp.concatenate([zb, b], 1)], 0)


def _layer_weights(P, l):
    w_in = P["w_in"][l]
    o1, o2, o3 = RET_COLS, RET_COLS + ATT_COLS, RET_COLS + ATT_COLS + RW_COLS
    w_ret, w_att, w_rw, w_gate = w_in[:, :o1], w_in[:, o1:o2], w_in[:, o2:o3], w_in[:, o3:]
    qk_gain = jnp.zeros((8, ATT_Q_WIDTH), F32)
    qk_gain = qk_gain.at[0].set(jnp.tile(P["q_norm_g"][l], ATT_Q_HEADS))
    qk_gain = qk_gain.at[1, :ATT_KV_WIDTH].set(jnp.tile(P["k_norm_g"][l], ATT_KV_HEADS))
    rw_vec = jnp.zeros((8, RWKV_WIDTH), F32)
    for j, name in enumerate(("rw_w0_f", "rw_w0_b", "rw_a0_f", "rw_a0_b", "rw_k_k", "rw_k_a")):
        rw_vec = rw_vec.at[j].set(P[name][l])
    rw_vec = rw_vec.at[6].set(P["rw_r_k"][l].reshape(RWKV_WIDTH))
    row = lambda v: v.reshape(1, -1)
    return {
        "norm1_g": row(P["norm1_g"][l]), "norm2_g": row(P["norm2_g"][l]),
        "w_ret": w_ret.astype(BF16), "w_att": w_att.astype(BF16), "w_rw": w_rw.astype(BF16),
        "w_gate": w_gate.astype(BF16), "qk_gain": qk_gain, "ret_gn": row(P["ret_gn"][l]),
        "shift_mu": P["shift_mu"][l], "rw_vec": rw_vec,
        "rw_w2": _block_diag2(P["rw_w2_f"][l], P["rw_w2_b"][l]).astype(BF16),
        "rw_a2": _block_diag2(P["rw_a2_f"][l], P["rw_a2_b"][l]).astype(BF16),
        "rw_g2": P["rw_g2"][l].astype(BF16), "rw_gn": row(P["rw_gn"][l]),
        "w_branch_a": P["w_branch_a"][l].astype(BF16), "w_branch_b": P["w_branch_b"][l].astype(BF16),
        "w_branch_c": P["w_branch_c"][l].astype(BF16), "w_out": P["w_out"][l].astype(BF16),
        "router_wt": P["router_w"][l].T,
        "router_bias": jnp.broadcast_to(P["router_bias"][l][:, None], (N_EXPERTS, V7X_LANES)),
        "sh_gu": jnp.concatenate([P["sh_w_gate"][l], P["sh_w_up"][l]], axis=1).astype(BF16),
        "sh_d": P["sh_w_down"][l].astype(BF16),
        "exp_gu": jnp.concatenate([P["exp_w_gate"][l], P["exp_w_up"][l]], axis=2).astype(BF16),
        "exp_d": P["exp_w_down"][l].astype(BF16),
    }


def _ones_block_diag(n):
    idx = np.arange(n) // HEAD_DIM
    return jnp.asarray(idx[:, None] == idx[None, :], BF16)


def _tables(T):
    rows = T // GRID_W
    row = jnp.repeat(jnp.arange(rows, dtype=F32), GRID_W)
    col = jnp.tile(jnp.arange(GRID_W, dtype=F32), rows)
    freqs = ROPE_THETA ** (-jnp.arange(ROPE_PAIRS_PER_AXIS, dtype=F32) / ROPE_PAIRS_PER_AXIS)
    ang = jnp.concatenate([row[:, None] * freqs, col[:, None] * freqs], axis=-1)
    cos, sin = jnp.cos(ang), jnp.sin(ang)
    return {
        "cos": jnp.tile(jnp.concatenate([cos, cos], axis=-1), (1, ATT_Q_HEADS)),
        "sin": jnp.tile(jnp.concatenate([-sin, sin], axis=-1), (1, ATT_Q_HEADS)),
        "ones128": _ones_block_diag(128), "ones256": _ones_block_diag(256), "ones512": _ones_block_diag(512),
    }


def _trunks(requests, P, layer_weights):
    groups = []
    for x, c in requests:
        B, T, D = x.shape
        groups.append({"B": B, "T": T, "x": x.reshape(B * T, D), "tabs": _tables(T),
                       "mods": _ada_mod(c, P["ada_w"], P["ada_b"])})
    final_g = P["final_g"].reshape(1, D_MODEL)
    depth = len(layer_weights)
    for l, lw in enumerate(layer_weights):
        mixed = []
        for g in groups:
            B, T, tabs, mod = g["B"], g["T"], g["tabs"], g["mods"][l]
            ret, q, k, v, gates, rwp = _input_projections(g["x"], mod, T, lw, tabs)
            ya = _retention(ret, B, T, lw["ret_gn"], tabs["ones256"])
            yb = _attention(q, k, v, B, T)
            mixed.append((mod, ya, yb, gates, rwp))
        scans = _rwkv_scan([(m[4], g["B"], g["T"]) for m, g in zip(mixed, groups)], groups[0]["tabs"]["ones256"])
        for g, (mod, ya, yb, gates, rwp), (yf, ybk) in zip(groups, mixed, scans):
            x = _merge(g["x"], mod, g["T"], ya, yb, yf, ybk, rwp, gates, lw, g["tabs"]["ones256"])
            g["x"] = _moe(x, mod, g["T"], lw, final_g, final=(l == depth - 1))
    return [g["x"].reshape(g["B"], g["T"], D_MODEL) for g in groups]


def kernel(x_prompt, x_sample, c_prompt, c_sample, norm1_g, norm2_g, final_g, ada_w, ada_b, w_in, ret_gn, q_norm_g, k_norm_g, shift_mu, rw_w0_f, rw_w2_f, rw_w0_b, rw_w2_b, rw_a0_f, rw_a2_f, rw_a0_b, rw_a2_b, rw_g2, rw_k_k, rw_k_a, rw_r_k, rw_gn, w_branch_a, w_branch_b, w_branch_c, w_out, router_w, router_bias, exp_w_gate, exp_w_up, exp_w_down, sh_w_gate, sh_w_up, sh_w_down):
    P = {
        "norm1_g": norm1_g, "norm2_g": norm2_g, "final_g": final_g, "ada_w": ada_w, "ada_b": ada_b,
        "w_in": w_in, "ret_gn": ret_gn, "q_norm_g": q_norm_g, "k_norm_g": k_norm_g, "shift_mu": shift_mu,
        "rw_w0_f": rw_w0_f, "rw_w2_f": rw_w2_f, "rw_w0_b": rw_w0_b, "rw_w2_b": rw_w2_b,
        "rw_a0_f": rw_a0_f, "rw_a2_f": rw_a2_f, "rw_a0_b": rw_a0_b, "rw_a2_b": rw_a2_b,
        "rw_g2": rw_g2, "rw_k_k": rw_k_k, "rw_k_a": rw_k_a, "rw_r_k": rw_r_k, "rw_gn": rw_gn,
        "w_branch_a": w_branch_a, "w_branch_b": w_branch_b, "w_branch_c": w_branch_c, "w_out": w_out,
        "router_w": router_w, "router_bias": router_bias,
        "exp_w_gate": exp_w_gate, "exp_w_up": exp_w_up, "exp_w_down": exp_w_down,
        "sh_w_gate": sh_w_gate, "sh_w_up": sh_w_up, "sh_w_down": sh_w_down,
    }
    layer_weights = [_layer_weights(P, l) for l in range(w_in.shape[0])]
    y_prompt, y_sample = _trunks([(x_prompt, c_prompt), (x_sample, c_sample)], P, layer_weights)
    return (y_prompt, y_sample)
```
